```python
import jax, jax.numpy as jnp
from jax import lax
import numpy as np

D_MODEL = 1024
BATCH = 8
SEQ = 8192
DEPTH = 1

CHUNK = 64
LEFT_CHUNKS = 8
BAND = (LEFT_CHUNKS + 1) * CHUNK
N_HEADS = 8
HEAD_DIM = 64
D_ATTN = N_HEADS * HEAD_DIM
REL_CLIP = 128
D_CONV = 512
CONV_WIDTH = 31
N_BRANCH = 2
D_IN = 3 * D_ATTN + 2 * D_CONV + N_BRANCH * D_MODEL
D_FF = 2816
EPS = 1e-6

kernel_name = "hybrid_chunked_attn_conformer_conv_macaron"


def rms_norm(x, g):
    xf = x.astype(jnp.float32)
    y = xf * lax.rsqrt(jnp.mean(xf * xf, axis=-1, keepdims=True) + EPS)
    return (y * g.astype(jnp.float32)).astype(x.dtype)


def layer_norm(x, g, b):
    xf = x.astype(jnp.float32)
    mu = jnp.mean(xf, axis=-1, keepdims=True)
    var = jnp.mean(jnp.square(xf - mu), axis=-1, keepdims=True)
    y = (xf - mu) * lax.rsqrt(var + EPS)
    return (y * g.astype(jnp.float32) + b.astype(jnp.float32)).astype(x.dtype)


def swiglu(x, w_gate, w_up, w_down):
    return (jax.nn.silu(x @ w_gate) * (x @ w_up)) @ w_down


def chunked_attention(q, k, v, rel_table):
    B, S, H, Dh = q.shape
    nc = S // CHUNK
    qc = q.reshape(B, nc, CHUNK, H, Dh)
    pad = ((0, 0), (LEFT_CHUNKS, 0), (0, 0), (0, 0), (0, 0))
    kp = jnp.pad(k.reshape(B, nc, CHUNK, H, Dh), pad)
    vp = jnp.pad(v.reshape(B, nc, CHUNK, H, Dh), pad)
    k_band = jnp.concatenate([kp[:, j:j + nc] for j in range(LEFT_CHUNKS + 1)], axis=2)
    v_band = jnp.concatenate([vp[:, j:j + nc] for j in range(LEFT_CHUNKS + 1)], axis=2)
    scores = jnp.einsum('bnqhd,bnkhd->bnhqk', qc, k_band).astype(jnp.float32) * (HEAD_DIM ** -0.5)
    qi = jnp.arange(CHUNK)[:, None]
    kj = jnp.arange(BAND)[None, :]
    dist = qi + LEFT_CHUNKS * CHUNK - kj
    idx = jnp.clip(dist, -REL_CLIP, REL_CLIP) + REL_CLIP
    bias = rel_table.astype(jnp.float32)[:, idx]
    scores = scores + bias[None, None]
    key_chunk = jnp.arange(nc)[:, None] + (jnp.arange(BAND) // CHUNK)[None, :] - LEFT_CHUNKS
    valid = (key_chunk >= 0)[None, :, None, None, :]
    scores = jnp.where(valid, scores, jnp.float32(-1e30))
    p = jax.nn.softmax(scores, axis=-1).astype(v.dtype)
    out = jnp.einsum('bnhqk,bnkhd->bnqhd', p, v_band)
    return out.reshape(B, S, H * Dh)


def conv_module(c_in, glu_bias, dw_w, dw_b, ln_g, ln_b, w_out):
    c = c_in + glu_bias
    c = c[..., :D_CONV] * jax.nn.sigmoid(c[..., D_CONV:])
    c = lax.conv_general_dilated(
        c, dw_w.astype(c.dtype), window_strides=(1,), padding=((CONV_WIDTH - 1, 0),),
        dimension_numbers=('NWC', 'WIO', 'NWC'), feature_group_count=D_CONV) + dw_b
    c = jax.nn.silu(layer_norm(c, ln_g, ln_b))
    return c @ w_out


def hybrid_mixer(u, w_in, gate_bias, rel_table, w_attn_out, conv_glu_bias, conv_dw_w,
                 conv_dw_b, conv_ln_g, conv_ln_b, conv_w_out, w_out):
    B, S, _ = u.shape
    proj = u @ w_in
    q, k, v, c_in, g = jnp.split(
        proj, [D_ATTN, 2 * D_ATTN, 3 * D_ATTN, 3 * D_ATTN + 2 * D_CONV], axis=-1)
    shp = (B, S, N_HEADS, HEAD_DIM)
    y_a = chunked_attention(q.reshape(shp), k.reshape(shp), v.reshape(shp), rel_table) @ w_attn_out
    y_b = conv_module(c_in, conv_glu_bias, conv_dw_w, conv_dw_b, conv_ln_g, conv_ln_b, conv_w_out)
    gates = jax.nn.sigmoid(g + gate_bias)
    merged = gates[..., :D_MODEL] * y_a + gates[..., D_MODEL:] * y_b
    return merged @ w_out


def _fwd_setup_inputs(seed: int = 0) -> dict:
    key = jax.random.key(seed)
    ks = jax.random.split(key, 32)
    L, D = DEPTH, D_MODEL

    def w(k, shape, fan_in):
        return jax.random.normal(k, shape, jnp.float32) * (fan_in ** -0.5)

    def gain(k, shape):
        return 1.0 + 0.05 * jax.random.normal(k, shape, jnp.float32)

    def small(k, shape, s=0.02):
        return s * jax.random.normal(k, shape, jnp.float32)

    return {
        "x": jax.random.normal(ks[0], (BATCH, SEQ, D), jnp.float32),
        "ffn1_norm_pre": gain(ks[1], (L, D)),
        "ffn1_w_gate": w(ks[2], (L, D, D_FF), D),
        "ffn1_w_up": w(ks[3], (L, D, D_FF), D),
        "ffn1_w_down": w(ks[4], (L, D_FF, D), D_FF),
        "ffn1_norm_post": gain(ks[5], (L, D)),
        "mix_norm_pre": gain(ks[6], (L, D)),
        "w_in": w(ks[7], (L, D, D_IN), D),
        "gate_bias": small(ks[8], (L, N_BRANCH * D), 0.1),
        "rel_table": small(ks[9], (L, N_HEADS, 2 * REL_CLIP + 1), 0.5),
        "w_attn_out": w(ks[10], (L, D_ATTN, D), D_ATTN),
        "conv_glu_bias": small(ks[11], (L, 2 * D_CONV)),
        "conv_dw_w": w(ks[12], (L, CONV_WIDTH, 1, D_CONV), CONV_WIDTH),
        "conv_dw_b": small(ks[13], (L, D_CONV)),
        "conv_ln_g": gain(ks[14], (L, D_CONV)),
        "conv_ln_b": small(ks[15], (L, D_CONV)),
        "conv_w_out": w(ks[16], (L, D_CONV, D), D_CONV),
        "w_out": w(ks[17], (L, D, D), D),
        "mix_norm_post": gain(ks[18], (L, D)),
        "ffn2_norm_pre": gain(ks[19], (L, D)),
        "ffn2_w_gate": w(ks[20], (L, D, D_FF), D),
        "ffn2_w_up": w(ks[21], (L, D, D_FF), D),
        "ffn2_w_down": w(ks[22], (L, D_FF, D), D_FF),
        "ffn2_norm_post": gain(ks[23], (L, D)),
    }


def _fwd_reference(x, ffn1_norm_pre, ffn1_w_gate, ffn1_w_up, ffn1_w_down, ffn1_norm_post,
              mix_norm_pre, w_in, gate_bias, rel_table, w_attn_out, conv_glu_bias,
              conv_dw_w, conv_dw_b, conv_ln_g, conv_ln_b, conv_w_out, w_out, mix_norm_post,
              ffn2_norm_pre, ffn2_w_gate, ffn2_w_up, ffn2_w_down, ffn2_norm_post):
    h = x
    for l in range(DEPTH):
        f = swiglu(rms_norm(h, ffn1_norm_pre[l]), ffn1_w_gate[l], ffn1_w_up[l], ffn1_w_down[l])
        h = h + 0.5 * rms_norm(f, ffn1_norm_post[l])
        m = hybrid_mixer(rms_norm(h, mix_norm_pre[l]), w_in[l], gate_bias[l], rel_table[l],
                         w_attn_out[l], conv_glu_bias[l], conv_dw_w[l], conv_dw_b[l],
                         conv_ln_g[l], conv_ln_b[l], conv_w_out[l], w_out[l])
        h = h + rms_norm(m, mix_norm_post[l])
        f = swiglu(rms_norm(h, ffn2_norm_pre[l]), ffn2_w_gate[l], ffn2_w_up[l], ffn2_w_down[l])
        h = h + 0.5 * rms_norm(f, ffn2_norm_post[l])
    return h


import jax as _jax
import jax.numpy as _jnp

TWIN_FORMAT = 'train_step'
FWD_PARAMS = ['x', 'ffn1_norm_pre', 'ffn1_w_gate', 'ffn1_w_up', 'ffn1_w_down', 'ffn1_norm_post', 'mix_norm_pre', 'w_in', 'gate_bias', 'rel_table', 'w_attn_out', 'conv_glu_bias', 'conv_dw_w', 'conv_dw_b', 'conv_ln_g', 'conv_ln_b', 'conv_w_out', 'w_out', 'mix_norm_post', 'ffn2_norm_pre', 'ffn2_w_gate', 'ffn2_w_up', 'ffn2_w_down', 'ffn2_norm_post']
TWIN_WEIGHTS = ['ffn1_norm_pre', 'ffn1_w_gate', 'ffn1_w_up', 'ffn1_w_down', 'ffn1_norm_post', 'mix_norm_pre', 'w_in', 'gate_bias', 'rel_table', 'w_attn_out', 'conv_glu_bias', 'conv_dw_w', 'conv_dw_b', 'conv_ln_g', 'conv_ln_b', 'conv_w_out', 'w_out', 'mix_norm_post', 'ffn2_norm_pre', 'ffn2_w_gate', 'ffn2_w_up', 'ffn2_w_down', 'ffn2_norm_post']
TWIN_DIFF_INPUT = 'x'
TWIN_INPUTS = ['x', 'ffn1_norm_pre', 'ffn1_w_gate', 'ffn1_w_up', 'ffn1_w_down', 'ffn1_norm_post', 'mix_norm_pre', 'w_in', 'gate_bias', 'rel_table', 'w_attn_out', 'conv_glu_bias', 'conv_dw_w', 'conv_dw_b', 'conv_ln_g', 'conv_ln_b', 'conv_w_out', 'w_out', 'mix_norm_post', 'ffn2_norm_pre', 'ffn2_w_gate', 'ffn2_w_up', 'ffn2_w_down', 'ffn2_norm_post', 'loss_target', 'm_ffn1_norm_pre', 'm_ffn1_w_gate', 'm_ffn1_w_up', 'm_ffn1_w_down', 'm_ffn1_norm_post', 'm_mix_norm_pre', 'm_w_in', 'm_gate_bias', 'm_rel_table', 'm_w_attn_out', 'm_conv_glu_bias', 'm_conv_dw_w', 'm_conv_dw_b', 'm_conv_ln_g', 'm_conv_ln_b', 'm_conv_w_out', 'm_w_out', 'm_mix_norm_post', 'm_ffn2_norm_pre', 'm_ffn2_w_gate', 'm_ffn2_w_up', 'm_ffn2_w_down', 'm_ffn2_norm_post', 'v_ffn1_norm_pre', 'v_ffn1_w_gate', 'v_ffn1_w_up', 'v_ffn1_w_down', 'v_ffn1_norm_post', 'v_mix_norm_pre', 'v_w_in', 'v_gate_bias', 'v_rel_table', 'v_w_attn_out', 'v_conv_glu_bias', 'v_conv_dw_w', 'v_conv_dw_b', 'v_conv_ln_g', 'v_conv_ln_b', 'v_conv_w_out', 'v_w_out', 'v_mix_norm_post', 'v_ffn2_norm_pre', 'v_ffn2_w_gate', 'v_ffn2_w_up', 'v_ffn2_w_down', 'v_ffn2_norm_post']
TWIN_OUTPUTS = ['loss', 'grad_x', 'grad_ffn1_norm_pre', 'grad_ffn1_w_gate', 'grad_ffn1_w_up', 'grad_ffn1_w_down', 'grad_ffn1_norm_post', 'grad_mix_norm_pre', 'grad_w_in', 'grad_gate_bias', 'grad_rel_table', 'grad_w_attn_out', 'grad_conv_glu_bias', 'grad_conv_dw_w', 'grad_conv_dw_b', 'grad_conv_ln_g', 'grad_conv_ln_b', 'grad_conv_w_out', 'grad_w_out', 'grad_mix_norm_post', 'grad_ffn2_norm_pre', 'grad_ffn2_w_gate', 'grad_ffn2_w_up', 'grad_ffn2_w_down', 'grad_ffn2_norm_post', 'delta_ffn1_norm_pre', 'delta_ffn1_w_gate', 'delta_ffn1_w_up', 'delta_ffn1_w_down', 'delta_ffn1_norm_post', 'delta_mix_norm_pre', 'delta_w_in', 'delta_gate_bias', 'delta_rel_table', 'delta_w_attn_out', 'delta_conv_glu_bias', 'delta_conv_dw_w', 'delta_conv_dw_b', 'delta_conv_ln_g', 'delta_conv_ln_b', 'delta_conv_w_out', 'delta_w_out', 'delta_mix_norm_post', 'delta_ffn2_norm_pre', 'delta_ffn2_w_gate', 'delta_ffn2_w_up', 'delta_ffn2_w_down', 'delta_ffn2_norm_post', 'new_m_ffn1_norm_pre', 'new_m_ffn1_w_gate', 'new_m_ffn1_w_up', 'new_m_ffn1_w_down', 'new_m_ffn1_norm_post', 'new_m_mix_norm_pre', 'new_m_w_in', 'new_m_gate_bias', 'new_m_rel_table', 'new_m_w_attn_out', 'new_m_conv_glu_bias', 'new_m_conv_dw_w', 'new_m_conv_dw_b', 'new_m_conv_ln_g', 'new_m_conv_ln_b', 'new_m_conv_w_out', 'new_m_w_out', 'new_m_mix_norm_post', 'new_m_ffn2_norm_pre', 'new_m_ffn2_w_gate', 'new_m_ffn2_w_up', 'new_m_ffn2_w_down', 'new_m_ffn2_norm_post', 'new_v_ffn1_norm_pre', 'new_v_ffn1_w_gate', 'new_v_ffn1_w_up', 'new_v_ffn1_w_down', 'new_v_ffn1_norm_post', 'new_v_mix_norm_pre', 'new_v_w_in', 'new_v_gate_bias', 'new_v_rel_table', 'new_v_w_attn_out', 'new_v_conv_glu_bias', 'new_v_conv_dw_w', 'new_v_conv_dw_b', 'new_v_conv_ln_g', 'new_v_conv_ln_b', 'new_v_conv_w_out', 'new_v_w_out', 'new_v_mix_norm_post', 'new_v_ffn2_norm_pre', 'new_v_ffn2_w_gate', 'new_v_ffn2_w_up', 'new_v_ffn2_w_down', 'new_v_ffn2_norm_post']
TWIN_LEAF_KINDS = {'loss': 'loss', 'grad_x': 'grad_x', 'grad_ffn1_norm_pre': 'grad_w', 'grad_ffn1_w_gate': 'grad_w', 'grad_ffn1_w_up': 'grad_w', 'grad_ffn1_w_down': 'grad_w', 'grad_ffn1_norm_post': 'grad_w', 'grad_mix_norm_pre': 'grad_w', 'grad_w_in': 'grad_w', 'grad_gate_bias': 'grad_w', 'grad_rel_table': 'grad_w', 'grad_w_attn_out': 'grad_w', 'grad_conv_glu_bias': 'grad_w', 'grad_conv_dw_w': 'grad_w', 'grad_conv_dw_b': 'grad_w', 'grad_conv_ln_g': 'grad_w', 'grad_conv_ln_b': 'grad_w', 'grad_conv_w_out': 'grad_w', 'grad_w_out': 'grad_w', 'grad_mix_norm_post': 'grad_w', 'grad_ffn2_norm_pre': 'grad_w', 'grad_ffn2_w_gate': 'grad_w', 'grad_ffn2_w_up': 'grad_w', 'grad_ffn2_w_down': 'grad_w', 'grad_ffn2_norm_post': 'grad_w', 'delta_ffn1_norm_pre': 'delta_w', 'delta_ffn1_w_gate': 'delta_w', 'delta_ffn1_w_up': 'delta_w', 'delta_ffn1_w_down': 'delta_w', 'delta_ffn1_norm_post': 'delta_w', 'delta_mix_norm_pre': 'delta_w', 'delta_w_in': 'delta_w', 'delta_gate_bias': 'delta_w', 'delta_rel_table': 'delta_w', 'delta_w_attn_out': 'delta_w', 'delta_conv_glu_bias': 'delta_w', 'delta_conv_dw_w': 'delta_w', 'delta_conv_dw_b': 'delta_w', 'delta_conv_ln_g': 'delta_w', 'delta_conv_ln_b': 'delta_w', 'delta_conv_w_out': 'delta_w', 'delta_w_out': 'delta_w', 'delta_mix_norm_post': 'delta_w', 'delta_ffn2_norm_pre': 'delta_w', 'delta_ffn2_w_gate': 'delta_w', 'delta_ffn2_w_up': 'delta_w', 'delta_ffn2_w_down': 'delta_w', 'delta_ffn2_norm_post': 'delta_w', 'new_m_ffn1_norm_pre': 'new_m', 'new_m_ffn1_w_gate': 'new_m', 'new_m_ffn1_w_up': 'new_m', 'new_m_ffn1_w_down': 'new_m', 'new_m_ffn1_norm_post': 'new_m', 'new_m_mix_norm_pre': 'new_m', 'new_m_w_in': 'new_m', 'new_m_gate_bias': 'new_m', 'new_m_rel_table': 'new_m', 'new_m_w_attn_out': 'new_m', 'new_m_conv_glu_bias': 'new_m', 'new_m_conv_dw_w': 'new_m', 'new_m_conv_dw_b': 'new_m', 'new_m_conv_ln_g': 'new_m', 'new_m_conv_ln_b': 'new_m', 'new_m_conv_w_out': 'new_m', 'new_m_w_out': 'new_m', 'new_m_mix_norm_post': 'new_m', 'new_m_ffn2_norm_pre': 'new_m', 'new_m_ffn2_w_gate': 'new_m', 'new_m_ffn2_w_up': 'new_m', 'new_m_ffn2_w_down': 'new_m', 'new_m_ffn2_norm_post': 'new_m', 'new_v_ffn1_norm_pre': 'new_v', 'new_v_ffn1_w_gate': 'new_v', 'new_v_ffn1_w_up': 'new_v', 'new_v_ffn1_w_down': 'new_v', 'new_v_ffn1_norm_post': 'new_v', 'new_v_mix_norm_pre': 'new_v', 'new_v_w_in': 'new_v', 'new_v_gate_bias': 'new_v', 'new_v_rel_table': 'new_v', 'new_v_w_attn_out': 'new_v', 'new_v_conv_glu_bias': 'new_v', 'new_v_conv_dw_w': 'new_v', 'new_v_conv_dw_b': 'new_v', 'new_v_conv_ln_g': 'new_v', 'new_v_conv_ln_b': 'new_v', 'new_v_conv_w_out': 'new_v', 'new_v_w_out': 'new_v', 'new_v_mix_norm_post': 'new_v', 'new_v_ffn2_norm_pre': 'new_v', 'new_v_ffn2_w_gate': 'new_v', 'new_v_ffn2_w_up': 'new_v', 'new_v_ffn2_w_down': 'new_v', 'new_v_ffn2_norm_post': 'new_v'}


def _forward(args):
    return _fwd_reference(*[args[k] for k in FWD_PARAMS])


def _output_shape():
    def fwd():
        inp = _fwd_setup_inputs(0)
        return _fwd_reference(*[inp[k] for k in FWD_PARAMS])
    out = _jax.eval_shape(fwd)
    return out.shape, out.dtype

N_MICROBATCH = 1
ADAM_LR = 0.001
ADAM_B1 = 0.9
ADAM_B2 = 0.999
ADAM_EPS = 1e-08
ADAM_WD = 0.01
ADAM_STEP = 10
PER_EXAMPLE_BATCH_AXIS = {'x': 0, 'loss_target': 0}
SHARED_INPUTS = []
_WEIGHT_DTYPES = {'ffn1_norm_pre': _jnp.float32, 'ffn1_w_gate': _jnp.float32, 'ffn1_w_up': _jnp.float32, 'ffn1_w_down': _jnp.float32, 'ffn1_norm_post': _jnp.float32, 'mix_norm_pre': _jnp.float32, 'w_in': _jnp.float32, 'gate_bias': _jnp.float32, 'rel_table': _jnp.float32, 'w_attn_out': _jnp.float32, 'conv_glu_bias': _jnp.float32, 'conv_dw_w': _jnp.float32, 'conv_dw_b': _jnp.float32, 'conv_ln_g': _jnp.float32, 'conv_ln_b': _jnp.float32, 'conv_w_out': _jnp.float32, 'w_out': _jnp.float32, 'mix_norm_post': _jnp.float32, 'ffn2_norm_pre': _jnp.float32, 'ffn2_w_gate': _jnp.float32, 'ffn2_w_up': _jnp.float32, 'ffn2_w_down': _jnp.float32, 'ffn2_norm_post': _jnp.float32}
MOMENT_SCALE = {'ffn1_norm_pre': 6.436750e-01, 'ffn1_w_gate': 2.335837e-01, 'ffn1_w_up': 2.571934e-01, 'ffn1_w_down': 4.285678e-01, 'ffn1_norm_post': 1.603755e+01, 'mix_norm_pre': 5.921454e-01, 'w_in': 2.788459e-01, 'gate_bias': 3.703914e-01, 'rel_table': 7.504259e-02, 'w_attn_out': 1.273413e-01, 'conv_glu_bias': 3.744316e+00, 'conv_dw_w': 8.060897e-01, 'conv_dw_b': 8.713945e+00, 'conv_ln_g': 3.342211e+00, 'conv_ln_b': 5.081684e+00, 'conv_w_out': 1.278745e+00, 'w_out': 1.297320e+00, 'mix_norm_post': 6.481896e+01, 'ffn2_norm_pre': 1.044916e+00, 'ffn2_w_gate': 2.887545e-01, 'ffn2_w_up': 5.336319e-01, 'ffn2_w_down': 9.009515e-01, 'ffn2_norm_post': 1.600070e+01}


def _to_microbatches(a, axis):
    t = _jnp.moveaxis(a, axis, 0)
    t = t.reshape((N_MICROBATCH, t.shape[0] // N_MICROBATCH) + t.shape[1:])
    return _jnp.moveaxis(t, 1, axis + 1)


def setup_inputs(seed: int = 0) -> dict:
    inp = _fwd_setup_inputs(seed)
    key = _jax.random.fold_in(_jax.random.key(seed), 7919)
    shape, _ = _output_shape()
    out = dict(inp)
    out["loss_target"] = _jax.random.normal(_jax.random.fold_in(key, 0), shape, _jnp.float32)
    for i, name in enumerate(TWIN_WEIGHTS):
        w = inp[name].astype(_jnp.float32)
        if MOMENT_SCALE is None:
            s = _jnp.sqrt(_jnp.mean(_jnp.square(w)) + 1e-30)
        else:
            s = MOMENT_SCALE[name]
        km, kv = _jax.random.split(_jax.random.fold_in(key, i + 1))
        out[name] = w
        out["m_" + name] = s * _jax.random.normal(km, w.shape, _jnp.float32)
        out["v_" + name] = (s * s) * _jax.random.uniform(kv, w.shape, _jnp.float32, 0.5, 1.5)
    if N_MICROBATCH > 1:
        for name, axis in PER_EXAMPLE_BATCH_AXIS.items():
            out[name] = _to_microbatches(out[name], axis)
    return {'x': out['x'], 'ffn1_norm_pre': out['ffn1_norm_pre'], 'ffn1_w_gate': out['ffn1_w_gate'], 'ffn1_w_up': out['ffn1_w_up'], 'ffn1_w_down': out['ffn1_w_down'], 'ffn1_norm_post': out['ffn1_norm_post'], 'mix_norm_pre': out['mix_norm_pre'], 'w_in': out['w_in'], 'gate_bias': out['gate_bias'], 'rel_table': out['rel_table'], 'w_attn_out': out['w_attn_out'], 'conv_glu_bias': out['conv_glu_bias'], 'conv_dw_w': out['conv_dw_w'], 'conv_dw_b': out['conv_dw_b'], 'conv_ln_g': out['conv_ln_g'], 'conv_ln_b': out['conv_ln_b'], 'conv_w_out': out['conv_w_out'], 'w_out': out['w_out'], 'mix_norm_post': out['mix_norm_post'], 'ffn2_norm_pre': out['ffn2_norm_pre'], 'ffn2_w_gate': out['ffn2_w_gate'], 'ffn2_w_up': out['ffn2_w_up'], 'ffn2_w_down': out['ffn2_w_down'], 'ffn2_norm_post': out['ffn2_norm_post'], 'loss_target': out['loss_target'], 'm_ffn1_norm_pre': out['m_ffn1_norm_pre'], 'm_ffn1_w_gate': out['m_ffn1_w_gate'], 'm_ffn1_w_up': out['m_ffn1_w_up'], 'm_ffn1_w_down': out['m_ffn1_w_down'], 'm_ffn1_norm_post': out['m_ffn1_norm_post'], 'm_mix_norm_pre': out['m_mix_norm_pre'], 'm_w_in': out['m_w_in'], 'm_gate_bias': out['m_gate_bias'], 'm_rel_table': out['m_rel_table'], 'm_w_attn_out': out['m_w_attn_out'], 'm_conv_glu_bias': out['m_conv_glu_bias'], 'm_conv_dw_w': out['m_conv_dw_w'], 'm_conv_dw_b': out['m_conv_dw_b'], 'm_conv_ln_g': out['m_conv_ln_g'], 'm_conv_ln_b': out['m_conv_ln_b'], 'm_conv_w_out': out['m_conv_w_out'], 'm_w_out': out['m_w_out'], 'm_mix_norm_post': out['m_mix_norm_post'], 'm_ffn2_norm_pre': out['m_ffn2_norm_pre'], 'm_ffn2_w_gate': out['m_ffn2_w_gate'], 'm_ffn2_w_up': out['m_ffn2_w_up'], 'm_ffn2_w_down': out['m_ffn2_w_down'], 'm_ffn2_norm_post': out['m_ffn2_norm_post'], 'v_ffn1_norm_pre': out['v_ffn1_norm_pre'], 'v_ffn1_w_gate': out['v_ffn1_w_gate'], 'v_ffn1_w_up': out['v_ffn1_w_up'], 'v_ffn1_w_down': out['v_ffn1_w_down'], 'v_ffn1_norm_post': out['v_ffn1_norm_post'], 'v_mix_norm_pre': out['v_mix_norm_pre'], 'v_w_in': out['v_w_in'], 'v_gate_bias': out['v_gate_bias'], 'v_rel_table': out['v_rel_table'], 'v_w_attn_out': out['v_w_attn_out'], 'v_conv_glu_bias': out['v_conv_glu_bias'], 'v_conv_dw_w': out['v_conv_dw_w'], 'v_conv_dw_b': out['v_conv_dw_b'], 'v_conv_ln_g': out['v_conv_ln_g'], 'v_conv_ln_b': out['v_conv_ln_b'], 'v_conv_w_out': out['v_conv_w_out'], 'v_w_out': out['v_w_out'], 'v_mix_norm_post': out['v_mix_norm_post'], 'v_ffn2_norm_pre': out['v_ffn2_norm_pre'], 'v_ffn2_w_gate': out['v_ffn2_w_gate'], 'v_ffn2_w_up': out['v_ffn2_w_up'], 'v_ffn2_w_down': out['v_ffn2_w_down'], 'v_ffn2_norm_post': out['v_ffn2_norm_post']}


def _loss(weights, diff, rest, loss_target):
    with _jax.named_scope("forward"):
        args = {**rest, TWIN_DIFF_INPUT: diff, **{k: w.astype(_WEIGHT_DTYPES[k]) for k, w in weights.items()}}
        y = _forward(args)
    with _jax.named_scope("loss_head"):
        err = _jnp.square(y.astype(_jnp.float32) - loss_target)
        return 0.5 * _jnp.sum(_jnp.mean(err, axis=-1)) if err.ndim else 0.5 * err


def _adamw(w, g, m, v):
    m = ADAM_B1 * m + (1.0 - ADAM_B1) * g
    v = ADAM_B2 * v + (1.0 - ADAM_B2) * _jnp.square(g)
    m_hat = m / (1.0 - ADAM_B1 ** ADAM_STEP)
    v_hat = v / (1.0 - ADAM_B2 ** ADAM_STEP)
    delta = -ADAM_LR * (m_hat / (_jnp.sqrt(v_hat) + ADAM_EPS) + ADAM_WD * w)
    return delta, m, v


def reference(x, ffn1_norm_pre, ffn1_w_gate, ffn1_w_up, ffn1_w_down, ffn1_norm_post, mix_norm_pre, w_in, gate_bias, rel_table, w_attn_out, conv_glu_bias, conv_dw_w, conv_dw_b, conv_ln_g, conv_ln_b, conv_w_out, w_out, mix_norm_post, ffn2_norm_pre, ffn2_w_gate, ffn2_w_up, ffn2_w_down, ffn2_norm_post, loss_target, m_ffn1_norm_pre, m_ffn1_w_gate, m_ffn1_w_up, m_ffn1_w_down, m_ffn1_norm_post, m_mix_norm_pre, m_w_in, m_gate_bias, m_rel_table, m_w_attn_out, m_conv_glu_bias, m_conv_dw_w, m_conv_dw_b, m_conv_ln_g, m_conv_ln_b, m_conv_w_out, m_w_out, m_mix_norm_post, m_ffn2_norm_pre, m_ffn2_w_gate, m_ffn2_w_up, m_ffn2_w_down, m_ffn2_norm_post, v_ffn1_norm_pre, v_ffn1_w_gate, v_ffn1_w_up, v_ffn1_w_down, v_ffn1_norm_post, v_mix_norm_pre, v_w_in, v_gate_bias, v_rel_table, v_w_attn_out, v_conv_glu_bias, v_conv_dw_w, v_conv_dw_b, v_conv_ln_g, v_conv_ln_b, v_conv_w_out, v_w_out, v_mix_norm_post, v_ffn2_norm_pre, v_ffn2_w_gate, v_ffn2_w_up, v_ffn2_w_down, v_ffn2_norm_post):
    given = dict(x=x, ffn1_norm_pre=ffn1_norm_pre, ffn1_w_gate=ffn1_w_gate, ffn1_w_up=ffn1_w_up, ffn1_w_down=ffn1_w_down, ffn1_norm_post=ffn1_norm_post, mix_norm_pre=mix_norm_pre, w_in=w_in, gate_bias=gate_bias, rel_table=rel_table, w_attn_out=w_attn_out, conv_glu_bias=conv_glu_bias, conv_dw_w=conv_dw_w, conv_dw_b=conv_dw_b, conv_ln_g=conv_ln_g, conv_ln_b=conv_ln_b, conv_w_out=conv_w_out, w_out=w_out, mix_norm_post=mix_norm_post, ffn2_norm_pre=ffn2_norm_pre, ffn2_w_gate=ffn2_w_gate, ffn2_w_up=ffn2_w_up, ffn2_w_down=ffn2_w_down, ffn2_norm_post=ffn2_norm_post, loss_target=loss_target, m_ffn1_norm_pre=m_ffn1_norm_pre, m_ffn1_w_gate=m_ffn1_w_gate, m_ffn1_w_up=m_ffn1_w_up, m_ffn1_w_down=m_ffn1_w_down, m_ffn1_norm_post=m_ffn1_norm_post, m_mix_norm_pre=m_mix_norm_pre, m_w_in=m_w_in, m_gate_bias=m_gate_bias, m_rel_table=m_rel_table, m_w_attn_out=m_w_attn_out, m_conv_glu_bias=m_conv_glu_bias, m_conv_dw_w=m_conv_dw_w, m_conv_dw_b=m_conv_dw_b, m_conv_ln_g=m_conv_ln_g, m_conv_ln_b=m_conv_ln_b, m_conv_w_out=m_conv_w_out, m_w_out=m_w_out, m_mix_norm_post=m_mix_norm_post, m_ffn2_norm_pre=m_ffn2_norm_pre, m_ffn2_w_gate=m_ffn2_w_gate, m_ffn2_w_up=m_ffn2_w_up, m_ffn2_w_down=m_ffn2_w_down, m_ffn2_norm_post=m_ffn2_norm_post, v_ffn1_norm_pre=v_ffn1_norm_pre, v_ffn1_w_gate=v_ffn1_w_gate, v_ffn1_w_up=v_ffn1_w_up, v_ffn1_w_down=v_ffn1_w_down, v_ffn1_norm_post=v_ffn1_norm_post, v_mix_norm_pre=v_mix_norm_pre, v_w_in=v_w_in, v_gate_bias=v_gate_bias, v_rel_table=v_rel_table, v_w_attn_out=v_w_attn_out, v_conv_glu_bias=v_conv_glu_bias, v_conv_dw_w=v_conv_dw_w, v_conv_dw_b=v_conv_dw_b, v_conv_ln_g=v_conv_ln_g, v_conv_ln_b=v_conv_ln_b, v_conv_w_out=v_conv_w_out, v_w_out=v_w_out, v_mix_norm_post=v_mix_norm_post, v_ffn2_norm_pre=v_ffn2_norm_pre, v_ffn2_w_gate=v_ffn2_w_gate, v_ffn2_w_up=v_ffn2_w_up, v_ffn2_w_down=v_ffn2_w_down, v_ffn2_norm_post=v_ffn2_norm_post)
    weights = {n: given[n] for n in TWIN_WEIGHTS}
    shared = {n: given[n] for n in SHARED_INPUTS}
    per_example = {n: given[n] for n in ['x']}
    grad_fn = _jax.value_and_grad(_loss, argnums=(0, 1))

    def one_microbatch(ex, loss_target):
        ex = dict(ex)
        diff = ex.pop(TWIN_DIFF_INPUT)
        return grad_fn(weights, diff, {**shared, **ex}, loss_target)

    if N_MICROBATCH == 1:
        loss, (grad_w, grad_x) = one_microbatch(per_example, given["loss_target"])
    else:
        def body(carry, xs):
            loss_sum, grad_sum = carry
            l_k, (gw_k, gx_k) = one_microbatch(xs[0], xs[1])
            with _jax.named_scope("update"):
                return (loss_sum + l_k, _jax.tree.map(_jnp.add, grad_sum, gw_k)), gx_k

        init = (_jnp.zeros((), _jnp.float32), _jax.tree.map(_jnp.zeros_like, weights))
        (loss, grad_w), grad_x = _jax.lax.scan(body, init, (per_example, given["loss_target"]))
    with _jax.named_scope("update"):
        delta_w, new_m, new_v = {}, {}, {}
        for n in TWIN_WEIGHTS:
            delta_w[n], new_m[n], new_v[n] = _adamw(weights[n], grad_w[n], given["m_" + n], given["v_" + n])
    return (loss, grad_x, *[grad_w[n] for n in TWIN_WEIGHTS], *[delta_w[n] for n in TWIN_WEIGHTS],
            *[new_m[n] for n in TWIN_WEIGHTS], *[new_v[n] for n in TWIN_WEIGHTS])
```

```python
import functools

import jax
import jax.numpy as jnp
from jax import lax
from jax.experimental import pallas as pl
from jax.experimental.pallas import tpu as pltpu

F32 = jnp.float32
BF16 = jnp.bfloat16

D = 1024
FF = 2816
DA = 512
DC = 512
NH = 8
CHUNK = 64
LEFT = 8
CW = 31
NREL = 257
EPS = 1e-6
NDEV = 8

QB = 2 * CHUNK
KW = (LEFT + 2) * CHUNK
KPAD = LEFT * CHUNK
RELW = 768
HALO = 32

TM = 512
VMEM_LIMIT = 56 * 1024 * 1024

ADAM_LR, ADAM_B1, ADAM_B2, ADAM_EPS, ADAM_WD, ADAM_STEP = 0.001, 0.9, 0.999, 1e-08, 0.01, 10

NT = (((1,), (1,)), ((), ()))
NN = (((1,), (0,)), ((), ()))
TN = (((0,), (0,)), ((), ()))

MESH = pl.DeviceIdType.MESH
ANY = pl.BlockSpec(memory_space=pl.ANY)


def _params(sem=None, vmem=VMEM_LIMIT):
    return pltpu.CompilerParams(dimension_semantics=sem, vmem_limit_bytes=vmem)


def _sigmoid(x):
    return 1.0 / (1.0 + jnp.exp(-x))


def _mm(name, grid, a_ops, b_ops, groups, dims, epi, outs, extras=(), carry=None):
    nk = grid[2]
    na, nb, ne, no, ng = len(a_ops), len(b_ops), len(extras), len(outs), len(groups)
    nc = 0 if carry is None else 1

    def body(*refs):
        a_refs = refs[:na]
        b_refs = refs[na:na + nb]
        e_refs = refs[na + nb:na + nb + ne]
        o_refs = refs[na + nb + ne + nc:na + nb + ne + nc + no]
        acc_refs = refs[na + nb + ne + nc + no:]
        k = pl.program_id(2)
        prods = []
        for grp in groups:
            p = None
            for ai, bi in grp:
                t = lax.dot_general(a_refs[ai][...], b_refs[bi][...], dims, preferred_element_type=F32)
                p = t if p is None else p + t
            prods.append(p)

        def finish(vals):
            res = epi(vals, [e[...] for e in e_refs])
            for o, r in zip(o_refs, res):
                o[...] = r.astype(o.dtype)

        if nk == 1:
            finish(prods)
        else:
            @pl.when(k == 0)
            def _():
                for acc, p in zip(acc_refs, prods):
                    acc[...] = p

            @pl.when(k > 0)
            def _():
                for acc, p in zip(acc_refs, prods):
                    acc[...] += p

            @pl.when(k == nk - 1)
            def _():
                finish([acc[...] for acc in acc_refs])

    in_specs = [pl.BlockSpec(blk, im) for _, blk, im in list(a_ops) + list(b_ops) + list(extras)]
    args = [arr for arr, _, _ in list(a_ops) + list(b_ops) + list(extras)]
    aliases = {}
    if carry is not None:
        in_specs.append(ANY)
        args.append(carry[0])
        aliases = {len(args) - 1: carry[1]}
    scratch = []
    if nk > 1:
        for _ in range(ng):
            blk = tuple(b for b in outs[0][2] if b is not None)
            scratch.append(pltpu.VMEM(blk, F32))
    res = pl.pallas_call(
        body,
        name=name,
        grid=grid,
        in_specs=in_specs,
        out_specs=[pl.BlockSpec(blk, im) for _, _, blk, im in outs],
        out_shape=[jax.ShapeDtypeStruct(shp, dt) for shp, dt, _, _ in outs],
        scratch_shapes=scratch,
        input_output_aliases=aliases,
        compiler_params=_params(("parallel", "parallel", "arbitrary")),
    )(*args)
    return res


def _first(accs, extras):
    return (accs[0],)


def _mm_simple(name, a, b, dims, out_dtype, tm, tn, b_row0=0, b_rows=None):
    m, kk = a.shape
    if dims is NT:
        n = b.shape[0] if b_rows is None else b_rows
        assert b.shape[1] == kk and b_row0 % tn == 0
        b_op = (b, (tn, kk), lambda i, j, q: (j + b_row0 // tn, 0))
    else:
        assert b.shape[0] == kk
        n = b.shape[1]
        b_op = (b, (kk, tn), lambda i, j, q: (0, j))
    a_op = (a, (tm, kk), lambda i, j, q: (i, 0))
    out = ((m, n), out_dtype, (tm, tn), lambda i, j, q: (i, j))
    return _mm(name, (m // tm, n // tn, 1), [a_op], [b_op], [[(0, 0)]], dims, _first, [out])[0]


def _rowwise(name, fn, tiled, params, outs, partials=(), tm=TM):
    t = tiled[0].shape[0]
    steps = t // tm
    nt, npar, no, npart = len(tiled), len(params), len(outs), len(partials)

    def body(*refs):
        t_refs = refs[:nt]
        p_refs = refs[nt:nt + npar]
        o_refs = refs[nt + npar:nt + npar + no]
        s_refs = refs[nt + npar + no:nt + npar + no + npart]
        acc_refs = refs[nt + npar + no + npart:]
        i = pl.program_id(0)
        res = fn(*[r[...] for r in t_refs], *[r[...] for r in p_refs])
        for o, r in zip(o_refs, res[:no]):
            o[...] = r.astype(o.dtype)

        @pl.when(i == 0)
        def _():
            for acc in acc_refs:
                acc[...] = jnp.zeros_like(acc)

        for acc, r in zip(acc_refs, res[no:]):
            acc[...] += r.reshape(tm // 8, 8, r.shape[-1]).sum(axis=0)

        @pl.when(i == steps - 1)
        def _():
            for s, acc in zip(s_refs, acc_refs):
                s[...] = acc[...].sum(axis=0, keepdims=True)

    in_specs = [pl.BlockSpec((tm, a.shape[1]), lambda i: (i, 0)) for a in tiled]
    in_specs += [pl.BlockSpec(p.shape, lambda i: (0, 0)) for p in params]
    out_specs = [pl.BlockSpec((tm, c), lambda i: (i, 0)) for c, _ in outs]
    out_specs += [pl.BlockSpec((1, c), lambda i: (0, 0)) for c in partials]
    out_shape = [jax.ShapeDtypeStruct((t, c), dt) for c, dt in outs]
    out_shape += [jax.ShapeDtypeStruct((1, c), F32) for c in partials]
    return pl.pallas_call(
        body,
        name=name,
        grid=(steps,),
        in_specs=in_specs,
        out_specs=out_specs,
        out_shape=out_shape,
        scratch_shapes=[pltpu.VMEM((8, c), F32) for c in partials],
        compiler_params=_params(("arbitrary",)),
    )(*tiled, *params)


def _rms(x):
    r = lax.rsqrt(jnp.mean(x * x, axis=-1, keepdims=True) + EPS)
    return x * r, r


def _rms_bwd(xhat, r, g, dy):
    dxh = dy * g
    dx = r * (dxh - xhat * jnp.mean(dxh * xhat, axis=-1, keepdims=True))
    return dx, dy * xhat


def _ffn_up(name, n, wa, part, tm=512, tf=1408):
    t = n.shape[0]

    def epi(accs, extras):
        a, b = accs
        return a, b, a * _sigmoid(a) * b

    a_op = (n, (tm, D), lambda f, i, q: (i, 0))
    b_ops = [(wa, (None, tf, D), lambda f, i, q: (part, f, 0)),
             (wa, (None, tf, D), lambda f, i, q: (part + 1, f, 0))]
    outs = [((t, FF), BF16, (tm, tf), lambda f, i, q: (i, f))] * 3
    return _mm(name, (FF // tf, t // tm, 1), [a_op], b_ops, [[(0, 0)], [(0, 1)]], NT, epi, outs)


def _ffn_down(name, s, wa, part, tm=512):
    t = s.shape[0]
    a_op = (s, (tm, FF), lambda i, j, q: (i, 0))
    b_op = (wa, (None, FF, D), lambda i, j, q: (part, 0, 0))
    out = ((t, D), F32, (tm, D), lambda i, j, q: (i, 0))
    return _mm(name, (t // tm, 1, 1), [a_op], [b_op], [[(0, 0)]], NN, _first, [out])[0]


def _ffn_bwd_act(name, df, wa, part, a, b, tm=512, tf=1408):
    t = df.shape[0]

    def epi(accs, extras):
        ds = accs[0]
        av, bv = extras[0].astype(F32), extras[1].astype(F32)
        sg = _sigmoid(av)
        da = ds * bv * (sg * (1.0 + av * (1.0 - sg)))
        db = ds * (av * sg)
        return (jnp.stack([da, db]),)

    a_op = (df, (tm, D), lambda f, i, q: (i, 0))
    b_op = (wa, (None, tf, D), lambda f, i, q: (part, f, 0))
    extras = [(a, (tm, tf), lambda f, i, q: (i, f)), (b, (tm, tf), lambda f, i, q: (i, f))]
    out = ((2, t, FF), BF16, (2, tm, tf), lambda f, i, q: (0, i, f))
    return _mm(name, (FF // tf, t // tm, 1), [a_op], [b_op], [[(0, 0)]], NT, epi, [out], extras)[0]


def _ffn_bwd_in(name, dab, wa, part, tm=1024):
    t = dab.shape[1]
    tm = min(tm, t)
    a_op = (dab, (None, tm, FF), lambda i, j, q: (q, i, 0))
    b_op = (wa, (None, FF, D), lambda i, j, q: (part + q, 0, 0))
    out = ((t, D), F32, (tm, D), lambda i, j, q: (i, 0))
    return _mm(name, (t // tm, 1, 2), [a_op], [b_op], [[(0, 0)]], NN, _first, [out])[0]


def _wgrad(name, dy_op, x, out_rows, out_block, out_map, gi, carry_buf, out_shape, tk=512):
    t, c = x.shape
    b_op = (x, (tk, c), lambda i, j, q: (q, 0))
    out = (out_shape, BF16, out_block, out_map)
    carry = None if carry_buf is None else (carry_buf, 0)
    return _mm(name, (gi, 1, t // tk), [dy_op], [b_op], [[(0, 0)]], TN, _first, [out], carry=carry)[0]


def _rel_onehot():
    j = lax.broadcasted_iota(jnp.int32, (384, RELW), 0)
    xx = lax.broadcasted_iota(jnp.int32, (384, RELW), 1)
    idx = jnp.clip(640 - xx, -128, 128) + 128
    return (j == idx).astype(F32)


def _band_ok():
    r = lax.broadcasted_iota(jnp.int32, (QB, KW), 0) >> 6
    c = lax.broadcasted_iota(jnp.int32, (QB, KW), 1) >> 6
    return (c >= r) & (c <= r + LEFT)


def _relbias_fwd(table):
    def body(t_ref, o_ref):
        rev = jnp.dot(t_ref[...], _rel_onehot(), precision=lax.Precision.HIGHEST, preferred_element_type=F32)
        for r in range(QB):
            row = pltpu.roll(rev, (RELW - (128 - r)) % RELW, 1)[:, :KW]
            rr = lax.broadcasted_iota(jnp.int32, (NH, KW), 1) >> 6
            ok = (rr >= (r // CHUNK)) & (rr <= (r // CHUNK) + LEFT)
            row = jnp.where(ok, row, -1e30)
            for h in range(NH):
                o_ref[h * QB + r:h * QB + r + 1, :] = row[h:h + 1, :]

    return pl.pallas_call(
        body,
        name="relbias_fwd",
        out_shape=jax.ShapeDtypeStruct((NH * QB, KW), F32),
        in_specs=[pl.BlockSpec(memory_space=pltpu.VMEM)],
        out_specs=pl.BlockSpec(memory_space=pltpu.VMEM),
        compiler_params=_params(),
    )(table)


def _relbias_bwd(dbias):
    def body(d_ref, o_ref, acc_ref):
        for h in range(NH):
            acc = jnp.zeros((1, RELW), F32)
            for r in range(QB):
                row = d_ref[h * QB + r:h * QB + r + 1, :]
                wide = jnp.concatenate([row, jnp.zeros((1, RELW - KW), F32)], axis=1)
                acc = acc + pltpu.roll(wide, 128 - r, 1)
            acc_ref[h:h + 1, :] = acc
        o_ref[...] = lax.dot_general(acc_ref[...], _rel_onehot(), NT, precision=lax.Precision.HIGHEST,
                                     preferred_element_type=F32)

    return pl.pallas_call(
        body,
        name="relbias_bwd",
        out_shape=jax.ShapeDtypeStruct((NH, 384), F32),
        in_specs=[pl.BlockSpec(memory_space=pltpu.VMEM)],
        out_specs=pl.BlockSpec(memory_space=pltpu.VMEM),
        scratch_shapes=[pltpu.VMEM((NH, RELW), F32)],
        compiler_params=_params(),
    )(dbias)


def _head_scores(q_pair, kw, hh, bias, key_ok):
    lane = lax.broadcasted_iota(jnp.int32, (1, 128), 1)
    mine = (lane < 64) if hh == 0 else (lane >= 64)
    qm = jnp.where(mine, q_pair, jnp.zeros_like(q_pair))
    s = lax.dot_general(qm, kw, NT, preferred_element_type=F32) * 0.125 + bias
    s = jnp.where(key_ok, s, -1e30)
    e = jnp.exp(s - jnp.max(s, axis=-1, keepdims=True))
    p = e / jnp.sum(e, axis=-1, keepdims=True)
    return qm, mine, p


def _attn_fwd(qkv, kvp, bias):
    t = qkv.shape[0]

    def body(q_ref, kv_ref, b_ref, o_ref):
        i = pl.program_id(0)
        start = pl.multiple_of(i * QB, QB)
        col = lax.broadcasted_iota(jnp.int32, (1, KW), 1)
        key_ok = col >= KPAD - i * QB
        for pair in range(4):
            lo = pair * 128
            kw = kv_ref[pl.ds(start, KW), lo:lo + 128]
            vw = kv_ref[pl.ds(start, KW), DA + lo:DA + lo + 128]
            q_pair = q_ref[:, lo:lo + 128]
            outs = []
            for hh in range(2):
                h = 2 * pair + hh
                _, mine, p = _head_scores(q_pair, kw, hh, b_ref[h * QB:(h + 1) * QB, :], key_ok)
                o = jnp.dot(p.astype(BF16), vw, preferred_element_type=F32)
                outs.append((mine, o))
            o_ref[:, lo:lo + 128] = jnp.where(outs[0][0], outs[0][1], outs[1][1]).astype(BF16)

    return pl.pallas_call(
        body,
        name="attn_fwd",
        grid=(t // QB,),
        in_specs=[pl.BlockSpec((QB, DA), lambda i: (i, 0)),
                  pl.BlockSpec(memory_space=pltpu.VMEM),
                  pl.BlockSpec(memory_space=pltpu.VMEM)],
        out_specs=pl.BlockSpec((QB, DA), lambda i: (i, 0)),
        out_shape=jax.ShapeDtypeStruct((t, DA), BF16),
        compiler_params=_params(("arbitrary",)),
    )(qkv, kvp, bias)


def _attn_bwd(qkv, kvp, bias, datt):
    t = qkv.shape[0]
    nb = t // QB
    flush = (KW - QB) // QB

    def body(q_ref, kv_ref, b_ref, do_ref, dq_ref, dkv_ref, db_ref, acc_ref):
        i = pl.program_id(0)

        @pl.when(i == 0)
        def _():
            acc_ref[...] = jnp.zeros_like(acc_ref)
            db_ref[...] = jnp.zeros_like(db_ref)

        @pl.when(i < nb)
        def _():
            start = pl.multiple_of(i * QB, QB)
            col = lax.broadcasted_iota(jnp.int32, (1, KW), 1)
            key_ok = col >= KPAD - i * QB
            for pair in range(4):
                lo = pair * 128
                kw = kv_ref[pl.ds(start, KW), lo:lo + 128]
                vw = kv_ref[pl.ds(start, KW), DA + lo:DA + lo + 128]
                q_pair = q_ref[:, lo:lo + 128]
                do_pair = do_ref[:, lo:lo + 128]
                dqs = []
                dk = None
                dv = None
                for hh in range(2):
                    h = 2 * pair + hh
                    qm, mine, p = _head_scores(q_pair, kw, hh, b_ref[h * QB:(h + 1) * QB, :], key_ok)
                    dom = jnp.where(mine, do_pair, jnp.zeros_like(do_pair))
                    dp = lax.dot_general(dom, vw, NT, preferred_element_type=F32)
                    ds = p * (dp - jnp.sum(p * dp, axis=-1, keepdims=True))
                    db_ref[h * QB:(h + 1) * QB, :] += ds
                    dsq = (ds * 0.125).astype(BF16)
                    dqs.append((mine, jnp.dot(dsq, kw, preferred_element_type=F32)))
                    dk_h = lax.dot_general(dsq, qm, TN, preferred_element_type=F32)
                    dv_h = lax.dot_general(p.astype(BF16), dom, TN, preferred_element_type=F32)
                    dk = dk_h if dk is None else dk + dk_h
                    dv = dv_h if dv is None else dv + dv_h
                dq_ref[:, lo:lo + 128] = jnp.where(dqs[0][0], dqs[0][1], dqs[1][1]).astype(BF16)
                acc_ref[:, lo:lo + 128] += dk
                acc_ref[:, DA + lo:DA + lo + 128] += dv

        dkv_ref[...] = acc_ref[0:QB, :].astype(BF16)
        rest = acc_ref[QB:KW, :]
        acc_ref[0:KW - QB, :] = rest
        acc_ref[KW - QB:KW, :] = jnp.zeros((QB, 2 * DA), F32)

    last = nb - 1
    return pl.pallas_call(
        body,
        name="attn_bwd",
        grid=(nb + flush,),
        in_specs=[pl.BlockSpec((QB, DA), lambda i: (jnp.minimum(i, last), 0)),
                  pl.BlockSpec(memory_space=pltpu.VMEM),
                  pl.BlockSpec(memory_space=pltpu.VMEM),
                  pl.BlockSpec((QB, DA), lambda i: (jnp.minimum(i, last), 0))],
        out_specs=[pl.BlockSpec((QB, DA), lambda i: (jnp.minimum(i, last), 0)),
                   pl.BlockSpec((QB, 2 * DA), lambda i: (i, 0)),
                   pl.BlockSpec((NH * QB, KW), lambda i: (0, 0))],
        out_shape=[jax.ShapeDtypeStruct((t, DA), BF16),
                   jax.ShapeDtypeStruct((t + KPAD, 2 * DA), BF16),
                   jax.ShapeDtypeStruct((NH * QB, KW), F32)],
        scratch_shapes=[pltpu.VMEM((KW, 2 * DA), F32)],
        compiler_params=_params(("arbitrary",)),
    )(qkv, kvp, bias, datt)


def _glu(c, gb):
    cb = c + gb
    return cb[:, :DC] * _sigmoid(cb[:, DC:])


def _ln_swish(pre, g, b):
    mu = jnp.mean(pre, axis=-1, keepdims=True)
    xc = pre - mu
    r = lax.rsqrt(jnp.mean(xc * xc, axis=-1, keepdims=True) + EPS)
    xhat = xc * r
    y = xhat * g + b
    return xhat, r, y


def _conv_fwd(cin, glu_b, dw_w, dw_b, ln_g, ln_b, tm=TM):
    t = cin.shape[0]
    hb = tm // HALO

    def body(c_ref, h_ref, gb_ref, w_ref, wb_ref, g_ref, b_ref, cs_ref, pre_ref, ext_ref):
        i = pl.program_id(0)
        halo = _glu(h_ref[...], gb_ref[...])
        ext_ref[0:HALO, :] = jnp.where(i > 0, halo, jnp.zeros_like(halo))
        ext_ref[HALO:HALO + tm, :] = _glu(c_ref[...], gb_ref[...])
        acc = jnp.zeros((tm, DC), F32) + wb_ref[...]
        for j in range(CW):
            acc = acc + w_ref[j:j + 1, :] * ext_ref[pl.ds(HALO - (CW - 1) + j, tm), :]
        pre_ref[...] = acc
        _, _, y = _ln_swish(acc, g_ref[...], b_ref[...])
        cs_ref[...] = (y * _sigmoid(y)).astype(BF16)

    vec = lambda n: pl.BlockSpec((1, n), lambda i: (0, 0))
    return pl.pallas_call(
        body,
        name="conv_fwd",
        grid=(t // tm,),
        in_specs=[pl.BlockSpec((tm, 2 * DC), lambda i: (i, 0)),
                  pl.BlockSpec((HALO, 2 * DC), lambda i: (jnp.maximum(i * hb - 1, 0), 0)),
                  vec(2 * DC), pl.BlockSpec((CW, DC), lambda i: (0, 0)), vec(DC), vec(DC), vec(DC)],
        out_specs=[pl.BlockSpec((tm, DC), lambda i: (i, 0)), pl.BlockSpec((tm, DC), lambda i: (i, 0))],
        out_shape=[jax.ShapeDtypeStruct((t, DC), BF16), jax.ShapeDtypeStruct((t, DC), F32)],
        scratch_shapes=[pltpu.VMEM((HALO + tm, DC), F32)],
        compiler_params=_params(("arbitrary",)),
    )(cin, cin, glu_b, dw_w, dw_b, ln_g, ln_b)


def _conv_bwd(dcs, pre, cin, glu_b, dw_w, ln_g, ln_b, tm=TM):
    t = cin.shape[0]
    hb = tm // HALO
    steps = t // tm
    nhb = t // HALO

    def dpre_of(dcs_v, pre_v, g, b):
        xhat, r, y = _ln_swish(pre_v, g, b)
        sg = _sigmoid(y)
        dy = dcs_v * (sg * (1.0 + y * (1.0 - sg)))
        dxh = dy * g
        dpre = r * (dxh - jnp.mean(dxh, axis=-1, keepdims=True)
                    - xhat * jnp.mean(dxh * xhat, axis=-1, keepdims=True))
        return dpre, dy * xhat, dy

    def body(dcs_ref, dcsn_ref, pre_ref, pren_ref, c_ref, ch_ref, gb_ref, w_ref, g_ref, b_ref,
             dc_ref, dgb_ref, dw_ref, dwb_ref, dg_ref, db_ref,
             gext_ref, dext_ref, a_gb, a_w, a_wb, a_g, a_b):
        i = pl.program_id(0)

        @pl.when(i == 0)
        def _():
            for a in (a_gb, a_w, a_wb, a_g, a_b):
                a[...] = jnp.zeros_like(a)

        g, b = g_ref[...], b_ref[...]
        dpre, dg_t, db_t = dpre_of(dcs_ref[...], pre_ref[...], g, b)
        dpre_n, _, _ = dpre_of(dcsn_ref[...], pren_ref[...], g, b)
        dext_ref[0:tm, :] = dpre
        dext_ref[tm:tm + HALO, :] = jnp.where(i < steps - 1, dpre_n, jnp.zeros_like(dpre_n))
        halo = _glu(ch_ref[...], gb_ref[...])
        gext_ref[0:HALO, :] = jnp.where(i > 0, halo, jnp.zeros_like(halo))
        cb = c_ref[...] + gb_ref[...]
        sg = _sigmoid(cb[:, DC:])
        gext_ref[HALO:HALO + tm, :] = cb[:, :DC] * sg

        fold = lambda v: v.reshape(tm // 8, 8, v.shape[-1]).sum(axis=0)
        dglu = jnp.zeros((tm, DC), F32)
        for j in range(CW):
            a_w[8 * j:8 * j + 8, :] += fold(dpre * gext_ref[pl.ds(HALO - (CW - 1) + j, tm), :])
            dglu = dglu + w_ref[j:j + 1, :] * dext_ref[pl.ds(CW - 1 - j, tm), :]
        da = dglu * sg
        dbv = dglu * cb[:, :DC] * sg * (1.0 - sg)
        dc = jnp.concatenate([da, dbv], axis=1)
        dc_ref[...] = dc.astype(BF16)
        a_gb[...] += fold(dc)
        a_wb[...] += fold(dpre)
        a_g[...] += fold(dg_t)
        a_b[...] += fold(db_t)

        @pl.when(i == steps - 1)
        def _():
            dgb_ref[...] = a_gb[...].sum(axis=0, keepdims=True)
            dw_ref[...] = a_w[...]
            dwb_ref[...] = a_wb[...].sum(axis=0, keepdims=True)
            dg_ref[...] = a_g[...].sum(axis=0, keepdims=True)
            db_ref[...] = a_b[...].sum(axis=0, keepdims=True)

    vec = lambda n: pl.BlockSpec((1, n), lambda i: (0, 0))
    nxt = lambda i: (jnp.minimum((i + 1) * hb, nhb - 1), 0)
    prv = lambda i: (jnp.maximum(i * hb - 1, 0), 0)
    return pl.pallas_call(
        body,
        name="conv_bwd",
        grid=(steps,),
        in_specs=[pl.BlockSpec((tm, DC), lambda i: (i, 0)), pl.BlockSpec((HALO, DC), nxt),
                  pl.BlockSpec((tm, DC), lambda i: (i, 0)), pl.BlockSpec((HALO, DC), nxt),
                  pl.BlockSpec((tm, 2 * DC), lambda i: (i, 0)), pl.BlockSpec((HALO, 2 * DC), prv),
                  vec(2 * DC), pl.BlockSpec((CW, DC), lambda i: (0, 0)), vec(DC), vec(DC)],
        out_specs=[pl.BlockSpec((tm, 2 * DC), lambda i: (i, 0)), vec(2 * DC),
                   pl.BlockSpec((CW * 8, DC), lambda i: (0, 0)), vec(DC), vec(DC), vec(DC)],
        out_shape=[jax.ShapeDtypeStruct((t, 2 * DC), BF16), jax.ShapeDtypeStruct((1, 2 * DC), F32),
                   jax.ShapeDtypeStruct((CW * 8, DC), F32), jax.ShapeDtypeStruct((1, DC), F32),
                   jax.ShapeDtypeStruct((1, DC), F32), jax.ShapeDtypeStruct((1, DC), F32)],
        scratch_shapes=[pltpu.VMEM((HALO + tm, DC), F32), pltpu.VMEM((tm + HALO, DC), F32),
                        pltpu.VMEM((8, 2 * DC), F32), pltpu.VMEM((CW * 8, DC), F32),
                        pltpu.VMEM((8, DC), F32), pltpu.VMEM((8, DC), F32), pltpu.VMEM((8, DC), F32)],
        compiler_params=_params(("arbitrary",)),
    )(dcs, dcs, pre, pre, cin, cin, glu_b, dw_w, ln_g, ln_b)


def _place():
    x, y, c = lax.axis_index("x"), lax.axis_index("y"), lax.axis_index("c")
    return x, y, c


def _peers(x, y, c):
    out = []
    for k in range(1, NDEV):
        fx, fy, fc = (k >> 2) & 1, (k >> 1) & 1, k & 1
        px = 1 - x if fx else x
        py = 1 - y if fy else y
        pc = 1 - c if fc else c
        out.append((px, py, pc))
    return out


def _all_gather(name, shards):
    n = len(shards)

    def body(*refs):
        ins = refs[:n]
        outs = refs[n:2 * n]
        send_sems, recv_sems, local_sems = refs[2 * n:]
        x, y, c = _place()
        me = 4 * x + 2 * y + c
        local = [pltpu.make_async_copy(ins[a], outs[a].at[:, me], local_sems.at[a]) for a in range(n)]
        for cp in local:
            cp.start()
        sends = []
        for k, (px, py, pc) in enumerate(_peers(x, y, c)):
            for a in range(n):
                cp = pltpu.make_async_remote_copy(
                    src_ref=ins[a], dst_ref=outs[a].at[:, me],
                    send_sem=send_sems.at[a, k], recv_sem=recv_sems.at[a, k],
                    device_id=(px, py, pc), device_id_type=MESH)
                cp.start()
                sends.append(cp)
        for k, (px, py, pc) in enumerate(_peers(x, y, c)):
            src = 4 * px + 2 * py + pc
            for a in range(n):
                pltpu.make_async_remote_copy(
                    src_ref=ins[a], dst_ref=outs[a].at[:, src],
                    send_sem=send_sems.at[a, k], recv_sem=recv_sems.at[a, k],
                    device_id=(px, py, pc), device_id_type=MESH).wait_recv()
        for cp in sends:
            cp.wait_send()
        for cp in local:
            cp.wait()

    return pl.pallas_call(
        body,
        name=name,
        in_specs=[ANY] * n,
        out_specs=[ANY] * n,
        out_shape=[jax.ShapeDtypeStruct((s.shape[0], NDEV) + s.shape[1:], s.dtype) for s in shards],
        scratch_shapes=[pltpu.SemaphoreType.DMA((n, NDEV - 1)), pltpu.SemaphoreType.DMA((n, NDEV - 1)),
                        pltpu.SemaphoreType.DMA((n,))],
        compiler_params=pltpu.CompilerParams(has_side_effects=True),
    )(*shards)


def _exchange(grads):
    n = len(grads)

    def body(*refs):
        ins = refs[:n]
        outs = refs[n:2 * n]
        send_sems, recv_sems, local_sems = refs[2 * n:]
        x, y, c = _place()
        me = 4 * x + 2 * y + c
        local = [pltpu.make_async_copy(ins[a].at[:, me], outs[a].at[me], local_sems.at[a]) for a in range(n)]
        for cp in local:
            cp.start()
        sends = []
        for k, (px, py, pc) in enumerate(_peers(x, y, c)):
            dst = 4 * px + 2 * py + pc
            for a in range(n):
                cp = pltpu.make_async_remote_copy(
                    src_ref=ins[a].at[:, dst], dst_ref=outs[a].at[me],
                    send_sem=send_sems.at[a, k], recv_sem=recv_sems.at[a, k],
                    device_id=(px, py, pc), device_id_type=MESH)
                cp.start()
                sends.append(cp)
        for k, (px, py, pc) in enumerate(_peers(x, y, c)):
            src = 4 * px + 2 * py + pc
            for a in range(n):
                pltpu.make_async_remote_copy(
                    src_ref=ins[a].at[:, me], dst_ref=outs[a].at[src],
                    send_sem=send_sems.at[a, k], recv_sem=recv_sems.at[a, k],
                    device_id=(px, py, pc), device_id_type=MESH).wait_recv()
        for cp in sends:
            cp.wait_send()
        for cp in local:
            cp.wait()

    return pl.pallas_call(
        body,
        name="grad_exchange",
        in_specs=[ANY] * n,
        out_specs=[ANY] * n,
        out_shape=[jax.ShapeDtypeStruct((NDEV, g.shape[0]) + g.shape[2:], g.dtype) for g in grads],
        scratch_shapes=[pltpu.SemaphoreType.DMA((n, NDEV - 1)), pltpu.SemaphoreType.DMA((n, NDEV - 1)),
                        pltpu.SemaphoreType.DMA((n,))],
        compiler_params=pltpu.CompilerParams(has_side_effects=True),
    )(*grads)


def _sum_devices(name, parts, tr):
    _, r, c = parts.shape

    def body(p_ref, o_ref):
        acc = p_ref[0].astype(F32)
        for d in range(1, NDEV):
            acc = acc + p_ref[d].astype(F32)
        o_ref[...] = acc

    return pl.pallas_call(
        body,
        name=name,
        grid=(r // tr,),
        in_specs=[pl.BlockSpec((NDEV, tr, c), lambda i: (0, i, 0))],
        out_specs=pl.BlockSpec((tr, c), lambda i: (i, 0)),
        out_shape=jax.ShapeDtypeStruct((r, c), F32),
        compiler_params=_params(("parallel",)),
    )(parts)


def _adamw(name, w, g, m, v, tr=None):
    r, c = w.shape
    tr = r if tr is None else tr

    def body(w_ref, g_ref, m_ref, v_ref, d_ref, nm_ref, nv_ref):
        gv = g_ref[...]
        nm = ADAM_B1 * m_ref[...] + (1.0 - ADAM_B1) * gv
        nv = ADAM_B2 * v_ref[...] + (1.0 - ADAM_B2) * (gv * gv)
        m_hat = nm / (1.0 - ADAM_B1 ** ADAM_STEP)
        v_hat = nv / (1.0 - ADAM_B2 ** ADAM_STEP)
        d_ref[...] = -ADAM_LR * (m_hat / (jnp.sqrt(v_hat) + ADAM_EPS) + ADAM_WD * w_ref[...])
        nm_ref[...] = nm
        nv_ref[...] = nv

    spec = pl.BlockSpec((tr, c), lambda i: (i, 0))
    return pl.pallas_call(
        body,
        name=name,
        grid=(r // tr,),
        in_specs=[spec] * 4,
        out_specs=[spec] * 3,
        out_shape=[jax.ShapeDtypeStruct((r, c), F32)] * 3,
        compiler_params=_params(("parallel",)),
    )(w, g, m, v)


SMALL = ["ffn1_norm_pre", "ffn1_norm_post", "mix_norm_pre", "gate_bias", "rel_table", "conv_glu_bias",
         "conv_dw_b", "conv_ln_g", "conv_ln_b", "mix_norm_post", "ffn2_norm_pre", "ffn2_norm_post"]
SMALL_ROWS = 24
DW_ROWS = 32


def _pack_small(vals):
    rows = []
    for name in SMALL:
        v = vals[name]
        if name == "rel_table":
            v = v.reshape(NH, -1)
            rows.append(jnp.pad(v, ((0, 0), (0, D - v.shape[1]))))
        else:
            v = v.reshape(-1)
            v = jnp.pad(v, (0, (-v.shape[0]) % D))
            rows.append(v.reshape(-1, D))
    out = jnp.concatenate(rows, axis=0)
    return jnp.pad(out, ((0, SMALL_ROWS - out.shape[0]), (0, 0)))


def _unpack_small(pack, shapes):
    out = {}
    r = 0
    for name in SMALL:
        shp = shapes[name]
        n = 1
        for s in shp:
            n *= s
        if name == "rel_table":
            out[name] = pack[r:r + NH, :NREL].reshape(shp)
            r += NH
        else:
            nr = -(-n // D)
            out[name] = pack[r:r + nr].reshape(-1)[:n].reshape(shp)
            r += nr
    return out


def kernel(x, ffn1_norm_pre, ffn1_w_gate, ffn1_w_up, ffn1_w_down, ffn1_norm_post, mix_norm_pre, w_in, gate_bias, rel_table, w_attn_out, conv_glu_bias, conv_dw_w, conv_dw_b, conv_ln_g, conv_ln_b, conv_w_out, w_out, mix_norm_post, ffn2_norm_pre, ffn2_w_gate, ffn2_w_up, ffn2_w_down, ffn2_norm_post, loss_target, m_ffn1_norm_pre, m_ffn1_w_gate, m_ffn1_w_up, m_ffn1_w_down, m_ffn1_norm_post, m_mix_norm_pre, m_w_in, m_gate_bias, m_rel_table, m_w_attn_out, m_conv_glu_bias, m_conv_dw_w, m_conv_dw_b, m_conv_ln_g, m_conv_ln_b, m_conv_w_out, m_w_out, m_mix_norm_post, m_ffn2_norm_pre, m_ffn2_w_gate, m_ffn2_w_up, m_ffn2_w_down, m_ffn2_norm_post, v_ffn1_norm_pre, v_ffn1_w_gate, v_ffn1_w_up, v_ffn1_w_down, v_ffn1_norm_post, v_mix_norm_pre, v_w_in, v_gate_bias, v_rel_table, v_w_attn_out, v_conv_glu_bias, v_conv_dw_w, v_conv_dw_b, v_conv_ln_g, v_conv_ln_b, v_conv_w_out, v_w_out, v_mix_norm_post, v_ffn2_norm_pre, v_ffn2_w_gate, v_ffn2_w_up, v_ffn2_w_down, v_ffn2_norm_post):
    return _step(dict(locals()))


WEIGHTS = ["ffn1_norm_pre", "ffn1_w_gate", "ffn1_w_up", "ffn1_w_down", "ffn1_norm_post", "mix_norm_pre", "w_in",
           "gate_bias", "rel_table", "w_attn_out", "conv_glu_bias", "conv_dw_w", "conv_dw_b", "conv_ln_g",
           "conv_ln_b", "conv_w_out", "w_out", "mix_norm_post", "ffn2_norm_pre", "ffn2_w_gate", "ffn2_w_up",
           "ffn2_w_down", "ffn2_norm_post"]
FS = FF // NDEV
PS = (3 * DA + 2 * DC + 2 * D) // NDEV
OS = D // NDEV


def _gather_weights(w):
    tr = lambda a: jnp.transpose(a[0]).astype(BF16)
    sh_a = jnp.stack([tr(w["ffn1_w_gate"]), tr(w["ffn1_w_up"]), w["ffn1_w_down"][0].astype(BF16),
                      tr(w["ffn2_w_gate"]), tr(w["ffn2_w_up"]), w["ffn2_w_down"][0].astype(BF16)])
    sh_b = tr(w["w_in"])[None]
    sh_c = w["w_out"][0].astype(BF16)[None]
    sh_d = jnp.stack([tr(w["w_attn_out"]), tr(w["conv_w_out"])])
    sh_e = jnp.pad(w["conv_dw_w"][0, :, 0, :], ((0, DW_ROWS - CW), (0, 0)))[None]
    wa, wb, wc, wd, we = _all_gather("gather_weights", [sh_a, sh_b, sh_c, sh_d, sh_e])
    dw_full = jnp.transpose(we[0], (1, 0, 2)).reshape(DW_ROWS, DC)[:CW]
    return (wa.reshape(6, FF, D), wb.reshape(NDEV * PS, D), wc.reshape(D, D), wd.reshape(2, D, DA), dw_full)


def _local_step(xs, target, wa, wb, wc, wd, dw_full, w, rel_table):
    t = xs.shape[0]
    vec = lambda n: w[n].reshape(1, -1)
    g_pre1, g_post1, g_mix, g_mixp = vec("ffn1_norm_pre"), vec("ffn1_norm_post"), vec("mix_norm_pre"), vec("mix_norm_post")
    g_pre2, g_post2 = vec("ffn2_norm_pre"), vec("ffn2_norm_post")
    gate_b, glu_b = vec("gate_bias"), vec("conv_glu_bias")
    dw_b, ln_g, ln_b = vec("conv_dw_b"), vec("conv_ln_g"), vec("conv_ln_b")

    (n1,) = _rowwise("pre1", lambda xv, g: ((_rms(xv)[0] * g),), [xs], [g_pre1], [(D, BF16)])
    a1, b1, s1 = _ffn_up("ffn1_up", n1, wa, 0)
    f1 = _ffn_down("ffn1_down", s1, wa, 2)

    def post1(xv, fv, gp, gm):
        h = xv + 0.5 * (_rms(fv)[0] * gp)
        return h, _rms(h)[0] * gm

    h1, u = _rowwise("post1", post1, [xs, f1], [g_post1, g_mix], [(D, F32), (D, BF16)])

    qkv = _mm_simple("proj_qkv", u, wb, NT, BF16, 512, 512, b_row0=0, b_rows=3 * DA)
    cin = _mm_simple("proj_conv", u, wb, NT, F32, 512, 512, b_row0=3 * DA, b_rows=2 * DC)
    gg = _mm_simple("proj_gate", u, wb, NT, F32, 512, 512, b_row0=3 * DA + 2 * DC, b_rows=2 * D)

    bias = _relbias_fwd(jnp.pad(rel_table[0], ((0, 0), (0, 384 - NREL))))
    kvp = jnp.pad(qkv[:, DA:], ((KPAD, 0), (0, 0)))
    att = _attn_fwd(qkv, kvp, bias)
    cs, pre = _conv_fwd(cin, glu_b, dw_full, dw_b, ln_g, ln_b)
    ya = _mm_simple("attn_out", att, wd[0], NT, F32, 512, 512)
    yb = _mm_simple("conv_out", cs, wd[1], NT, F32, 512, 512)

    def merge(yav, ybv, gv, gb):
        gates = _sigmoid(gv + gb)
        return (gates[:, :D] * yav + gates[:, D:] * ybv,)

    (merged,) = _rowwise("merge", merge, [ya, yb, gg], [gate_b], [(D, BF16)])
    mm_ = _mm_simple("mix_out", merged, wc, NN, F32, 512, 512)

    def postm(hv, mv, gp, g2):
        h = hv + _rms(mv)[0] * gp
        return h, _rms(h)[0] * g2

    h2, n2 = _rowwise("postm", postm, [h1, mm_], [g_mixp, g_pre2], [(D, F32), (D, BF16)])
    a2, b2, s2 = _ffn_up("ffn2_up", n2, wa, 3)
    f2 = _ffn_down("ffn2_down", s2, wa, 5)

    def post2(hv, fv, tv, gp):
        fh, r = _rms(fv)
        yv = hv + 0.5 * (fh * gp)
        err = yv - tv
        dy = err * (1.0 / D)
        df, dg = _rms_bwd(fh, r, gp, 0.5 * dy)
        return dy, df, (0.5 / D) * (err * err), dg

    dy, df2, loss_row, d_post2 = _rowwise("post2", post2, [h2, f2, target], [g_post2],
                                          [(D, F32), (D, BF16)], [D, D])

    dab2 = _ffn_bwd_act("ffn2_bwd_act", df2, wa, 5, a2, b2)
    dn2 = _ffn_bwd_in("ffn2_bwd_in", dab2, wa, 3)

    def bwd_pre2(hv, dnv, dyv, mv, g2, gp):
        hh, r = _rms(hv)
        dx, dg2 = _rms_bwd(hh, r, g2, dnv)
        dh = dyv + dx
        mh, rm = _rms(mv)
        dm, dgp = _rms_bwd(mh, rm, gp, dh)
        return dh, dm, dg2, dgp

    dh2, dm, d_pre2, d_mixp = _rowwise("bwd_pre2", bwd_pre2, [h2, dn2, dy, mm_], [g_pre2, g_mixp],
                                       [(D, F32), (D, BF16)], [D, D])
    dmerged = _mm_simple("mix_out_bwd", dm, wc, NT, F32, 512, 512)

    def merge_bwd(dmv, yav, ybv, gv, gb):
        gates = _sigmoid(gv + gb)
        ga, gbb = gates[:, :D], gates[:, D:]
        dgg = jnp.concatenate([dmv * yav * ga * (1.0 - ga), dmv * ybv * gbb * (1.0 - gbb)], axis=1)
        return dmv * ga, dmv * gbb, dgg, dgg

    dya, dyb, dgg, d_gate_b = _rowwise("merge_bwd", merge_bwd, [dmerged, ya, yb, gg], [gate_b],
                                       [(D, BF16), (D, BF16), (2 * D, BF16)], [2 * D])
    datt = _mm_simple("attn_out_bwd", dya, wd[0], NN, BF16, 512, 512)
    dcs = _mm_simple("conv_out_bwd", dyb, wd[1], NN, F32, 512, 512)
    dq, dkvp, dbias = _attn_bwd(qkv, kvp, bias, datt)
    d_rel = _relbias_bwd(dbias)
    dcin, d_glu_b, d_dw8, d_dw_b, d_ln_g, d_ln_b = _conv_bwd(dcs, pre, cin, glu_b, dw_full, ln_g, ln_b)

    tmu = 512
    a_ops = [(dq, (tmu, DA), lambda i, j, q: (i, 0)),
             (dkvp, (tmu, 2 * DA), lambda i, j, q: (i + KPAD // tmu, 0)),
             (dcin, (tmu, 2 * DC), lambda i, j, q: (i, 0)),
             (dgg, (tmu, 2 * D), lambda i, j, q: (i, 0))]
    whole = lambda i, j, q: (0, 0)
    b_ops = [(wb[:DA], (DA, D), whole), (wb[DA:3 * DA], (2 * DA, D), whole),
             (wb[3 * DA:3 * DA + 2 * DC], (2 * DC, D), whole), (wb[3 * DA + 2 * DC:], (2 * D, D), whole)]
    du = _mm("proj_bwd", (t // tmu, 1, 1), a_ops, b_ops, [[(0, 0), (1, 1), (2, 2), (3, 3)]], NN, _first,
             [((t, D), F32, (tmu, D), lambda i, j, q: (i, 0))])[0]

    def bwd_mix(hv, duv, dhv, fv, gm, gp):
        hh, r = _rms(hv)
        dx, dgm = _rms_bwd(hh, r, gm, duv)
        dh = dhv + dx
        fh, rf = _rms(fv)
        df, dgp = _rms_bwd(fh, rf, gp, 0.5 * dh)
        return dh, df, dgm, dgp

    dh1, df1, d_mix, d_post1 = _rowwise("bwd_mix", bwd_mix, [h1, du, dh2, f1], [g_mix, g_post1],
                                        [(D, F32), (D, BF16)], [D, D])
    dab1 = _ffn_bwd_act("ffn1_bwd_act", df1, wa, 2, a1, b1)
    dn1 = _ffn_bwd_in("ffn1_bwd_in", dab1, wa, 0)

    def bwd_pre1(xv, dnv, dhv, g1):
        xh, r = _rms(xv)
        dx, dg1 = _rms_bwd(xh, r, g1, dnv)
        return dhv + dx, dg1

    dx, d_pre1 = _rowwise("bwd_pre1", bwd_pre1, [xs, dn1, dh1], [g_pre1], [(D, F32)], [D])

    tmw = 1408
    nfi = FF // tmw
    ga_shape = (6, FF, D)

    def ffn_wgrads(tag, dab, nrm, s, df, part, carry_buf):
        g = _wgrad(tag + "_wgrad_gu", (dab, (None, 512, tmw), lambda i, j, q: (i // nfi, q, i % nfi)), nrm, FF,
                   (None, tmw, D), lambda i, j, q: (part + i // nfi, i % nfi, 0), 2 * nfi, carry_buf, ga_shape)
        return _wgrad(tag + "_wgrad_d", (s, (512, tmw), lambda i, j, q: (q, i)), df, FF,
                      (None, tmw, D), lambda i, j, q: (part + 2, i, 0), nfi, g, ga_shape)

    g_a = ffn_wgrads("ffn1", dab1, n1, s1, df1, 0, None)
    g_a = ffn_wgrads("ffn2", dab2, n2, s2, df2, 3, g_a)

    gb_shape = (3 * DA + 2 * DC + 2 * D, D)
    g_b = _wgrad("proj_wgrad_q", (dq, (512, DA), lambda i, j, q: (q, 0)), u, DA,
                 (DA, D), lambda i, j, q: (0, 0), 1, None, gb_shape)
    g_b = _wgrad("proj_wgrad_kv", (dkvp, (512, DA), lambda i, j, q: (q + KPAD // 512, i)), u, 2 * DA,
                 (DA, D), lambda i, j, q: (1 + i, 0), 2, g_b, gb_shape)
    g_b = _wgrad("proj_wgrad_c", (dcin, (512, DA), lambda i, j, q: (q, i)), u, 2 * DC,
                 (DA, D), lambda i, j, q: (3 + i, 0), 2, g_b, gb_shape)
    g_b = _wgrad("proj_wgrad_g", (dgg, (512, DA), lambda i, j, q: (q, i)), u, 2 * D,
                 (DA, D), lambda i, j, q: (5 + i, 0), 4, g_b, gb_shape)
    g_c = _wgrad("mix_out_wgrad", (merged, (512, D), lambda i, j, q: (q, 0)), dm, D,
                 (D, D), lambda i, j, q: (0, 0), 1, None, (D, D))
    gd_shape = (2, D, DA)
    g_d = _wgrad("attn_out_wgrad", (dya, (512, D), lambda i, j, q: (q, 0)), att, D,
                 (None, D, DA), lambda i, j, q: (0, 0, 0), 1, None, gd_shape)
    g_d = _wgrad("conv_out_wgrad", (dyb, (512, D), lambda i, j, q: (q, 0)), cs, D,
                 (None, D, DA), lambda i, j, q: (1, 0, 0), 1, g_d, gd_shape)

    small_g = {"ffn1_norm_pre": d_pre1, "ffn1_norm_post": d_post1, "mix_norm_pre": d_mix, "gate_bias": d_gate_b,
               "rel_table": d_rel[:, :NREL], "conv_glu_bias": d_glu_b, "conv_dw_b": d_dw_b, "conv_ln_g": d_ln_g,
               "conv_ln_b": d_ln_b, "mix_norm_post": d_mixp, "ffn2_norm_pre": d_pre2, "ffn2_norm_post": d_post2}
    return loss_row, dx, g_a, g_b, g_c, g_d, small_g, d_dw8


def _step(args):
    names = WEIGHTS
    w = {n: args[n] for n in names}
    fs, ps, os_ = FS, PS, OS
    conv_dw_w = args["conv_dw_w"]
    wa, wb, wc, wd, dw_full = _gather_weights(w)
    loss_row, dx, g_a, g_b, g_c, g_d, small_g, d_dw8 = _local_step(
        args["x"][0], args["loss_target"][0], wa, wb, wc, wd, dw_full, w, args["rel_table"])

    g_small = _pack_small(small_g)
    g_dw = jnp.pad(d_dw8, ((0, 8 * (DW_ROWS - CW)), (0, 0)))
    g_dw = g_dw.reshape(DW_ROWS * 8, NDEV, DC // NDEV).transpose(1, 0, 2)

    x_a, x_b, x_c, x_d, x_dw = _exchange([
        g_a.reshape(6, NDEV, fs, D), g_b.reshape(1, NDEV, ps, D), g_c.reshape(1, NDEV, os_, D),
        g_d.reshape(2, NDEV, os_, DA), g_dw[None]])
    (x_s,) = _all_gather("gather_small_grads", [g_small[None]])

    s_a = _sum_devices("sum_ffn", x_a.reshape(NDEV, 6 * fs, D), 6 * fs // 4).reshape(6, fs, D)
    s_b = _sum_devices("sum_proj", x_b.reshape(NDEV, ps, D), ps)
    s_c = _sum_devices("sum_mix", x_c.reshape(NDEV, os_, D), os_)
    s_d = _sum_devices("sum_out", x_d.reshape(NDEV, 2 * os_, DA), 2 * os_).reshape(2, os_, DA)
    s_dw = _sum_devices("sum_dw", x_dw.reshape(NDEV, DW_ROWS * 8, DC // NDEV), DW_ROWS * 8)
    s_s = _sum_devices("sum_small", x_s.reshape(NDEV, SMALL_ROWS, D), SMALL_ROWS)

    grads = {
        "ffn1_w_gate": jnp.transpose(s_a[0])[None], "ffn1_w_up": jnp.transpose(s_a[1])[None], "ffn1_w_down": s_a[2][None],
        "ffn2_w_gate": jnp.transpose(s_a[3])[None], "ffn2_w_up": jnp.transpose(s_a[4])[None], "ffn2_w_down": s_a[5][None],
        "w_in": jnp.transpose(s_b)[None], "w_out": s_c[None],
        "w_attn_out": jnp.transpose(s_d[0])[None], "conv_w_out": jnp.transpose(s_d[1])[None],
    }
    shapes = {n: w[n].shape for n in SMALL}
    grads.update(_unpack_small(s_s, shapes))

    deltas, new_m, new_v = {}, {}, {}
    big = ["ffn1_w_gate", "ffn1_w_up", "ffn1_w_down", "w_in", "w_attn_out", "conv_w_out", "w_out",
           "ffn2_w_gate", "ffn2_w_up", "ffn2_w_down"]
    for n in big:
        shp = w[n].shape
        two = lambda a: a.reshape(shp[1], shp[2])
        rows = shp[1]
        tr_ = rows // 2 if rows % 16 == 0 else rows
        d_, m_, v_ = _adamw("adamw_" + n, two(w[n]), two(grads[n]), two(args["m_" + n]), two(args["v_" + n]), tr_)
        deltas[n], new_m[n], new_v[n] = d_.reshape(shp), m_.reshape(shp), v_.reshape(shp)

    wp = _pack_small({n: w[n] for n in SMALL})
    mp = _pack_small({n: args["m_" + n] for n in SMALL})
    vp = _pack_small({n: args["v_" + n] for n in SMALL})
    d_, m_, v_ = _adamw("adamw_small", wp, s_s, mp, vp)
    for dst, pack in ((deltas, d_), (new_m, m_), (new_v, v_)):
        dst.update(_unpack_small(pack, shapes))

    g_dw_own = _fold8("fold_dw", s_dw)[:CW]
    grads["conv_dw_w"] = g_dw_own.reshape(1, CW, 1, DC // NDEV)
    flat = lambda a: a.reshape(CW, DC // NDEV)
    d_, m_, v_ = _adamw("adamw_dw", flat(conv_dw_w), g_dw_own, flat(args["m_conv_dw_w"]), flat(args["v_conv_dw_w"]))
    shp = conv_dw_w.shape
    deltas["conv_dw_w"], new_m["conv_dw_w"], new_v["conv_dw_w"] = d_.reshape(shp), m_.reshape(shp), v_.reshape(shp)

    loss = lax.psum(jnp.sum(loss_row), ("x", "y", "c"))
    return (loss, dx[None], *[grads[n] for n in names], *[deltas[n] for n in names],
            *[new_m[n] for n in names], *[new_v[n] for n in names])


def _fold8(name, a):
    r8, c = a.shape

    def body(a_ref, o_ref):
        o_ref[...] = a_ref[...].reshape(r8 // 8, 8, c).sum(axis=1)

    return pl.pallas_call(
        body,
        name=name,
        out_shape=jax.ShapeDtypeStruct((r8 // 8, c), F32),
        in_specs=[pl.BlockSpec(memory_space=pltpu.VMEM)],
        out_specs=pl.BlockSpec(memory_space=pltpu.VMEM),
        compiler_params=_params(),
    )(a)
```

```python
import functools

import jax
import jax.numpy as jnp
from jax import lax
from jax.experimental import pallas as pl
from jax.experimental.pallas import tpu as pltpu

F32 = jnp.float32
BF16 = jnp.bfloat16

D = 1024
FF = 2816
DA = 512
DC = 512
NH = 8
CHUNK = 64
LEFT = 8
CW = 31
NREL = 257
EPS = 1e-6
NDEV = 8

QB = 2 * CHUNK
KW = (LEFT + 2) * CHUNK
KPAD = LEFT * CHUNK
RELW = 768
HALO = 32

TM = 512
VMEM_LIMIT = 56 * 1024 * 1024

ADAM_LR, ADAM_B1, ADAM_B2, ADAM_EPS, ADAM_WD, ADAM_STEP = 0.001, 0.9, 0.999, 1e-08, 0.01, 10

NT = (((1,), (1,)), ((), ()))
NN = (((1,), (0,)), ((), ()))
TN = (((0,), (0,)), ((), ()))

MESH = pl.DeviceIdType.MESH
ANY = pl.BlockSpec(memory_space=pl.ANY)


def _params(sem=None, vmem=VMEM_LIMIT):
    return pltpu.CompilerParams(dimension_semantics=sem, vmem_limit_bytes=vmem)


def _sigmoid(x):
    return 1.0 / (1.0 + jnp.exp(-x))


def _mm(name, grid, a_ops, b_ops, groups, dims, epi, outs, extras=(), carry=None, job=None):
    nk = grid[2]
    na, nb, ne, no, ng = len(a_ops), len(b_ops), len(extras), len(outs), len(groups)
    nc = 0 if carry is None else 1

    def body(*refs):
        a_refs = refs[:na]
        b_refs = refs[na:na + nb]
        e_refs = refs[na + nb:na + nb + ne]
        o_refs = refs[na + nb + ne + nc:na + nb + ne + nc + no]
        acc_refs = refs[na + nb + ne + nc + no:]
        k = pl.program_id(2)
        prods = []
        for grp in groups:
            p = None
            for ai, bi in grp:
                t = lax.dot_general(a_refs[ai][...], b_refs[bi][...], dims, preferred_element_type=F32)
                p = t if p is None else p + t
            prods.append(p)

        def finish(vals):
            res = epi(vals, [e[...] for e in e_refs])
            for o, r in zip(o_refs, res):
                o[...] = r.astype(o.dtype)

        if nk == 1:
            finish(prods)
        else:
            @pl.when(k == 0)
            def _():
                for acc, p in zip(acc_refs, prods):
                    acc[...] = p

            @pl.when(k > 0)
            def _():
                for acc, p in zip(acc_refs, prods):
                    acc[...] += p

            @pl.when(k == nk - 1)
            def _():
                finish([acc[...] for acc in acc_refs])

    in_specs = [pl.BlockSpec(blk, im) for _, blk, im in list(a_ops) + list(b_ops) + list(extras)]
    args = [arr for arr, _, _ in list(a_ops) + list(b_ops) + list(extras)]
    aliases = {}
    if carry is not None:
        in_specs.append(ANY)
        args.append(carry[0])
        aliases = {len(args) - 1: carry[1]}
    scratch = []
    if nk > 1:
        for _ in range(ng):
            blk = tuple(b for b in outs[0][2] if b is not None)
            scratch.append(pltpu.VMEM(blk, F32))
    res, jres = _call(
        body, name=name, grid=grid, in_specs=in_specs, args=args,
        out_specs=[pl.BlockSpec(blk, im) for _, _, blk, im in outs],
        out_shape=[jax.ShapeDtypeStruct(shp, dt) for shp, dt, _, _ in outs],
        scratch=scratch, sem=("parallel", "parallel", "arbitrary"), aliases=aliases, job=job)
    return res if job is None else (res, jres)


def _first(accs, extras):
    return (accs[0],)


def _mm_simple(name, a, b, dims, out_dtype, tm, tn, b_row0=0, b_rows=None):
    m, kk = a.shape
    if dims is NT:
        n = b.shape[0] if b_rows is None else b_rows
        assert b.shape[1] == kk and b_row0 % tn == 0
        b_op = (b, (tn, kk), lambda i, j, q: (j + b_row0 // tn, 0))
    else:
        assert b.shape[0] == kk
        n = b.shape[1]
        b_op = (b, (kk, tn), lambda i, j, q: (0, j))
    a_op = (a, (tm, kk), lambda i, j, q: (i, 0))
    out = ((m, n), out_dtype, (tm, tn), lambda i, j, q: (i, j))
    return _mm(name, (m // tm, n // tn, 1), [a_op], [b_op], [[(0, 0)]], dims, _first, [out])[0]


def _rowwise(name, fn, tiled, params, outs, partials=(), tm=TM, job=None):
    t = tiled[0].shape[0]
    steps = t // tm
    nt, npar, no, npart = len(tiled), len(params), len(outs), len(partials)

    def body(*refs):
        t_refs = refs[:nt]
        p_refs = refs[nt:nt + npar]
        o_refs = refs[nt + npar:nt + npar + no]
        s_refs = refs[nt + npar + no:nt + npar + no + npart]
        acc_refs = refs[nt + npar + no + npart:]
        i = pl.program_id(0)
        res = fn(*[r[...] for r in t_refs], *[r[...] for r in p_refs])
        for o, r in zip(o_refs, res[:no]):
            o[...] = r.astype(o.dtype)

        @pl.when(i == 0)
        def _():
            for acc in acc_refs:
                acc[...] = jnp.zeros_like(acc)

        for acc, r in zip(acc_refs, res[no:]):
            acc[...] += r.reshape(tm // 8, 8, r.shape[-1]).sum(axis=0)

        @pl.when(i == steps - 1)
        def _():
            for s, acc in zip(s_refs, acc_refs):
                s[...] = acc[...].sum(axis=0, keepdims=True)

    in_specs = [pl.BlockSpec((tm, a.shape[1]), lambda i: (i, 0)) for a in tiled]
    in_specs += [pl.BlockSpec(p.shape, lambda i: (0, 0)) for p in params]
    out_specs = [pl.BlockSpec((tm, c), lambda i: (i, 0)) for c, _ in outs]
    out_specs += [pl.BlockSpec((1, c), lambda i: (0, 0)) for c in partials]
    out_shape = [jax.ShapeDtypeStruct((t, c), dt) for c, dt in outs]
    out_shape += [jax.ShapeDtypeStruct((1, c), F32) for c in partials]
    res, jres = _call(body, name=name, grid=(steps,), in_specs=in_specs, args=[*tiled, *params], out_specs=out_specs,
                      out_shape=out_shape, scratch=[pltpu.VMEM((8, c), F32) for c in partials], sem=("arbitrary",),
                      job=job)
    return res if job is None else (res, jres)


def _rms(x):
    r = lax.rsqrt(jnp.mean(x * x, axis=-1, keepdims=True) + EPS)
    return x * r, r


def _rms_bwd(xhat, r, g, dy):
    dxh = dy * g
    dx = r * (dxh - xhat * jnp.mean(dxh * xhat, axis=-1, keepdims=True))
    return dx, dy * xhat


def _ffn_up(name, n, wa, part, tm=512, tf=1408, job=None):
    t = n.shape[0]

    def epi(accs, extras):
        a, b = accs
        return a, b, a * _sigmoid(a) * b

    a_op = (n, (tm, D), lambda f, i, q: (i, 0))
    b_ops = [(wa, (None, tf, D), lambda f, i, q: (part, f, 0)),
             (wa, (None, tf, D), lambda f, i, q: (part + 1, f, 0))]
    outs = [((t, FF), BF16, (tm, tf), lambda f, i, q: (i, f))] * 3
    return _mm(name, (FF // tf, t // tm, 1), [a_op], b_ops, [[(0, 0)], [(0, 1)]], NT, epi, outs, job=job)


def _ffn_down(name, s, wa, part, tm=512):
    t = s.shape[0]
    a_op = (s, (tm, FF), lambda i, j, q: (i, 0))
    b_op = (wa, (None, FF, D), lambda i, j, q: (part, 0, 0))
    out = ((t, D), F32, (tm, D), lambda i, j, q: (i, 0))
    return _mm(name, (t // tm, 1, 1), [a_op], [b_op], [[(0, 0)]], NN, _first, [out])[0]


def _ffn_bwd_act(name, df, wa, part, a, b, tm=512, tf=1408, job=None):
    t = df.shape[0]

    def epi(accs, extras):
        ds = accs[0]
        av, bv = extras[0].astype(F32), extras[1].astype(F32)
        sg = _sigmoid(av)
        da = ds * bv * (sg * (1.0 + av * (1.0 - sg)))
        db = ds * (av * sg)
        return (jnp.stack([da, db]),)

    a_op = (df, (tm, D), lambda f, i, q: (i, 0))
    b_op = (wa, (None, tf, D), lambda f, i, q: (part, f, 0))
    extras = [(a, (tm, tf), lambda f, i, q: (i, f)), (b, (tm, tf), lambda f, i, q: (i, f))]
    out = ((2, t, FF), BF16, (2, tm, tf), lambda f, i, q: (0, i, f))
    res = _mm(name, (FF // tf, t // tm, 1), [a_op], [b_op], [[(0, 0)]], NT, epi, [out], extras, job=job)
    return res[0] if job is None else (res[0][0], res[1])


def _ffn_bwd_in(name, dab, wa, part, tm=1024, job=None):
    t = dab.shape[1]
    tm = min(tm, t)
    a_op = (dab, (None, tm, FF), lambda i, j, q: (q, i, 0))
    b_op = (wa, (None, FF, D), lambda i, j, q: (part + q, 0, 0))
    out = ((t, D), F32, (tm, D), lambda i, j, q: (i, 0))
    res = _mm(name, (t // tm, 1, 2), [a_op], [b_op], [[(0, 0)]], NN, _first, [out], job=job)
    return res[0] if job is None else (res[0][0], res[1])


def _wgrad(name, dy_op, x, out_block, out_map, gi, carry_buf, out_shape, tk=512, job=None):
    t, c = x.shape
    b_op = (x, (tk, c), lambda i, j, q: (q, 0))
    out = (out_shape, BF16, out_block, out_map)
    carry = None if carry_buf is None else (carry_buf, 0)
    res = _mm(name, (gi, 1, t // tk), [dy_op], [b_op], [[(0, 0)]], TN, _first, [out], carry=carry, job=job)
    return res[0] if job is None else (res[0][0], res[1])


def _rel_onehot():
    j = lax.broadcasted_iota(jnp.int32, (384, RELW), 0)
    xx = lax.broadcasted_iota(jnp.int32, (384, RELW), 1)
    idx = jnp.clip(640 - xx, -128, 128) + 128
    return (j == idx).astype(F32)


def _band_ok():
    r = lax.broadcasted_iota(jnp.int32, (QB, KW), 0) >> 6
    c = lax.broadcasted_iota(jnp.int32, (QB, KW), 1) >> 6
    return (c >= r) & (c <= r + LEFT)


def _relbias_fwd(table):
    def body(t_ref, o_ref):
        rev = jnp.dot(t_ref[...], _rel_onehot(), precision=lax.Precision.HIGHEST, preferred_element_type=F32)
        for r in range(QB):
            row = pltpu.roll(rev, (RELW - (128 - r)) % RELW, 1)[:, :KW]
            rr = lax.broadcasted_iota(jnp.int32, (NH, KW), 1) >> 6
            ok = (rr >= (r // CHUNK)) & (rr <= (r // CHUNK) + LEFT)
            row = jnp.where(ok, row, -1e30)
            for h in range(NH):
                o_ref[h * QB + r:h * QB + r + 1, :] = row[h:h + 1, :]

    return pl.pallas_call(
        body,
        name="relbias_fwd",
        out_shape=jax.ShapeDtypeStruct((NH * QB, KW), F32),
        in_specs=[pl.BlockSpec(memory_space=pltpu.VMEM)],
        out_specs=pl.BlockSpec(memory_space=pltpu.VMEM),
        compiler_params=_params(),
    )(table)


def _relbias_bwd(dbias):
    def body(d_ref, o_ref, acc_ref):
        for h in range(NH):
            acc = jnp.zeros((1, RELW), F32)
            for r in range(QB):
                row = d_ref[h * QB + r:h * QB + r + 1, :]
                wide = jnp.concatenate([row, jnp.zeros((1, RELW - KW), F32)], axis=1)
                acc = acc + pltpu.roll(wide, 128 - r, 1)
            acc_ref[h:h + 1, :] = acc
        o_ref[...] = lax.dot_general(acc_ref[...], _rel_onehot(), NT, precision=lax.Precision.HIGHEST,
                                     preferred_element_type=F32)

    return pl.pallas_call(
        body,
        name="relbias_bwd",
        out_shape=jax.ShapeDtypeStruct((NH, 384), F32),
        in_specs=[pl.BlockSpec(memory_space=pltpu.VMEM)],
        out_specs=pl.BlockSpec(memory_space=pltpu.VMEM),
        scratch_shapes=[pltpu.VMEM((NH, RELW), F32)],
        compiler_params=_params(),
    )(dbias)


def _head_scores(q_pair, kw, hh, bias, key_ok):
    lane = lax.broadcasted_iota(jnp.int32, (1, 128), 1)
    mine = (lane < 64) if hh == 0 else (lane >= 64)
    qm = jnp.where(mine, q_pair, jnp.zeros_like(q_pair))
    s = lax.dot_general(qm, kw, NT, preferred_element_type=F32) * 0.125 + bias
    s = jnp.where(key_ok, s, -1e30)
    e = jnp.exp(s - jnp.max(s, axis=-1, keepdims=True))
    p = e / jnp.sum(e, axis=-1, keepdims=True)
    return qm, mine, p


def _attn_fwd(qkv, kvp, bias, job=None):
    t = qkv.shape[0]

    def body(q_ref, kv_ref, b_ref, o_ref):
        i = pl.program_id(0)
        start = pl.multiple_of(i * QB, QB)
        col = lax.broadcasted_iota(jnp.int32, (1, KW), 1)
        key_ok = col >= KPAD - i * QB
        for pair in range(4):
            lo = pair * 128
            kw = kv_ref[pl.ds(start, KW), lo:lo + 128]
            vw = kv_ref[pl.ds(start, KW), DA + lo:DA + lo + 128]
            q_pair = q_ref[:, lo:lo + 128]
            outs = []
            for hh in range(2):
                h = 2 * pair + hh
                _, mine, p = _head_scores(q_pair, kw, hh, b_ref[h * QB:(h + 1) * QB, :], key_ok)
                o = jnp.dot(p.astype(BF16), vw, preferred_element_type=F32)
                outs.append((mine, o))
            o_ref[:, lo:lo + 128] = jnp.where(outs[0][0], outs[0][1], outs[1][1]).astype(BF16)

    res, jres = _call(
        body, name="attn_fwd", grid=(t // QB,),
        in_specs=[pl.BlockSpec((QB, DA), lambda i: (i, 0)),
                  pl.BlockSpec(memory_space=pltpu.VMEM),
                  pl.BlockSpec(memory_space=pltpu.VMEM)],
        args=[qkv, kvp, bias],
        out_specs=[pl.BlockSpec((QB, DA), lambda i: (i, 0))],
        out_shape=[jax.ShapeDtypeStruct((t, DA), BF16)],
        sem=("arbitrary",), job=job)
    return res[0], jres


def _attn_bwd(qkv, kvp, bias, datt, job=None):
    t = qkv.shape[0]
    nb = t // QB
    flush = (KW - QB) // QB

    def body(q_ref, kv_ref, b_ref, do_ref, dq_ref, dkv_ref, db_ref, acc_ref):
        i = pl.program_id(0)

        @pl.when(i == 0)
        def _():
            acc_ref[...] = jnp.zeros_like(acc_ref)
            db_ref[...] = jnp.zeros_like(db_ref)

        @pl.when(i < nb)
        def _():
            start = pl.multiple_of(i * QB, QB)
            col = lax.broadcasted_iota(jnp.int32, (1, KW), 1)
            key_ok = col >= KPAD - i * QB
            for pair in range(4):
                lo = pair * 128
                kw = kv_ref[pl.ds(start, KW), lo:lo + 128]
                vw = kv_ref[pl.ds(start, KW), DA + lo:DA + lo + 128]
                q_pair = q_ref[:, lo:lo + 128]
                do_pair = do_ref[:, lo:lo + 128]
                dqs = []
                dk = None
                dv = None
                for hh in range(2):
                    h = 2 * pair + hh
                    qm, mine, p = _head_scores(q_pair, kw, hh, b_ref[h * QB:(h + 1) * QB, :], key_ok)
                    dom = jnp.where(mine, do_pair, jnp.zeros_like(do_pair))
                    dp = lax.dot_general(dom, vw, NT, preferred_element_type=F32)
                    ds = p * (dp - jnp.sum(p * dp, axis=-1, keepdims=True))
                    db_ref[h * QB:(h + 1) * QB, :] += ds
                    dsq = (ds * 0.125).astype(BF16)
                    dqs.append((mine, jnp.dot(dsq, kw, preferred_element_type=F32)))
                    dk_h = lax.dot_general(dsq, qm, TN, preferred_element_type=F32)
                    dv_h = lax.dot_general(p.astype(BF16), dom, TN, preferred_element_type=F32)
                    dk = dk_h if dk is None else dk + dk_h
                    dv = dv_h if dv is None else dv + dv_h
                dq_ref[:, lo:lo + 128] = jnp.where(dqs[0][0], dqs[0][1], dqs[1][1]).astype(BF16)
                acc_ref[:, lo:lo + 128] += dk
                acc_ref[:, DA + lo:DA + lo + 128] += dv

        dkv_ref[...] = acc_ref[0:QB, :].astype(BF16)
        rest = acc_ref[QB:KW, :]
        acc_ref[0:KW - QB, :] = rest
        acc_ref[KW - QB:KW, :] = jnp.zeros((QB, 2 * DA), F32)

    last = nb - 1
    res, jres = _call(
        body, name="attn_bwd", grid=(nb + flush,),
        in_specs=[pl.BlockSpec((QB, DA), lambda i: (jnp.minimum(i, last), 0)),
                  pl.BlockSpec(memory_space=pltpu.VMEM),
                  pl.BlockSpec(memory_space=pltpu.VMEM),
                  pl.BlockSpec((QB, DA), lambda i: (jnp.minimum(i, last), 0))],
        args=[qkv, kvp, bias, datt],
        out_specs=[pl.BlockSpec((QB, DA), lambda i: (jnp.minimum(i, last), 0)),
                   pl.BlockSpec((QB, 2 * DA), lambda i: (i, 0)),
                   pl.BlockSpec((NH * QB, KW), lambda i: (0, 0))],
        out_shape=[jax.ShapeDtypeStruct((t, DA), BF16),
                   jax.ShapeDtypeStruct((t + KPAD, 2 * DA), BF16),
                   jax.ShapeDtypeStruct((NH * QB, KW), F32)],
        scratch=[pltpu.VMEM((KW, 2 * DA), F32)], sem=("arbitrary",), job=job)
    return res, jres


def _glu(c, gb):
    cb = c + gb
    return cb[:, :DC] * _sigmoid(cb[:, DC:])


def _ln_swish(pre, g, b):
    mu = jnp.mean(pre, axis=-1, keepdims=True)
    xc = pre - mu
    r = lax.rsqrt(jnp.mean(xc * xc, axis=-1, keepdims=True) + EPS)
    xhat = xc * r
    y = xhat * g + b
    return xhat, r, y


def _conv_fwd(cin, glu_b, dw_w, dw_b, ln_g, ln_b, tm=TM):
    t = cin.shape[0]
    hb = tm // HALO

    def body(c_ref, h_ref, gb_ref, w_ref, wb_ref, g_ref, b_ref, cs_ref, pre_ref, ext_ref):
        i = pl.program_id(0)
        halo = _glu(h_ref[...], gb_ref[...])
        ext_ref[0:HALO, :] = jnp.where(i > 0, halo, jnp.zeros_like(halo))
        ext_ref[HALO:HALO + tm, :] = _glu(c_ref[...], gb_ref[...])
        acc = jnp.zeros((tm, DC), F32) + wb_ref[...]
        for j in range(CW):
            acc = acc + w_ref[j:j + 1, :] * ext_ref[pl.ds(HALO - (CW - 1) + j, tm), :]
        pre_ref[...] = acc
        _, _, y = _ln_swish(acc, g_ref[...], b_ref[...])
        cs_ref[...] = (y * _sigmoid(y)).astype(BF16)

    vec = lambda n: pl.BlockSpec((1, n), lambda i: (0, 0))
    return pl.pallas_call(
        body,
        name="conv_fwd",
        grid=(t // tm,),
        in_specs=[pl.BlockSpec((tm, 2 * DC), lambda i: (i, 0)),
                  pl.BlockSpec((HALO, 2 * DC), lambda i: (jnp.maximum(i * hb - 1, 0), 0)),
                  vec(2 * DC), pl.BlockSpec((CW, DC), lambda i: (0, 0)), vec(DC), vec(DC), vec(DC)],
        out_specs=[pl.BlockSpec((tm, DC), lambda i: (i, 0)), pl.BlockSpec((tm, DC), lambda i: (i, 0))],
        out_shape=[jax.ShapeDtypeStruct((t, DC), BF16), jax.ShapeDtypeStruct((t, DC), F32)],
        scratch_shapes=[pltpu.VMEM((HALO + tm, DC), F32)],
        compiler_params=_params(("arbitrary",)),
    )(cin, cin, glu_b, dw_w, dw_b, ln_g, ln_b)


def _conv_bwd(dcs, pre, cin, glu_b, dw_w, ln_g, ln_b, tm=TM):
    t = cin.shape[0]
    hb = tm // HALO
    steps = t // tm
    nhb = t // HALO

    def dpre_of(dcs_v, pre_v, g, b):
        xhat, r, y = _ln_swish(pre_v, g, b)
        sg = _sigmoid(y)
        dy = dcs_v * (sg * (1.0 + y * (1.0 - sg)))
        dxh = dy * g
        dpre = r * (dxh - jnp.mean(dxh, axis=-1, keepdims=True)
                    - xhat * jnp.mean(dxh * xhat, axis=-1, keepdims=True))
        return dpre, dy * xhat, dy

    def body(dcs_ref, dcsn_ref, pre_ref, pren_ref, c_ref, ch_ref, gb_ref, w_ref, g_ref, b_ref,
             dc_ref, dgb_ref, dw_ref, dwb_ref, dg_ref, db_ref,
             gext_ref, dext_ref, a_gb, a_w, a_wb, a_g, a_b):
        i = pl.program_id(0)

        @pl.when(i == 0)
        def _():
            for a in (a_gb, a_w, a_wb, a_g, a_b):
                a[...] = jnp.zeros_like(a)

        g, b = g_ref[...], b_ref[...]
        dpre, dg_t, db_t = dpre_of(dcs_ref[...], pre_ref[...], g, b)
        dpre_n, _, _ = dpre_of(dcsn_ref[...], pren_ref[...], g, b)
        dext_ref[0:tm, :] = dpre
        dext_ref[tm:tm + HALO, :] = jnp.where(i < steps - 1, dpre_n, jnp.zeros_like(dpre_n))
        halo = _glu(ch_ref[...], gb_ref[...])
        gext_ref[0:HALO, :] = jnp.where(i > 0, halo, jnp.zeros_like(halo))
        cb = c_ref[...] + gb_ref[...]
        sg = _sigmoid(cb[:, DC:])
        gext_ref[HALO:HALO + tm, :] = cb[:, :DC] * sg

        fold = lambda v: v.reshape(tm // 8, 8, v.shape[-1]).sum(axis=0)
        dglu = jnp.zeros((tm, DC), F32)
        for j in range(CW):
            a_w[8 * j:8 * j + 8, :] += fold(dpre * gext_ref[pl.ds(HALO - (CW - 1) + j, tm), :])
            dglu = dglu + w_ref[j:j + 1, :] * dext_ref[pl.ds(CW - 1 - j, tm), :]
        da = dglu * sg
        dbv = dglu * cb[:, :DC] * sg * (1.0 - sg)
        dc = jnp.concatenate([da, dbv], axis=1)
        dc_ref[...] = dc.astype(BF16)
        a_gb[...] += fold(dc)
        a_wb[...] += fold(dpre)
        a_g[...] += fold(dg_t)
        a_b[...] += fold(db_t)

        @pl.when(i == steps - 1)
        def _():
            dgb_ref[...] = a_gb[...].sum(axis=0, keepdims=True)
            dw_ref[...] = a_w[...]
            dwb_ref[...] = a_wb[...].sum(axis=0, keepdims=True)
            dg_ref[...] = a_g[...].sum(axis=0, keepdims=True)
            db_ref[...] = a_b[...].sum(axis=0, keepdims=True)

    vec = lambda n: pl.BlockSpec((1, n), lambda i: (0, 0))
    nxt = lambda i: (jnp.minimum((i + 1) * hb, nhb - 1), 0)
    prv = lambda i: (jnp.maximum(i * hb - 1, 0), 0)
    return pl.pallas_call(
        body,
        name="conv_bwd",
        grid=(steps,),
        in_specs=[pl.BlockSpec((tm, DC), lambda i: (i, 0)), pl.BlockSpec((HALO, DC), nxt),
                  pl.BlockSpec((tm, DC), lambda i: (i, 0)), pl.BlockSpec((HALO, DC), nxt),
                  pl.BlockSpec((tm, 2 * DC), lambda i: (i, 0)), pl.BlockSpec((HALO, 2 * DC), prv),
                  vec(2 * DC), pl.BlockSpec((CW, DC), lambda i: (0, 0)), vec(DC), vec(DC)],
        out_specs=[pl.BlockSpec((tm, 2 * DC), lambda i: (i, 0)), vec(2 * DC),
                   pl.BlockSpec((CW * 8, DC), lambda i: (0, 0)), vec(DC), vec(DC), vec(DC)],
        out_shape=[jax.ShapeDtypeStruct((t, 2 * DC), BF16), jax.ShapeDtypeStruct((1, 2 * DC), F32),
                   jax.ShapeDtypeStruct((CW * 8, DC), F32), jax.ShapeDtypeStruct((1, DC), F32),
                   jax.ShapeDtypeStruct((1, DC), F32), jax.ShapeDtypeStruct((1, DC), F32)],
        scratch_shapes=[pltpu.VMEM((HALO + tm, DC), F32), pltpu.VMEM((tm + HALO, DC), F32),
                        pltpu.VMEM((8, 2 * DC), F32), pltpu.VMEM((CW * 8, DC), F32),
                        pltpu.VMEM((8, DC), F32), pltpu.VMEM((8, DC), F32), pltpu.VMEM((8, DC), F32)],
        compiler_params=_params(("arbitrary",)),
    )(dcs, dcs, pre, pre, cin, cin, glu_b, dw_w, ln_g, ln_b)


def _place():
    x, y, c = lax.axis_index("x"), lax.axis_index("y"), lax.axis_index("c")
    return x, y, c


def _peers(x, y, c):
    out = []
    for k in range(1, NDEV):
        fx, fy, fc = (k >> 2) & 1, (k >> 1) & 1, k & 1
        px = 1 - x if fx else x
        py = 1 - y if fy else y
        pc = 1 - c if fc else c
        out.append((px, py, pc))
    return out


def _job_out_shapes(job):
    kind, arrays = job
    if kind == "gather":
        return [jax.ShapeDtypeStruct((a.shape[0], NDEV) + a.shape[1:], a.dtype) for a in arrays]
    return [jax.ShapeDtypeStruct((NDEV, a.shape[0]) + a.shape[2:], a.dtype) for a in arrays]


def _job_scratch(job):
    n = len(job[1])
    return [pltpu.SemaphoreType.DMA((n, NDEV - 1)), pltpu.SemaphoreType.DMA((n, NDEV - 1)),
            pltpu.SemaphoreType.DMA((n,))]


def _gather_parts(ins, outs, send_sems, recv_sems, local_sems):
    x, y, c = _place()
    me, sib = (x, y, c), (x, y, 1 - c)
    chips = [(1 - x, y), (x, 1 - y), (1 - x, 1 - y)]

    def copy(a, k, block, to, src=None):
        px, py, pc = block
        dst = outs[a].at[:, 4 * px + 2 * py + pc]
        return pltpu.make_async_remote_copy(
            src_ref=dst if src is None else src, dst_ref=dst,
            send_sem=send_sems.at[a, k], recv_sem=recv_sems.at[a, k], device_id=to, device_id_type=MESH)

    n = len(ins)
    local = [pltpu.make_async_copy(ins[a], outs[a].at[:, 4 * x + 2 * y + c], local_sems.at[a]) for a in range(n)]
    first = [[copy(a, 0, me, sib, src=ins[a])] + [copy(a, 1 + j, me, (*chip, c), src=ins[a])
                                                   for j, chip in enumerate(chips)] for a in range(n)]

    def start():
        for a in range(n):
            local[a].start()
            for cp in first[a]:
                cp.start()

    def finish():
        passed = []
        for j, chip in enumerate(chips):
            for a in range(n):
                copy(a, 1 + j, (*chip, c), me).wait_recv()
                cp = copy(a, 4 + j, (*chip, c), sib)
                cp.start()
                passed.append(cp)
        for a in range(n):
            copy(a, 0, sib, me).wait_recv()
            for j, chip in enumerate(chips):
                copy(a, 4 + j, (*chip, 1 - c), me).wait_recv()
        for a in range(n):
            for cp in first[a]:
                cp.wait_send()
            local[a].wait()
        for cp in passed:
            cp.wait_send()

    return start, finish


def _exchange_parts(ins, outs, send_sems, recv_sems, local_sems):
    x, y, c = _place()
    me = 4 * x + 2 * y + c
    n = len(ins)
    peers = _peers(x, y, c)
    local = [pltpu.make_async_copy(ins[a].at[:, me], outs[a].at[me], local_sems.at[a]) for a in range(n)]

    def copy(a, k):
        px, py, pc = peers[k]
        return pltpu.make_async_remote_copy(
            src_ref=ins[a].at[:, 4 * px + 2 * py + pc], dst_ref=outs[a].at[me],
            send_sem=send_sems.at[a, k], recv_sem=recv_sems.at[a, k], device_id=peers[k], device_id_type=MESH)

    def arrival(a, k):
        px, py, pc = peers[k]
        return pltpu.make_async_remote_copy(
            src_ref=ins[a].at[:, me], dst_ref=outs[a].at[4 * px + 2 * py + pc],
            send_sem=send_sems.at[a, k], recv_sem=recv_sems.at[a, k], device_id=peers[k], device_id_type=MESH)

    def start():
        for a in range(n):
            local[a].start()
            for k in range(NDEV - 1):
                copy(a, k).start()

    def finish():
        for a in range(n):
            for k in range(NDEV - 1):
                arrival(a, k).wait_recv()
        for a in range(n):
            for k in range(NDEV - 1):
                copy(a, k).wait_send()
            local[a].wait()

    return start, finish


def _call(body, *, name, grid, in_specs, args, out_specs, out_shape, scratch=(), sem=None, aliases=None, job=None):
    aliases = dict(aliases or {})
    if job is None:
        res = pl.pallas_call(
            body, name=name, grid=grid, in_specs=list(in_specs), out_specs=list(out_specs),
            out_shape=list(out_shape), scratch_shapes=list(scratch), input_output_aliases=aliases,
            compiler_params=_params(sem))(*args)
        return list(res), []
    kind, arrays = job
    n_in, n_out, n_scr, nj = len(args), len(out_shape), len(scratch), len(arrays)

    def wrapped(*refs):
        ins = refs[:n_in]
        jin = refs[n_in:n_in + nj]
        o0 = n_in + nj
        outs = refs[o0:o0 + n_out]
        jout = refs[o0 + n_out:o0 + n_out + nj]
        s0 = o0 + n_out + nj
        scr = refs[s0:s0 + n_scr]
        sems = refs[s0 + n_scr:]
        parts = _gather_parts if kind == "gather" else _exchange_parts
        start, finish = parts(jin, jout, *sems)
        if not grid:
            start()
            body(*ins, *outs, *scr)
            finish()
            return
        first = last = None
        for d, g in enumerate(grid):
            f, l = pl.program_id(d) == 0, pl.program_id(d) == g - 1
            first = f if first is None else jnp.logical_and(first, f)
            last = l if last is None else jnp.logical_and(last, l)
        pl.when(first)(start)
        body(*ins, *outs, *scr)
        pl.when(last)(finish)

    res = pl.pallas_call(
        wrapped, name=name, grid=grid, in_specs=list(in_specs) + [ANY] * nj,
        out_specs=list(out_specs) + [ANY] * nj, out_shape=list(out_shape) + _job_out_shapes(job),
        scratch_shapes=list(scratch) + _job_scratch(job), input_output_aliases=aliases,
        compiler_params=pltpu.CompilerParams(
            dimension_semantics=None if not grid else ("arbitrary",) * len(grid),
            vmem_limit_bytes=VMEM_LIMIT, has_side_effects=True))(*args, *arrays)
    return list(res[:n_out]), list(res[n_out:])


def _comm_only(name, job):
    return _call(lambda: None, name=name, grid=(), in_specs=[], args=[], out_specs=[], out_shape=[], job=job)[1]


def _sum_devices(name, parts, tr):
    _, r, c = parts.shape

    def body(p_ref, o_ref):
        acc = p_ref[0].astype(F32)
        for d in range(1, NDEV):
            acc = acc + p_ref[d].astype(F32)
        o_ref[...] = acc

    return pl.pallas_call(
        body,
        name=name,
        grid=(r // tr,),
        in_specs=[pl.BlockSpec((NDEV, tr, c), lambda i: (0, i, 0))],
        out_specs=pl.BlockSpec((tr, c), lambda i: (i, 0)),
        out_shape=jax.ShapeDtypeStruct((r, c), F32),
        compiler_params=_params(("parallel",)),
    )(parts)


def _adamw(name, w, g, m, v, tr=None):
    r, c = w.shape
    tr = r if tr is None else tr

    def body(w_ref, g_ref, m_ref, v_ref, d_ref, nm_ref, nv_ref):
        gv = g_ref[...]
        nm = ADAM_B1 * m_ref[...] + (1.0 - ADAM_B1) * gv
        nv = ADAM_B2 * v_ref[...] + (1.0 - ADAM_B2) * (gv * gv)
        m_hat = nm / (1.0 - ADAM_B1 ** ADAM_STEP)
        v_hat = nv / (1.0 - ADAM_B2 ** ADAM_STEP)
        d_ref[...] = -ADAM_LR * (m_hat / (jnp.sqrt(v_hat) + ADAM_EPS) + ADAM_WD * w_ref[...])
        nm_ref[...] = nm
        nv_ref[...] = nv

    spec = pl.BlockSpec((tr, c), lambda i: (i, 0))
    return pl.pallas_call(
        body,
        name=name,
        grid=(r // tr,),
        in_specs=[spec] * 4,
        out_specs=[spec] * 3,
        out_shape=[jax.ShapeDtypeStruct((r, c), F32)] * 3,
        compiler_params=_params(("parallel",)),
    )(w, g, m, v)


SMALL = ["ffn1_norm_pre", "ffn1_norm_post", "mix_norm_pre", "gate_bias", "rel_table", "conv_glu_bias",
         "conv_dw_b", "conv_ln_g", "conv_ln_b", "mix_norm_post", "ffn2_norm_pre", "ffn2_norm_post"]
SMALL_ROWS = 24
DW_ROWS = 32


def _pack_small(vals):
    rows = []
    for name in SMALL:
        v = vals[name]
        if name == "rel_table":
            v = v.reshape(NH, -1)
            rows.append(jnp.pad(v, ((0, 0), (0, D - v.shape[1]))))
        else:
            v = v.reshape(-1)
            v = jnp.pad(v, (0, (-v.shape[0]) % D))
            rows.append(v.reshape(-1, D))
    out = jnp.concatenate(rows, axis=0)
    return jnp.pad(out, ((0, SMALL_ROWS - out.shape[0]), (0, 0)))


def _unpack_small(pack, shapes):
    out = {}
    r = 0
    for name in SMALL:
        shp = shapes[name]
        n = 1
        for s in shp:
            n *= s
        if name == "rel_table":
            out[name] = pack[r:r + NH, :NREL].reshape(shp)
            r += NH
        else:
            nr = -(-n // D)
            out[name] = pack[r:r + nr].reshape(-1)[:n].reshape(shp)
            r += nr
    return out


def kernel(x, ffn1_norm_pre, ffn1_w_gate, ffn1_w_up, ffn1_w_down, ffn1_norm_post, mix_norm_pre, w_in, gate_bias, rel_table, w_attn_out, conv_glu_bias, conv_dw_w, conv_dw_b, conv_ln_g, conv_ln_b, conv_w_out, w_out, mix_norm_post, ffn2_norm_pre, ffn2_w_gate, ffn2_w_up, ffn2_w_down, ffn2_norm_post, loss_target, m_ffn1_norm_pre, m_ffn1_w_gate, m_ffn1_w_up, m_ffn1_w_down, m_ffn1_norm_post, m_mix_norm_pre, m_w_in, m_gate_bias, m_rel_table, m_w_attn_out, m_conv_glu_bias, m_conv_dw_w, m_conv_dw_b, m_conv_ln_g, m_conv_ln_b, m_conv_w_out, m_w_out, m_mix_norm_post, m_ffn2_norm_pre, m_ffn2_w_gate, m_ffn2_w_up, m_ffn2_w_down, m_ffn2_norm_post, v_ffn1_norm_pre, v_ffn1_w_gate, v_ffn1_w_up, v_ffn1_w_down, v_ffn1_norm_post, v_mix_norm_pre, v_w_in, v_gate_bias, v_rel_table, v_w_attn_out, v_conv_glu_bias, v_conv_dw_w, v_conv_dw_b, v_conv_ln_g, v_conv_ln_b, v_conv_w_out, v_w_out, v_mix_norm_post, v_ffn2_norm_pre, v_ffn2_w_gate, v_ffn2_w_up, v_ffn2_w_down, v_ffn2_norm_post):
    return _step(dict(locals()))


WEIGHTS = ["ffn1_norm_pre", "ffn1_w_gate", "ffn1_w_up", "ffn1_w_down", "ffn1_norm_post", "mix_norm_pre", "w_in",
           "gate_bias", "rel_table", "w_attn_out", "conv_glu_bias", "conv_dw_w", "conv_dw_b", "conv_ln_g",
           "conv_ln_b", "conv_w_out", "w_out", "mix_norm_post", "ffn2_norm_pre", "ffn2_w_gate", "ffn2_w_up",
           "ffn2_w_down", "ffn2_norm_post"]
FS = FF // NDEV
PS = (3 * DA + 2 * DC + 2 * D) // NDEV
OS = D // NDEV


def _local_step(xs, target, w, rel_table):
    t = xs.shape[0]
    vec = lambda n: w[n].reshape(1, -1)
    g_pre1, g_post1, g_mix, g_mixp = vec("ffn1_norm_pre"), vec("ffn1_norm_post"), vec("mix_norm_pre"), vec("mix_norm_post")
    g_pre2, g_post2 = vec("ffn2_norm_pre"), vec("ffn2_norm_post")
    gate_b, glu_b = vec("gate_bias"), vec("conv_glu_bias")
    dw_b, ln_g, ln_b = vec("conv_dw_b"), vec("conv_ln_g"), vec("conv_ln_b")

    tr = lambda a: jnp.transpose(a[0]).astype(BF16)
    sh_gu1 = jnp.stack([tr(w["ffn1_w_gate"]), tr(w["ffn1_w_up"])])
    sh_mid = [w["ffn1_w_down"].astype(BF16), tr(w["w_in"])[None], w["w_out"].astype(BF16),
              jnp.stack([tr(w["w_attn_out"]), tr(w["conv_w_out"])]),
              jnp.pad(w["conv_dw_w"][0, :, 0, :], ((0, DW_ROWS - CW), (0, 0)))[None]]
    sh_2 = jnp.stack([tr(w["ffn2_w_gate"]), tr(w["ffn2_w_up"]), w["ffn2_w_down"][0].astype(BF16)])

    (n1,), (w_gu1,) = _rowwise("pre1", lambda xv, g: ((_rms(xv)[0] * g),), [xs], [g_pre1], [(D, BF16)],
                               job=("gather", [sh_gu1]))
    w_gu1 = w_gu1.reshape(2, FF, D)
    (a1, b1, s1), (w_d1, wb, wc, wd, we) = _ffn_up("ffn1_up", n1, w_gu1, 0, job=("gather", sh_mid))
    w_d1, wb, wc, wd = w_d1.reshape(1, FF, D), wb.reshape(NDEV * PS, D), wc.reshape(D, D), wd.reshape(2, D, DA)
    dw_full = jnp.transpose(we[0], (1, 0, 2)).reshape(DW_ROWS, DC)[:CW]
    f1 = _ffn_down("ffn1_down", s1, w_d1, 0)

    def post1(xv, fv, gp, gm):
        h = xv + 0.5 * (_rms(fv)[0] * gp)
        return h, _rms(h)[0] * gm

    h1, u = _rowwise("post1", post1, [xs, f1], [g_post1, g_mix], [(D, F32), (D, BF16)])

    qkv = _mm_simple("proj_qkv", u, wb, NT, BF16, 512, 512, b_row0=0, b_rows=3 * DA)
    cin = _mm_simple("proj_conv", u, wb, NT, F32, 512, 512, b_row0=3 * DA, b_rows=2 * DC)
    gg = _mm_simple("proj_gate", u, wb, NT, F32, 512, 512, b_row0=3 * DA + 2 * DC, b_rows=2 * D)

    bias = _relbias_fwd(jnp.pad(rel_table[0], ((0, 0), (0, 384 - NREL))))
    kvp = jnp.pad(qkv[:, DA:], ((KPAD, 0), (0, 0)))
    att, (w_2,) = _attn_fwd(qkv, kvp, bias, job=("gather", [sh_2]))
    w_2 = w_2.reshape(3, FF, D)
    cs, pre = _conv_fwd(cin, glu_b, dw_full, dw_b, ln_g, ln_b)
    ya = _mm_simple("attn_out", att, wd[0], NT, F32, 512, 512)
    yb = _mm_simple("conv_out", cs, wd[1], NT, F32, 512, 512)

    def merge(yav, ybv, gv, gb):
        gates = _sigmoid(gv + gb)
        return (gates[:, :D] * yav + gates[:, D:] * ybv,)

    (merged,) = _rowwise("merge", merge, [ya, yb, gg], [gate_b], [(D, BF16)])
    mm_ = _mm_simple("mix_out", merged, wc, NN, F32, 512, 512)

    def postm(hv, mv, gp, g2):
        h = hv + _rms(mv)[0] * gp
        return h, _rms(h)[0] * g2

    h2, n2 = _rowwise("postm", postm, [h1, mm_], [g_mixp, g_pre2], [(D, F32), (D, BF16)])
    a2, b2, s2 = _ffn_up("ffn2_up", n2, w_2, 0)
    f2 = _ffn_down("ffn2_down", s2, w_2, 2)

    def post2(hv, fv, tv, gp):
        fh, r = _rms(fv)
        yv = hv + 0.5 * (fh * gp)
        err = yv - tv
        dy = err * (1.0 / D)
        df, dg = _rms_bwd(fh, r, gp, 0.5 * dy)
        return dy, df, (0.5 / D) * (err * err), dg

    dy, df2, loss_row, d_post2 = _rowwise("post2", post2, [h2, f2, target], [g_post2],
                                          [(D, F32), (D, BF16)], [D, D])

    tmw = 1408
    nfi = FF // tmw

    def wgrad_down(name, s, df, shape, part, carry_buf, job=None):
        return _wgrad(name, (s, (512, tmw), lambda i, j, q: (q, i)), df,
                      (None, tmw, D), lambda i, j, q: (part, i, 0), nfi, carry_buf, shape, job=job)

    def wgrad_gate_up(name, dab, nrm, shape, carry_buf, job=None):
        return _wgrad(name, (dab, (None, 512, tmw), lambda i, j, q: (i // nfi, q, i % nfi)), nrm,
                      (None, tmw, D), lambda i, j, q: (i // nfi, i % nfi, 0), 2 * nfi, carry_buf, shape, job=job)

    g_2 = wgrad_down("ffn2_wgrad_d", s2, df2, (3, FF, D), 2, None)
    dab2 = _ffn_bwd_act("ffn2_bwd_act", df2, w_2, 2, a2, b2)
    g_2 = wgrad_gate_up("ffn2_wgrad_gu", dab2, n2, (3, FF, D), g_2)
    dn2 = _ffn_bwd_in("ffn2_bwd_in", dab2, w_2, 0)

    def bwd_pre2(hv, dnv, dyv, mv, g2, gp):
        hh, r = _rms(hv)
        dx, dg2 = _rms_bwd(hh, r, g2, dnv)
        dh = dyv + dx
        mh, rm = _rms(mv)
        dm, dgp = _rms_bwd(mh, rm, gp, dh)
        return dh, dm, dg2, dgp

    dh2, dm, d_pre2, d_mixp = _rowwise("bwd_pre2", bwd_pre2, [h2, dn2, dy, mm_], [g_pre2, g_mixp],
                                       [(D, F32), (D, BF16)], [D, D])
    dmerged = _mm_simple("mix_out_bwd", dm, wc, NT, F32, 512, 512)

    def merge_bwd(dmv, yav, ybv, gv, gb):
        gates = _sigmoid(gv + gb)
        ga, gbb = gates[:, :D], gates[:, D:]
        dgg = jnp.concatenate([dmv * yav * ga * (1.0 - ga), dmv * ybv * gbb * (1.0 - gbb)], axis=1)
        return dmv * ga, dmv * gbb, dgg, dgg

    dya, dyb, dgg, d_gate_b = _rowwise("merge_bwd", merge_bwd, [dmerged, ya, yb, gg], [gate_b],
                                       [(D, BF16), (D, BF16), (2 * D, BF16)], [2 * D])
    datt = _mm_simple("attn_out_bwd", dya, wd[0], NN, BF16, 512, 512)
    dcs = _mm_simple("conv_out_bwd", dyb, wd[1], NN, F32, 512, 512)
    g_c = _wgrad("mix_out_wgrad", (merged, (512, D), lambda i, j, q: (q, 0)), dm,
                 (D, D), lambda i, j, q: (0, 0), 1, None, (D, D))
    g_d = _wgrad("attn_out_wgrad", (dya, (512, D), lambda i, j, q: (q, 0)), att,
                 (None, D, DA), lambda i, j, q: (0, 0, 0), 1, None, (2, D, DA))
    g_d = _wgrad("conv_out_wgrad", (dyb, (512, D), lambda i, j, q: (q, 0)), cs,
                 (None, D, DA), lambda i, j, q: (1, 0, 0), 1, g_d, (2, D, DA))
    (dq, dkvp, dbias), (x_2, x_c, x_d) = _attn_bwd(
        qkv, kvp, bias, datt,
        job=("exchange", [g_2.reshape(3, NDEV, FS, D), g_c.reshape(1, NDEV, OS, D), g_d.reshape(2, NDEV, OS, DA)]))
    d_rel = _relbias_bwd(dbias)
    dcin, d_glu_b, d_dw8, d_dw_b, d_ln_g, d_ln_b = _conv_bwd(dcs, pre, cin, glu_b, dw_full, ln_g, ln_b)
    g_dw = jnp.pad(d_dw8, ((0, 8 * (DW_ROWS - CW)), (0, 0)))
    g_dw = g_dw.reshape(DW_ROWS * 8, NDEV, DC // NDEV).transpose(1, 0, 2)

    gb_shape = (NDEV * PS, D)
    g_b = _wgrad("proj_wgrad_q", (dq, (512, DA), lambda i, j, q: (q, 0)), u,
                 (DA, D), lambda i, j, q: (0, 0), 1, None, gb_shape)
    g_b = _wgrad("proj_wgrad_kv", (dkvp, (512, DA), lambda i, j, q: (q + KPAD // 512, i)), u,
                 (DA, D), lambda i, j, q: (1 + i, 0), 2, g_b, gb_shape)
    g_b = _wgrad("proj_wgrad_c", (dcin, (512, DA), lambda i, j, q: (q, i)), u,
                 (DA, D), lambda i, j, q: (3 + i, 0), 2, g_b, gb_shape)
    g_b = _wgrad("proj_wgrad_g", (dgg, (512, DA), lambda i, j, q: (q, i)), u,
                 (DA, D), lambda i, j, q: (5 + i, 0), 4, g_b, gb_shape)

    tmu = 512
    a_ops = [(dq, (tmu, DA), lambda i, j, q: (i, 0)),
             (dkvp, (tmu, 2 * DA), lambda i, j, q: (i + KPAD // tmu, 0)),
             (dcin, (tmu, 2 * DC), lambda i, j, q: (i, 0)),
             (dgg, (tmu, 2 * D), lambda i, j, q: (i, 0))]
    whole = lambda i, j, q: (0, 0)
    b_ops = [(wb[:DA], (DA, D), whole), (wb[DA:3 * DA], (2 * DA, D), whole),
             (wb[3 * DA:3 * DA + 2 * DC], (2 * DC, D), whole), (wb[3 * DA + 2 * DC:], (2 * D, D), whole)]
    du = _mm("proj_bwd", (t // tmu, 1, 1), a_ops, b_ops, [[(0, 0), (1, 1), (2, 2), (3, 3)]], NN, _first,
             [((t, D), F32, (tmu, D), lambda i, j, q: (i, 0))])[0]

    def bwd_mix(hv, duv, dhv, fv, gm, gp):
        hh, r = _rms(hv)
        dx, dgm = _rms_bwd(hh, r, gm, duv)
        dh = dhv + dx
        fh, rf = _rms(fv)
        df, dgp = _rms_bwd(fh, rf, gp, 0.5 * dh)
        return dh, df, dgm, dgp

    dh1, df1, d_mix, d_post1 = _rowwise("bwd_mix", bwd_mix, [h1, du, dh2, f1], [g_mix, g_post1],
                                        [(D, F32), (D, BF16)], [D, D])
    g_d1 = wgrad_down("ffn1_wgrad_d", s1, df1, (1, FF, D), 0, None)
    dab1, (x_b, x_dw) = _ffn_bwd_act("ffn1_bwd_act", df1, w_d1, 0, a1, b1,
                                     job=("exchange", [g_b.reshape(1, NDEV, PS, D), g_dw[None]]))
    g_gu1, (x_d1,) = wgrad_gate_up("ffn1_wgrad_gu", dab1, n1, (2, FF, D), None,
                                   job=("exchange", [g_d1.reshape(1, NDEV, FS, D)]))
    dn1, (x_gu1,) = _ffn_bwd_in("ffn1_bwd_in", dab1, w_gu1, 0, job=("exchange", [g_gu1.reshape(2, NDEV, FS, D)]))

    def bwd_pre1(xv, dnv, dhv, g1):
        xh, r = _rms(xv)
        dx, dg1 = _rms_bwd(xh, r, g1, dnv)
        return dhv + dx, dg1

    dx, d_pre1 = _rowwise("bwd_pre1", bwd_pre1, [xs, dn1, dh1], [g_pre1], [(D, F32)], [D])

    small_g = {"ffn1_norm_pre": d_pre1, "ffn1_norm_post": d_post1, "mix_norm_pre": d_mix, "gate_bias": d_gate_b,
               "rel_table": d_rel[:, :NREL], "conv_glu_bias": d_glu_b, "conv_dw_b": d_dw_b, "conv_ln_g": d_ln_g,
               "conv_ln_b": d_ln_b, "mix_norm_post": d_mixp, "ffn2_norm_pre": d_pre2, "ffn2_norm_post": d_post2}
    return loss_row, dx, (x_gu1, x_d1, x_2, x_b, x_c, x_d, x_dw), small_g


def _step(args):
    names = WEIGHTS
    w = {n: args[n] for n in names}
    fs, ps, os_ = FS, PS, OS
    conv_dw_w = args["conv_dw_w"]
    loss_row, dx, (x_gu1, x_d1, x_2, x_b, x_c, x_d, x_dw), small_g = _local_step(
        args["x"][0], args["loss_target"][0], w, args["rel_table"])

    g_small = _pack_small(small_g)
    (x_s,) = _comm_only("gather_small_grads", ("gather", [g_small[None]]))

    s_gu1 = _sum_devices("sum_ffn1_gu", x_gu1.reshape(NDEV, 2 * fs, D), fs).reshape(2, fs, D)
    s_d1 = _sum_devices("sum_ffn1_d", x_d1.reshape(NDEV, fs, D), fs)
    s_2 = _sum_devices("sum_ffn2", x_2.reshape(NDEV, 3 * fs, D), fs).reshape(3, fs, D)
    s_b = _sum_devices("sum_proj", x_b.reshape(NDEV, ps, D), ps)
    s_c = _sum_devices("sum_mix", x_c.reshape(NDEV, os_, D), os_)
    s_d = _sum_devices("sum_out", x_d.reshape(NDEV, 2 * os_, DA), 2 * os_).reshape(2, os_, DA)
    s_dw = _sum_devices("sum_dw", x_dw.reshape(NDEV, DW_ROWS * 8, DC // NDEV), DW_ROWS * 8)
    s_s = _sum_devices("sum_small", x_s.reshape(NDEV, SMALL_ROWS, D), SMALL_ROWS)

    grads = {
        "ffn1_w_gate": jnp.transpose(s_gu1[0])[None], "ffn1_w_up": jnp.transpose(s_gu1[1])[None], "ffn1_w_down": s_d1[None],
        "ffn2_w_gate": jnp.transpose(s_2[0])[None], "ffn2_w_up": jnp.transpose(s_2[1])[None], "ffn2_w_down": s_2[2][None],
        "w_in": jnp.transpose(s_b)[None], "w_out": s_c[None],
        "w_attn_out": jnp.transpose(s_d[0])[None], "conv_w_out": jnp.transpose(s_d[1])[None],
    }
    shapes = {n: w[n].shape for n in SMALL}
    grads.update(_unpack_small(s_s, shapes))

    deltas, new_m, new_v = {}, {}, {}
    big = ["ffn1_w_gate", "ffn1_w_up", "ffn1_w_down", "w_in", "w_attn_out", "conv_w_out", "w_out",
           "ffn2_w_gate", "ffn2_w_up", "ffn2_w_down"]
    for n in big:
        shp = w[n].shape
        two = lambda a: a.reshape(shp[1], shp[2])
        rows = shp[1]
        tr_ = rows // 2 if rows % 16 == 0 else rows
        d_, m_, v_ = _adamw("adamw_" + n, two(w[n]), two(grads[n]), two(args["m_" + n]), two(args["v_" + n]), tr_)
        deltas[n], new_m[n], new_v[n] = d_.reshape(shp), m_.reshape(shp), v_.reshape(shp)

    wp = _pack_small({n: w[n] for n in SMALL})
    mp = _pack_small({n: args["m_" + n] for n in SMALL})
    vp = _pack_small({n: args["v_" + n] for n in SMALL})
    d_, m_, v_ = _adamw("adamw_small", wp, s_s, mp, vp)
    for dst, pack in ((deltas, d_), (new_m, m_), (new_v, v_)):
        dst.update(_unpack_small(pack, shapes))

    g_dw_own = _fold8("fold_dw", s_dw)[:CW]
    grads["conv_dw_w"] = g_dw_own.reshape(1, CW, 1, DC // NDEV)
    flat = lambda a: a.reshape(CW, DC // NDEV)
    d_, m_, v_ = _adamw("adamw_dw", flat(conv_dw_w), g_dw_own, flat(args["m_conv_dw_w"]), flat(args["v_conv_dw_w"]))
    shp = conv_dw_w.shape
    deltas["conv_dw_w"], new_m["conv_dw_w"], new_v["conv_dw_w"] = d_.reshape(shp), m_.reshape(shp), v_.reshape(shp)

    loss = lax.psum(jnp.sum(loss_row), ("x", "y", "c"))
    return (loss, dx[None], *[grads[n] for n in names], *[deltas[n] for n in names],
            *[new_m[n] for n in names], *[new_v[n] for n in names])


def _fold8(name, a):
    r8, c = a.shape

    def body(a_ref, o_ref):
        o_ref[...] = a_ref[...].reshape(r8 // 8, 8, c).sum(axis=1)

    return pl.pallas_call(
        body,
        name=name,
        out_shape=jax.ShapeDtypeStruct((r8 // 8, c), F32),
        in_specs=[pl.BlockSpec(memory_space=pltpu.VMEM)],
        out_specs=pl.BlockSpec(memory_space=pltpu.VMEM),
        compiler_params=_params(),
    )(a)
```

```python
import functools

import jax
import jax.numpy as jnp
from jax import lax
from jax.experimental import pallas as pl
from jax.experimental.pallas import tpu as pltpu

F32 = jnp.float32
BF16 = jnp.bfloat16

D = 1024
FF = 2816
DA = 512
DC = 512
NH = 8
CHUNK = 64
LEFT = 8
CW = 31
NREL = 257
EPS = 1e-6
NDEV = 8

QB = 2 * CHUNK
KW = (LEFT + 2) * CHUNK
KPAD = LEFT * CHUNK
RELW = 768
HALO = 32

TM = 512
VMEM_LIMIT = 56 * 1024 * 1024

ADAM_LR, ADAM_B1, ADAM_B2, ADAM_EPS, ADAM_WD, ADAM_STEP = 0.001, 0.9, 0.999, 1e-08, 0.01, 10

NT = (((1,), (1,)), ((), ()))
NN = (((1,), (0,)), ((), ()))
TN = (((0,), (0,)), ((), ()))

MESH = pl.DeviceIdType.MESH
ANY = pl.BlockSpec(memory_space=pl.ANY)


def _params(sem=None, vmem=VMEM_LIMIT):
    return pltpu.CompilerParams(dimension_semantics=sem, vmem_limit_bytes=vmem)


def _sigmoid(x):
    return 1.0 / (1.0 + jnp.exp(-x))


def _mm(name, grid, a_ops, b_ops, groups, dims, epi, outs, extras=(), carry=None, job=None):
    nk = grid[2]
    na, nb, ne, no, ng = len(a_ops), len(b_ops), len(extras), len(outs), len(groups)
    nc = 0 if carry is None else 1

    def body(*refs):
        a_refs = refs[:na]
        b_refs = refs[na:na + nb]
        e_refs = refs[na + nb:na + nb + ne]
        o_refs = refs[na + nb + ne + nc:na + nb + ne + nc + no]
        acc_refs = refs[na + nb + ne + nc + no:]
        k = pl.program_id(2)
        prods = []
        for grp in groups:
            p = None
            for ai, bi in grp:
                t = lax.dot_general(a_refs[ai][...], b_refs[bi][...], dims, preferred_element_type=F32)
                p = t if p is None else p + t
            prods.append(p)

        def finish(vals):
            res = epi(vals, [e[...] for e in e_refs])
            for o, r in zip(o_refs, res):
                o[...] = r.astype(o.dtype)

        if nk == 1:
            finish(prods)
        else:
            @pl.when(k == 0)
            def _():
                for acc, p in zip(acc_refs, prods):
                    acc[...] = p

            @pl.when(k > 0)
            def _():
                for acc, p in zip(acc_refs, prods):
                    acc[...] += p

            @pl.when(k == nk - 1)
            def _():
                finish([acc[...] for acc in acc_refs])

    in_specs = [pl.BlockSpec(blk, im) for _, blk, im in list(a_ops) + list(b_ops) + list(extras)]
    args = [arr for arr, _, _ in list(a_ops) + list(b_ops) + list(extras)]
    aliases = {}
    if carry is not None:
        in_specs.append(ANY)
        args.append(carry[0])
        aliases = {len(args) - 1: carry[1]}
    scratch = []
    if nk > 1:
        for _ in range(ng):
            blk = tuple(b for b in outs[0][2] if b is not None)
            scratch.append(pltpu.VMEM(blk, F32))
    res, jres = _call(
        body, name=name, grid=grid, in_specs=in_specs, args=args,
        out_specs=[pl.BlockSpec(blk, im) for _, _, blk, im in outs],
        out_shape=[jax.ShapeDtypeStruct(shp, dt) for shp, dt, _, _ in outs],
        scratch=scratch, sem=("parallel", "parallel", "arbitrary"), aliases=aliases, job=job)
    return res if job is None else (res, jres)


def _first(accs, extras):
    return (accs[0],)


def _mm_simple(name, a, b, dims, out_dtype, tm, tn, b_row0=0, b_rows=None):
    m, kk = a.shape
    if dims is NT:
        n = b.shape[0] if b_rows is None else b_rows
        assert b.shape[1] == kk and b_row0 % tn == 0
        b_op = (b, (tn, kk), lambda i, j, q: (j + b_row0 // tn, 0))
    else:
        assert b.shape[0] == kk
        n = b.shape[1]
        b_op = (b, (kk, tn), lambda i, j, q: (0, j))
    a_op = (a, (tm, kk), lambda i, j, q: (i, 0))
    out = ((m, n), out_dtype, (tm, tn), lambda i, j, q: (i, j))
    return _mm(name, (m // tm, n // tn, 1), [a_op], [b_op], [[(0, 0)]], dims, _first, [out])[0]


def _rowwise(name, fn, tiled, params, outs, partials=(), tm=TM, job=None):
    t = tiled[0].shape[0]
    steps = t // tm
    nt, npar, no, npart = len(tiled), len(params), len(outs), len(partials)

    def body(*refs):
        t_refs = refs[:nt]
        p_refs = refs[nt:nt + npar]
        o_refs = refs[nt + npar:nt + npar + no]
        s_refs = refs[nt + npar + no:nt + npar + no + npart]
        acc_refs = refs[nt + npar + no + npart:]
        i = pl.program_id(0)
        res = fn(*[r[...] for r in t_refs], *[r[...] for r in p_refs])
        for o, r in zip(o_refs, res[:no]):
            o[...] = r.astype(o.dtype)

        @pl.when(i == 0)
        def _():
            for acc in acc_refs:
                acc[...] = jnp.zeros_like(acc)

        for acc, r in zip(acc_refs, res[no:]):
            acc[...] += r.reshape(tm // 8, 8, r.shape[-1]).sum(axis=0)

        @pl.when(i == steps - 1)
        def _():
            for s, acc in zip(s_refs, acc_refs):
                s[...] = acc[...].sum(axis=0, keepdims=True)

    in_specs = [pl.BlockSpec((tm, a.shape[1]), lambda i: (i, 0)) for a in tiled]
    in_specs += [pl.BlockSpec(p.shape, lambda i: (0, 0)) for p in params]
    out_specs = [pl.BlockSpec((tm, c), lambda i: (i, 0)) for c, _ in outs]
    out_specs += [pl.BlockSpec((1, c), lambda i: (0, 0)) for c in partials]
    out_shape = [jax.ShapeDtypeStruct((t, c), dt) for c, dt in outs]
    out_shape += [jax.ShapeDtypeStruct((1, c), F32) for c in partials]
    res, jres = _call(body, name=name, grid=(steps,), in_specs=in_specs, args=[*tiled, *params], out_specs=out_specs,
                      out_shape=out_shape, scratch=[pltpu.VMEM((8, c), F32) for c in partials], sem=("arbitrary",),
                      job=job)
    return res if job is None else (res, jres)


def _rms(x):
    r = lax.rsqrt(jnp.mean(x * x, axis=-1, keepdims=True) + EPS)
    return x * r, r


def _rms_bwd(xhat, r, g, dy):
    dxh = dy * g
    dx = r * (dxh - xhat * jnp.mean(dxh * xhat, axis=-1, keepdims=True))
    return dx, dy * xhat


def _ffn_up(name, n, wa, part, tm=512, tf=1408, job=None):
    t = n.shape[0]

    def epi(accs, extras):
        a, b = accs
        return a, b, a * _sigmoid(a) * b

    a_op = (n, (tm, D), lambda f, i, q: (i, 0))
    b_ops = [(wa, (None, tf, D), lambda f, i, q: (part, f, 0)),
             (wa, (None, tf, D), lambda f, i, q: (part + 1, f, 0))]
    outs = [((t, FF), BF16, (tm, tf), lambda f, i, q: (i, f))] * 3
    return _mm(name, (FF // tf, t // tm, 1), [a_op], b_ops, [[(0, 0)], [(0, 1)]], NT, epi, outs, job=job)


def _ffn_down(name, s, wa, part, tm=512):
    t = s.shape[0]
    a_op = (s, (tm, FF), lambda i, j, q: (i, 0))
    b_op = (wa, (None, FF, D), lambda i, j, q: (part, 0, 0))
    out = ((t, D), F32, (tm, D), lambda i, j, q: (i, 0))
    return _mm(name, (t // tm, 1, 1), [a_op], [b_op], [[(0, 0)]], NN, _first, [out])[0]


def _ffn_bwd_act(name, df, wa, part, a, b, tm=512, tf=1408, job=None):
    t = df.shape[0]

    def epi(accs, extras):
        ds = accs[0]
        av, bv = extras[0].astype(F32), extras[1].astype(F32)
        sg = _sigmoid(av)
        da = ds * bv * (sg * (1.0 + av * (1.0 - sg)))
        db = ds * (av * sg)
        return (jnp.stack([da, db]),)

    a_op = (df, (tm, D), lambda f, i, q: (i, 0))
    b_op = (wa, (None, tf, D), lambda f, i, q: (part, f, 0))
    extras = [(a, (tm, tf), lambda f, i, q: (i, f)), (b, (tm, tf), lambda f, i, q: (i, f))]
    out = ((2, t, FF), BF16, (2, tm, tf), lambda f, i, q: (0, i, f))
    res = _mm(name, (FF // tf, t // tm, 1), [a_op], [b_op], [[(0, 0)]], NT, epi, [out], extras, job=job)
    return res[0] if job is None else (res[0][0], res[1])


def _ffn_bwd_in(name, dab, wa, part, tm=1024, job=None):
    t = dab.shape[1]
    tm = min(tm, t)
    a_op = (dab, (None, tm, FF), lambda i, j, q: (q, i, 0))
    b_op = (wa, (None, FF, D), lambda i, j, q: (part + q, 0, 0))
    out = ((t, D), F32, (tm, D), lambda i, j, q: (i, 0))
    res = _mm(name, (t // tm, 1, 2), [a_op], [b_op], [[(0, 0)]], NN, _first, [out], job=job)
    return res[0] if job is None else (res[0][0], res[1])


def _wgrad(name, dy_op, x, out_block, out_map, gi, carry_buf, out_shape, tk=512, job=None):
    t, c = x.shape
    b_op = (x, (tk, c), lambda i, j, q: (q, 0))
    out = (out_shape, BF16, out_block, out_map)
    carry = None if carry_buf is None else (carry_buf, 0)
    res = _mm(name, (gi, 1, t // tk), [dy_op], [b_op], [[(0, 0)]], TN, _first, [out], carry=carry, job=job)
    return res[0] if job is None else (res[0][0], res[1])


def _rel_onehot():
    j = lax.broadcasted_iota(jnp.int32, (384, RELW), 0)
    xx = lax.broadcasted_iota(jnp.int32, (384, RELW), 1)
    idx = jnp.clip(640 - xx, -128, 128) + 128
    return (j == idx).astype(F32)


def _band_ok():
    r = lax.broadcasted_iota(jnp.int32, (QB, KW), 0) >> 6
    c = lax.broadcasted_iota(jnp.int32, (QB, KW), 1) >> 6
    return (c >= r) & (c <= r + LEFT)


def _relbias_fwd(table):
    def body(t_ref, o_ref):
        rev = jnp.dot(t_ref[...], _rel_onehot(), precision=lax.Precision.HIGHEST, preferred_element_type=F32)
        for r in range(QB):
            row = pltpu.roll(rev, (RELW - (128 - r)) % RELW, 1)[:, :KW]
            rr = lax.broadcasted_iota(jnp.int32, (NH, KW), 1) >> 6
            ok = (rr >= (r // CHUNK)) & (rr <= (r // CHUNK) + LEFT)
            row = jnp.where(ok, row, -1e30)
            for h in range(NH):
                o_ref[h * QB + r:h * QB + r + 1, :] = row[h:h + 1, :]

    return pl.pallas_call(
        body,
        name="relbias_fwd",
        out_shape=jax.ShapeDtypeStruct((NH * QB, KW), F32),
        in_specs=[pl.BlockSpec(memory_space=pltpu.VMEM)],
        out_specs=pl.BlockSpec(memory_space=pltpu.VMEM),
        compiler_params=_params(),
    )(table)


def _relbias_bwd(dbias):
    def body(d_ref, o_ref, acc_ref):
        for h in range(NH):
            acc = jnp.zeros((1, RELW), F32)
            for r in range(QB):
                row = d_ref[h * QB + r:h * QB + r + 1, :]
                wide = jnp.concatenate([row, jnp.zeros((1, RELW - KW), F32)], axis=1)
                acc = acc + pltpu.roll(wide, 128 - r, 1)
            acc_ref[h:h + 1, :] = acc
        o_ref[...] = lax.dot_general(acc_ref[...], _rel_onehot(), NT, precision=lax.Precision.HIGHEST,
                                     preferred_element_type=F32)

    return pl.pallas_call(
        body,
        name="relbias_bwd",
        out_shape=jax.ShapeDtypeStruct((NH, 384), F32),
        in_specs=[pl.BlockSpec(memory_space=pltpu.VMEM)],
        out_specs=pl.BlockSpec(memory_space=pltpu.VMEM),
        scratch_shapes=[pltpu.VMEM((NH, RELW), F32)],
        compiler_params=_params(),
    )(dbias)


def _stack_heads(x_pair):
    first = lax.broadcasted_iota(jnp.int32, (1, 128), 1) < 64
    zero = jnp.zeros_like(x_pair)
    return jnp.concatenate([jnp.where(first, x_pair, zero), jnp.where(first, zero, x_pair)], axis=0), first


def _pair_probs(qs, kw, bias, key_ok):
    s = lax.dot_general(qs, kw, NT, preferred_element_type=F32) + bias
    if key_ok is not None:
        s = jnp.where(key_ok, s, -1e30)
    e = jnp.exp(s - jnp.max(s, axis=-1, keepdims=True))
    return e * (1.0 / jnp.sum(e, axis=-1, keepdims=True))


def _attn_fwd(qkv, kvp, bias, job=None):
    t = qkv.shape[0]

    def body(q_ref, kv_ref, b_ref, o_ref):
        i = pl.program_id(0)

        def run(masked):
            start = pl.multiple_of(i * QB, QB)
            col = lax.broadcasted_iota(jnp.int32, (1, KW), 1)
            key_ok = (col >= KPAD - i * QB) if masked else None
            for pair in range(4):
                lo = pair * 128
                kw = kv_ref[pl.ds(start, KW), lo:lo + 128]
                vw = kv_ref[pl.ds(start, KW), DA + lo:DA + lo + 128]
                qs, first = _stack_heads(q_ref[:, lo:lo + 128])
                p = _pair_probs(qs * 0.125, kw, b_ref[2 * pair * QB:(2 * pair + 2) * QB, :], key_ok)
                o = jnp.dot(p.astype(BF16), vw, preferred_element_type=F32)
                o_ref[:, lo:lo + 128] = jnp.where(first, o[:QB], o[QB:]).astype(BF16)

        pl.when(i < KPAD // QB)(lambda: run(True))
        pl.when(i >= KPAD // QB)(lambda: run(False))

    res, jres = _call(
        body, name="attn_fwd", grid=(t // QB,),
        in_specs=[pl.BlockSpec((QB, DA), lambda i: (i, 0)),
                  pl.BlockSpec(memory_space=pltpu.VMEM),
                  pl.BlockSpec(memory_space=pltpu.VMEM)],
        args=[qkv, kvp, bias],
        out_specs=[pl.BlockSpec((QB, DA), lambda i: (i, 0))],
        out_shape=[jax.ShapeDtypeStruct((t, DA), BF16)],
        sem=("arbitrary",), job=job)
    return res[0], jres


def _attn_bwd(qkv, kvp, bias, datt, job=None):
    t = qkv.shape[0]
    nb = t // QB
    flush = (KW - QB) // QB

    def body(q_ref, kv_ref, b_ref, do_ref, dq_ref, dkv_ref, db_ref, acc_ref):
        i = pl.program_id(0)

        @pl.when(i == 0)
        def _():
            acc_ref[...] = jnp.zeros_like(acc_ref)
            db_ref[...] = jnp.zeros_like(db_ref)

        def run(masked):
            start = pl.multiple_of(i * QB, QB)
            col = lax.broadcasted_iota(jnp.int32, (1, KW), 1)
            key_ok = (col >= KPAD - i * QB) if masked else None
            for pair in range(4):
                lo = pair * 128
                kw = kv_ref[pl.ds(start, KW), lo:lo + 128]
                vw = kv_ref[pl.ds(start, KW), DA + lo:DA + lo + 128]
                qs, first = _stack_heads(q_ref[:, lo:lo + 128])
                qs = qs * 0.125
                dos, _ = _stack_heads(do_ref[:, lo:lo + 128])
                rows = slice(2 * pair * QB, (2 * pair + 2) * QB)
                p = _pair_probs(qs, kw, b_ref[rows, :], key_ok)
                dp = lax.dot_general(dos, vw, NT, preferred_element_type=F32)
                ds = p * (dp - jnp.sum(p * dp, axis=-1, keepdims=True))
                db_ref[rows, :] += ds
                dsb = ds.astype(BF16)
                dq = jnp.dot(dsb, kw, preferred_element_type=F32)
                dq_ref[:, lo:lo + 128] = (jnp.where(first, dq[:QB], dq[QB:]) * 0.125).astype(BF16)
                acc_ref[:, lo:lo + 128] += lax.dot_general(dsb, qs, TN, preferred_element_type=F32)
                acc_ref[:, DA + lo:DA + lo + 128] += lax.dot_general(p.astype(BF16), dos, TN,
                                                                     preferred_element_type=F32)

        pl.when(i < KPAD // QB)(lambda: run(True))
        pl.when(jnp.logical_and(i >= KPAD // QB, i < nb))(lambda: run(False))

        dkv_ref[...] = acc_ref[0:QB, :].astype(BF16)
        rest = acc_ref[QB:KW, :]
        acc_ref[0:KW - QB, :] = rest
        acc_ref[KW - QB:KW, :] = jnp.zeros((QB, 2 * DA), F32)

    last = nb - 1
    res, jres = _call(
        body, name="attn_bwd", grid=(nb + flush,),
        in_specs=[pl.BlockSpec((QB, DA), lambda i: (jnp.minimum(i, last), 0)),
                  pl.BlockSpec(memory_space=pltpu.VMEM),
                  pl.BlockSpec(memory_space=pltpu.VMEM),
                  pl.BlockSpec((QB, DA), lambda i: (jnp.minimum(i, last), 0))],
        args=[qkv, kvp, bias, datt],
        out_specs=[pl.BlockSpec((QB, DA), lambda i: (jnp.minimum(i, last), 0)),
                   pl.BlockSpec((QB, 2 * DA), lambda i: (i, 0)),
                   pl.BlockSpec((NH * QB, KW), lambda i: (0, 0))],
        out_shape=[jax.ShapeDtypeStruct((t, DA), BF16),
                   jax.ShapeDtypeStruct((t + KPAD, 2 * DA), BF16),
                   jax.ShapeDtypeStruct((NH * QB, KW), F32)],
        scratch=[pltpu.VMEM((KW, 2 * DA), F32)], sem=("arbitrary",), job=job)
    return res, jres


def _glu(c, gb):
    cb = c + gb
    return cb[:, :DC] * _sigmoid(cb[:, DC:])


def _ln_swish(pre, g, b):
    mu = jnp.mean(pre, axis=-1, keepdims=True)
    xc = pre - mu
    r = lax.rsqrt(jnp.mean(xc * xc, axis=-1, keepdims=True) + EPS)
    xhat = xc * r
    y = xhat * g + b
    return xhat, r, y


RT = 32


def _shifted_copies(src_ref, sh_ref, rows):
    for b in range(1, 8):
        sh_ref[b - 1, :, :] = src_ref[pl.ds(b, rows), :]


def _tap(src_ref, sh_ref, off, r0, rows=RT):
    a, b = divmod(off, 8)
    ref = src_ref if b == 0 else sh_ref.at[b - 1]
    if isinstance(r0, int):
        return ref[r0 + 8 * a:r0 + 8 * a + rows, :]
    return ref[pl.ds(pl.multiple_of(r0 + 8 * a, 8), rows), :]


def _conv_fwd(cin, glu_b, dw_w, dw_b, ln_g, ln_b, tm=TM):
    t = cin.shape[0]
    hb = tm // HALO

    def body(c_ref, h_ref, gb_ref, w_ref, wb_ref, g_ref, b_ref, cs_ref, pre_ref, ext_ref, sh_ref):
        i = pl.program_id(0)
        halo = _glu(h_ref[...], gb_ref[...])
        ext_ref[0:HALO, :] = jnp.where(i > 0, halo, jnp.zeros_like(halo))
        ext_ref[HALO:HALO + tm, :] = _glu(c_ref[...], gb_ref[...])
        ext_ref[HALO + tm:HALO + tm + 8, :] = jnp.zeros((8, DC), F32)
        _shifted_copies(ext_ref, sh_ref, HALO + tm)

        def tile(rt, carry):
            r0 = pl.multiple_of(rt * RT, RT)
            acc = jnp.zeros((RT, DC), F32) + wb_ref[...]
            for j in range(CW):
                acc = acc + w_ref[j:j + 1, :] * _tap(ext_ref, sh_ref, HALO - (CW - 1) + j, r0)
            pre_ref[pl.ds(r0, RT), :] = acc
            _, _, y = _ln_swish(acc, g_ref[...], b_ref[...])
            cs_ref[pl.ds(r0, RT), :] = (y * _sigmoid(y)).astype(BF16)
            return carry

        lax.fori_loop(0, tm // RT, tile, 0)

    vec = lambda n: pl.BlockSpec((1, n), lambda i: (0, 0))
    return pl.pallas_call(
        body,
        name="conv_fwd",
        grid=(t // tm,),
        in_specs=[pl.BlockSpec((tm, 2 * DC), lambda i: (i, 0)),
                  pl.BlockSpec((HALO, 2 * DC), lambda i: (jnp.maximum(i * hb - 1, 0), 0)),
                  vec(2 * DC), pl.BlockSpec((CW, DC), lambda i: (0, 0)), vec(DC), vec(DC), vec(DC)],
        out_specs=[pl.BlockSpec((tm, DC), lambda i: (i, 0)), pl.BlockSpec((tm, DC), lambda i: (i, 0))],
        out_shape=[jax.ShapeDtypeStruct((t, DC), BF16), jax.ShapeDtypeStruct((t, DC), F32)],
        scratch_shapes=[pltpu.VMEM((HALO + tm + 8, DC), F32), pltpu.VMEM((7, HALO + tm, DC), F32)],
        compiler_params=_params(("arbitrary",)),
    )(cin, cin, glu_b, dw_w, dw_b, ln_g, ln_b)


def _conv_bwd(dcs, pre, cin, glu_b, dw_w, ln_g, ln_b, tm=TM):
    t = cin.shape[0]
    hb = tm // HALO
    steps = t // tm
    nhb = t // HALO

    def dpre_of(dcs_v, pre_v, g, b):
        xhat, r, y = _ln_swish(pre_v, g, b)
        sg = _sigmoid(y)
        dy = dcs_v * (sg * (1.0 + y * (1.0 - sg)))
        dxh = dy * g
        dpre = r * (dxh - jnp.mean(dxh, axis=-1, keepdims=True)
                    - xhat * jnp.mean(dxh * xhat, axis=-1, keepdims=True))
        return dpre, dy * xhat, dy

    def body(dcs_ref, dcsn_ref, pre_ref, pren_ref, c_ref, ch_ref, gb_ref, w_ref, g_ref, b_ref,
             dc_ref, dgb_ref, dw_ref, dwb_ref, dg_ref, db_ref,
             gext_ref, dext_ref, shg_ref, shd_ref, a_gb, a_w, a_wb, a_g, a_b):
        i = pl.program_id(0)

        @pl.when(i == 0)
        def _():
            for a in (a_gb, a_w, a_wb, a_g, a_b):
                a[...] = jnp.zeros_like(a)

        fold = lambda v: v.reshape(v.shape[0] // 8, 8, v.shape[-1]).sum(axis=0)
        g, b = g_ref[...], b_ref[...]
        dpre, dg_t, db_t = dpre_of(dcs_ref[...], pre_ref[...], g, b)
        dpre_n, _, _ = dpre_of(dcsn_ref[...], pren_ref[...], g, b)
        dext_ref[0:tm, :] = dpre
        dext_ref[tm:tm + HALO, :] = jnp.where(i < steps - 1, dpre_n, jnp.zeros_like(dpre_n))
        dext_ref[tm + HALO:tm + HALO + 8, :] = jnp.zeros((8, DC), F32)
        a_wb[...] += fold(dpre)
        a_g[...] += fold(dg_t)
        a_b[...] += fold(db_t)
        halo = _glu(ch_ref[...], gb_ref[...])
        gext_ref[0:HALO, :] = jnp.where(i > 0, halo, jnp.zeros_like(halo))
        gext_ref[HALO:HALO + tm, :] = _glu(c_ref[...], gb_ref[...])
        gext_ref[HALO + tm:HALO + tm + 8, :] = jnp.zeros((8, DC), F32)
        _shifted_copies(gext_ref, shg_ref, HALO + tm)
        _shifted_copies(dext_ref, shd_ref, HALO + tm)

        for j in range(CW):
            a_w[8 * j:8 * j + 8, :] += fold(dext_ref[0:tm, :] * _tap(gext_ref, shg_ref, HALO - (CW - 1) + j, 0, tm))

        def tile(rt, carry):
            r0 = pl.multiple_of(rt * RT, RT)
            dglu = jnp.zeros((RT, DC), F32)
            for j in range(CW):
                dglu = dglu + w_ref[j:j + 1, :] * _tap(dext_ref, shd_ref, CW - 1 - j, r0)
            cb = c_ref[pl.ds(r0, RT), :] + gb_ref[...]
            sg = _sigmoid(cb[:, DC:])
            dc = jnp.concatenate([dglu * sg, dglu * cb[:, :DC] * sg * (1.0 - sg)], axis=1)
            dc_ref[pl.ds(r0, RT), :] = dc.astype(BF16)
            a_gb[...] += fold(dc)
            return carry

        lax.fori_loop(0, tm // RT, tile, 0)

        @pl.when(i == steps - 1)
        def _():
            dgb_ref[...] = a_gb[...].sum(axis=0, keepdims=True)
            dw_ref[...] = a_w[...]
            dwb_ref[...] = a_wb[...].sum(axis=0, keepdims=True)
            dg_ref[...] = a_g[...].sum(axis=0, keepdims=True)
            db_ref[...] = a_b[...].sum(axis=0, keepdims=True)

    vec = lambda n: pl.BlockSpec((1, n), lambda i: (0, 0))
    nxt = lambda i: (jnp.minimum((i + 1) * hb, nhb - 1), 0)
    prv = lambda i: (jnp.maximum(i * hb - 1, 0), 0)
    return pl.pallas_call(
        body,
        name="conv_bwd",
        grid=(steps,),
        in_specs=[pl.BlockSpec((tm, DC), lambda i: (i, 0)), pl.BlockSpec((HALO, DC), nxt),
                  pl.BlockSpec((tm, DC), lambda i: (i, 0)), pl.BlockSpec((HALO, DC), nxt),
                  pl.BlockSpec((tm, 2 * DC), lambda i: (i, 0)), pl.BlockSpec((HALO, 2 * DC), prv),
                  vec(2 * DC), pl.BlockSpec((CW, DC), lambda i: (0, 0)), vec(DC), vec(DC)],
        out_specs=[pl.BlockSpec((tm, 2 * DC), lambda i: (i, 0)), vec(2 * DC),
                   pl.BlockSpec((CW * 8, DC), lambda i: (0, 0)), vec(DC), vec(DC), vec(DC)],
        out_shape=[jax.ShapeDtypeStruct((t, 2 * DC), BF16), jax.ShapeDtypeStruct((1, 2 * DC), F32),
                   jax.ShapeDtypeStruct((CW * 8, DC), F32), jax.ShapeDtypeStruct((1, DC), F32),
                   jax.ShapeDtypeStruct((1, DC), F32), jax.ShapeDtypeStruct((1, DC), F32)],
        scratch_shapes=[pltpu.VMEM((HALO + tm + 8, DC), F32), pltpu.VMEM((tm + HALO + 8, DC), F32),
                        pltpu.VMEM((7, HALO + tm, DC), F32), pltpu.VMEM((7, HALO + tm, DC), F32),
                        pltpu.VMEM((8, 2 * DC), F32), pltpu.VMEM((CW * 8, DC), F32),
                        pltpu.VMEM((8, DC), F32), pltpu.VMEM((8, DC), F32), pltpu.VMEM((8, DC), F32)],
        compiler_params=_params(("arbitrary",)),
    )(dcs, dcs, pre, pre, cin, cin, glu_b, dw_w, ln_g, ln_b)


def _place():
    x, y, c = lax.axis_index("x"), lax.axis_index("y"), lax.axis_index("c")
    return x, y, c


def _peers(x, y, c):
    out = []
    for k in range(1, NDEV):
        fx, fy, fc = (k >> 2) & 1, (k >> 1) & 1, k & 1
        px = 1 - x if fx else x
        py = 1 - y if fy else y
        pc = 1 - c if fc else c
        out.append((px, py, pc))
    return out


def _job_out_shapes(job):
    kind, arrays = job
    if kind == "gather":
        return [jax.ShapeDtypeStruct((a.shape[0], NDEV) + a.shape[1:], a.dtype) for a in arrays]
    return [jax.ShapeDtypeStruct((NDEV, a.shape[0]) + a.shape[2:], a.dtype) for a in arrays]


def _job_scratch(job):
    n = len(job[1])
    return [pltpu.SemaphoreType.DMA((n, NDEV - 1)), pltpu.SemaphoreType.DMA((n, NDEV - 1)),
            pltpu.SemaphoreType.DMA((n,))]


def _gather_parts(ins, outs, send_sems, recv_sems, local_sems):
    x, y, c = _place()
    me, sib = (x, y, c), (x, y, 1 - c)
    chips = [(1 - x, y), (x, 1 - y), (1 - x, 1 - y)]

    def copy(a, k, block, to, src=None):
        px, py, pc = block
        dst = outs[a].at[:, 4 * px + 2 * py + pc]
        return pltpu.make_async_remote_copy(
            src_ref=dst if src is None else src, dst_ref=dst,
            send_sem=send_sems.at[a, k], recv_sem=recv_sems.at[a, k], device_id=to, device_id_type=MESH)

    n = len(ins)
    local = [pltpu.make_async_copy(ins[a], outs[a].at[:, 4 * x + 2 * y + c], local_sems.at[a]) for a in range(n)]
    first = [[copy(a, 0, me, sib, src=ins[a])] + [copy(a, 1 + j, me, (*chip, c), src=ins[a])
                                                   for j, chip in enumerate(chips)] for a in range(n)]

    def start():
        for a in range(n):
            local[a].start()
            for cp in first[a]:
                cp.start()

    def finish():
        passed = []
        for j, chip in enumerate(chips):
            for a in range(n):
                copy(a, 1 + j, (*chip, c), me).wait_recv()
                cp = copy(a, 4 + j, (*chip, c), sib)
                cp.start()
                passed.append(cp)
        for a in range(n):
            copy(a, 0, sib, me).wait_recv()
            for j, chip in enumerate(chips):
                copy(a, 4 + j, (*chip, 1 - c), me).wait_recv()
        for a in range(n):
            for cp in first[a]:
                cp.wait_send()
            local[a].wait()
        for cp in passed:
            cp.wait_send()

    return start, finish


def _exchange_parts(ins, outs, send_sems, recv_sems, local_sems):
    x, y, c = _place()
    me = 4 * x + 2 * y + c
    n = len(ins)
    peers = _peers(x, y, c)
    local = [pltpu.make_async_copy(ins[a].at[:, me], outs[a].at[me], local_sems.at[a]) for a in range(n)]

    def copy(a, k):
        px, py, pc = peers[k]
        return pltpu.make_async_remote_copy(
            src_ref=ins[a].at[:, 4 * px + 2 * py + pc], dst_ref=outs[a].at[me],
            send_sem=send_sems.at[a, k], recv_sem=recv_sems.at[a, k], device_id=peers[k], device_id_type=MESH)

    def arrival(a, k):
        px, py, pc = peers[k]
        return pltpu.make_async_remote_copy(
            src_ref=ins[a].at[:, me], dst_ref=outs[a].at[4 * px + 2 * py + pc],
            send_sem=send_sems.at[a, k], recv_sem=recv_sems.at[a, k], device_id=peers[k], device_id_type=MESH)

    def start():
        for a in range(n):
            local[a].start()
            for k in range(NDEV - 1):
                copy(a, k).start()

    def finish():
        for a in range(n):
            for k in range(NDEV - 1):
                arrival(a, k).wait_recv()
        for a in range(n):
            for k in range(NDEV - 1):
                copy(a, k).wait_send()
            local[a].wait()

    return start, finish


def _call(body, *, name, grid, in_specs, args, out_specs, out_shape, scratch=(), sem=None, aliases=None, job=None):
    aliases = dict(aliases or {})
    if job is None:
        res = pl.pallas_call(
            body, name=name, grid=grid, in_specs=list(in_specs), out_specs=list(out_specs),
            out_shape=list(out_shape), scratch_shapes=list(scratch), input_output_aliases=aliases,
            compiler_params=_params(sem))(*args)
        return list(res), []
    kind, arrays = job
    n_in, n_out, n_scr, nj = len(args), len(out_shape), len(scratch), len(arrays)

    def wrapped(*refs):
        ins = refs[:n_in]
        jin = refs[n_in:n_in + nj]
        o0 = n_in + nj
        outs = refs[o0:o0 + n_out]
        jout = refs[o0 + n_out:o0 + n_out + nj]
        s0 = o0 + n_out + nj
        scr = refs[s0:s0 + n_scr]
        sems = refs[s0 + n_scr:]
        parts = _gather_parts if kind == "gather" else _exchange_parts
        start, finish = parts(jin, jout, *sems)
        if not grid:
            start()
            body(*ins, *outs, *scr)
            finish()
            return
        first = last = None
        for d, g in enumerate(grid):
            f, l = pl.program_id(d) == 0, pl.program_id(d) == g - 1
            first = f if first is None else jnp.logical_and(first, f)
            last = l if last is None else jnp.logical_and(last, l)
        pl.when(first)(start)
        body(*ins, *outs, *scr)
        pl.when(last)(finish)

    res = pl.pallas_call(
        wrapped, name=name, grid=grid, in_specs=list(in_specs) + [ANY] * nj,
        out_specs=list(out_specs) + [ANY] * nj, out_shape=list(out_shape) + _job_out_shapes(job),
        scratch_shapes=list(scratch) + _job_scratch(job), input_output_aliases=aliases,
        compiler_params=pltpu.CompilerParams(
            dimension_semantics=None if not grid else ("arbitrary",) * len(grid),
            vmem_limit_bytes=VMEM_LIMIT, has_side_effects=True))(*args, *arrays)
    return list(res[:n_out]), list(res[n_out:])


def _comm_only(name, job):
    return _call(lambda: None, name=name, grid=(), in_specs=[], args=[], out_specs=[], out_shape=[], job=job)[1]


def _sum_devices(name, parts, tr):
    _, r, c = parts.shape

    def body(p_ref, o_ref):
        acc = p_ref[0].astype(F32)
        for d in range(1, NDEV):
            acc = acc + p_ref[d].astype(F32)
        o_ref[...] = acc

    return pl.pallas_call(
        body,
        name=name,
        grid=(r // tr,),
        in_specs=[pl.BlockSpec((NDEV, tr, c), lambda i: (0, i, 0))],
        out_specs=pl.BlockSpec((tr, c), lambda i: (i, 0)),
        out_shape=jax.ShapeDtypeStruct((r, c), F32),
        compiler_params=_params(("parallel",)),
    )(parts)


def _adamw(name, w, g, m, v, tr=None):
    r, c = w.shape
    tr = r if tr is None else tr

    def body(w_ref, g_ref, m_ref, v_ref, d_ref, nm_ref, nv_ref):
        gv = g_ref[...]
        nm = ADAM_B1 * m_ref[...] + (1.0 - ADAM_B1) * gv
        nv = ADAM_B2 * v_ref[...] + (1.0 - ADAM_B2) * (gv * gv)
        m_hat = nm / (1.0 - ADAM_B1 ** ADAM_STEP)
        v_hat = nv / (1.0 - ADAM_B2 ** ADAM_STEP)
        d_ref[...] = -ADAM_LR * (m_hat / (jnp.sqrt(v_hat) + ADAM_EPS) + ADAM_WD * w_ref[...])
        nm_ref[...] = nm
        nv_ref[...] = nv

    spec = pl.BlockSpec((tr, c), lambda i: (i, 0))
    return pl.pallas_call(
        body,
        name=name,
        grid=(r // tr,),
        in_specs=[spec] * 4,
        out_specs=[spec] * 3,
        out_shape=[jax.ShapeDtypeStruct((r, c), F32)] * 3,
        compiler_params=_params(("parallel",)),
    )(w, g, m, v)


SMALL = ["ffn1_norm_pre", "ffn1_norm_post", "mix_norm_pre", "gate_bias", "rel_table", "conv_glu_bias",
         "conv_dw_b", "conv_ln_g", "conv_ln_b", "mix_norm_post", "ffn2_norm_pre", "ffn2_norm_post"]
SMALL_ROWS = 24
DW_ROWS = 32


def _pack_small(vals):
    rows = []
    for name in SMALL:
        v = vals[name]
        if name == "rel_table":
            v = v.reshape(NH, -1)
            rows.append(jnp.pad(v, ((0, 0), (0, D - v.shape[1]))))
        else:
            v = v.reshape(-1)
            v = jnp.pad(v, (0, (-v.shape[0]) % D))
            rows.append(v.reshape(-1, D))
    out = jnp.concatenate(rows, axis=0)
    return jnp.pad(out, ((0, SMALL_ROWS - out.shape[0]), (0, 0)))


def _unpack_small(pack, shapes):
    out = {}
    r = 0
    for name in SMALL:
        shp = shapes[name]
        n = 1
        for s in shp:
            n *= s
        if name == "rel_table":
            out[name] = pack[r:r + NH, :NREL].reshape(shp)
            r += NH
        else:
            nr = -(-n // D)
            out[name] = pack[r:r + nr].reshape(-1)[:n].reshape(shp)
            r += nr
    return out


def kernel(x, ffn1_norm_pre, ffn1_w_gate, ffn1_w_up, ffn1_w_down, ffn1_norm_post, mix_norm_pre, w_in, gate_bias, rel_table, w_attn_out, conv_glu_bias, conv_dw_w, conv_dw_b, conv_ln_g, conv_ln_b, conv_w_out, w_out, mix_norm_post, ffn2_norm_pre, ffn2_w_gate, ffn2_w_up, ffn2_w_down, ffn2_norm_post, loss_target, m_ffn1_norm_pre, m_ffn1_w_gate, m_ffn1_w_up, m_ffn1_w_down, m_ffn1_norm_post, m_mix_norm_pre, m_w_in, m_gate_bias, m_rel_table, m_w_attn_out, m_conv_glu_bias, m_conv_dw_w, m_conv_dw_b, m_conv_ln_g, m_conv_ln_b, m_conv_w_out, m_w_out, m_mix_norm_post, m_ffn2_norm_pre, m_ffn2_w_gate, m_ffn2_w_up, m_ffn2_w_down, m_ffn2_norm_post, v_ffn1_norm_pre, v_ffn1_w_gate, v_ffn1_w_up, v_ffn1_w_down, v_ffn1_norm_post, v_mix_norm_pre, v_w_in, v_gate_bias, v_rel_table, v_w_attn_out, v_conv_glu_bias, v_conv_dw_w, v_conv_dw_b, v_conv_ln_g, v_conv_ln_b, v_conv_w_out, v_w_out, v_mix_norm_post, v_ffn2_norm_pre, v_ffn2_w_gate, v_ffn2_w_up, v_ffn2_w_down, v_ffn2_norm_post):
    return _step(dict(locals()))


WEIGHTS = ["ffn1_norm_pre", "ffn1_w_gate", "ffn1_w_up", "ffn1_w_down", "ffn1_norm_post", "mix_norm_pre", "w_in",
           "gate_bias", "rel_table", "w_attn_out", "conv_glu_bias", "conv_dw_w", "conv_dw_b", "conv_ln_g",
           "conv_ln_b", "conv_w_out", "w_out", "mix_norm_post", "ffn2_norm_pre", "ffn2_w_gate", "ffn2_w_up",
           "ffn2_w_down", "ffn2_norm_post"]
FS = FF // NDEV
PS = (3 * DA + 2 * DC + 2 * D) // NDEV
OS = D // NDEV


def _local_step(xs, target, w, rel_table):
    t = xs.shape[0]
    vec = lambda n: w[n].reshape(1, -1)
    g_pre1, g_post1, g_mix, g_mixp = vec("ffn1_norm_pre"), vec("ffn1_norm_post"), vec("mix_norm_pre"), vec("mix_norm_post")
    g_pre2, g_post2 = vec("ffn2_norm_pre"), vec("ffn2_norm_post")
    gate_b, glu_b = vec("gate_bias"), vec("conv_glu_bias")
    dw_b, ln_g, ln_b = vec("conv_dw_b"), vec("conv_ln_g"), vec("conv_ln_b")

    tr = lambda a: jnp.transpose(a[0]).astype(BF16)
    sh_gu1 = jnp.stack([tr(w["ffn1_w_gate"]), tr(w["ffn1_w_up"])])
    sh_mid = [w["ffn1_w_down"].astype(BF16), tr(w["w_in"])[None], w["w_out"].astype(BF16),
              jnp.stack([tr(w["w_attn_out"]), tr(w["conv_w_out"])]),
              jnp.pad(w["conv_dw_w"][0, :, 0, :], ((0, DW_ROWS - CW), (0, 0)))[None]]
    sh_2 = jnp.stack([tr(w["ffn2_w_gate"]), tr(w["ffn2_w_up"]), w["ffn2_w_down"][0].astype(BF16)])

    (n1,), (w_gu1,) = _rowwise("pre1", lambda xv, g: ((_rms(xv)[0] * g),), [xs], [g_pre1], [(D, BF16)],
                               job=("gather", [sh_gu1]))
    w_gu1 = w_gu1.reshape(2, FF, D)
    (a1, b1, s1), (w_d1, wb, wc, wd, we) = _ffn_up("ffn1_up", n1, w_gu1, 0, job=("gather", sh_mid))
    w_d1, wb, wc, wd = w_d1.reshape(1, FF, D), wb.reshape(NDEV * PS, D), wc.reshape(D, D), wd.reshape(2, D, DA)
    dw_full = jnp.transpose(we[0], (1, 0, 2)).reshape(DW_ROWS, DC)[:CW]
    f1 = _ffn_down("ffn1_down", s1, w_d1, 0)

    def post1(xv, fv, gp, gm):
        h = xv + 0.5 * (_rms(fv)[0] * gp)
        return h, _rms(h)[0] * gm

    h1, u = _rowwise("post1", post1, [xs, f1], [g_post1, g_mix], [(D, F32), (D, BF16)])

    qkv = _mm_simple("proj_qkv", u, wb, NT, BF16, 512, 512, b_row0=0, b_rows=3 * DA)
    cin = _mm_simple("proj_conv", u, wb, NT, F32, 512, 512, b_row0=3 * DA, b_rows=2 * DC)
    gg = _mm_simple("proj_gate", u, wb, NT, F32, 512, 512, b_row0=3 * DA + 2 * DC, b_rows=2 * D)

    bias = _relbias_fwd(jnp.pad(rel_table[0], ((0, 0), (0, 384 - NREL))))
    kvp = jnp.pad(qkv[:, DA:], ((KPAD, 0), (0, 0)))
    att, (w_2,) = _attn_fwd(qkv, kvp, bias, job=("gather", [sh_2]))
    w_2 = w_2.reshape(3, FF, D)
    cs, pre = _conv_fwd(cin, glu_b, dw_full, dw_b, ln_g, ln_b)
    ya = _mm_simple("attn_out", att, wd[0], NT, F32, 512, 512)
    yb = _mm_simple("conv_out", cs, wd[1], NT, F32, 512, 512)

    def merge(yav, ybv, gv, gb):
        gates = _sigmoid(gv + gb)
        return (gates[:, :D] * yav + gates[:, D:] * ybv,)

    (merged,) = _rowwise("merge", merge, [ya, yb, gg], [gate_b], [(D, BF16)])
    mm_ = _mm_simple("mix_out", merged, wc, NN, F32, 512, 512)

    def postm(hv, mv, gp, g2):
        h = hv + _rms(mv)[0] * gp
        return h, _rms(h)[0] * g2

    h2, n2 = _rowwise("postm", postm, [h1, mm_], [g_mixp, g_pre2], [(D, F32), (D, BF16)])
    a2, b2, s2 = _ffn_up("ffn2_up", n2, w_2, 0)
    f2 = _ffn_down("ffn2_down", s2, w_2, 2)

    def post2(hv, fv, tv, gp):
        fh, r = _rms(fv)
        yv = hv + 0.5 * (fh * gp)
        err = yv - tv
        dy = err * (1.0 / D)
        df, dg = _rms_bwd(fh, r, gp, 0.5 * dy)
        return dy, df, (0.5 / D) * (err * err), dg

    dy, df2, loss_row, d_post2 = _rowwise("post2", post2, [h2, f2, target], [g_post2],
                                          [(D, F32), (D, BF16)], [D, D])

    tmw = 1408
    nfi = FF // tmw

    def wgrad_down(name, s, df, shape, part, carry_buf, job=None):
        return _wgrad(name, (s, (512, tmw), lambda i, j, q: (q, i)), df,
                      (None, tmw, D), lambda i, j, q: (part, i, 0), nfi, carry_buf, shape, job=job)

    def wgrad_gate_up(name, dab, nrm, shape, carry_buf, job=None):
        return _wgrad(name, (dab, (None, 512, tmw), lambda i, j, q: (i // nfi, q, i % nfi)), nrm,
                      (None, tmw, D), lambda i, j, q: (i // nfi, i % nfi, 0), 2 * nfi, carry_buf, shape, job=job)

    g_2 = wgrad_down("ffn2_wgrad_d", s2, df2, (3, FF, D), 2, None)
    dab2 = _ffn_bwd_act("ffn2_bwd_act", df2, w_2, 2, a2, b2)
    g_2 = wgrad_gate_up("ffn2_wgrad_gu", dab2, n2, (3, FF, D), g_2)
    dn2 = _ffn_bwd_in("ffn2_bwd_in", dab2, w_2, 0)

    def bwd_pre2(hv, dnv, dyv, mv, g2, gp):
        hh, r = _rms(hv)
        dx, dg2 = _rms_bwd(hh, r, g2, dnv)
        dh = dyv + dx
        mh, rm = _rms(mv)
        dm, dgp = _rms_bwd(mh, rm, gp, dh)
        return dh, dm, dg2, dgp

    dh2, dm, d_pre2, d_mixp = _rowwise("bwd_pre2", bwd_pre2, [h2, dn2, dy, mm_], [g_pre2, g_mixp],
                                       [(D, F32), (D, BF16)], [D, D])
    dmerged = _mm_simple("mix_out_bwd", dm, wc, NT, F32, 512, 512)

    def merge_bwd(dmv, yav, ybv, gv, gb):
        gates = _sigmoid(gv + gb)
        ga, gbb = gates[:, :D], gates[:, D:]
        dgg = jnp.concatenate([dmv * yav * ga * (1.0 - ga), dmv * ybv * gbb * (1.0 - gbb)], axis=1)
        return dmv * ga, dmv * gbb, dgg, dgg

    dya, dyb, dgg, d_gate_b = _rowwise("merge_bwd", merge_bwd, [dmerged, ya, yb, gg], [gate_b],
                                       [(D, BF16), (D, BF16), (2 * D, BF16)], [2 * D])
    datt = _mm_simple("attn_out_bwd", dya, wd[0], NN, BF16, 512, 512)
    dcs = _mm_simple("conv_out_bwd", dyb, wd[1], NN, F32, 512, 512)
    g_c = _wgrad("mix_out_wgrad", (merged, (512, D), lambda i, j, q: (q, 0)), dm,
                 (D, D), lambda i, j, q: (0, 0), 1, None, (D, D))
    g_d = _wgrad("attn_out_wgrad", (dya, (512, D), lambda i, j, q: (q, 0)), att,
                 (None, D, DA), lambda i, j, q: (0, 0, 0), 1, None, (2, D, DA))
    g_d = _wgrad("conv_out_wgrad", (dyb, (512, D), lambda i, j, q: (q, 0)), cs,
                 (None, D, DA), lambda i, j, q: (1, 0, 0), 1, g_d, (2, D, DA))
    (dq, dkvp, dbias), (x_2, x_c, x_d) = _attn_bwd(
        qkv, kvp, bias, datt,
        job=("exchange", [g_2.reshape(3, NDEV, FS, D), g_c.reshape(1, NDEV, OS, D), g_d.reshape(2, NDEV, OS, DA)]))
    d_rel = _relbias_bwd(dbias)
    dcin, d_glu_b, d_dw8, d_dw_b, d_ln_g, d_ln_b = _conv_bwd(dcs, pre, cin, glu_b, dw_full, ln_g, ln_b)
    g_dw = jnp.pad(d_dw8, ((0, 8 * (DW_ROWS - CW)), (0, 0)))
    g_dw = g_dw.reshape(DW_ROWS * 8, NDEV, DC // NDEV).transpose(1, 0, 2)

    gb_shape = (NDEV * PS, D)
    g_b = _wgrad("proj_wgrad_q", (dq, (512, DA), lambda i, j, q: (q, 0)), u,
                 (DA, D), lambda i, j, q: (0, 0), 1, None, gb_shape)
    g_b = _wgrad("proj_wgrad_kv", (dkvp, (512, DA), lambda i, j, q: (q + KPAD // 512, i)), u,
                 (DA, D), lambda i, j, q: (1 + i, 0), 2, g_b, gb_shape)
    g_b = _wgrad("proj_wgrad_c", (dcin, (512, DA), lambda i, j, q: (q, i)), u,
                 (DA, D), lambda i, j, q: (3 + i, 0), 2, g_b, gb_shape)
    g_b = _wgrad("proj_wgrad_g", (dgg, (512, DA), lambda i, j, q: (q, i)), u,
                 (DA, D), lambda i, j, q: (5 + i, 0), 4, g_b, gb_shape)

    tmu = 512
    a_ops = [(dq, (tmu, DA), lambda i, j, q: (i, 0)),
             (dkvp, (tmu, 2 * DA), lambda i, j, q: (i + KPAD // tmu, 0)),
             (dcin, (tmu, 2 * DC), lambda i, j, q: (i, 0)),
             (dgg, (tmu, 2 * D), lambda i, j, q: (i, 0))]
    whole = lambda i, j, q: (0, 0)
    b_ops = [(wb[:DA], (DA, D), whole), (wb[DA:3 * DA], (2 * DA, D), whole),
             (wb[3 * DA:3 * DA + 2 * DC], (2 * DC, D), whole), (wb[3 * DA + 2 * DC:], (2 * D, D), whole)]
    du = _mm("proj_bwd", (t // tmu, 1, 1), a_ops, b_ops, [[(0, 0), (1, 1), (2, 2), (3, 3)]], NN, _first,
             [((t, D), F32, (tmu, D), lambda i, j, q: (i, 0))])[0]

    def bwd_mix(hv, duv, dhv, fv, gm, gp):
        hh, r = _rms(hv)
        dx, dgm = _rms_bwd(hh, r, gm, duv)
        dh = dhv + dx
        fh, rf = _rms(fv)
        df, dgp = _rms_bwd(fh, rf, gp, 0.5 * dh)
        return dh, df, dgm, dgp

    dh1, df1, d_mix, d_post1 = _rowwise("bwd_mix", bwd_mix, [h1, du, dh2, f1], [g_mix, g_post1],
                                        [(D, F32), (D, BF16)], [D, D])
    g_d1 = wgrad_down("ffn1_wgrad_d", s1, df1, (1, FF, D), 0, None)
    dab1, (x_b, x_dw) = _ffn_bwd_act("ffn1_bwd_act", df1, w_d1, 0, a1, b1,
                                     job=("exchange", [g_b.reshape(1, NDEV, PS, D), g_dw[None]]))
    g_gu1, (x_d1,) = wgrad_gate_up("ffn1_wgrad_gu", dab1, n1, (2, FF, D), None,
                                   job=("exchange", [g_d1.reshape(1, NDEV, FS, D)]))
    dn1, (x_gu1,) = _ffn_bwd_in("ffn1_bwd_in", dab1, w_gu1, 0, job=("exchange", [g_gu1.reshape(2, NDEV, FS, D)]))

    def bwd_pre1(xv, dnv, dhv, g1):
        xh, r = _rms(xv)
        dx, dg1 = _rms_bwd(xh, r, g1, dnv)
        return dhv + dx, dg1

    dx, d_pre1 = _rowwise("bwd_pre1", bwd_pre1, [xs, dn1, dh1], [g_pre1], [(D, F32)], [D])

    small_g = {"ffn1_norm_pre": d_pre1, "ffn1_norm_post": d_post1, "mix_norm_pre": d_mix, "gate_bias": d_gate_b,
               "rel_table": d_rel[:, :NREL], "conv_glu_bias": d_glu_b, "conv_dw_b": d_dw_b, "conv_ln_g": d_ln_g,
               "conv_ln_b": d_ln_b, "mix_norm_post": d_mixp, "ffn2_norm_pre": d_pre2, "ffn2_norm_post": d_post2}
    return loss_row, dx, (x_gu1, x_d1, x_2, x_b, x_c, x_d, x_dw), small_g


def _step(args):
    names = WEIGHTS
    w = {n: args[n] for n in names}
    fs, ps, os_ = FS, PS, OS
    conv_dw_w = args["conv_dw_w"]
    loss_row, dx, (x_gu1, x_d1, x_2, x_b, x_c, x_d, x_dw), small_g = _local_step(
        args["x"][0], args["loss_target"][0], w, args["rel_table"])

    g_small = _pack_small(small_g)
    (x_s,) = _comm_only("gather_small_grads", ("gather", [g_small[None]]))

    s_gu1 = _sum_devices("sum_ffn1_gu", x_gu1.reshape(NDEV, 2 * fs, D), fs).reshape(2, fs, D)
    s_d1 = _sum_devices("sum_ffn1_d", x_d1.reshape(NDEV, fs, D), fs)
    s_2 = _sum_devices("sum_ffn2", x_2.reshape(NDEV, 3 * fs, D), fs).reshape(3, fs, D)
    s_b = _sum_devices("sum_proj", x_b.reshape(NDEV, ps, D), ps)
    s_c = _sum_devices("sum_mix", x_c.reshape(NDEV, os_, D), os_)
    s_d = _sum_devices("sum_out", x_d.reshape(NDEV, 2 * os_, DA), 2 * os_).reshape(2, os_, DA)
    s_dw = _sum_devices("sum_dw", x_dw.reshape(NDEV, DW_ROWS * 8, DC // NDEV), DW_ROWS * 8)
    s_s = _sum_devices("sum_small", x_s.reshape(NDEV, SMALL_ROWS, D), SMALL_ROWS)

    grads = {
        "ffn1_w_gate": jnp.transpose(s_gu1[0])[None], "ffn1_w_up": jnp.transpose(s_gu1[1])[None], "ffn1_w_down": s_d1[None],
        "ffn2_w_gate": jnp.transpose(s_2[0])[None], "ffn2_w_up": jnp.transpose(s_2[1])[None], "ffn2_w_down": s_2[2][None],
        "w_in": jnp.transpose(s_b)[None], "w_out": s_c[None],
        "w_attn_out": jnp.transpose(s_d[0])[None], "conv_w_out": jnp.transpose(s_d[1])[None],
    }
    shapes = {n: w[n].shape for n in SMALL}
    grads.update(_unpack_small(s_s, shapes))

    deltas, new_m, new_v = {}, {}, {}
    big = ["ffn1_w_gate", "ffn1_w_up", "ffn1_w_down", "w_in", "w_attn_out", "conv_w_out", "w_out",
           "ffn2_w_gate", "ffn2_w_up", "ffn2_w_down"]
    for n in big:
        shp = w[n].shape
        two = lambda a: a.reshape(shp[1], shp[2])
        rows = shp[1]
        tr_ = rows // 2 if rows % 16 == 0 else rows
        d_, m_, v_ = _adamw("adamw_" + n, two(w[n]), two(grads[n]), two(args["m_" + n]), two(args["v_" + n]), tr_)
        deltas[n], new_m[n], new_v[n] = d_.reshape(shp), m_.reshape(shp), v_.reshape(shp)

    wp = _pack_small({n: w[n] for n in SMALL})
    mp = _pack_small({n: args["m_" + n] for n in SMALL})
    vp = _pack_small({n: args["v_" + n] for n in SMALL})
    d_, m_, v_ = _adamw("adamw_small", wp, s_s, mp, vp)
    for dst, pack in ((deltas, d_), (new_m, m_), (new_v, v_)):
        dst.update(_unpack_small(pack, shapes))

    g_dw_own = _fold8("fold_dw", s_dw)[:CW]
    grads["conv_dw_w"] = g_dw_own.reshape(1, CW, 1, DC // NDEV)
    flat = lambda a: a.reshape(CW, DC // NDEV)
    d_, m_, v_ = _adamw("adamw_dw", flat(conv_dw_w), g_dw_own, flat(args["m_conv_dw_w"]), flat(args["v_conv_dw_w"]))
    shp = conv_dw_w.shape
    deltas["conv_dw_w"], new_m["conv_dw_w"], new_v["conv_dw_w"] = d_.reshape(shp), m_.reshape(shp), v_.reshape(shp)

    loss = lax.psum(jnp.sum(loss_row), ("x", "y", "c"))
    return (loss, dx[None], *[grads[n] for n in names], *[deltas[n] for n in names],
            *[new_m[n] for n in names], *[new_v[n] for n in names])


def _fold8(name, a):
    r8, c = a.shape

    def body(a_ref, o_ref):
        o_ref[...] = a_ref[...].reshape(r8 // 8, 8, c).sum(axis=1)

    return pl.pallas_call(
        body,
        name=name,
        out_shape=jax.ShapeDtypeStruct((r8 // 8, c), F32),
        in_specs=[pl.BlockSpec(memory_space=pltpu.VMEM)],
        out_specs=pl.BlockSpec(memory_space=pltpu.VMEM),
        compiler_params=_params(),
    )(a)
```

```python
import functools

import jax
import jax.numpy as jnp
from jax import lax
from jax.experimental import pallas as pl
from jax.experimental.pallas import tpu as pltpu

F32 = jnp.float32
BF16 = jnp.bfloat16

D = 1024
FF = 2816
DA = 512
DC = 512
NH = 8
CHUNK = 64
LEFT = 8
CW = 31
NREL = 257
EPS = 1e-6
NDEV = 8

QB = 2 * CHUNK
KW = (LEFT + 2) * CHUNK
KPAD = LEFT * CHUNK
RELW = 768
HALO = 32

TM = 512
VMEM_LIMIT = 56 * 1024 * 1024

ADAM_LR, ADAM_B1, ADAM_B2, ADAM_EPS, ADAM_WD, ADAM_STEP = 0.001, 0.9, 0.999, 1e-08, 0.01, 10

NT = (((1,), (1,)), ((), ()))
NN = (((1,), (0,)), ((), ()))
TN = (((0,), (0,)), ((), ()))

MESH = pl.DeviceIdType.MESH
ANY = pl.BlockSpec(memory_space=pl.ANY)


def _params(sem=None, vmem=VMEM_LIMIT):
    return pltpu.CompilerParams(dimension_semantics=sem, vmem_limit_bytes=vmem)


def _sigmoid(x):
    return 0.5 * jnp.tanh(0.5 * x) + 0.5


def _mm(name, grid, a_ops, b_ops, groups, dims, epi, outs, extras=(), carry=None, job=None):
    nk = grid[2]
    na, nb, ne, no, ng = len(a_ops), len(b_ops), len(extras), len(outs), len(groups)
    nc = 0 if carry is None else 1

    def body(*refs):
        a_refs = refs[:na]
        b_refs = refs[na:na + nb]
        e_refs = refs[na + nb:na + nb + ne]
        o_refs = refs[na + nb + ne + nc:na + nb + ne + nc + no]
        acc_refs = refs[na + nb + ne + nc + no:]
        k = pl.program_id(2)
        prods = []
        for grp in groups:
            p = None
            for ai, bi in grp:
                t = lax.dot_general(a_refs[ai][...], b_refs[bi][...], dims, preferred_element_type=F32)
                p = t if p is None else p + t
            prods.append(p)

        def finish(vals):
            res = epi(vals, [e[...] for e in e_refs])
            for o, r in zip(o_refs, res):
                o[...] = r.astype(o.dtype)

        if nk == 1:
            finish(prods)
        else:
            @pl.when(k == 0)
            def _():
                for acc, p in zip(acc_refs, prods):
                    acc[...] = p

            @pl.when(k > 0)
            def _():
                for acc, p in zip(acc_refs, prods):
                    acc[...] += p

            @pl.when(k == nk - 1)
            def _():
                finish([acc[...] for acc in acc_refs])

    in_specs = [pl.BlockSpec(blk, im) for _, blk, im in list(a_ops) + list(b_ops) + list(extras)]
    args = [arr for arr, _, _ in list(a_ops) + list(b_ops) + list(extras)]
    aliases = {}
    if carry is not None:
        in_specs.append(ANY)
        args.append(carry[0])
        aliases = {len(args) - 1: carry[1]}
    scratch = []
    if nk > 1:
        for _ in range(ng):
            blk = tuple(b for b in outs[0][2] if b is not None)
            scratch.append(pltpu.VMEM(blk, F32))
    res, jres = _call(
        body, name=name, grid=grid, in_specs=in_specs, args=args,
        out_specs=[pl.BlockSpec(blk, im) for _, _, blk, im in outs],
        out_shape=[jax.ShapeDtypeStruct(shp, dt) for shp, dt, _, _ in outs],
        scratch=scratch, sem=("parallel", "parallel", "arbitrary"), aliases=aliases, job=job)
    return res if job is None else (res, jres)


def _first(accs, extras):
    return (accs[0],)


def _mm_simple(name, a, b, dims, out_dtype, tm, tn, b_row0=0, b_rows=None):
    m, kk = a.shape
    tm = min(tm, m)
    if dims is NT:
        n = b.shape[0] if b_rows is None else b_rows
        assert b.shape[1] == kk and b_row0 % tn == 0
        b_op = (b, (tn, kk), lambda i, j, q: (j + b_row0 // tn, 0))
    else:
        assert b.shape[0] == kk
        n = b.shape[1]
        b_op = (b, (kk, tn), lambda i, j, q: (0, j))
    a_op = (a, (tm, kk), lambda i, j, q: (i, 0))
    out = ((m, n), out_dtype, (tm, tn), lambda i, j, q: (i, j))
    return _mm(name, (m // tm, n // tn, 1), [a_op], [b_op], [[(0, 0)]], dims, _first, [out])[0]


def _rowwise(name, fn, tiled, params, outs, partials=(), tm=TM, job=None):
    t = tiled[0].shape[0]
    steps = t // tm
    nt, npar, no, npart = len(tiled), len(params), len(outs), len(partials)

    def body(*refs):
        t_refs = refs[:nt]
        p_refs = refs[nt:nt + npar]
        o_refs = refs[nt + npar:nt + npar + no]
        s_refs = refs[nt + npar + no:nt + npar + no + npart]
        acc_refs = refs[nt + npar + no + npart:]
        i = pl.program_id(0)
        res = fn(*[r[...].astype(F32) for r in t_refs], *[r[...] for r in p_refs])
        for o, r in zip(o_refs, res[:no]):
            o[...] = r.astype(o.dtype)

        @pl.when(i == 0)
        def _():
            for acc in acc_refs:
                acc[...] = jnp.zeros_like(acc)

        for acc, r in zip(acc_refs, res[no:]):
            acc[...] += r.reshape(tm // 8, 8, r.shape[-1]).sum(axis=0)

        @pl.when(i == steps - 1)
        def _():
            for s, acc in zip(s_refs, acc_refs):
                s[...] = acc[...].sum(axis=0, keepdims=True)

    in_specs = [pl.BlockSpec((tm, a.shape[1]), lambda i: (i, 0)) for a in tiled]
    in_specs += [pl.BlockSpec(p.shape, lambda i: (0, 0)) for p in params]
    out_specs = [pl.BlockSpec((tm, c), lambda i: (i, 0)) for c, _ in outs]
    out_specs += [pl.BlockSpec((1, c), lambda i: (0, 0)) for c in partials]
    out_shape = [jax.ShapeDtypeStruct((t, c), dt) for c, dt in outs]
    out_shape += [jax.ShapeDtypeStruct((1, c), F32) for c in partials]
    res, jres = _call(body, name=name, grid=(steps,), in_specs=in_specs, args=[*tiled, *params], out_specs=out_specs,
                      out_shape=out_shape, scratch=[pltpu.VMEM((8, c), F32) for c in partials], sem=("arbitrary",),
                      job=job)
    return res if job is None else (res, jres)


def _rms(x):
    r = lax.rsqrt(jnp.mean(x * x, axis=-1, keepdims=True) + EPS)
    return x * r, r


def _rms_bwd(xhat, r, g, dy):
    dxh = dy * g
    dx = r * (dxh - xhat * jnp.mean(dxh * xhat, axis=-1, keepdims=True))
    return dx, dy * xhat


def _ffn_up(name, n, wa, part, tm=512, tf=1408, job=None):
    t = n.shape[0]

    def epi(accs, extras):
        a, b = accs
        return a, b, a * _sigmoid(a) * b

    a_op = (n, (tm, D), lambda f, i, q: (i, 0))
    b_ops = [(wa, (None, tf, D), lambda f, i, q: (part, f, 0)),
             (wa, (None, tf, D), lambda f, i, q: (part + 1, f, 0))]
    outs = [((t, FF), BF16, (tm, tf), lambda f, i, q: (i, f))] * 3
    return _mm(name, (FF // tf, t // tm, 1), [a_op], b_ops, [[(0, 0)], [(0, 1)]], NT, epi, outs, job=job)


def _ffn_down(name, s, wa, part, tm=512):
    t = s.shape[0]
    a_op = (s, (tm, FF), lambda i, j, q: (i, 0))
    b_op = (wa, (None, FF, D), lambda i, j, q: (part, 0, 0))
    out = ((t, D), F32, (tm, D), lambda i, j, q: (i, 0))
    return _mm(name, (t // tm, 1, 1), [a_op], [b_op], [[(0, 0)]], NN, _first, [out])[0]


def _ffn_bwd_act(name, df, wa, part, a, b, tm=512, tf=1408, job=None):
    t = df.shape[0]

    def epi(accs, extras):
        ds = accs[0]
        av, bv = extras[0].astype(F32), extras[1].astype(F32)
        sg = _sigmoid(av)
        da = ds * bv * (sg * (1.0 + av * (1.0 - sg)))
        db = ds * (av * sg)
        return (jnp.stack([da, db]),)

    a_op = (df, (tm, D), lambda f, i, q: (i, 0))
    b_op = (wa, (None, tf, D), lambda f, i, q: (part, f, 0))
    extras = [(a, (tm, tf), lambda f, i, q: (i, f)), (b, (tm, tf), lambda f, i, q: (i, f))]
    out = ((2, t, FF), BF16, (2, tm, tf), lambda f, i, q: (0, i, f))
    res = _mm(name, (FF // tf, t // tm, 1), [a_op], [b_op], [[(0, 0)]], NT, epi, [out], extras, job=job)
    return res[0] if job is None else (res[0][0], res[1])


def _ffn_bwd_in(name, dab, wa, part, tm=1024, job=None):
    t = dab.shape[1]
    tm = min(tm, t)
    a_op = (dab, (None, tm, FF), lambda i, j, q: (q, i, 0))
    b_op = (wa, (None, FF, D), lambda i, j, q: (part + q, 0, 0))
    out = ((t, D), F32, (tm, D), lambda i, j, q: (i, 0))
    res = _mm(name, (t // tm, 1, 2), [a_op], [b_op], [[(0, 0)]], NN, _first, [out], job=job)
    return res[0] if job is None else (res[0][0], res[1])


def _wgrad(name, dy_op, x, out_block, out_map, gi, carry_buf, out_shape, tk=512, job=None):
    t, c = x.shape
    b_op = (x, (tk, c), lambda i, j, q: (q, 0))
    out = (out_shape, BF16, out_block, out_map)
    carry = None if carry_buf is None else (carry_buf, 0)
    res = _mm(name, (gi, 1, t // tk), [dy_op], [b_op], [[(0, 0)]], TN, _first, [out], carry=carry, job=job)
    return res[0] if job is None else (res[0][0], res[1])


def _rel_onehot():
    j = lax.broadcasted_iota(jnp.int32, (384, RELW), 0)
    xx = lax.broadcasted_iota(jnp.int32, (384, RELW), 1)
    idx = jnp.clip(640 - xx, -128, 128) + 128
    return (j == idx).astype(F32)


def _band_ok():
    r = lax.broadcasted_iota(jnp.int32, (QB, KW), 0) >> 6
    c = lax.broadcasted_iota(jnp.int32, (QB, KW), 1) >> 6
    return (c >= r) & (c <= r + LEFT)


def _relbias_fwd(table):
    def body(t_ref, o_ref):
        rev = jnp.dot(t_ref[...], _rel_onehot(), precision=lax.Precision.HIGHEST, preferred_element_type=F32)
        for r in range(QB):
            row = pltpu.roll(rev, (RELW - (128 - r)) % RELW, 1)[:, :KW]
            rr = lax.broadcasted_iota(jnp.int32, (NH, KW), 1) >> 6
            ok = (rr >= (r // CHUNK)) & (rr <= (r // CHUNK) + LEFT)
            row = jnp.where(ok, row, -1e30)
            for h in range(NH):
                o_ref[h * QB + r:h * QB + r + 1, :] = row[h:h + 1, :]

    return pl.pallas_call(
        body,
        name="relbias_fwd",
        out_shape=jax.ShapeDtypeStruct((NH * QB, KW), F32),
        in_specs=[pl.BlockSpec(memory_space=pltpu.VMEM)],
        out_specs=pl.BlockSpec(memory_space=pltpu.VMEM),
        compiler_params=_params(),
    )(table)


def _relbias_bwd(dbias):
    def body(d_ref, o_ref, acc_ref):
        for h in range(NH):
            acc = jnp.zeros((1, RELW), F32)
            for r in range(QB):
                row = d_ref[h * QB + r:h * QB + r + 1, :]
                wide = jnp.concatenate([row, jnp.zeros((1, RELW - KW), F32)], axis=1)
                acc = acc + pltpu.roll(wide, 128 - r, 1)
            acc_ref[h:h + 1, :] = acc
        o_ref[...] = lax.dot_general(acc_ref[...], _rel_onehot(), NT, precision=lax.Precision.HIGHEST,
                                     preferred_element_type=F32)

    return pl.pallas_call(
        body,
        name="relbias_bwd",
        out_shape=jax.ShapeDtypeStruct((NH, 384), F32),
        in_specs=[pl.BlockSpec(memory_space=pltpu.VMEM)],
        out_specs=pl.BlockSpec(memory_space=pltpu.VMEM),
        scratch_shapes=[pltpu.VMEM((NH, RELW), F32)],
        compiler_params=_params(),
    )(dbias)


def _stack_heads(x_pair):
    first = lax.broadcasted_iota(jnp.int32, (1, 128), 1) < 64
    zero = jnp.zeros_like(x_pair)
    return jnp.concatenate([jnp.where(first, x_pair, zero), jnp.where(first, zero, x_pair)], axis=0), first


def _pair_probs(qs, kw, bias, key_ok):
    s = lax.dot_general(qs, kw, NT, preferred_element_type=F32) + bias
    if key_ok is not None:
        s = jnp.where(key_ok, s, -1e30)
    e = jnp.exp(s - jnp.max(s, axis=-1, keepdims=True))
    return e * (1.0 / jnp.sum(e, axis=-1, keepdims=True))


def _attn_fwd(qkv, kvp, bias, job=None):
    t = qkv.shape[0]

    def body(q_ref, kv_ref, b_ref, o_ref):
        i = pl.program_id(0)

        def run(masked):
            start = pl.multiple_of(i * QB, QB)
            col = lax.broadcasted_iota(jnp.int32, (1, KW), 1)
            key_ok = (col >= KPAD - i * QB) if masked else None
            for pair in range(4):
                lo = pair * 128
                kw = kv_ref[pl.ds(start, KW), lo:lo + 128]
                vw = kv_ref[pl.ds(start, KW), DA + lo:DA + lo + 128]
                qs, first = _stack_heads(q_ref[:, lo:lo + 128])
                p = _pair_probs(qs * 0.125, kw, b_ref[2 * pair * QB:(2 * pair + 2) * QB, :], key_ok)
                o = jnp.dot(p.astype(BF16), vw, preferred_element_type=F32)
                o_ref[:, lo:lo + 128] = jnp.where(first, o[:QB], o[QB:]).astype(BF16)

        pl.when(i < KPAD // QB)(lambda: run(True))
        pl.when(i >= KPAD // QB)(lambda: run(False))

    res, jres = _call(
        body, name="attn_fwd", grid=(t // QB,),
        in_specs=[pl.BlockSpec((QB, DA), lambda i: (i, 0)),
                  pl.BlockSpec(memory_space=pltpu.VMEM),
                  pl.BlockSpec(memory_space=pltpu.VMEM)],
        args=[qkv, kvp, bias],
        out_specs=[pl.BlockSpec((QB, DA), lambda i: (i, 0))],
        out_shape=[jax.ShapeDtypeStruct((t, DA), BF16)],
        sem=("arbitrary",), job=job)
    return res[0], jres


def _attn_bwd(qkv, kvp, bias, datt, job=None):
    t = qkv.shape[0]
    nb = t // QB
    flush = (KW - QB) // QB

    def body(q_ref, kv_ref, b_ref, do_ref, dq_ref, dkv_ref, db_ref, acc_ref):
        i = pl.program_id(0)

        @pl.when(i == 0)
        def _():
            acc_ref[...] = jnp.zeros_like(acc_ref)
            db_ref[...] = jnp.zeros_like(db_ref)

        def run(masked):
            start = pl.multiple_of(i * QB, QB)
            col = lax.broadcasted_iota(jnp.int32, (1, KW), 1)
            key_ok = (col >= KPAD - i * QB) if masked else None
            for pair in range(4):
                lo = pair * 128
                kw = kv_ref[pl.ds(start, KW), lo:lo + 128]
                vw = kv_ref[pl.ds(start, KW), DA + lo:DA + lo + 128]
                qs, first = _stack_heads(q_ref[:, lo:lo + 128])
                qs = qs * 0.125
                dos, _ = _stack_heads(do_ref[:, lo:lo + 128])
                rows = slice(2 * pair * QB, (2 * pair + 2) * QB)
                p = _pair_probs(qs, kw, b_ref[rows, :], key_ok)
                dp = lax.dot_general(dos, vw, NT, preferred_element_type=F32)
                ds = p * (dp - jnp.sum(p * dp, axis=-1, keepdims=True))
                db_ref[rows, :] += ds
                dsb = ds.astype(BF16)
                dq = jnp.dot(dsb, kw, preferred_element_type=F32)
                dq_ref[:, lo:lo + 128] = (jnp.where(first, dq[:QB], dq[QB:]) * 0.125).astype(BF16)
                acc_ref[:, lo:lo + 128] += lax.dot_general(dsb, qs, TN, preferred_element_type=F32)
                acc_ref[:, DA + lo:DA + lo + 128] += lax.dot_general(p.astype(BF16), dos, TN,
                                                                     preferred_element_type=F32)

        pl.when(i < KPAD // QB)(lambda: run(True))
        pl.when(jnp.logical_and(i >= KPAD // QB, i < nb))(lambda: run(False))

        dkv_ref[...] = acc_ref[0:QB, :].astype(BF16)
        rest = acc_ref[QB:KW, :]
        acc_ref[0:KW - QB, :] = rest
        acc_ref[KW - QB:KW, :] = jnp.zeros((QB, 2 * DA), F32)

    last = nb - 1
    res, jres = _call(
        body, name="attn_bwd", grid=(nb + flush,),
        in_specs=[pl.BlockSpec((QB, DA), lambda i: (jnp.minimum(i, last), 0)),
                  pl.BlockSpec(memory_space=pltpu.VMEM),
                  pl.BlockSpec(memory_space=pltpu.VMEM),
                  pl.BlockSpec((QB, DA), lambda i: (jnp.minimum(i, last), 0))],
        args=[qkv, kvp, bias, datt],
        out_specs=[pl.BlockSpec((QB, DA), lambda i: (jnp.minimum(i, last), 0)),
                   pl.BlockSpec((QB, 2 * DA), lambda i: (i, 0)),
                   pl.BlockSpec((NH * QB, KW), lambda i: (0, 0))],
        out_shape=[jax.ShapeDtypeStruct((t, DA), BF16),
                   jax.ShapeDtypeStruct((t + KPAD, 2 * DA), BF16),
                   jax.ShapeDtypeStruct((NH * QB, KW), F32)],
        scratch=[pltpu.VMEM((KW, 2 * DA), F32)], sem=("arbitrary",), job=job)
    return res, jres


def _glu(c, gb):
    cb = c + gb
    return cb[:, :DC] * _sigmoid(cb[:, DC:])


def _ln_swish(pre, g, b):
    mu = jnp.mean(pre, axis=-1, keepdims=True)
    xc = pre - mu
    r = lax.rsqrt(jnp.mean(xc * xc, axis=-1, keepdims=True) + EPS)
    xhat = xc * r
    y = xhat * g + b
    return xhat, r, y


RT = 32


def _shifted_copies(src_ref, sh_ref, rows):
    for b in range(1, 8):
        sh_ref[b - 1, :, :] = src_ref[pl.ds(b, rows), :]


def _tap(src_ref, sh_ref, off, r0, rows=RT):
    a, b = divmod(off, 8)
    ref = src_ref if b == 0 else sh_ref.at[b - 1]
    if isinstance(r0, int):
        return ref[r0 + 8 * a:r0 + 8 * a + rows, :]
    return ref[pl.ds(pl.multiple_of(r0 + 8 * a, 8), rows), :]


def _conv_fwd(cin, glu_b, dw_w, dw_b, ln_g, ln_b, tm=TM):
    t = cin.shape[0]
    hb = tm // HALO

    def body(c_ref, h_ref, gb_ref, w_ref, wb_ref, g_ref, b_ref, cs_ref, pre_ref, ext_ref, sh_ref):
        i = pl.program_id(0)
        halo = _glu(h_ref[...], gb_ref[...])
        ext_ref[0:HALO, :] = jnp.where(i > 0, halo, jnp.zeros_like(halo))
        ext_ref[HALO:HALO + tm, :] = _glu(c_ref[...], gb_ref[...])
        ext_ref[HALO + tm:HALO + tm + 8, :] = jnp.zeros((8, DC), F32)
        _shifted_copies(ext_ref, sh_ref, HALO + tm)

        def tile(rt, carry):
            r0 = pl.multiple_of(rt * RT, RT)
            acc = jnp.zeros((RT, DC), F32) + wb_ref[...]
            for j in range(CW):
                acc = acc + w_ref[j:j + 1, :] * _tap(ext_ref, sh_ref, HALO - (CW - 1) + j, r0)
            pre_ref[pl.ds(r0, RT), :] = acc
            _, _, y = _ln_swish(acc, g_ref[...], b_ref[...])
            cs_ref[pl.ds(r0, RT), :] = (y * _sigmoid(y)).astype(BF16)
            return carry

        lax.fori_loop(0, tm // RT, tile, 0)

    vec = lambda n: pl.BlockSpec((1, n), lambda i: (0, 0))
    return pl.pallas_call(
        body,
        name="conv_fwd",
        grid=(t // tm,),
        in_specs=[pl.BlockSpec((tm, 2 * DC), lambda i: (i, 0)),
                  pl.BlockSpec((HALO, 2 * DC), lambda i: (jnp.maximum(i * hb - 1, 0), 0)),
                  vec(2 * DC), pl.BlockSpec((CW, DC), lambda i: (0, 0)), vec(DC), vec(DC), vec(DC)],
        out_specs=[pl.BlockSpec((tm, DC), lambda i: (i, 0)), pl.BlockSpec((tm, DC), lambda i: (i, 0))],
        out_shape=[jax.ShapeDtypeStruct((t, DC), BF16), jax.ShapeDtypeStruct((t, DC), F32)],
        scratch_shapes=[pltpu.VMEM((HALO + tm + 8, DC), F32), pltpu.VMEM((7, HALO + tm, DC), F32)],
        compiler_params=_params(("arbitrary",)),
    )(cin, cin, glu_b, dw_w, dw_b, ln_g, ln_b)


def _conv_bwd(dcs, pre, cin, glu_b, dw_w, ln_g, ln_b, tm=TM):
    t = cin.shape[0]
    hb = tm // HALO
    steps = t // tm
    nhb = t // HALO

    def dpre_of(dcs_v, pre_v, g, b):
        xhat, r, y = _ln_swish(pre_v, g, b)
        sg = _sigmoid(y)
        dy = dcs_v * (sg * (1.0 + y * (1.0 - sg)))
        dxh = dy * g
        dpre = r * (dxh - jnp.mean(dxh, axis=-1, keepdims=True)
                    - xhat * jnp.mean(dxh * xhat, axis=-1, keepdims=True))
        return dpre, dy * xhat, dy

    def body(dcs_ref, dcsn_ref, pre_ref, pren_ref, c_ref, ch_ref, gb_ref, w_ref, g_ref, b_ref,
             dc_ref, dgb_ref, dw_ref, dwb_ref, dg_ref, db_ref,
             gext_ref, dext_ref, shg_ref, shd_ref, a_gb, a_w, a_wb, a_g, a_b):
        i = pl.program_id(0)

        @pl.when(i == 0)
        def _():
            for a in (a_gb, a_w, a_wb, a_g, a_b):
                a[...] = jnp.zeros_like(a)

        fold = lambda v: v.reshape(v.shape[0] // 8, 8, v.shape[-1]).sum(axis=0)
        g, b = g_ref[...], b_ref[...]
        dpre, dg_t, db_t = dpre_of(dcs_ref[...], pre_ref[...], g, b)
        dpre_n, _, _ = dpre_of(dcsn_ref[...], pren_ref[...], g, b)
        dext_ref[0:tm, :] = dpre
        dext_ref[tm:tm + HALO, :] = jnp.where(i < steps - 1, dpre_n, jnp.zeros_like(dpre_n))
        dext_ref[tm + HALO:tm + HALO + 8, :] = jnp.zeros((8, DC), F32)
        a_wb[...] += fold(dpre)
        a_g[...] += fold(dg_t)
        a_b[...] += fold(db_t)
        halo = _glu(ch_ref[...], gb_ref[...])
        gext_ref[0:HALO, :] = jnp.where(i > 0, halo, jnp.zeros_like(halo))
        gext_ref[HALO:HALO + tm, :] = _glu(c_ref[...], gb_ref[...])
        gext_ref[HALO + tm:HALO + tm + 8, :] = jnp.zeros((8, DC), F32)
        _shifted_copies(gext_ref, shg_ref, HALO + tm)
        _shifted_copies(dext_ref, shd_ref, HALO + tm)

        for j in range(CW):
            a_w[8 * j:8 * j + 8, :] += fold(dext_ref[0:tm, :] * _tap(gext_ref, shg_ref, HALO - (CW - 1) + j, 0, tm))

        def tile(rt, carry):
            r0 = pl.multiple_of(rt * RT, RT)
            dglu = jnp.zeros((RT, DC), F32)
            for j in range(CW):
                dglu = dglu + w_ref[j:j + 1, :] * _tap(dext_ref, shd_ref, CW - 1 - j, r0)
            cb = c_ref[pl.ds(r0, RT), :] + gb_ref[...]
            sg = _sigmoid(cb[:, DC:])
            dc = jnp.concatenate([dglu * sg, dglu * cb[:, :DC] * sg * (1.0 - sg)], axis=1)
            dc_ref[pl.ds(r0, RT), :] = dc.astype(BF16)
            a_gb[...] += fold(dc)
            return carry

        lax.fori_loop(0, tm // RT, tile, 0)

        @pl.when(i == steps - 1)
        def _():
            dgb_ref[...] = a_gb[...].sum(axis=0, keepdims=True)
            dw_ref[...] = a_w[...]
            dwb_ref[...] = a_wb[...].sum(axis=0, keepdims=True)
            dg_ref[...] = a_g[...].sum(axis=0, keepdims=True)
            db_ref[...] = a_b[...].sum(axis=0, keepdims=True)

    vec = lambda n: pl.BlockSpec((1, n), lambda i: (0, 0))
    nxt = lambda i: (jnp.minimum((i + 1) * hb, nhb - 1), 0)
    prv = lambda i: (jnp.maximum(i * hb - 1, 0), 0)
    return pl.pallas_call(
        body,
        name="conv_bwd",
        grid=(steps,),
        in_specs=[pl.BlockSpec((tm, DC), lambda i: (i, 0)), pl.BlockSpec((HALO, DC), nxt),
                  pl.BlockSpec((tm, DC), lambda i: (i, 0)), pl.BlockSpec((HALO, DC), nxt),
                  pl.BlockSpec((tm, 2 * DC), lambda i: (i, 0)), pl.BlockSpec((HALO, 2 * DC), prv),
                  vec(2 * DC), pl.BlockSpec((CW, DC), lambda i: (0, 0)), vec(DC), vec(DC)],
        out_specs=[pl.BlockSpec((tm, 2 * DC), lambda i: (i, 0)), vec(2 * DC),
                   pl.BlockSpec((CW * 8, DC), lambda i: (0, 0)), vec(DC), vec(DC), vec(DC)],
        out_shape=[jax.ShapeDtypeStruct((t, 2 * DC), BF16), jax.ShapeDtypeStruct((1, 2 * DC), F32),
                   jax.ShapeDtypeStruct((CW * 8, DC), F32), jax.ShapeDtypeStruct((1, DC), F32),
                   jax.ShapeDtypeStruct((1, DC), F32), jax.ShapeDtypeStruct((1, DC), F32)],
        scratch_shapes=[pltpu.VMEM((HALO + tm + 8, DC), F32), pltpu.VMEM((tm + HALO + 8, DC), F32),
                        pltpu.VMEM((7, HALO + tm, DC), F32), pltpu.VMEM((7, HALO + tm, DC), F32),
                        pltpu.VMEM((8, 2 * DC), F32), pltpu.VMEM((CW * 8, DC), F32),
                        pltpu.VMEM((8, DC), F32), pltpu.VMEM((8, DC), F32), pltpu.VMEM((8, DC), F32)],
        compiler_params=_params(("arbitrary",)),
    )(dcs, dcs, pre, pre, cin, cin, glu_b, dw_w, ln_g, ln_b)


def _place():
    x, y, c = lax.axis_index("x"), lax.axis_index("y"), lax.axis_index("c")
    return x, y, c


def _peers(x, y, c):
    out = []
    for k in range(1, NDEV):
        fx, fy, fc = (k >> 2) & 1, (k >> 1) & 1, k & 1
        px = 1 - x if fx else x
        py = 1 - y if fy else y
        pc = 1 - c if fc else c
        out.append((px, py, pc))
    return out


def _job_out_shapes(job):
    kind, arrays = job
    if kind == "gather":
        return [jax.ShapeDtypeStruct((a.shape[0], NDEV) + a.shape[1:], a.dtype) for a in arrays]
    return [jax.ShapeDtypeStruct((NDEV, a.shape[0]) + a.shape[2:], a.dtype) for a in arrays]


def _job_scratch(job):
    n = len(job[1])
    return [pltpu.SemaphoreType.DMA((n, NDEV - 1)), pltpu.SemaphoreType.DMA((n, NDEV - 1)),
            pltpu.SemaphoreType.DMA((n,))]


def _gather_parts(ins, outs, send_sems, recv_sems, local_sems):
    x, y, c = _place()
    me, sib = (x, y, c), (x, y, 1 - c)
    chips = [(1 - x, y), (x, 1 - y), (1 - x, 1 - y)]

    def copy(a, k, block, to, src=None):
        px, py, pc = block
        dst = outs[a].at[:, 4 * px + 2 * py + pc]
        return pltpu.make_async_remote_copy(
            src_ref=dst if src is None else src, dst_ref=dst,
            send_sem=send_sems.at[a, k], recv_sem=recv_sems.at[a, k], device_id=to, device_id_type=MESH)

    n = len(ins)
    local = [pltpu.make_async_copy(ins[a], outs[a].at[:, 4 * x + 2 * y + c], local_sems.at[a]) for a in range(n)]
    first = [[copy(a, 0, me, sib, src=ins[a])] + [copy(a, 1 + j, me, (*chip, c), src=ins[a])
                                                   for j, chip in enumerate(chips)] for a in range(n)]

    def start():
        for a in range(n):
            local[a].start()
            for cp in first[a]:
                cp.start()

    def finish():
        passed = []
        for j, chip in enumerate(chips):
            for a in range(n):
                copy(a, 1 + j, (*chip, c), me).wait_recv()
                cp = copy(a, 4 + j, (*chip, c), sib)
                cp.start()
                passed.append(cp)
        for a in range(n):
            copy(a, 0, sib, me).wait_recv()
            for j, chip in enumerate(chips):
                copy(a, 4 + j, (*chip, 1 - c), me).wait_recv()
        for a in range(n):
            for cp in first[a]:
                cp.wait_send()
            local[a].wait()
        for cp in passed:
            cp.wait_send()

    return start, finish


def _exchange_parts(ins, outs, send_sems, recv_sems, local_sems):
    x, y, c = _place()
    me = 4 * x + 2 * y + c
    n = len(ins)
    peers = _peers(x, y, c)
    local = [pltpu.make_async_copy(ins[a].at[:, me], outs[a].at[me], local_sems.at[a]) for a in range(n)]

    def copy(a, k):
        px, py, pc = peers[k]
        return pltpu.make_async_remote_copy(
            src_ref=ins[a].at[:, 4 * px + 2 * py + pc], dst_ref=outs[a].at[me],
            send_sem=send_sems.at[a, k], recv_sem=recv_sems.at[a, k], device_id=peers[k], device_id_type=MESH)

    def arrival(a, k):
        px, py, pc = peers[k]
        return pltpu.make_async_remote_copy(
            src_ref=ins[a].at[:, me], dst_ref=outs[a].at[4 * px + 2 * py + pc],
            send_sem=send_sems.at[a, k], recv_sem=recv_sems.at[a, k], device_id=peers[k], device_id_type=MESH)

    def start():
        for a in range(n):
            local[a].start()
            for k in range(NDEV - 1):
                copy(a, k).start()

    def finish():
        for a in range(n):
            for k in range(NDEV - 1):
                arrival(a, k).wait_recv()
        for a in range(n):
            for k in range(NDEV - 1):
                copy(a, k).wait_send()
            local[a].wait()

    return start, finish


def _call(body, *, name, grid, in_specs, args, out_specs, out_shape, scratch=(), sem=None, aliases=None, job=None):
    aliases = dict(aliases or {})
    if job is None:
        res = pl.pallas_call(
            body, name=name, grid=grid, in_specs=list(in_specs), out_specs=list(out_specs),
            out_shape=list(out_shape), scratch_shapes=list(scratch), input_output_aliases=aliases,
            compiler_params=_params(sem))(*args)
        return list(res), []
    kind, arrays = job
    n_in, n_out, n_scr, nj = len(args), len(out_shape), len(scratch), len(arrays)

    def wrapped(*refs):
        ins = refs[:n_in]
        jin = refs[n_in:n_in + nj]
        o0 = n_in + nj
        outs = refs[o0:o0 + n_out]
        jout = refs[o0 + n_out:o0 + n_out + nj]
        s0 = o0 + n_out + nj
        scr = refs[s0:s0 + n_scr]
        sems = refs[s0 + n_scr:]
        parts = _gather_parts if kind == "gather" else _exchange_parts
        start, finish = parts(jin, jout, *sems)
        if not grid:
            start()
            body(*ins, *outs, *scr)
            finish()
            return
        first = last = None
        for d, g in enumerate(grid):
            f, l = pl.program_id(d) == 0, pl.program_id(d) == g - 1
            first = f if first is None else jnp.logical_and(first, f)
            last = l if last is None else jnp.logical_and(last, l)
        pl.when(first)(start)
        body(*ins, *outs, *scr)
        pl.when(last)(finish)

    res = pl.pallas_call(
        wrapped, name=name, grid=grid, in_specs=list(in_specs) + [ANY] * nj,
        out_specs=list(out_specs) + [ANY] * nj, out_shape=list(out_shape) + _job_out_shapes(job),
        scratch_shapes=list(scratch) + _job_scratch(job), input_output_aliases=aliases,
        compiler_params=pltpu.CompilerParams(
            dimension_semantics=None if not grid else ("arbitrary",) * len(grid),
            vmem_limit_bytes=VMEM_LIMIT, has_side_effects=True))(*args, *arrays)
    return list(res[:n_out]), list(res[n_out:])


def _comm_only(name, job):
    return _call(lambda: None, name=name, grid=(), in_specs=[], args=[], out_specs=[], out_shape=[], job=job)[1]


def _sum_devices(name, parts, tr):
    _, r, c = parts.shape

    def body(p_ref, o_ref):
        acc = p_ref[0].astype(F32)
        for d in range(1, NDEV):
            acc = acc + p_ref[d].astype(F32)
        o_ref[...] = acc

    return pl.pallas_call(
        body,
        name=name,
        grid=(r // tr,),
        in_specs=[pl.BlockSpec((NDEV, tr, c), lambda i: (0, i, 0))],
        out_specs=pl.BlockSpec((tr, c), lambda i: (i, 0)),
        out_shape=jax.ShapeDtypeStruct((r, c), F32),
        compiler_params=_params(("parallel",)),
    )(parts)


def _adamw(name, w, g, m, v, tr=None):
    r, c = w.shape
    tr = r if tr is None else tr

    def body(w_ref, g_ref, m_ref, v_ref, d_ref, nm_ref, nv_ref):
        gv = g_ref[...]
        nm = ADAM_B1 * m_ref[...] + (1.0 - ADAM_B1) * gv
        nv = ADAM_B2 * v_ref[...] + (1.0 - ADAM_B2) * (gv * gv)
        m_hat = nm / (1.0 - ADAM_B1 ** ADAM_STEP)
        v_hat = nv / (1.0 - ADAM_B2 ** ADAM_STEP)
        d_ref[...] = -ADAM_LR * (m_hat / (jnp.sqrt(v_hat) + ADAM_EPS) + ADAM_WD * w_ref[...])
        nm_ref[...] = nm
        nv_ref[...] = nv

    spec = pl.BlockSpec((tr, c), lambda i: (i, 0))
    return pl.pallas_call(
        body,
        name=name,
        grid=(r // tr,),
        in_specs=[spec] * 4,
        out_specs=[spec] * 3,
        out_shape=[jax.ShapeDtypeStruct((r, c), F32)] * 3,
        compiler_params=_params(("parallel",)),
    )(w, g, m, v)


SMALL = ["ffn1_norm_pre", "ffn1_norm_post", "mix_norm_pre", "gate_bias", "rel_table", "conv_glu_bias",
         "conv_dw_b", "conv_ln_g", "conv_ln_b", "mix_norm_post", "ffn2_norm_pre", "ffn2_norm_post"]
SMALL_ROWS = 24
DW_ROWS = 32


def _pack_small(vals):
    rows = []
    for name in SMALL:
        v = vals[name]
        if name == "rel_table":
            v = v.reshape(NH, -1)
            rows.append(jnp.pad(v, ((0, 0), (0, D - v.shape[1]))))
        else:
            v = v.reshape(-1)
            v = jnp.pad(v, (0, (-v.shape[0]) % D))
            rows.append(v.reshape(-1, D))
    out = jnp.concatenate(rows, axis=0)
    return jnp.pad(out, ((0, SMALL_ROWS - out.shape[0]), (0, 0)))


def _unpack_small(pack, shapes):
    out = {}
    r = 0
    for name in SMALL:
        shp = shapes[name]
        n = 1
        for s in shp:
            n *= s
        if name == "rel_table":
            out[name] = pack[r:r + NH, :NREL].reshape(shp)
            r += NH
        else:
            nr = -(-n // D)
            out[name] = pack[r:r + nr].reshape(-1)[:n].reshape(shp)
            r += nr
    return out


def kernel(x, ffn1_norm_pre, ffn1_w_gate, ffn1_w_up, ffn1_w_down, ffn1_norm_post, mix_norm_pre, w_in, gate_bias, rel_table, w_attn_out, conv_glu_bias, conv_dw_w, conv_dw_b, conv_ln_g, conv_ln_b, conv_w_out, w_out, mix_norm_post, ffn2_norm_pre, ffn2_w_gate, ffn2_w_up, ffn2_w_down, ffn2_norm_post, loss_target, m_ffn1_norm_pre, m_ffn1_w_gate, m_ffn1_w_up, m_ffn1_w_down, m_ffn1_norm_post, m_mix_norm_pre, m_w_in, m_gate_bias, m_rel_table, m_w_attn_out, m_conv_glu_bias, m_conv_dw_w, m_conv_dw_b, m_conv_ln_g, m_conv_ln_b, m_conv_w_out, m_w_out, m_mix_norm_post, m_ffn2_norm_pre, m_ffn2_w_gate, m_ffn2_w_up, m_ffn2_w_down, m_ffn2_norm_post, v_ffn1_norm_pre, v_ffn1_w_gate, v_ffn1_w_up, v_ffn1_w_down, v_ffn1_norm_post, v_mix_norm_pre, v_w_in, v_gate_bias, v_rel_table, v_w_attn_out, v_conv_glu_bias, v_conv_dw_w, v_conv_dw_b, v_conv_ln_g, v_conv_ln_b, v_conv_w_out, v_w_out, v_mix_norm_post, v_ffn2_norm_pre, v_ffn2_w_gate, v_ffn2_w_up, v_ffn2_w_down, v_ffn2_norm_post):
    return _step(dict(locals()))


WEIGHTS = ["ffn1_norm_pre", "ffn1_w_gate", "ffn1_w_up", "ffn1_w_down", "ffn1_norm_post", "mix_norm_pre", "w_in",
           "gate_bias", "rel_table", "w_attn_out", "conv_glu_bias", "conv_dw_w", "conv_dw_b", "conv_ln_g",
           "conv_ln_b", "conv_w_out", "w_out", "mix_norm_post", "ffn2_norm_pre", "ffn2_w_gate", "ffn2_w_up",
           "ffn2_w_down", "ffn2_norm_post"]
FS = FF // NDEV
PS = (3 * DA + 2 * DC + 2 * D) // NDEV
OS = D // NDEV


def _local_step(xs, target, w, rel_table):
    t = xs.shape[0]
    vec = lambda n: w[n].reshape(1, -1)
    g_pre1, g_post1, g_mix, g_mixp = vec("ffn1_norm_pre"), vec("ffn1_norm_post"), vec("mix_norm_pre"), vec("mix_norm_post")
    g_pre2, g_post2 = vec("ffn2_norm_pre"), vec("ffn2_norm_post")
    gate_b, glu_b = vec("gate_bias"), vec("conv_glu_bias")
    dw_b, ln_g, ln_b = vec("conv_dw_b"), vec("conv_ln_g"), vec("conv_ln_b")

    tr = lambda a: jnp.transpose(a[0]).astype(BF16)
    sh_gu1 = jnp.stack([tr(w["ffn1_w_gate"]), tr(w["ffn1_w_up"])])
    sh_mid = [w["ffn1_w_down"].astype(BF16), tr(w["w_in"])[None], w["w_out"].astype(BF16),
              jnp.stack([tr(w["w_attn_out"]), tr(w["conv_w_out"])]),
              jnp.pad(w["conv_dw_w"][0, :, 0, :], ((0, DW_ROWS - CW), (0, 0)))[None]]
    sh_2 = jnp.stack([tr(w["ffn2_w_gate"]), tr(w["ffn2_w_up"]), w["ffn2_w_down"][0].astype(BF16)])

    (n1,), (w_gu1,) = _rowwise("pre1", lambda xv, g: ((_rms(xv)[0] * g),), [xs], [g_pre1], [(D, BF16)],
                               job=("gather", [sh_gu1]))
    w_gu1 = w_gu1.reshape(2, FF, D)
    (a1, b1, s1), (w_d1, wb, wc, wd, we) = _ffn_up("ffn1_up", n1, w_gu1, 0, job=("gather", sh_mid))
    w_d1, wb, wc, wd = w_d1.reshape(1, FF, D), wb.reshape(NDEV * PS, D), wc.reshape(D, D), wd.reshape(2, D, DA)
    dw_full = jnp.transpose(we[0], (1, 0, 2)).reshape(DW_ROWS, DC)[:CW]
    f1 = _ffn_down("ffn1_down", s1, w_d1, 0)

    def post1(xv, fv, gp, gm):
        h = xv + 0.5 * (_rms(fv)[0] * gp)
        return h, _rms(h)[0] * gm

    h1, u = _rowwise("post1", post1, [xs, f1], [g_post1, g_mix], [(D, F32), (D, BF16)])

    qkv = _mm_simple("proj_qkv", u, wb, NT, BF16, 1024, 512, b_row0=0, b_rows=3 * DA)
    cin = _mm_simple("proj_conv", u, wb, NT, F32, 1024, 512, b_row0=3 * DA, b_rows=2 * DC)
    gg = _mm_simple("proj_gate", u, wb, NT, BF16, 1024, 512, b_row0=3 * DA + 2 * DC, b_rows=2 * D)

    bias = _relbias_fwd(jnp.pad(rel_table[0], ((0, 0), (0, 384 - NREL))))
    kvp = jnp.pad(qkv[:, DA:], ((KPAD, 0), (0, 0)))
    att, (w_2,) = _attn_fwd(qkv, kvp, bias, job=("gather", [sh_2]))
    w_2 = w_2.reshape(3, FF, D)
    cs, pre = _conv_fwd(cin, glu_b, dw_full, dw_b, ln_g, ln_b)
    ya = _mm_simple("attn_out", att, wd[0], NT, BF16, 1024, 1024)
    yb = _mm_simple("conv_out", cs, wd[1], NT, BF16, 1024, 1024)

    def merge(yav, ybv, gv, gb):
        gates = _sigmoid(gv + gb)
        return (gates[:, :D] * yav + gates[:, D:] * ybv,)

    (merged,) = _rowwise("merge", merge, [ya, yb, gg], [gate_b], [(D, BF16)])
    mm_ = _mm_simple("mix_out", merged, wc, NN, F32, 1024, 1024)

    def postm(hv, mv, gp, g2):
        h = hv + _rms(mv)[0] * gp
        return h, _rms(h)[0] * g2

    h2, n2 = _rowwise("postm", postm, [h1, mm_], [g_mixp, g_pre2], [(D, F32), (D, BF16)])
    a2, b2, s2 = _ffn_up("ffn2_up", n2, w_2, 0)
    f2 = _ffn_down("ffn2_down", s2, w_2, 2)

    def post2(hv, fv, tv, gp):
        fh, r = _rms(fv)
        yv = hv + 0.5 * (fh * gp)
        err = yv - tv
        dy = err * (1.0 / D)
        df, dg = _rms_bwd(fh, r, gp, 0.5 * dy)
        return dy, df, (0.5 / D) * (err * err), dg

    dy, df2, loss_row, d_post2 = _rowwise("post2", post2, [h2, f2, target], [g_post2],
                                          [(D, F32), (D, BF16)], [D, D])

    tmw = 1408
    nfi = FF // tmw
    tkf = min(2048, t)
    tkp = min(1024, t)

    def wgrad_down(name, s, df, shape, part, carry_buf, job=None):
        return _wgrad(name, (s, (tkf, tmw), lambda i, j, q: (q, i)), df,
                      (None, tmw, D), lambda i, j, q: (part, i, 0), nfi, carry_buf, shape, tk=tkf, job=job)

    def wgrad_gate_up(name, dab, nrm, shape, carry_buf, job=None):
        return _wgrad(name, (dab, (None, tkf, tmw), lambda i, j, q: (i // nfi, q, i % nfi)), nrm,
                      (None, tmw, D), lambda i, j, q: (i // nfi, i % nfi, 0), 2 * nfi, carry_buf, shape,
                      tk=tkf, job=job)

    g_2 = wgrad_down("ffn2_wgrad_d", s2, df2, (3, FF, D), 2, None)
    dab2 = _ffn_bwd_act("ffn2_bwd_act", df2, w_2, 2, a2, b2)
    g_2 = wgrad_gate_up("ffn2_wgrad_gu", dab2, n2, (3, FF, D), g_2)
    dn2 = _ffn_bwd_in("ffn2_bwd_in", dab2, w_2, 0)

    def bwd_pre2(hv, dnv, dyv, mv, g2, gp):
        hh, r = _rms(hv)
        dx, dg2 = _rms_bwd(hh, r, g2, dnv)
        dh = dyv + dx
        mh, rm = _rms(mv)
        dm, dgp = _rms_bwd(mh, rm, gp, dh)
        return dh, dm, dg2, dgp

    dh2, dm, d_pre2, d_mixp = _rowwise("bwd_pre2", bwd_pre2, [h2, dn2, dy, mm_], [g_pre2, g_mixp],
                                       [(D, F32), (D, BF16)], [D, D])
    dmerged = _mm_simple("mix_out_bwd", dm, wc, NT, BF16, 1024, 1024)

    def merge_bwd(dmv, yav, ybv, gv, gb):
        gates = _sigmoid(gv + gb)
        ga, gbb = gates[:, :D], gates[:, D:]
        dgg = jnp.concatenate([dmv * yav * ga * (1.0 - ga), dmv * ybv * gbb * (1.0 - gbb)], axis=1)
        return dmv * ga, dmv * gbb, dgg, dgg

    dya, dyb, dgg, d_gate_b = _rowwise("merge_bwd", merge_bwd, [dmerged, ya, yb, gg], [gate_b],
                                       [(D, BF16), (D, BF16), (2 * D, BF16)], [2 * D])
    datt = _mm_simple("attn_out_bwd", dya, wd[0], NN, BF16, 1024, 512)
    dcs = _mm_simple("conv_out_bwd", dyb, wd[1], NN, F32, 1024, 512)
    g_c = _wgrad("mix_out_wgrad", (merged, (tkp, D), lambda i, j, q: (q, 0)), dm,
                 (D, D), lambda i, j, q: (0, 0), 1, None, (D, D), tk=tkp)
    g_d = _wgrad("attn_out_wgrad", (dya, (tkp, D), lambda i, j, q: (q, 0)), att,
                 (None, D, DA), lambda i, j, q: (0, 0, 0), 1, None, (2, D, DA), tk=tkp)
    g_d = _wgrad("conv_out_wgrad", (dyb, (tkp, D), lambda i, j, q: (q, 0)), cs,
                 (None, D, DA), lambda i, j, q: (1, 0, 0), 1, g_d, (2, D, DA), tk=tkp)
    (dq, dkvp, dbias), (x_2, x_c, x_d) = _attn_bwd(
        qkv, kvp, bias, datt,
        job=("exchange", [g_2.reshape(3, NDEV, FS, D), g_c.reshape(1, NDEV, OS, D), g_d.reshape(2, NDEV, OS, DA)]))
    d_rel = _relbias_bwd(dbias)
    dcin, d_glu_b, d_dw8, d_dw_b, d_ln_g, d_ln_b = _conv_bwd(dcs, pre, cin, glu_b, dw_full, ln_g, ln_b)
    g_dw = jnp.pad(d_dw8, ((0, 8 * (DW_ROWS - CW)), (0, 0)))
    g_dw = g_dw.reshape(DW_ROWS * 8, NDEV, DC // NDEV).transpose(1, 0, 2)

    top = lambda i, j, q: (0, 0)
    g_b = jnp.concatenate([
        _wgrad("proj_wgrad_q", (dq, (tkp, DA), lambda i, j, q: (q, 0)), u, (DA, D), top, 1, None, (DA, D), tk=tkp),
        _wgrad("proj_wgrad_kv", (dkvp, (KPAD, 2 * DA), lambda i, j, q: (q + 1, 0)), u, (2 * DA, D), top, 1, None,
               (2 * DA, D), tk=KPAD),
        _wgrad("proj_wgrad_c", (dcin, (tkp, 2 * DC), lambda i, j, q: (q, 0)), u, (2 * DC, D), top, 1, None,
               (2 * DC, D), tk=tkp),
        _wgrad("proj_wgrad_g", (dgg, (tkp, 2 * D), lambda i, j, q: (q, 0)), u, (2 * D, D), top, 1, None,
               (2 * D, D), tk=tkp)], axis=0)

    tmu = 512
    a_ops = [(dq, (tmu, DA), lambda i, j, q: (i, 0)),
             (dkvp, (tmu, 2 * DA), lambda i, j, q: (i + KPAD // tmu, 0)),
             (dcin, (tmu, 2 * DC), lambda i, j, q: (i, 0)),
             (dgg, (tmu, 2 * D), lambda i, j, q: (i, 0))]
    whole = lambda i, j, q: (0, 0)
    b_ops = [(wb[:DA], (DA, D), whole), (wb[DA:3 * DA], (2 * DA, D), whole),
             (wb[3 * DA:3 * DA + 2 * DC], (2 * DC, D), whole), (wb[3 * DA + 2 * DC:], (2 * D, D), whole)]
    du = _mm("proj_bwd", (t // tmu, 1, 1), a_ops, b_ops, [[(0, 0), (1, 1), (2, 2), (3, 3)]], NN, _first,
             [((t, D), F32, (tmu, D), lambda i, j, q: (i, 0))])[0]

    def bwd_mix(hv, duv, dhv, fv, gm, gp):
        hh, r = _rms(hv)
        dx, dgm = _rms_bwd(hh, r, gm, duv)
        dh = dhv + dx
        fh, rf = _rms(fv)
        df, dgp = _rms_bwd(fh, rf, gp, 0.5 * dh)
        return dh, df, dgm, dgp

    dh1, df1, d_mix, d_post1 = _rowwise("bwd_mix", bwd_mix, [h1, du, dh2, f1], [g_mix, g_post1],
                                        [(D, F32), (D, BF16)], [D, D])
    g_d1 = wgrad_down("ffn1_wgrad_d", s1, df1, (1, FF, D), 0, None)
    dab1, (x_b, x_dw) = _ffn_bwd_act("ffn1_bwd_act", df1, w_d1, 0, a1, b1,
                                     job=("exchange", [g_b.reshape(1, NDEV, PS, D), g_dw[None]]))
    g_gu1, (x_d1,) = wgrad_gate_up("ffn1_wgrad_gu", dab1, n1, (2, FF, D), None,
                                   job=("exchange", [g_d1.reshape(1, NDEV, FS, D)]))
    dn1, (x_gu1,) = _ffn_bwd_in("ffn1_bwd_in", dab1, w_gu1, 0, job=("exchange", [g_gu1.reshape(2, NDEV, FS, D)]))

    def bwd_pre1(xv, dnv, dhv, g1):
        xh, r = _rms(xv)
        dx, dg1 = _rms_bwd(xh, r, g1, dnv)
        return dhv + dx, dg1

    dx, d_pre1 = _rowwise("bwd_pre1", bwd_pre1, [xs, dn1, dh1], [g_pre1], [(D, F32)], [D])

    small_g = {"ffn1_norm_pre": d_pre1, "ffn1_norm_post": d_post1, "mix_norm_pre": d_mix, "gate_bias": d_gate_b,
               "rel_table": d_rel[:, :NREL], "conv_glu_bias": d_glu_b, "conv_dw_b": d_dw_b, "conv_ln_g": d_ln_g,
               "conv_ln_b": d_ln_b, "mix_norm_post": d_mixp, "ffn2_norm_pre": d_pre2, "ffn2_norm_post": d_post2}
    return loss_row, dx, (x_gu1, x_d1, x_2, x_b, x_c, x_d, x_dw), small_g


def _step(args):
    names = WEIGHTS
    w = {n: args[n] for n in names}
    fs, ps, os_ = FS, PS, OS
    conv_dw_w = args["conv_dw_w"]
    loss_row, dx, (x_gu1, x_d1, x_2, x_b, x_c, x_d, x_dw), small_g = _local_step(
        args["x"][0], args["loss_target"][0], w, args["rel_table"])

    g_small = _pack_small(small_g)
    (x_s,) = _comm_only("gather_small_grads", ("gather", [g_small[None]]))

    s_gu1 = _sum_devices("sum_ffn1_gu", x_gu1.reshape(NDEV, 2 * fs, D), fs).reshape(2, fs, D)
    s_d1 = _sum_devices("sum_ffn1_d", x_d1.reshape(NDEV, fs, D), fs)
    s_2 = _sum_devices("sum_ffn2", x_2.reshape(NDEV, 3 * fs, D), fs).reshape(3, fs, D)
    s_b = _sum_devices("sum_proj", x_b.reshape(NDEV, ps, D), ps)
    s_c = _sum_devices("sum_mix", x_c.reshape(NDEV, os_, D), os_)
    s_d = _sum_devices("sum_out", x_d.reshape(NDEV, 2 * os_, DA), 2 * os_).reshape(2, os_, DA)
    s_dw = _sum_devices("sum_dw", x_dw.reshape(NDEV, DW_ROWS * 8, DC // NDEV), DW_ROWS * 8)
    s_s = _sum_devices("sum_small", x_s.reshape(NDEV, SMALL_ROWS, D), SMALL_ROWS)

    grads = {
        "ffn1_w_gate": jnp.transpose(s_gu1[0])[None], "ffn1_w_up": jnp.transpose(s_gu1[1])[None], "ffn1_w_down": s_d1[None],
        "ffn2_w_gate": jnp.transpose(s_2[0])[None], "ffn2_w_up": jnp.transpose(s_2[1])[None], "ffn2_w_down": s_2[2][None],
        "w_in": jnp.transpose(s_b)[None], "w_out": s_c[None],
        "w_attn_out": jnp.transpose(s_d[0])[None], "conv_w_out": jnp.transpose(s_d[1])[None],
    }
    shapes = {n: w[n].shape for n in SMALL}
    grads.update(_unpack_small(s_s, shapes))

    deltas, new_m, new_v = {}, {}, {}
    big = ["ffn1_w_gate", "ffn1_w_up", "ffn1_w_down", "w_in", "w_attn_out", "conv_w_out", "w_out",
           "ffn2_w_gate", "ffn2_w_up", "ffn2_w_down"]
    for n in big:
        shp = w[n].shape
        two = lambda a: a.reshape(shp[1], shp[2])
        rows = shp[1]
        tr_ = rows // 2 if rows % 16 == 0 else rows
        d_, m_, v_ = _adamw("adamw_" + n, two(w[n]), two(grads[n]), two(args["m_" + n]), two(args["v_" + n]), tr_)
        deltas[n], new_m[n], new_v[n] = d_.reshape(shp), m_.reshape(shp), v_.reshape(shp)

    wp = _pack_small({n: w[n] for n in SMALL})
    mp = _pack_small({n: args["m_" + n] for n in SMALL})
    vp = _pack_small({n: args["v_" + n] for n in SMALL})
    d_, m_, v_ = _adamw("adamw_small", wp, s_s, mp, vp)
    for dst, pack in ((deltas, d_), (new_m, m_), (new_v, v_)):
        dst.update(_unpack_small(pack, shapes))

    g_dw_own = _fold8("fold_dw", s_dw)[:CW]
    grads["conv_dw_w"] = g_dw_own.reshape(1, CW, 1, DC // NDEV)
    flat = lambda a: a.reshape(CW, DC // NDEV)
    d_, m_, v_ = _adamw("adamw_dw", flat(conv_dw_w), g_dw_own, flat(args["m_conv_dw_w"]), flat(args["v_conv_dw_w"]))
    shp = conv_dw_w.shape
    deltas["conv_dw_w"], new_m["conv_dw_w"], new_v["conv_dw_w"] = d_.reshape(shp), m_.reshape(shp), v_.reshape(shp)

    loss = lax.psum(jnp.sum(loss_row), ("x", "y", "c"))
    return (loss, dx[None], *[grads[n] for n in names], *[deltas[n] for n in names],
            *[new_m[n] for n in names], *[new_v[n] for n in names])


def _fold8(name, a):
    r8, c = a.shape

    def body(a_ref, o_ref):
        o_ref[...] = a_ref[...].reshape(r8 // 8, 8, c).sum(axis=1)

    return pl.pallas_call(
        body,
        name=name,
        out_shape=jax.ShapeDtypeStruct((r8 // 8, c), F32),
        in_specs=[pl.BlockSpec(memory_space=pltpu.VMEM)],
        out_specs=pl.BlockSpec(memory_space=pltpu.VMEM),
        compiler_params=_params(),
    )(a)
```

```python
import functools

import jax
import jax.numpy as jnp
from jax import lax
from jax.experimental import pallas as pl
from jax.experimental.pallas import tpu as pltpu

F32 = jnp.float32
BF16 = jnp.bfloat16

D = 1024
FF = 2816
DA = 512
DC = 512
NH = 8
CHUNK = 64
LEFT = 8
CW = 31
NREL = 257
EPS = 1e-6
NDEV = 8

QB = 4 * CHUNK
KW = LEFT * CHUNK + QB
KPAD = LEFT * CHUNK
RELW = KW + QB
HALO = 32

TM = 512
VMEM_LIMIT = 56 * 1024 * 1024

ADAM_LR, ADAM_B1, ADAM_B2, ADAM_EPS, ADAM_WD, ADAM_STEP = 0.001, 0.9, 0.999, 1e-08, 0.01, 10

NT = (((1,), (1,)), ((), ()))
NN = (((1,), (0,)), ((), ()))
TN = (((0,), (0,)), ((), ()))

MESH = pl.DeviceIdType.MESH
ANY = pl.BlockSpec(memory_space=pl.ANY)


def _params(sem=None, vmem=VMEM_LIMIT):
    return pltpu.CompilerParams(dimension_semantics=sem, vmem_limit_bytes=vmem)


def _sigmoid(x):
    return 0.5 * jnp.tanh(0.5 * x) + 0.5


def _mm(name, grid, a_ops, b_ops, groups, dims, epi, outs, extras=(), carry=None, job=None):
    nk = grid[2]
    na, nb, ne, no, ng = len(a_ops), len(b_ops), len(extras), len(outs), len(groups)
    nc = 0 if carry is None else 1

    def body(*refs):
        a_refs = refs[:na]
        b_refs = refs[na:na + nb]
        e_refs = refs[na + nb:na + nb + ne]
        o_refs = refs[na + nb + ne + nc:na + nb + ne + nc + no]
        acc_refs = refs[na + nb + ne + nc + no:]
        k = pl.program_id(2)
        prods = []
        for grp in groups:
            p = None
            for ai, bi in grp:
                t = lax.dot_general(a_refs[ai][...], b_refs[bi][...], dims, preferred_element_type=F32)
                p = t if p is None else p + t
            prods.append(p)

        def finish(vals):
            res = epi(vals, [e[...] for e in e_refs])
            for o, r in zip(o_refs, res):
                o[...] = r.astype(o.dtype)

        if nk == 1:
            finish(prods)
        else:
            @pl.when(k == 0)
            def _():
                for acc, p in zip(acc_refs, prods):
                    acc[...] = p

            @pl.when(k > 0)
            def _():
                for acc, p in zip(acc_refs, prods):
                    acc[...] += p

            @pl.when(k == nk - 1)
            def _():
                finish([acc[...] for acc in acc_refs])

    in_specs = [pl.BlockSpec(blk, im) for _, blk, im in list(a_ops) + list(b_ops) + list(extras)]
    args = [arr for arr, _, _ in list(a_ops) + list(b_ops) + list(extras)]
    aliases = {}
    if carry is not None:
        in_specs.append(ANY)
        args.append(carry[0])
        aliases = {len(args) - 1: carry[1]}
    scratch = []
    if nk > 1:
        for _ in range(ng):
            blk = tuple(b for b in outs[0][2] if b is not None)
            scratch.append(pltpu.VMEM(blk, F32))
    res, jres = _call(
        body, name=name, grid=grid, in_specs=in_specs, args=args,
        out_specs=[pl.BlockSpec(blk, im) for _, _, blk, im in outs],
        out_shape=[jax.ShapeDtypeStruct(shp, dt) for shp, dt, _, _ in outs],
        scratch=scratch, sem=("parallel", "parallel", "arbitrary"), aliases=aliases, job=job)
    return res if job is None else (res, jres)


def _first(accs, extras):
    return (accs[0],)


def _mm_simple(name, a, b, dims, out_dtype, tm, tn, b_row0=0, b_rows=None):
    m, kk = a.shape
    tm = min(tm, m)
    if dims is NT:
        n = b.shape[0] if b_rows is None else b_rows
        assert b.shape[1] == kk and b_row0 % tn == 0
        b_op = (b, (tn, kk), lambda i, j, q: (j + b_row0 // tn, 0))
    else:
        assert b.shape[0] == kk
        n = b.shape[1]
        b_op = (b, (kk, tn), lambda i, j, q: (0, j))
    a_op = (a, (tm, kk), lambda i, j, q: (i, 0))
    out = ((m, n), out_dtype, (tm, tn), lambda i, j, q: (i, j))
    return _mm(name, (m // tm, n // tn, 1), [a_op], [b_op], [[(0, 0)]], dims, _first, [out])[0]


def _rowwise(name, fn, tiled, params, outs, partials=(), tm=TM, job=None):
    t = tiled[0].shape[0]
    steps = t // tm
    nt, npar, no, npart = len(tiled), len(params), len(outs), len(partials)

    def body(*refs):
        t_refs = refs[:nt]
        p_refs = refs[nt:nt + npar]
        o_refs = refs[nt + npar:nt + npar + no]
        s_refs = refs[nt + npar + no:nt + npar + no + npart]
        acc_refs = refs[nt + npar + no + npart:]
        i = pl.program_id(0)
        res = fn(*[r[...].astype(F32) for r in t_refs], *[r[...] for r in p_refs])
        for o, r in zip(o_refs, res[:no]):
            o[...] = r.astype(o.dtype)

        @pl.when(i == 0)
        def _():
            for acc in acc_refs:
                acc[...] = jnp.zeros_like(acc)

        for acc, r in zip(acc_refs, res[no:]):
            acc[...] += r.reshape(tm // 8, 8, r.shape[-1]).sum(axis=0)

        @pl.when(i == steps - 1)
        def _():
            for s, acc in zip(s_refs, acc_refs):
                s[...] = acc[...].sum(axis=0, keepdims=True)

    in_specs = [pl.BlockSpec((tm, a.shape[1]), lambda i: (i, 0)) for a in tiled]
    in_specs += [pl.BlockSpec(p.shape, lambda i: (0, 0)) for p in params]
    out_specs = [pl.BlockSpec((tm, c), lambda i: (i, 0)) for c, _ in outs]
    out_specs += [pl.BlockSpec((1, c), lambda i: (0, 0)) for c in partials]
    out_shape = [jax.ShapeDtypeStruct((t, c), dt) for c, dt in outs]
    out_shape += [jax.ShapeDtypeStruct((1, c), F32) for c in partials]
    res, jres = _call(body, name=name, grid=(steps,), in_specs=in_specs, args=[*tiled, *params], out_specs=out_specs,
                      out_shape=out_shape, scratch=[pltpu.VMEM((8, c), F32) for c in partials], sem=("arbitrary",),
                      job=job)
    return res if job is None else (res, jres)


def _rms(x):
    r = lax.rsqrt(jnp.mean(x * x, axis=-1, keepdims=True) + EPS)
    return x * r, r


def _rms_bwd(xhat, r, g, dy):
    dxh = dy * g
    dx = r * (dxh - xhat * jnp.mean(dxh * xhat, axis=-1, keepdims=True))
    return dx, dy * xhat


def _ffn_up(name, n, wa, part, tm=512, tf=1408, job=None):
    t = n.shape[0]

    def epi(accs, extras):
        a, b = accs
        return a, b, a * _sigmoid(a) * b

    a_op = (n, (tm, D), lambda f, i, q: (i, 0))
    b_ops = [(wa, (None, tf, D), lambda f, i, q: (part, f, 0)),
             (wa, (None, tf, D), lambda f, i, q: (part + 1, f, 0))]
    outs = [((t, FF), BF16, (tm, tf), lambda f, i, q: (i, f))] * 3
    return _mm(name, (FF // tf, t // tm, 1), [a_op], b_ops, [[(0, 0)], [(0, 1)]], NT, epi, outs, job=job)


def _ffn_down(name, s, wa, part, tm=512):
    t = s.shape[0]
    a_op = (s, (tm, FF), lambda i, j, q: (i, 0))
    b_op = (wa, (None, FF, D), lambda i, j, q: (part, 0, 0))
    out = ((t, D), F32, (tm, D), lambda i, j, q: (i, 0))
    return _mm(name, (t // tm, 1, 1), [a_op], [b_op], [[(0, 0)]], NN, _first, [out])[0]


def _ffn_bwd_act(name, df, wa, part, a, b, tm=512, tf=1408, job=None):
    t = df.shape[0]

    def epi(accs, extras):
        ds = accs[0]
        av, bv = extras[0].astype(F32), extras[1].astype(F32)
        sg = _sigmoid(av)
        da = ds * bv * (sg * (1.0 + av * (1.0 - sg)))
        db = ds * (av * sg)
        return (jnp.stack([da, db]),)

    a_op = (df, (tm, D), lambda f, i, q: (i, 0))
    b_op = (wa, (None, tf, D), lambda f, i, q: (part, f, 0))
    extras = [(a, (tm, tf), lambda f, i, q: (i, f)), (b, (tm, tf), lambda f, i, q: (i, f))]
    out = ((2, t, FF), BF16, (2, tm, tf), lambda f, i, q: (0, i, f))
    res = _mm(name, (FF // tf, t // tm, 1), [a_op], [b_op], [[(0, 0)]], NT, epi, [out], extras, job=job)
    return res[0] if job is None else (res[0][0], res[1])


def _ffn_bwd_in(name, dab, wa, part, tm=1024, job=None):
    t = dab.shape[1]
    tm = min(tm, t)
    a_op = (dab, (None, tm, FF), lambda i, j, q: (q, i, 0))
    b_op = (wa, (None, FF, D), lambda i, j, q: (part + q, 0, 0))
    out = ((t, D), F32, (tm, D), lambda i, j, q: (i, 0))
    res = _mm(name, (t // tm, 1, 2), [a_op], [b_op], [[(0, 0)]], NN, _first, [out], job=job)
    return res[0] if job is None else (res[0][0], res[1])


def _wgrad(name, dy_op, x, out_block, out_map, gi, carry_buf, out_shape, tk=512, job=None):
    t, c = x.shape
    b_op = (x, (tk, c), lambda i, j, q: (q, 0))
    out = (out_shape, BF16, out_block, out_map)
    carry = None if carry_buf is None else (carry_buf, 0)
    res = _mm(name, (gi, 1, t // tk), [dy_op], [b_op], [[(0, 0)]], TN, _first, [out], carry=carry, job=job)
    return res[0] if job is None else (res[0][0], res[1])


def _rel_onehot():
    j = lax.broadcasted_iota(jnp.int32, (384, RELW), 0)
    xx = lax.broadcasted_iota(jnp.int32, (384, RELW), 1)
    idx = jnp.clip(KPAD + QB - xx, -128, 128) + 128
    return (j == idx).astype(F32)


def _relbias_fwd(table):
    def body(t_ref, o_ref):
        rev = jnp.dot(t_ref[...], _rel_onehot(), precision=lax.Precision.HIGHEST, preferred_element_type=F32)
        for r in range(QB):
            row = pltpu.roll(rev, (RELW - (QB - r)) % RELW, 1)[:, :KW]
            rr = lax.broadcasted_iota(jnp.int32, (NH, KW), 1) >> 6
            ok = (rr >= (r // CHUNK)) & (rr <= (r // CHUNK) + LEFT)
            row = jnp.where(ok, row, -1e30)
            for h in range(NH):
                o_ref[h * QB + r:h * QB + r + 1, :] = row[h:h + 1, :]

    return pl.pallas_call(
        body,
        name="relbias_fwd",
        out_shape=jax.ShapeDtypeStruct((NH * QB, KW), F32),
        in_specs=[pl.BlockSpec(memory_space=pltpu.VMEM)],
        out_specs=pl.BlockSpec(memory_space=pltpu.VMEM),
        compiler_params=_params(),
    )(table)


def _relbias_bwd(dbias):
    def body(d_ref, o_ref, acc_ref):
        for h in range(NH):
            acc = jnp.zeros((1, RELW), F32)
            for r in range(QB):
                row = d_ref[h * QB + r:h * QB + r + 1, :]
                wide = jnp.concatenate([row, jnp.zeros((1, RELW - KW), F32)], axis=1)
                acc = acc + pltpu.roll(wide, QB - r, 1)
            acc_ref[h:h + 1, :] = acc
        o_ref[...] = lax.dot_general(acc_ref[...], _rel_onehot(), NT, precision=lax.Precision.HIGHEST,
                                     preferred_element_type=F32)

    return pl.pallas_call(
        body,
        name="relbias_bwd",
        out_shape=jax.ShapeDtypeStruct((NH, 384), F32),
        in_specs=[pl.BlockSpec(memory_space=pltpu.VMEM)],
        out_specs=pl.BlockSpec(memory_space=pltpu.VMEM),
        scratch_shapes=[pltpu.VMEM((NH, RELW), F32)],
        compiler_params=_params(),
    )(dbias)


def _stack_heads(x_pair):
    first = lax.broadcasted_iota(jnp.int32, (1, 128), 1) < 64
    zero = jnp.zeros_like(x_pair)
    return jnp.concatenate([jnp.where(first, x_pair, zero), jnp.where(first, zero, x_pair)], axis=0), first


def _pair_probs(qs, kw, bias, key_ok):
    s = lax.dot_general(qs, kw, NT, preferred_element_type=F32) + bias
    if key_ok is not None:
        s = jnp.where(key_ok, s, -1e30)
    e = jnp.exp(s - jnp.max(s, axis=-1, keepdims=True))
    return e * (1.0 / jnp.sum(e, axis=-1, keepdims=True))


def _attn_fwd(qkv, kvp, bias, job=None):
    t = qkv.shape[0]

    def body(q_ref, kv_ref, b_ref, o_ref):
        i = pl.program_id(0)

        def run(masked):
            start = pl.multiple_of(i * QB, QB)
            col = lax.broadcasted_iota(jnp.int32, (1, KW), 1)
            key_ok = (col >= KPAD - i * QB) if masked else None
            for pair in range(4):
                lo = pair * 128
                kw = kv_ref[pl.ds(start, KW), lo:lo + 128]
                vw = kv_ref[pl.ds(start, KW), DA + lo:DA + lo + 128]
                qs, first = _stack_heads(q_ref[:, lo:lo + 128])
                p = _pair_probs(qs * 0.125, kw, b_ref[2 * pair * QB:(2 * pair + 2) * QB, :], key_ok)
                o = jnp.dot(p.astype(BF16), vw, preferred_element_type=F32)
                o_ref[:, lo:lo + 128] = jnp.where(first, o[:QB], o[QB:]).astype(BF16)

        pl.when(i < KPAD // QB)(lambda: run(True))
        pl.when(i >= KPAD // QB)(lambda: run(False))

    res, jres = _call(
        body, name="attn_fwd", grid=(t // QB,),
        in_specs=[pl.BlockSpec((QB, DA), lambda i: (i, 0)),
                  pl.BlockSpec(memory_space=pltpu.VMEM),
                  pl.BlockSpec(memory_space=pltpu.VMEM)],
        args=[qkv, kvp, bias],
        out_specs=[pl.BlockSpec((QB, DA), lambda i: (i, 0))],
        out_shape=[jax.ShapeDtypeStruct((t, DA), BF16)],
        sem=("arbitrary",), job=job)
    return res[0], jres


def _attn_bwd(qkv, kvp, bias, datt, job=None):
    t = qkv.shape[0]
    nb = t // QB
    flush = (KW - QB) // QB

    def body(q_ref, kv_ref, b_ref, do_ref, dq_ref, dkv_ref, db_out, acc_ref, db_ref):
        i = pl.program_id(0)

        @pl.when(i == 0)
        def _():
            acc_ref[...] = jnp.zeros_like(acc_ref)
            db_ref[...] = jnp.zeros_like(db_ref)

        def run(masked):
            start = pl.multiple_of(i * QB, QB)
            col = lax.broadcasted_iota(jnp.int32, (1, KW), 1)
            key_ok = (col >= KPAD - i * QB) if masked else None
            for pair in range(4):
                lo = pair * 128
                kw = kv_ref[pl.ds(start, KW), lo:lo + 128]
                vw = kv_ref[pl.ds(start, KW), DA + lo:DA + lo + 128]
                qs, first = _stack_heads(q_ref[:, lo:lo + 128])
                qs = qs * 0.125
                dos, _ = _stack_heads(do_ref[:, lo:lo + 128])
                rows = slice(2 * pair * QB, (2 * pair + 2) * QB)
                p = _pair_probs(qs, kw, b_ref[rows, :], key_ok)
                dp = lax.dot_general(dos, vw, NT, preferred_element_type=F32)
                ds = p * (dp - jnp.sum(p * dp, axis=-1, keepdims=True))
                db_ref[rows, :] += ds
                dsb = ds.astype(BF16)
                dq = jnp.dot(dsb, kw, preferred_element_type=F32)
                dq_ref[:, lo:lo + 128] = (jnp.where(first, dq[:QB], dq[QB:]) * 0.125).astype(BF16)
                acc_ref[:, lo:lo + 128] += lax.dot_general(dsb, qs, TN, preferred_element_type=F32)
                acc_ref[:, DA + lo:DA + lo + 128] += lax.dot_general(p.astype(BF16), dos, TN,
                                                                     preferred_element_type=F32)

        pl.when(i < KPAD // QB)(lambda: run(True))
        pl.when(jnp.logical_and(i >= KPAD // QB, i < nb))(lambda: run(False))

        dkv_ref[...] = acc_ref[0:QB, :].astype(BF16)
        rest = acc_ref[QB:KW, :]
        acc_ref[0:KW - QB, :] = rest
        acc_ref[KW - QB:KW, :] = jnp.zeros((QB, 2 * DA), F32)

        @pl.when(i == nb + flush - 1)
        def _():
            pltpu.sync_copy(db_ref, db_out)

    last = nb - 1
    res, jres = _call(
        body, name="attn_bwd", grid=(nb + flush,),
        in_specs=[pl.BlockSpec((QB, DA), lambda i: (jnp.minimum(i, last), 0)),
                  pl.BlockSpec(memory_space=pltpu.VMEM),
                  pl.BlockSpec(memory_space=pltpu.VMEM),
                  pl.BlockSpec((QB, DA), lambda i: (jnp.minimum(i, last), 0))],
        args=[qkv, kvp, bias, datt],
        out_specs=[pl.BlockSpec((QB, DA), lambda i: (jnp.minimum(i, last), 0)),
                   pl.BlockSpec((QB, 2 * DA), lambda i: (i, 0)),
                   ANY],
        out_shape=[jax.ShapeDtypeStruct((t, DA), BF16),
                   jax.ShapeDtypeStruct((t + KPAD, 2 * DA), BF16),
                   jax.ShapeDtypeStruct((NH * QB, KW), F32)],
        scratch=[pltpu.VMEM((KW, 2 * DA), F32), pltpu.VMEM((NH * QB, KW), F32)], sem=("arbitrary",), job=job)
    return res, jres


def _glu(c, gb):
    cb = c + gb
    return cb[:, :DC] * _sigmoid(cb[:, DC:])


def _ln_swish(pre, g, b):
    mu = jnp.mean(pre, axis=-1, keepdims=True)
    xc = pre - mu
    r = lax.rsqrt(jnp.mean(xc * xc, axis=-1, keepdims=True) + EPS)
    xhat = xc * r
    y = xhat * g + b
    return xhat, r, y


RT = 32


def _shifted_copies(src_ref, sh_ref, rows):
    for b in range(1, 8):
        sh_ref[b - 1, :, :] = src_ref[pl.ds(b, rows), :]


def _tap(src_ref, sh_ref, off, r0, rows=RT):
    a, b = divmod(off, 8)
    ref = src_ref if b == 0 else sh_ref.at[b - 1]
    if isinstance(r0, int):
        return ref[r0 + 8 * a:r0 + 8 * a + rows, :]
    return ref[pl.ds(pl.multiple_of(r0 + 8 * a, 8), rows), :]


def _conv_fwd(cin, glu_b, dw_w, dw_b, ln_g, ln_b, tm=TM):
    t = cin.shape[0]
    hb = tm // HALO

    def body(c_ref, h_ref, gb_ref, w_ref, wb_ref, g_ref, b_ref, cs_ref, pre_ref, ext_ref, sh_ref):
        i = pl.program_id(0)
        halo = _glu(h_ref[...], gb_ref[...])
        ext_ref[0:HALO, :] = jnp.where(i > 0, halo, jnp.zeros_like(halo))
        ext_ref[HALO:HALO + tm, :] = _glu(c_ref[...], gb_ref[...])
        ext_ref[HALO + tm:HALO + tm + 8, :] = jnp.zeros((8, DC), F32)
        _shifted_copies(ext_ref, sh_ref, HALO + tm)

        def tile(rt, carry):
            r0 = pl.multiple_of(rt * RT, RT)
            acc = jnp.zeros((RT, DC), F32) + wb_ref[...]
            for j in range(CW):
                acc = acc + w_ref[j:j + 1, :] * _tap(ext_ref, sh_ref, HALO - (CW - 1) + j, r0)
            pre_ref[pl.ds(r0, RT), :] = acc
            return carry

        lax.fori_loop(0, tm // RT, tile, 0, unroll=2)
        _, _, y = _ln_swish(pre_ref[...], g_ref[...], b_ref[...])
        cs_ref[...] = (y * _sigmoid(y)).astype(BF16)

    vec = lambda n: pl.BlockSpec((1, n), lambda i: (0, 0))
    return pl.pallas_call(
        body,
        name="conv_fwd",
        grid=(t // tm,),
        in_specs=[pl.BlockSpec((tm, 2 * DC), lambda i: (i, 0)),
                  pl.BlockSpec((HALO, 2 * DC), lambda i: (jnp.maximum(i * hb - 1, 0), 0)),
                  vec(2 * DC), pl.BlockSpec((CW, DC), lambda i: (0, 0)), vec(DC), vec(DC), vec(DC)],
        out_specs=[pl.BlockSpec((tm, DC), lambda i: (i, 0)), pl.BlockSpec((tm, DC), lambda i: (i, 0))],
        out_shape=[jax.ShapeDtypeStruct((t, DC), BF16), jax.ShapeDtypeStruct((t, DC), F32)],
        scratch_shapes=[pltpu.VMEM((HALO + tm + 8, DC), F32), pltpu.VMEM((7, HALO + tm, DC), F32)],
        compiler_params=_params(("arbitrary",)),
    )(cin, cin, glu_b, dw_w, dw_b, ln_g, ln_b)


def _conv_bwd(dcs, pre, cin, glu_b, dw_w, ln_g, ln_b, tm=TM):
    t = cin.shape[0]
    hb = tm // HALO
    steps = t // tm
    nhb = t // HALO

    def dpre_of(dcs_v, pre_v, g, b):
        xhat, r, y = _ln_swish(pre_v, g, b)
        sg = _sigmoid(y)
        dy = dcs_v * (sg * (1.0 + y * (1.0 - sg)))
        dxh = dy * g
        dpre = r * (dxh - jnp.mean(dxh, axis=-1, keepdims=True)
                    - xhat * jnp.mean(dxh * xhat, axis=-1, keepdims=True))
        return dpre, dy * xhat, dy

    def body(dcs_ref, dcsn_ref, pre_ref, pren_ref, c_ref, ch_ref, gb_ref, w_ref, g_ref, b_ref,
             dc_ref, dgb_ref, dw_ref, dwb_ref, dg_ref, db_ref,
             gext_ref, dext_ref, shg_ref, shd_ref, a_gb, a_w, a_wb, a_g, a_b):
        i = pl.program_id(0)

        @pl.when(i == 0)
        def _():
            for a in (a_gb, a_w, a_wb, a_g, a_b):
                a[...] = jnp.zeros_like(a)

        fold = lambda v: v.reshape(v.shape[0] // 8, 8, v.shape[-1]).sum(axis=0)
        g, b = g_ref[...], b_ref[...]
        dpre, dg_t, db_t = dpre_of(dcs_ref[...], pre_ref[...], g, b)
        dpre_n, _, _ = dpre_of(dcsn_ref[...], pren_ref[...], g, b)
        dext_ref[0:tm, :] = dpre
        dext_ref[tm:tm + HALO, :] = jnp.where(i < steps - 1, dpre_n, jnp.zeros_like(dpre_n))
        dext_ref[tm + HALO:tm + HALO + 8, :] = jnp.zeros((8, DC), F32)
        a_wb[...] += fold(dpre)
        a_g[...] += fold(dg_t)
        a_b[...] += fold(db_t)
        halo = _glu(ch_ref[...], gb_ref[...])
        gext_ref[0:HALO, :] = jnp.where(i > 0, halo, jnp.zeros_like(halo))
        gext_ref[HALO:HALO + tm, :] = _glu(c_ref[...], gb_ref[...])
        gext_ref[HALO + tm:HALO + tm + 8, :] = jnp.zeros((8, DC), F32)
        _shifted_copies(gext_ref, shg_ref, HALO + tm)
        _shifted_copies(dext_ref, shd_ref, HALO + tm)

        for j in range(CW):
            a_w[8 * j:8 * j + 8, :] += fold(dext_ref[0:tm, :] * _tap(gext_ref, shg_ref, HALO - (CW - 1) + j, 0, tm))

        def tile(rt, carry):
            r0 = pl.multiple_of(rt * RT, RT)
            dglu = jnp.zeros((RT, DC), F32)
            for j in range(CW):
                dglu = dglu + w_ref[j:j + 1, :] * _tap(dext_ref, shd_ref, CW - 1 - j, r0)
            gext_ref[pl.ds(r0, RT), :] = dglu
            return carry

        lax.fori_loop(0, tm // RT, tile, 0, unroll=2)
        dglu = gext_ref[0:tm, :]
        cb = c_ref[...] + gb_ref[...]
        sg = _sigmoid(cb[:, DC:])
        dc = jnp.concatenate([dglu * sg, dglu * cb[:, :DC] * sg * (1.0 - sg)], axis=1)
        dc_ref[...] = dc.astype(BF16)
        a_gb[...] += fold(dc)

        @pl.when(i == steps - 1)
        def _():
            dgb_ref[...] = a_gb[...].sum(axis=0, keepdims=True)
            dw_ref[...] = a_w[...]
            dwb_ref[...] = a_wb[...].sum(axis=0, keepdims=True)
            dg_ref[...] = a_g[...].sum(axis=0, keepdims=True)
            db_ref[...] = a_b[...].sum(axis=0, keepdims=True)

    vec = lambda n: pl.BlockSpec((1, n), lambda i: (0, 0))
    nxt = lambda i: (jnp.minimum((i + 1) * hb, nhb - 1), 0)
    prv = lambda i: (jnp.maximum(i * hb - 1, 0), 0)
    return pl.pallas_call(
        body,
        name="conv_bwd",
        grid=(steps,),
        in_specs=[pl.BlockSpec((tm, DC), lambda i: (i, 0)), pl.BlockSpec((HALO, DC), nxt),
                  pl.BlockSpec((tm, DC), lambda i: (i, 0)), pl.BlockSpec((HALO, DC), nxt),
                  pl.BlockSpec((tm, 2 * DC), lambda i: (i, 0)), pl.BlockSpec((HALO, 2 * DC), prv),
                  vec(2 * DC), pl.BlockSpec((CW, DC), lambda i: (0, 0)), vec(DC), vec(DC)],
        out_specs=[pl.BlockSpec((tm, 2 * DC), lambda i: (i, 0)), vec(2 * DC),
                   pl.BlockSpec((CW * 8, DC), lambda i: (0, 0)), vec(DC), vec(DC), vec(DC)],
        out_shape=[jax.ShapeDtypeStruct((t, 2 * DC), BF16), jax.ShapeDtypeStruct((1, 2 * DC), F32),
                   jax.ShapeDtypeStruct((CW * 8, DC), F32), jax.ShapeDtypeStruct((1, DC), F32),
                   jax.ShapeDtypeStruct((1, DC), F32), jax.ShapeDtypeStruct((1, DC), F32)],
        scratch_shapes=[pltpu.VMEM((HALO + tm + 8, DC), F32), pltpu.VMEM((tm + HALO + 8, DC), F32),
                        pltpu.VMEM((7, HALO + tm, DC), F32), pltpu.VMEM((7, HALO + tm, DC), F32),
                        pltpu.VMEM((8, 2 * DC), F32), pltpu.VMEM((CW * 8, DC), F32),
                        pltpu.VMEM((8, DC), F32), pltpu.VMEM((8, DC), F32), pltpu.VMEM((8, DC), F32)],
        compiler_params=_params(("arbitrary",)),
    )(dcs, dcs, pre, pre, cin, cin, glu_b, dw_w, ln_g, ln_b)


def _place():
    x, y, c = lax.axis_index("x"), lax.axis_index("y"), lax.axis_index("c")
    return x, y, c


def _peers(x, y, c):
    out = []
    for k in range(1, NDEV):
        fx, fy, fc = (k >> 2) & 1, (k >> 1) & 1, k & 1
        px = 1 - x if fx else x
        py = 1 - y if fy else y
        pc = 1 - c if fc else c
        out.append((px, py, pc))
    return out


def _job_out_shapes(job):
    kind, arrays = job
    if kind == "gather":
        return [jax.ShapeDtypeStruct((a.shape[0], NDEV) + a.shape[1:], a.dtype) for a in arrays]
    return [jax.ShapeDtypeStruct((NDEV, a.shape[0]) + a.shape[2:], a.dtype) for a in arrays]


def _job_scratch(job):
    n = len(job[1])
    return [pltpu.SemaphoreType.DMA((n, NDEV - 1)), pltpu.SemaphoreType.DMA((n, NDEV - 1)),
            pltpu.SemaphoreType.DMA((n,))]


def _gather_parts(ins, outs, send_sems, recv_sems, local_sems):
    x, y, c = _place()
    me, sib = (x, y, c), (x, y, 1 - c)
    chips = [(1 - x, y), (x, 1 - y), (1 - x, 1 - y)]

    def copy(a, k, block, to, src=None):
        px, py, pc = block
        dst = outs[a].at[:, 4 * px + 2 * py + pc]
        return pltpu.make_async_remote_copy(
            src_ref=dst if src is None else src, dst_ref=dst,
            send_sem=send_sems.at[a, k], recv_sem=recv_sems.at[a, k], device_id=to, device_id_type=MESH)

    n = len(ins)
    local = [pltpu.make_async_copy(ins[a], outs[a].at[:, 4 * x + 2 * y + c], local_sems.at[a]) for a in range(n)]
    first = [[copy(a, 0, me, sib, src=ins[a])] + [copy(a, 1 + j, me, (*chip, c), src=ins[a])
                                                   for j, chip in enumerate(chips)] for a in range(n)]

    def start():
        for a in range(n):
            local[a].start()
            for cp in first[a]:
                cp.start()

    def finish():
        passed = []
        for j, chip in enumerate(chips):
            for a in range(n):
                copy(a, 1 + j, (*chip, c), me).wait_recv()
                cp = copy(a, 4 + j, (*chip, c), sib)
                cp.start()
                passed.append(cp)
        for a in range(n):
            copy(a, 0, sib, me).wait_recv()
            for j, chip in enumerate(chips):
                copy(a, 4 + j, (*chip, 1 - c), me).wait_recv()
        for a in range(n):
            for cp in first[a]:
                cp.wait_send()
            local[a].wait()
        for cp in passed:
            cp.wait_send()

    return start, finish


def _exchange_parts(ins, outs, send_sems, recv_sems, local_sems):
    x, y, c = _place()
    me = 4 * x + 2 * y + c
    n = len(ins)
    peers = _peers(x, y, c)
    local = [pltpu.make_async_copy(ins[a].at[:, me], outs[a].at[me], local_sems.at[a]) for a in range(n)]

    def copy(a, k):
        px, py, pc = peers[k]
        return pltpu.make_async_remote_copy(
            src_ref=ins[a].at[:, 4 * px + 2 * py + pc], dst_ref=outs[a].at[me],
            send_sem=send_sems.at[a, k], recv_sem=recv_sems.at[a, k], device_id=peers[k], device_id_type=MESH)

    def arrival(a, k):
        px, py, pc = peers[k]
        return pltpu.make_async_remote_copy(
            src_ref=ins[a].at[:, me], dst_ref=outs[a].at[4 * px + 2 * py + pc],
            send_sem=send_sems.at[a, k], recv_sem=recv_sems.at[a, k], device_id=peers[k], device_id_type=MESH)

    def start():
        for a in range(n):
            local[a].start()
            for k in range(NDEV - 1):
                copy(a, k).start()

    def finish():
        for a in range(n):
            for k in range(NDEV - 1):
                arrival(a, k).wait_recv()
        for a in range(n):
            for k in range(NDEV - 1):
                copy(a, k).wait_send()
            local[a].wait()

    return start, finish


def _call(body, *, name, grid, in_specs, args, out_specs, out_shape, scratch=(), sem=None, aliases=None, job=None):
    aliases = dict(aliases or {})
    if job is None:
        res = pl.pallas_call(
            body, name=name, grid=grid, in_specs=list(in_specs), out_specs=list(out_specs),
            out_shape=list(out_shape), scratch_shapes=list(scratch), input_output_aliases=aliases,
            compiler_params=_params(sem))(*args)
        return list(res), []
    kind, arrays = job
    n_in, n_out, n_scr, nj = len(args), len(out_shape), len(scratch), len(arrays)

    def wrapped(*refs):
        ins = refs[:n_in]
        jin = refs[n_in:n_in + nj]
        o0 = n_in + nj
        outs = refs[o0:o0 + n_out]
        jout = refs[o0 + n_out:o0 + n_out + nj]
        s0 = o0 + n_out + nj
        scr = refs[s0:s0 + n_scr]
        sems = refs[s0 + n_scr:]
        parts = _gather_parts if kind == "gather" else _exchange_parts
        start, finish = parts(jin, jout, *sems)
        if not grid:
            start()
            body(*ins, *outs, *scr)
            finish()
            return
        first = last = None
        for d, g in enumerate(grid):
            f, l = pl.program_id(d) == 0, pl.program_id(d) == g - 1
            first = f if first is None else jnp.logical_and(first, f)
            last = l if last is None else jnp.logical_and(last, l)
        pl.when(first)(start)
        body(*ins, *outs, *scr)
        pl.when(last)(finish)

    res = pl.pallas_call(
        wrapped, name=name, grid=grid, in_specs=list(in_specs) + [ANY] * nj,
        out_specs=list(out_specs) + [ANY] * nj, out_shape=list(out_shape) + _job_out_shapes(job),
        scratch_shapes=list(scratch) + _job_scratch(job), input_output_aliases=aliases,
        compiler_params=pltpu.CompilerParams(
            dimension_semantics=None if not grid else ("arbitrary",) * len(grid),
            vmem_limit_bytes=VMEM_LIMIT, has_side_effects=True))(*args, *arrays)
    return list(res[:n_out]), list(res[n_out:])


def _comm_only(name, job):
    return _call(lambda: None, name=name, grid=(), in_specs=[], args=[], out_specs=[], out_shape=[], job=job)[1]


def _sum_devices(name, parts, tr):
    _, r, c = parts.shape

    def body(p_ref, o_ref):
        acc = p_ref[0].astype(F32)
        for d in range(1, NDEV):
            acc = acc + p_ref[d].astype(F32)
        o_ref[...] = acc

    return pl.pallas_call(
        body,
        name=name,
        grid=(r // tr,),
        in_specs=[pl.BlockSpec((NDEV, tr, c), lambda i: (0, i, 0))],
        out_specs=pl.BlockSpec((tr, c), lambda i: (i, 0)),
        out_shape=jax.ShapeDtypeStruct((r, c), F32),
        compiler_params=_params(("parallel",)),
    )(parts)


def _adamw(name, w, g, m, v, tr=None):
    r, c = w.shape
    tr = r if tr is None else tr

    def body(w_ref, g_ref, m_ref, v_ref, d_ref, nm_ref, nv_ref):
        gv = g_ref[...]
        nm = ADAM_B1 * m_ref[...] + (1.0 - ADAM_B1) * gv
        nv = ADAM_B2 * v_ref[...] + (1.0 - ADAM_B2) * (gv * gv)
        m_hat = nm / (1.0 - ADAM_B1 ** ADAM_STEP)
        v_hat = nv / (1.0 - ADAM_B2 ** ADAM_STEP)
        d_ref[...] = -ADAM_LR * (m_hat / (jnp.sqrt(v_hat) + ADAM_EPS) + ADAM_WD * w_ref[...])
        nm_ref[...] = nm
        nv_ref[...] = nv

    spec = pl.BlockSpec((tr, c), lambda i: (i, 0))
    return pl.pallas_call(
        body,
        name=name,
        grid=(r // tr,),
        in_specs=[spec] * 4,
        out_specs=[spec] * 3,
        out_shape=[jax.ShapeDtypeStruct((r, c), F32)] * 3,
        compiler_params=_params(("parallel",)),
    )(w, g, m, v)


SMALL = ["ffn1_norm_pre", "ffn1_norm_post", "mix_norm_pre", "gate_bias", "rel_table", "conv_glu_bias",
         "conv_dw_b", "conv_ln_g", "conv_ln_b", "mix_norm_post", "ffn2_norm_pre", "ffn2_norm_post"]
SMALL_ROWS = 24
DW_ROWS = 32


def _pack_small(vals, extra=None):
    rows = []
    for name in SMALL:
        v = vals[name]
        if name == "rel_table":
            v = v.reshape(NH, -1)
            rows.append(jnp.pad(v, ((0, 0), (0, D - v.shape[1]))))
        else:
            v = v.reshape(-1)
            v = jnp.pad(v, (0, (-v.shape[0]) % D))
            rows.append(v.reshape(-1, D))
    if extra is not None:
        rows.append(extra)
    out = jnp.concatenate(rows, axis=0)
    return jnp.pad(out, ((0, SMALL_ROWS - out.shape[0]), (0, 0)))


LOSS_ROW = 20


def _small_update(parts, wp, mp, vp, shapes):
    layout = []
    r = 0
    for name in SMALL:
        n = 1
        for s in shapes[name]:
            n *= s
        if name == "rel_table":
            layout.append((name, r, NH, D))
            r += NH
        else:
            nr = -(-n // D)
            layout.append((name, r, nr, n // nr))
            r += nr

    def body(p_ref, w_ref, m_ref, v_ref, *o_refs):
        g = p_ref[0]
        for d in range(1, NDEV):
            g = g + p_ref[d]
        nm = ADAM_B1 * m_ref[...] + (1.0 - ADAM_B1) * g
        nv = ADAM_B2 * v_ref[...] + (1.0 - ADAM_B2) * (g * g)
        m_hat = nm / (1.0 - ADAM_B1 ** ADAM_STEP)
        v_hat = nv / (1.0 - ADAM_B2 ** ADAM_STEP)
        delta = -ADAM_LR * (m_hat / (jnp.sqrt(v_hat) + ADAM_EPS) + ADAM_WD * w_ref[...])
        k = 0
        for _, r0, nr, nc in layout:
            for val in (g, delta, nm, nv):
                o = o_refs[k]
                k += 1
                if nr == NH:
                    o[...] = val[r0:r0 + NH, :]
                else:
                    for q in range(nr):
                        o[:, q * nc:(q + 1) * nc] = val[r0 + q:r0 + q + 1, 0:nc]
        o_refs[k][...] = g[LOSS_ROW:LOSS_ROW + 1, :]

    out_shape = []
    for _, r0, nr, nc in layout:
        shp = (NH, D) if nr == NH else (1, nr * nc)
        out_shape += [jax.ShapeDtypeStruct(shp, F32)] * 4
    out_shape.append(jax.ShapeDtypeStruct((1, D), F32))
    vm = pl.BlockSpec(memory_space=pltpu.VMEM)
    res = pl.pallas_call(body, name="small_update", out_shape=out_shape, in_specs=[vm] * 4,
                         out_specs=[vm] * len(out_shape), compiler_params=_params())(parts, wp, mp, vp)
    out = {}
    for q, (name, r0, nr, nc) in enumerate(layout):
        vals = res[4 * q:4 * q + 4]
        if nr == NH:
            vals = [v[:, :NREL] for v in vals]
        out[name] = [v.reshape(shapes[name]) for v in vals]
    return out, res[-1]


def kernel(x, ffn1_norm_pre, ffn1_w_gate, ffn1_w_up, ffn1_w_down, ffn1_norm_post, mix_norm_pre, w_in, gate_bias, rel_table, w_attn_out, conv_glu_bias, conv_dw_w, conv_dw_b, conv_ln_g, conv_ln_b, conv_w_out, w_out, mix_norm_post, ffn2_norm_pre, ffn2_w_gate, ffn2_w_up, ffn2_w_down, ffn2_norm_post, loss_target, m_ffn1_norm_pre, m_ffn1_w_gate, m_ffn1_w_up, m_ffn1_w_down, m_ffn1_norm_post, m_mix_norm_pre, m_w_in, m_gate_bias, m_rel_table, m_w_attn_out, m_conv_glu_bias, m_conv_dw_w, m_conv_dw_b, m_conv_ln_g, m_conv_ln_b, m_conv_w_out, m_w_out, m_mix_norm_post, m_ffn2_norm_pre, m_ffn2_w_gate, m_ffn2_w_up, m_ffn2_w_down, m_ffn2_norm_post, v_ffn1_norm_pre, v_ffn1_w_gate, v_ffn1_w_up, v_ffn1_w_down, v_ffn1_norm_post, v_mix_norm_pre, v_w_in, v_gate_bias, v_rel_table, v_w_attn_out, v_conv_glu_bias, v_conv_dw_w, v_conv_dw_b, v_conv_ln_g, v_conv_ln_b, v_conv_w_out, v_w_out, v_mix_norm_post, v_ffn2_norm_pre, v_ffn2_w_gate, v_ffn2_w_up, v_ffn2_w_down, v_ffn2_norm_post):
    return _step(dict(locals()))


WEIGHTS = ["ffn1_norm_pre", "ffn1_w_gate", "ffn1_w_up", "ffn1_w_down", "ffn1_norm_post", "mix_norm_pre", "w_in",
           "gate_bias", "rel_table", "w_attn_out", "conv_glu_bias", "conv_dw_w", "conv_dw_b", "conv_ln_g",
           "conv_ln_b", "conv_w_out", "w_out", "mix_norm_post", "ffn2_norm_pre", "ffn2_w_gate", "ffn2_w_up",
           "ffn2_w_down", "ffn2_norm_post"]
FS = FF // NDEV
PS = (3 * DA + 2 * DC + 2 * D) // NDEV
OS = D // NDEV


def _local_step(xs, target, w, rel_table):
    t = xs.shape[0]
    vec = lambda n: w[n].reshape(1, -1)
    g_pre1, g_post1, g_mix, g_mixp = vec("ffn1_norm_pre"), vec("ffn1_norm_post"), vec("mix_norm_pre"), vec("mix_norm_post")
    g_pre2, g_post2 = vec("ffn2_norm_pre"), vec("ffn2_norm_post")
    gate_b, glu_b = vec("gate_bias"), vec("conv_glu_bias")
    dw_b, ln_g, ln_b = vec("conv_dw_b"), vec("conv_ln_g"), vec("conv_ln_b")

    tr = lambda a: jnp.transpose(a[0]).astype(BF16)
    sh_gu1 = jnp.stack([tr(w["ffn1_w_gate"]), tr(w["ffn1_w_up"])])
    sh_mid = [w["ffn1_w_down"].astype(BF16), tr(w["w_in"])[None], w["w_out"].astype(BF16),
              jnp.stack([tr(w["w_attn_out"]), tr(w["conv_w_out"])]),
              jnp.pad(w["conv_dw_w"][0, :, 0, :], ((0, DW_ROWS - CW), (0, 0)))[None]]
    sh_2 = jnp.stack([tr(w["ffn2_w_gate"]), tr(w["ffn2_w_up"]), w["ffn2_w_down"][0].astype(BF16)])

    (n1,), (w_gu1,) = _rowwise("pre1", lambda xv, g: ((_rms(xv)[0] * g),), [xs], [g_pre1], [(D, BF16)],
                               job=("gather", [sh_gu1]))
    w_gu1 = w_gu1.reshape(2, FF, D)
    (a1, b1, s1), (w_d1, wb, wc, wd, we) = _ffn_up("ffn1_up", n1, w_gu1, 0, job=("gather", sh_mid))
    w_d1, wb, wc, wd = w_d1.reshape(1, FF, D), wb.reshape(NDEV * PS, D), wc.reshape(D, D), wd.reshape(2, D, DA)
    dw_full = jnp.transpose(we[0], (1, 0, 2)).reshape(DW_ROWS, DC)[:CW]
    f1 = _ffn_down("ffn1_down", s1, w_d1, 0)

    def post1(xv, fv, gp, gm):
        h = xv + 0.5 * (_rms(fv)[0] * gp)
        return h, _rms(h)[0] * gm

    h1, u = _rowwise("post1", post1, [xs, f1], [g_post1, g_mix], [(D, F32), (D, BF16)])

    qkv = _mm_simple("proj_qkv", u, wb, NT, BF16, 1024, 512, b_row0=0, b_rows=3 * DA)
    cin = _mm_simple("proj_conv", u, wb, NT, F32, 1024, 512, b_row0=3 * DA, b_rows=2 * DC)
    gg = _mm_simple("proj_gate", u, wb, NT, BF16, 1024, 512, b_row0=3 * DA + 2 * DC, b_rows=2 * D)

    bias = _relbias_fwd(jnp.pad(rel_table[0], ((0, 0), (0, 384 - NREL))))
    kvp = jnp.pad(qkv[:, DA:], ((KPAD, 0), (0, 0)))
    att, (w_2,) = _attn_fwd(qkv, kvp, bias, job=("gather", [sh_2]))
    w_2 = w_2.reshape(3, FF, D)
    cs, pre = _conv_fwd(cin, glu_b, dw_full, dw_b, ln_g, ln_b)
    ya = _mm_simple("attn_out", att, wd[0], NT, BF16, 1024, 1024)
    yb = _mm_simple("conv_out", cs, wd[1], NT, BF16, 1024, 1024)

    def merge(yav, ybv, gv, gb):
        gates = _sigmoid(gv + gb)
        return (gates[:, :D] * yav + gates[:, D:] * ybv,)

    (merged,) = _rowwise("merge", merge, [ya, yb, gg], [gate_b], [(D, BF16)])
    mm_ = _mm_simple("mix_out", merged, wc, NN, F32, 1024, 1024)

    def postm(hv, mv, gp, g2):
        h = hv + _rms(mv)[0] * gp
        return h, _rms(h)[0] * g2

    h2, n2 = _rowwise("postm", postm, [h1, mm_], [g_mixp, g_pre2], [(D, F32), (D, BF16)])
    a2, b2, s2 = _ffn_up("ffn2_up", n2, w_2, 0)
    f2 = _ffn_down("ffn2_down", s2, w_2, 2)

    def post2(hv, fv, tv, gp):
        fh, r = _rms(fv)
        yv = hv + 0.5 * (fh * gp)
        err = yv - tv
        dy = err * (1.0 / D)
        df, dg = _rms_bwd(fh, r, gp, 0.5 * dy)
        return dy, df, (0.5 / D) * (err * err), dg

    dy, df2, loss_row, d_post2 = _rowwise("post2", post2, [h2, f2, target], [g_post2],
                                          [(D, F32), (D, BF16)], [D, D])

    tmw = 1408
    nfi = FF // tmw
    tkf = min(2048, t)
    tkp = min(1024, t)

    def wgrad_down(name, s, df, shape, part, carry_buf, job=None):
        return _wgrad(name, (s, (tkf, tmw), lambda i, j, q: (q, i)), df,
                      (None, tmw, D), lambda i, j, q: (part, i, 0), nfi, carry_buf, shape, tk=tkf, job=job)

    def wgrad_gate_up(name, dab, nrm, shape, carry_buf, job=None):
        return _wgrad(name, (dab, (None, tkf, tmw), lambda i, j, q: (i // nfi, q, i % nfi)), nrm,
                      (None, tmw, D), lambda i, j, q: (i // nfi, i % nfi, 0), 2 * nfi, carry_buf, shape,
                      tk=tkf, job=job)

    g_2 = wgrad_down("ffn2_wgrad_d", s2, df2, (3, FF, D), 2, None)
    dab2 = _ffn_bwd_act("ffn2_bwd_act", df2, w_2, 2, a2, b2)
    g_2 = wgrad_gate_up("ffn2_wgrad_gu", dab2, n2, (3, FF, D), g_2)
    dn2 = _ffn_bwd_in("ffn2_bwd_in", dab2, w_2, 0)

    def bwd_pre2(hv, dnv, dyv, mv, g2, gp):
        hh, r = _rms(hv)
        dx, dg2 = _rms_bwd(hh, r, g2, dnv)
        dh = dyv + dx
        mh, rm = _rms(mv)
        dm, dgp = _rms_bwd(mh, rm, gp, dh)
        return dh, dm, dg2, dgp

    dh2, dm, d_pre2, d_mixp = _rowwise("bwd_pre2", bwd_pre2, [h2, dn2, dy, mm_], [g_pre2, g_mixp],
                                       [(D, F32), (D, BF16)], [D, D])
    dmerged = _mm_simple("mix_out_bwd", dm, wc, NT, BF16, 1024, 1024)

    def merge_bwd(dmv, yav, ybv, gv, gb):
        gates = _sigmoid(gv + gb)
        ga, gbb = gates[:, :D], gates[:, D:]
        dgg = jnp.concatenate([dmv * yav * ga * (1.0 - ga), dmv * ybv * gbb * (1.0 - gbb)], axis=1)
        return dmv * ga, dmv * gbb, dgg, dgg

    dya, dyb, dgg, d_gate_b = _rowwise("merge_bwd", merge_bwd, [dmerged, ya, yb, gg], [gate_b],
                                       [(D, BF16), (D, BF16), (2 * D, BF16)], [2 * D])
    datt = _mm_simple("attn_out_bwd", dya, wd[0], NN, BF16, 1024, 512)
    dcs = _mm_simple("conv_out_bwd", dyb, wd[1], NN, F32, 1024, 512)
    g_c = _wgrad("mix_out_wgrad", (merged, (tkp, D), lambda i, j, q: (q, 0)), dm,
                 (D, D), lambda i, j, q: (0, 0), 1, None, (D, D), tk=tkp)
    g_d = _wgrad("attn_out_wgrad", (dya, (tkp, D), lambda i, j, q: (q, 0)), att,
                 (None, D, DA), lambda i, j, q: (0, 0, 0), 1, None, (2, D, DA), tk=tkp)
    g_d = _wgrad("conv_out_wgrad", (dyb, (tkp, D), lambda i, j, q: (q, 0)), cs,
                 (None, D, DA), lambda i, j, q: (1, 0, 0), 1, g_d, (2, D, DA), tk=tkp)
    (dq, dkvp, dbias), (x_2, x_c, x_d) = _attn_bwd(
        qkv, kvp, bias, datt,
        job=("exchange", [g_2.reshape(3, NDEV, FS, D), g_c.reshape(1, NDEV, OS, D), g_d.reshape(2, NDEV, OS, DA)]))
    d_rel = _relbias_bwd(dbias)
    dcin, d_glu_b, d_dw8, d_dw_b, d_ln_g, d_ln_b = _conv_bwd(dcs, pre, cin, glu_b, dw_full, ln_g, ln_b)
    g_dw = jnp.pad(d_dw8, ((0, 8 * (DW_ROWS - CW)), (0, 0)))
    g_dw = g_dw.reshape(DW_ROWS * 8, NDEV, DC // NDEV).transpose(1, 0, 2)

    top = lambda i, j, q: (0, 0)
    g_b = jnp.concatenate([
        _wgrad("proj_wgrad_q", (dq, (tkp, DA), lambda i, j, q: (q, 0)), u, (DA, D), top, 1, None, (DA, D), tk=tkp),
        _wgrad("proj_wgrad_kv", (dkvp, (KPAD, 2 * DA), lambda i, j, q: (q + 1, 0)), u, (2 * DA, D), top, 1, None,
               (2 * DA, D), tk=KPAD),
        _wgrad("proj_wgrad_c", (dcin, (tkp, 2 * DC), lambda i, j, q: (q, 0)), u, (2 * DC, D), top, 1, None,
               (2 * DC, D), tk=tkp),
        _wgrad("proj_wgrad_g", (dgg, (tkp, 2 * D), lambda i, j, q: (q, 0)), u, (2 * D, D), top, 1, None,
               (2 * D, D), tk=tkp)], axis=0)

    tmu = 512
    a_ops = [(dq, (tmu, DA), lambda i, j, q: (i, 0)),
             (dkvp, (tmu, 2 * DA), lambda i, j, q: (i + KPAD // tmu, 0)),
             (dcin, (tmu, 2 * DC), lambda i, j, q: (i, 0)),
             (dgg, (tmu, 2 * D), lambda i, j, q: (i, 0))]
    whole = lambda i, j, q: (0, 0)
    b_ops = [(wb[:DA], (DA, D), whole), (wb[DA:3 * DA], (2 * DA, D), whole),
             (wb[3 * DA:3 * DA + 2 * DC], (2 * DC, D), whole), (wb[3 * DA + 2 * DC:], (2 * D, D), whole)]
    du = _mm("proj_bwd", (t // tmu, 1, 1), a_ops, b_ops, [[(0, 0), (1, 1), (2, 2), (3, 3)]], NN, _first,
             [((t, D), F32, (tmu, D), lambda i, j, q: (i, 0))])[0]

    def bwd_mix(hv, duv, dhv, fv, gm, gp):
        hh, r = _rms(hv)
        dx, dgm = _rms_bwd(hh, r, gm, duv)
        dh = dhv + dx
        fh, rf = _rms(fv)
        df, dgp = _rms_bwd(fh, rf, gp, 0.5 * dh)
        return dh, df, dgm, dgp

    dh1, df1, d_mix, d_post1 = _rowwise("bwd_mix", bwd_mix, [h1, du, dh2, f1], [g_mix, g_post1],
                                        [(D, F32), (D, BF16)], [D, D])
    g_d1 = wgrad_down("ffn1_wgrad_d", s1, df1, (1, FF, D), 0, None)
    dab1, (x_b, x_dw) = _ffn_bwd_act("ffn1_bwd_act", df1, w_d1, 0, a1, b1,
                                     job=("exchange", [g_b.reshape(1, NDEV, PS, D), g_dw[None]]))
    g_gu1, (x_d1,) = wgrad_gate_up("ffn1_wgrad_gu", dab1, n1, (2, FF, D), None,
                                   job=("exchange", [g_d1.reshape(1, NDEV, FS, D)]))
    dn1, (x_gu1,) = _ffn_bwd_in("ffn1_bwd_in", dab1, w_gu1, 0, job=("exchange", [g_gu1.reshape(2, NDEV, FS, D)]))

    def bwd_pre1(xv, dnv, dhv, g1):
        xh, r = _rms(xv)
        dx, dg1 = _rms_bwd(xh, r, g1, dnv)
        return dhv + dx, dg1

    dx, d_pre1 = _rowwise("bwd_pre1", bwd_pre1, [xs, dn1, dh1], [g_pre1], [(D, F32)], [D])

    small_g = {"ffn1_norm_pre": d_pre1, "ffn1_norm_post": d_post1, "mix_norm_pre": d_mix, "gate_bias": d_gate_b,
               "rel_table": d_rel[:, :NREL], "conv_glu_bias": d_glu_b, "conv_dw_b": d_dw_b, "conv_ln_g": d_ln_g,
               "conv_ln_b": d_ln_b, "mix_norm_post": d_mixp, "ffn2_norm_pre": d_pre2, "ffn2_norm_post": d_post2}
    return loss_row, dx, (x_gu1, x_d1, x_2, x_b, x_c, x_d, x_dw), small_g


def _step(args):
    names = WEIGHTS
    w = {n: args[n] for n in names}
    fs, ps, os_ = FS, PS, OS
    conv_dw_w = args["conv_dw_w"]
    loss_row, dx, (x_gu1, x_d1, x_2, x_b, x_c, x_d, x_dw), small_g = _local_step(
        args["x"][0], args["loss_target"][0], w, args["rel_table"])

    g_small = _pack_small(small_g, extra=loss_row)
    (x_s,) = _comm_only("gather_small_grads", ("gather", [g_small[None]]))

    s_gu1 = _sum_devices("sum_ffn1_gu", x_gu1.reshape(NDEV, 2 * fs, D), fs).reshape(2, fs, D)
    s_d1 = _sum_devices("sum_ffn1_d", x_d1.reshape(NDEV, fs, D), fs)
    s_2 = _sum_devices("sum_ffn2", x_2.reshape(NDEV, 3 * fs, D), fs).reshape(3, fs, D)
    s_b = _sum_devices("sum_proj", x_b.reshape(NDEV, ps, D), ps)
    s_c = _sum_devices("sum_mix", x_c.reshape(NDEV, os_, D), os_)
    s_d = _sum_devices("sum_out", x_d.reshape(NDEV, 2 * os_, DA), 2 * os_).reshape(2, os_, DA)
    s_dw = _sum_devices("sum_dw", x_dw.reshape(NDEV, DW_ROWS * 8, DC // NDEV), DW_ROWS * 8)

    grads = {
        "ffn1_w_gate": jnp.transpose(s_gu1[0])[None], "ffn1_w_up": jnp.transpose(s_gu1[1])[None], "ffn1_w_down": s_d1[None],
        "ffn2_w_gate": jnp.transpose(s_2[0])[None], "ffn2_w_up": jnp.transpose(s_2[1])[None], "ffn2_w_down": s_2[2][None],
        "w_in": jnp.transpose(s_b)[None], "w_out": s_c[None],
        "w_attn_out": jnp.transpose(s_d[0])[None], "conv_w_out": jnp.transpose(s_d[1])[None],
    }
    shapes = {n: w[n].shape for n in SMALL}
    deltas, new_m, new_v = {}, {}, {}
    small, loss_terms = _small_update(
        x_s.reshape(NDEV, SMALL_ROWS, D), _pack_small({n: w[n] for n in SMALL}),
        _pack_small({n: args["m_" + n] for n in SMALL}), _pack_small({n: args["v_" + n] for n in SMALL}), shapes)
    for n in SMALL:
        grads[n], deltas[n], new_m[n], new_v[n] = small[n]

    big = ["ffn1_w_gate", "ffn1_w_up", "ffn1_w_down", "w_in", "w_attn_out", "conv_w_out", "w_out",
           "ffn2_w_gate", "ffn2_w_up", "ffn2_w_down"]
    for n in big:
        shp = w[n].shape
        two = lambda a: a.reshape(shp[1], shp[2])
        rows = shp[1]
        tr_ = rows // 2 if rows % 16 == 0 else rows
        d_, m_, v_ = _adamw("adamw_" + n, two(w[n]), two(grads[n]), two(args["m_" + n]), two(args["v_" + n]), tr_)
        deltas[n], new_m[n], new_v[n] = d_.reshape(shp), m_.reshape(shp), v_.reshape(shp)

    g_dw_own = _fold8("fold_dw", s_dw)[:CW]
    grads["conv_dw_w"] = g_dw_own.reshape(1, CW, 1, DC // NDEV)
    flat = lambda a: a.reshape(CW, DC // NDEV)
    d_, m_, v_ = _adamw("adamw_dw", flat(conv_dw_w), g_dw_own, flat(args["m_conv_dw_w"]), flat(args["v_conv_dw_w"]))
    shp = conv_dw_w.shape
    deltas["conv_dw_w"], new_m["conv_dw_w"], new_v["conv_dw_w"] = d_.reshape(shp), m_.reshape(shp), v_.reshape(shp)

    loss = jnp.sum(loss_terms)
    return (loss, dx[None], *[grads[n] for n in names], *[deltas[n] for n in names],
            *[new_m[n] for n in names], *[new_v[n] for n in names])


def _fold8(name, a):
    r8, c = a.shape

    def body(a_ref, o_ref):
        o_ref[...] = a_ref[...].reshape(r8 // 8, 8, c).sum(axis=1)

    return pl.pallas_call(
        body,
        name=name,
        out_shape=jax.ShapeDtypeStruct((r8 // 8, c), F32),
        in_specs=[pl.BlockSpec(memory_space=pltpu.VMEM)],
        out_specs=pl.BlockSpec(memory_space=pltpu.VMEM),
        compiler_params=_params(),
    )(a)
```

```python
import functools

import jax
import jax.numpy as jnp
from jax import lax
from jax.experimental import pallas as pl
from jax.experimental.pallas import tpu as pltpu

F32 = jnp.float32
BF16 = jnp.bfloat16

D = 1024
FF = 2816
DA = 512
DC = 512
NH = 8
CHUNK = 64
LEFT = 8
CW = 31
NREL = 257
EPS = 1e-6
NDEV = 8

QB = 4 * CHUNK
KW = LEFT * CHUNK + QB
KPAD = LEFT * CHUNK
RELW = KW + QB
HALO = 32

TM = 512
VMEM_LIMIT = 56 * 1024 * 1024

ADAM_LR, ADAM_B1, ADAM_B2, ADAM_EPS, ADAM_WD, ADAM_STEP = 0.001, 0.9, 0.999, 1e-08, 0.01, 10

NT = (((1,), (1,)), ((), ()))
NN = (((1,), (0,)), ((), ()))
TN = (((0,), (0,)), ((), ()))

MESH = pl.DeviceIdType.MESH
ANY = pl.BlockSpec(memory_space=pl.ANY)


def _params(sem=None, vmem=VMEM_LIMIT):
    return pltpu.CompilerParams(dimension_semantics=sem, vmem_limit_bytes=vmem)


def _sigmoid(x):
    return 0.5 * jnp.tanh(0.5 * x) + 0.5


def _mm(name, grid, a_ops, b_ops, groups, dims, epi, outs, extras=(), carry=None, job=None, params=(),
        partials=()):
    nk = grid[2]
    na, nb, ne, no, ng = len(a_ops), len(b_ops), len(extras), len(outs), len(groups)
    npar, npart = len(params), len(partials)
    nc = 0 if carry is None else 1

    def body(*refs):
        a_refs = refs[:na]
        b_refs = refs[na:na + nb]
        e_refs = refs[na + nb:na + nb + ne]
        p_refs = refs[na + nb + ne:na + nb + ne + npar]
        o0 = na + nb + ne + npar + nc
        o_refs = refs[o0:o0 + no]
        s_refs = refs[o0 + no:o0 + no + npart]
        acc_refs = refs[o0 + no + npart:o0 + no + npart + (ng if nk > 1 else 0)]
        part_refs = refs[len(refs) - npart:] if npart else ()
        k = pl.program_id(2)
        prods = []
        for grp in groups:
            p = None
            for ai, bi in grp:
                t = lax.dot_general(a_refs[ai][...], b_refs[bi][...], dims, preferred_element_type=F32)
                p = t if p is None else p + t
            prods.append(p)

        def finish(vals):
            res = epi(vals, [e[...] for e in e_refs] + [p[...] for p in p_refs])
            for o, r in zip(o_refs, res[:no]):
                o[...] = r.astype(o.dtype)
            if npart:
                i, j = pl.program_id(0), pl.program_id(1)

                @pl.when(jnp.logical_and(i == 0, j == 0))
                def _():
                    for acc in part_refs:
                        acc[...] = jnp.zeros_like(acc)

                for acc, r in zip(part_refs, res[no:]):
                    acc[...] += r.reshape(r.shape[0] // 8, 8, r.shape[-1]).sum(axis=0)

                @pl.when(jnp.logical_and(i == grid[0] - 1, j == grid[1] - 1))
                def _():
                    for s, acc in zip(s_refs, part_refs):
                        s[...] = acc[...].sum(axis=0, keepdims=True)

        if nk == 1:
            finish(prods)
        else:
            @pl.when(k == 0)
            def _():
                for acc, p in zip(acc_refs, prods):
                    acc[...] = p

            @pl.when(k > 0)
            def _():
                for acc, p in zip(acc_refs, prods):
                    acc[...] += p

            @pl.when(k == nk - 1)
            def _():
                finish([acc[...] for acc in acc_refs])

    in_specs = [pl.BlockSpec(blk, im) for _, blk, im in list(a_ops) + list(b_ops) + list(extras)]
    in_specs += [pl.BlockSpec(p.shape, lambda i, j, q: (0, 0)) for p in params]
    args = [arr for arr, _, _ in list(a_ops) + list(b_ops) + list(extras)] + list(params)
    aliases = {}
    if carry is not None:
        in_specs.append(ANY)
        args.append(carry[0])
        aliases = {len(args) - 1: carry[1]}
    scratch = []
    if nk > 1:
        for _ in range(ng):
            blk = tuple(b for b in outs[0][2] if b is not None)
            scratch.append(pltpu.VMEM(blk, F32))
    scratch += [pltpu.VMEM((8, c), F32) for c in partials]
    res, jres = _call(
        body, name=name, grid=grid, in_specs=in_specs, args=args,
        out_specs=[pl.BlockSpec(blk, im) for _, _, blk, im in outs]
        + [pl.BlockSpec((1, c), lambda i, j, q: (0, 0)) for c in partials],
        out_shape=[jax.ShapeDtypeStruct(shp, dt) for shp, dt, _, _ in outs]
        + [jax.ShapeDtypeStruct((1, c), F32) for c in partials],
        scratch=scratch, sem=("arbitrary",) * 3 if partials else ("parallel", "parallel", "arbitrary"),
        aliases=aliases, job=job)
    return res if job is None else (res, jres)


def _first(accs, extras):
    return (accs[0],)


def _mm_rows(name, t, tm, a_ops, b_ops, groups, dims, fn, extras, params, outs, partials=(), nk=1, job=None):
    ne = len(extras)

    def epi(accs, rest):
        return fn(accs[0], *[r.astype(F32) for r in rest[:ne]], *rest[ne:])

    e_ops = [(arr, (tm, arr.shape[1]), lambda i, j, q: (i, 0)) for arr in extras]
    o_ops = [((t, c), dt, (tm, c), lambda i, j, q: (i, 0)) for c, dt in outs]
    return _mm(name, (t // tm, 1, nk), a_ops, b_ops, groups, dims, epi, o_ops, e_ops, job=job, params=params,
               partials=partials)


def _mm_simple(name, a, b, dims, out_dtype, tm, tn, b_row0=0, b_rows=None):
    m, kk = a.shape
    tm = min(tm, m)
    if dims is NT:
        n = b.shape[0] if b_rows is None else b_rows
        assert b.shape[1] == kk and b_row0 % tn == 0
        b_op = (b, (tn, kk), lambda i, j, q: (j + b_row0 // tn, 0))
    else:
        assert b.shape[0] == kk
        n = b.shape[1]
        b_op = (b, (kk, tn), lambda i, j, q: (0, j))
    a_op = (a, (tm, kk), lambda i, j, q: (i, 0))
    out = ((m, n), out_dtype, (tm, tn), lambda i, j, q: (i, j))
    return _mm(name, (m // tm, n // tn, 1), [a_op], [b_op], [[(0, 0)]], dims, _first, [out])[0]


def _rowwise(name, fn, tiled, params, outs, partials=(), tm=TM, job=None):
    t = tiled[0].shape[0]
    steps = t // tm
    nt, npar, no, npart = len(tiled), len(params), len(outs), len(partials)

    def body(*refs):
        t_refs = refs[:nt]
        p_refs = refs[nt:nt + npar]
        o_refs = refs[nt + npar:nt + npar + no]
        s_refs = refs[nt + npar + no:nt + npar + no + npart]
        acc_refs = refs[nt + npar + no + npart:]
        i = pl.program_id(0)
        res = fn(*[r[...].astype(F32) for r in t_refs], *[r[...] for r in p_refs])
        for o, r in zip(o_refs, res[:no]):
            o[...] = r.astype(o.dtype)

        @pl.when(i == 0)
        def _():
            for acc in acc_refs:
                acc[...] = jnp.zeros_like(acc)

        for acc, r in zip(acc_refs, res[no:]):
            acc[...] += r.reshape(tm // 8, 8, r.shape[-1]).sum(axis=0)

        @pl.when(i == steps - 1)
        def _():
            for s, acc in zip(s_refs, acc_refs):
                s[...] = acc[...].sum(axis=0, keepdims=True)

    in_specs = [pl.BlockSpec((tm, a.shape[1]), lambda i: (i, 0)) for a in tiled]
    in_specs += [pl.BlockSpec(p.shape, lambda i: (0, 0)) for p in params]
    out_specs = [pl.BlockSpec((tm, c), lambda i: (i, 0)) for c, _ in outs]
    out_specs += [pl.BlockSpec((1, c), lambda i: (0, 0)) for c in partials]
    out_shape = [jax.ShapeDtypeStruct((t, c), dt) for c, dt in outs]
    out_shape += [jax.ShapeDtypeStruct((1, c), F32) for c in partials]
    res, jres = _call(body, name=name, grid=(steps,), in_specs=in_specs, args=[*tiled, *params], out_specs=out_specs,
                      out_shape=out_shape, scratch=[pltpu.VMEM((8, c), F32) for c in partials], sem=("arbitrary",),
                      job=job)
    return res if job is None else (res, jres)


def _rms(x):
    r = lax.rsqrt(jnp.mean(x * x, axis=-1, keepdims=True) + EPS)
    return x * r, r


def _rms_bwd(xhat, r, g, dy):
    dxh = dy * g
    dx = r * (dxh - xhat * jnp.mean(dxh * xhat, axis=-1, keepdims=True))
    return dx, dy * xhat


def _ffn_up(name, n, wa, part, tm=512, tf=1408, job=None):
    t = n.shape[0]

    def epi(accs, extras):
        a, b = accs
        sg = _sigmoid(a)
        silu = a * sg
        return silu, b * (sg + silu * (1.0 - sg)), silu * b

    a_op = (n, (tm, D), lambda f, i, q: (i, 0))
    b_ops = [(wa, (None, tf, D), lambda f, i, q: (part, f, 0)),
             (wa, (None, tf, D), lambda f, i, q: (part + 1, f, 0))]
    outs = [((t, FF), BF16, (tm, tf), lambda f, i, q: (i, f))] * 3
    return _mm(name, (FF // tf, t // tm, 1), [a_op], b_ops, [[(0, 0)], [(0, 1)]], NT, epi, outs, job=job)


def _ffn_down(name, s, wa, part, tm=512):
    t = s.shape[0]
    a_op = (s, (tm, FF), lambda i, j, q: (i, 0))
    b_op = (wa, (None, FF, D), lambda i, j, q: (part, 0, 0))
    out = ((t, D), F32, (tm, D), lambda i, j, q: (i, 0))
    return _mm(name, (t // tm, 1, 1), [a_op], [b_op], [[(0, 0)]], NN, _first, [out])[0]


def _ffn_bwd_act(name, df, wa, part, a, b, tm=512, tf=1408, job=None):
    t = df.shape[0]

    def epi(accs, extras):
        ds = accs[0]
        return (jnp.stack([ds * extras[1].astype(F32), ds * extras[0].astype(F32)]),)

    a_op = (df, (tm, D), lambda f, i, q: (i, 0))
    b_op = (wa, (None, tf, D), lambda f, i, q: (part, f, 0))
    extras = [(a, (tm, tf), lambda f, i, q: (i, f)), (b, (tm, tf), lambda f, i, q: (i, f))]
    out = ((2, t, FF), BF16, (2, tm, tf), lambda f, i, q: (0, i, f))
    res = _mm(name, (FF // tf, t // tm, 1), [a_op], [b_op], [[(0, 0)]], NT, epi, [out], extras, job=job)
    return res[0] if job is None else (res[0][0], res[1])


def _ffn_bwd_in(name, dab, wa, part, tm=1024, job=None):
    t = dab.shape[1]
    tm = min(tm, t)
    a_op = (dab, (None, tm, FF), lambda i, j, q: (q, i, 0))
    b_op = (wa, (None, FF, D), lambda i, j, q: (part + q, 0, 0))
    out = ((t, D), F32, (tm, D), lambda i, j, q: (i, 0))
    res = _mm(name, (t // tm, 1, 2), [a_op], [b_op], [[(0, 0)]], NN, _first, [out], job=job)
    return res[0] if job is None else (res[0][0], res[1])


def _wgrad(name, dy_op, x, out_block, out_map, gi, carry_buf, out_shape, tk=512, job=None):
    t, c = x.shape
    b_op = (x, (tk, c), lambda i, j, q: (q, 0))
    out = (out_shape, BF16, out_block, out_map)
    carry = None if carry_buf is None else (carry_buf, 0)
    res = _mm(name, (gi, 1, t // tk), [dy_op], [b_op], [[(0, 0)]], TN, _first, [out], carry=carry, job=job)
    return res[0] if job is None else (res[0][0], res[1])


def _rel_onehot():
    j = lax.broadcasted_iota(jnp.int32, (384, RELW), 0)
    xx = lax.broadcasted_iota(jnp.int32, (384, RELW), 1)
    idx = jnp.clip(KPAD + QB - xx, -128, 128) + 128
    return (j == idx).astype(F32)


def _relbias_fwd(table):
    def body(t_ref, o_ref):
        rev = jnp.dot(t_ref[...], _rel_onehot(), precision=lax.Precision.HIGHEST, preferred_element_type=F32)
        for r in range(QB):
            row = pltpu.roll(rev, (RELW - (QB - r)) % RELW, 1)[:, :KW]
            rr = lax.broadcasted_iota(jnp.int32, (NH, KW), 1) >> 6
            ok = (rr >= (r // CHUNK)) & (rr <= (r // CHUNK) + LEFT)
            row = jnp.where(ok, row, -1e30)
            for h in range(NH):
                o_ref[h * QB + r:h * QB + r + 1, :] = row[h:h + 1, :]

    return pl.pallas_call(
        body,
        name="relbias_fwd",
        out_shape=jax.ShapeDtypeStruct((NH * QB, KW), F32),
        in_specs=[pl.BlockSpec(memory_space=pltpu.VMEM)],
        out_specs=pl.BlockSpec(memory_space=pltpu.VMEM),
        compiler_params=_params(),
    )(table)


def _relbias_bwd(dbias):
    def body(d_ref, o_ref, acc_ref):
        for h in range(NH):
            acc = jnp.zeros((1, RELW), F32)
            for r in range(QB):
                row = d_ref[h * QB + r:h * QB + r + 1, :]
                wide = jnp.concatenate([row, jnp.zeros((1, RELW - KW), F32)], axis=1)
                acc = acc + pltpu.roll(wide, QB - r, 1)
            acc_ref[h:h + 1, :] = acc
        o_ref[...] = lax.dot_general(acc_ref[...], _rel_onehot(), NT, precision=lax.Precision.HIGHEST,
                                     preferred_element_type=F32)

    return pl.pallas_call(
        body,
        name="relbias_bwd",
        out_shape=jax.ShapeDtypeStruct((NH, 384), F32),
        in_specs=[pl.BlockSpec(memory_space=pltpu.VMEM)],
        out_specs=pl.BlockSpec(memory_space=pltpu.VMEM),
        scratch_shapes=[pltpu.VMEM((NH, RELW), F32)],
        compiler_params=_params(),
    )(dbias)


def _stack_heads(x_pair):
    first = lax.broadcasted_iota(jnp.int32, (1, 128), 1) < 64
    zero = jnp.zeros_like(x_pair)
    return jnp.concatenate([jnp.where(first, x_pair, zero), jnp.where(first, zero, x_pair)], axis=0), first


def _pair_probs(qs, kw, bias, key_ok):
    s = lax.dot_general(qs, kw, NT, preferred_element_type=F32) + bias
    if key_ok is not None:
        s = jnp.where(key_ok, s, -1e30)
    e = jnp.exp(s - jnp.max(s, axis=-1, keepdims=True))
    return e * (1.0 / jnp.sum(e, axis=-1, keepdims=True))


def _attn_fwd(qkv, kvp, bias, job=None):
    t = qkv.shape[0]

    def body(q_ref, kv_ref, b_ref, o_ref):
        i = pl.program_id(0)

        def run(masked):
            start = pl.multiple_of(i * QB, QB)
            col = lax.broadcasted_iota(jnp.int32, (1, KW), 1)
            key_ok = (col >= KPAD - i * QB) if masked else None
            for pair in range(4):
                lo = pair * 128
                kw = kv_ref[pl.ds(start, KW), lo:lo + 128]
                vw = kv_ref[pl.ds(start, KW), DA + lo:DA + lo + 128]
                qs, first = _stack_heads(q_ref[:, lo:lo + 128])
                p = _pair_probs(qs * 0.125, kw, b_ref[2 * pair * QB:(2 * pair + 2) * QB, :], key_ok)
                o = jnp.dot(p.astype(BF16), vw, preferred_element_type=F32)
                o_ref[:, lo:lo + 128] = jnp.where(first, o[:QB], o[QB:]).astype(BF16)

        pl.when(i < KPAD // QB)(lambda: run(True))
        pl.when(i >= KPAD // QB)(lambda: run(False))

    res, jres = _call(
        body, name="attn_fwd", grid=(t // QB,),
        in_specs=[pl.BlockSpec((QB, DA), lambda i: (i, 0)),
                  pl.BlockSpec(memory_space=pltpu.VMEM),
                  pl.BlockSpec(memory_space=pltpu.VMEM)],
        args=[qkv, kvp, bias],
        out_specs=[pl.BlockSpec((QB, DA), lambda i: (i, 0))],
        out_shape=[jax.ShapeDtypeStruct((t, DA), BF16)],
        sem=("arbitrary",), job=job)
    return res[0], jres


def _attn_bwd(qkv, kvp, bias, datt, job=None):
    t = qkv.shape[0]
    nb = t // QB
    flush = (KW - QB) // QB

    def body(q_ref, kv_ref, b_ref, do_ref, dq_ref, dkv_ref, db_out, acc_ref, db_ref):
        i = pl.program_id(0)

        @pl.when(i == 0)
        def _():
            acc_ref[...] = jnp.zeros_like(acc_ref)
            db_ref[...] = jnp.zeros_like(db_ref)

        def run(masked):
            start = pl.multiple_of(i * QB, QB)
            col = lax.broadcasted_iota(jnp.int32, (1, KW), 1)
            key_ok = (col >= KPAD - i * QB) if masked else None
            for pair in range(4):
                lo = pair * 128
                kw = kv_ref[pl.ds(start, KW), lo:lo + 128]
                vw = kv_ref[pl.ds(start, KW), DA + lo:DA + lo + 128]
                qs, first = _stack_heads(q_ref[:, lo:lo + 128])
                qs = qs * 0.125
                dos, _ = _stack_heads(do_ref[:, lo:lo + 128])
                rows = slice(2 * pair * QB, (2 * pair + 2) * QB)
                p = _pair_probs(qs, kw, b_ref[rows, :], key_ok)
                dp = lax.dot_general(dos, vw, NT, preferred_element_type=F32)
                ds = p * (dp - jnp.sum(p * dp, axis=-1, keepdims=True))
                db_ref[rows, :] += ds
                dsb = ds.astype(BF16)
                dq = jnp.dot(dsb, kw, preferred_element_type=F32)
                dq_ref[:, lo:lo + 128] = (jnp.where(first, dq[:QB], dq[QB:]) * 0.125).astype(BF16)
                acc_ref[:, lo:lo + 128] += lax.dot_general(dsb, qs, TN, preferred_element_type=F32)
                acc_ref[:, DA + lo:DA + lo + 128] += lax.dot_general(p.astype(BF16), dos, TN,
                                                                     preferred_element_type=F32)

        pl.when(i < KPAD // QB)(lambda: run(True))
        pl.when(jnp.logical_and(i >= KPAD // QB, i < nb))(lambda: run(False))

        dkv_ref[...] = acc_ref[0:QB, :].astype(BF16)
        rest = acc_ref[QB:KW, :]
        acc_ref[0:KW - QB, :] = rest
        acc_ref[KW - QB:KW, :] = jnp.zeros((QB, 2 * DA), F32)

        @pl.when(i == nb + flush - 1)
        def _():
            pltpu.sync_copy(db_ref, db_out)

    last = nb - 1
    res, jres = _call(
        body, name="attn_bwd", grid=(nb + flush,),
        in_specs=[pl.BlockSpec((QB, DA), lambda i: (jnp.minimum(i, last), 0)),
                  pl.BlockSpec(memory_space=pltpu.VMEM),
                  pl.BlockSpec(memory_space=pltpu.VMEM),
                  pl.BlockSpec((QB, DA), lambda i: (jnp.minimum(i, last), 0))],
        args=[qkv, kvp, bias, datt],
        out_specs=[pl.BlockSpec((QB, DA), lambda i: (jnp.minimum(i, last), 0)),
                   pl.BlockSpec((QB, 2 * DA), lambda i: (i, 0)),
                   ANY],
        out_shape=[jax.ShapeDtypeStruct((t, DA), BF16),
                   jax.ShapeDtypeStruct((t + KPAD, 2 * DA), BF16),
                   jax.ShapeDtypeStruct((NH * QB, KW), F32)],
        scratch=[pltpu.VMEM((KW, 2 * DA), F32), pltpu.VMEM((NH * QB, KW), F32)], sem=("arbitrary",), job=job)
    return res, jres


def _glu(c, gb):
    cb = c + gb
    return cb[:, :DC] * _sigmoid(cb[:, DC:])


def _ln_swish(pre, g, b):
    mu = jnp.mean(pre, axis=-1, keepdims=True)
    xc = pre - mu
    r = lax.rsqrt(jnp.mean(xc * xc, axis=-1, keepdims=True) + EPS)
    xhat = xc * r
    y = xhat * g + b
    return xhat, r, y


RT = 32


def _shifted_copies(src_ref, sh_ref, rows):
    for b in range(1, 8):
        sh_ref[b - 1, :, :] = src_ref[pl.ds(b, rows), :]


def _tap(src_ref, sh_ref, off, r0, rows=RT):
    a, b = divmod(off, 8)
    ref = src_ref if b == 0 else sh_ref.at[b - 1]
    if isinstance(r0, int):
        return ref[r0 + 8 * a:r0 + 8 * a + rows, :]
    return ref[pl.ds(pl.multiple_of(r0 + 8 * a, 8), rows), :]


def _conv_fwd(cin, glu_b, dw_w, dw_b, ln_g, ln_b, tm=TM):
    t = cin.shape[0]
    hb = tm // HALO

    def body(c_ref, h_ref, gb_ref, w_ref, wb_ref, g_ref, b_ref, cs_ref, pre_ref, ext_ref, sh_ref):
        i = pl.program_id(0)
        halo = _glu(h_ref[...], gb_ref[...])
        ext_ref[0:HALO, :] = jnp.where(i > 0, halo, jnp.zeros_like(halo))
        ext_ref[HALO:HALO + tm, :] = _glu(c_ref[...], gb_ref[...])
        ext_ref[HALO + tm:HALO + tm + 8, :] = jnp.zeros((8, DC), F32)
        _shifted_copies(ext_ref, sh_ref, HALO + tm)

        def tile(rt, carry):
            r0 = pl.multiple_of(rt * RT, RT)
            acc = jnp.zeros((RT, DC), F32) + wb_ref[...]
            for j in range(CW):
                acc = acc + w_ref[j:j + 1, :] * _tap(ext_ref, sh_ref, HALO - (CW - 1) + j, r0)
            pre_ref[pl.ds(r0, RT), :] = acc
            return carry

        lax.fori_loop(0, tm // RT, tile, 0, unroll=2)
        _, _, y = _ln_swish(pre_ref[...], g_ref[...], b_ref[...])
        cs_ref[...] = (y * _sigmoid(y)).astype(BF16)

    vec = lambda n: pl.BlockSpec((1, n), lambda i: (0, 0))
    return pl.pallas_call(
        body,
        name="conv_fwd",
        grid=(t // tm,),
        in_specs=[pl.BlockSpec((tm, 2 * DC), lambda i: (i, 0)),
                  pl.BlockSpec((HALO, 2 * DC), lambda i: (jnp.maximum(i * hb - 1, 0), 0)),
                  vec(2 * DC), pl.BlockSpec((CW, DC), lambda i: (0, 0)), vec(DC), vec(DC), vec(DC)],
        out_specs=[pl.BlockSpec((tm, DC), lambda i: (i, 0)), pl.BlockSpec((tm, DC), lambda i: (i, 0))],
        out_shape=[jax.ShapeDtypeStruct((t, DC), BF16), jax.ShapeDtypeStruct((t, DC), F32)],
        scratch_shapes=[pltpu.VMEM((HALO + tm + 8, DC), F32), pltpu.VMEM((7, HALO + tm, DC), F32)],
        compiler_params=_params(("arbitrary",)),
    )(cin, cin, glu_b, dw_w, dw_b, ln_g, ln_b)


def _conv_bwd(dcs, pre, cin, glu_b, dw_w, ln_g, ln_b, tm=TM):
    t = cin.shape[0]
    hb = tm // HALO
    steps = t // tm
    nhb = t // HALO

    def dpre_of(dcs_v, pre_v, g, b):
        xhat, r, y = _ln_swish(pre_v, g, b)
        sg = _sigmoid(y)
        dy = dcs_v * (sg * (1.0 + y * (1.0 - sg)))
        dxh = dy * g
        dpre = r * (dxh - jnp.mean(dxh, axis=-1, keepdims=True)
                    - xhat * jnp.mean(dxh * xhat, axis=-1, keepdims=True))
        return dpre, dy * xhat, dy

    def body(dcs_ref, dcsn_ref, pre_ref, pren_ref, c_ref, ch_ref, gb_ref, w_ref, g_ref, b_ref,
             dc_ref, dgb_ref, dw_ref, dwb_ref, dg_ref, db_ref,
             gext_ref, dext_ref, shg_ref, shd_ref, a_gb, a_w, a_wb, a_g, a_b):
        i = pl.program_id(0)

        @pl.when(i == 0)
        def _():
            for a in (a_gb, a_w, a_wb, a_g, a_b):
                a[...] = jnp.zeros_like(a)

        fold = lambda v: v.reshape(v.shape[0] // 8, 8, v.shape[-1]).sum(axis=0)
        g, b = g_ref[...], b_ref[...]
        dpre, dg_t, db_t = dpre_of(dcs_ref[...], pre_ref[...], g, b)
        dpre_n, _, _ = dpre_of(dcsn_ref[...], pren_ref[...], g, b)
        dext_ref[0:tm, :] = dpre
        dext_ref[tm:tm + HALO, :] = jnp.where(i < steps - 1, dpre_n, jnp.zeros_like(dpre_n))
        dext_ref[tm + HALO:tm + HALO + 8, :] = jnp.zeros((8, DC), F32)
        a_wb[...] += fold(dpre)
        a_g[...] += fold(dg_t)
        a_b[...] += fold(db_t)
        halo = _glu(ch_ref[...], gb_ref[...])
        gext_ref[0:HALO, :] = jnp.where(i > 0, halo, jnp.zeros_like(halo))
        gext_ref[HALO:HALO + tm, :] = _glu(c_ref[...], gb_ref[...])
        gext_ref[HALO + tm:HALO + tm + 8, :] = jnp.zeros((8, DC), F32)
        _shifted_copies(gext_ref, shg_ref, HALO + tm)
        _shifted_copies(dext_ref, shd_ref, HALO + tm)

        for j in range(CW):
            a_w[8 * j:8 * j + 8, :] += fold(dext_ref[0:tm, :] * _tap(gext_ref, shg_ref, HALO - (CW - 1) + j, 0, tm))

        def tile(rt, carry):
            r0 = pl.multiple_of(rt * RT, RT)
            dglu = jnp.zeros((RT, DC), F32)
            for j in range(CW):
                dglu = dglu + w_ref[j:j + 1, :] * _tap(dext_ref, shd_ref, CW - 1 - j, r0)
            gext_ref[pl.ds(r0, RT), :] = dglu
            return carry

        lax.fori_loop(0, tm // RT, tile, 0, unroll=2)
        dglu = gext_ref[0:tm, :]
        cb = c_ref[...] + gb_ref[...]
        sg = _sigmoid(cb[:, DC:])
        dc = jnp.concatenate([dglu * sg, dglu * cb[:, :DC] * sg * (1.0 - sg)], axis=1)
        dc_ref[...] = dc.astype(BF16)
        a_gb[...] += fold(dc)

        @pl.when(i == steps - 1)
        def _():
            dgb_ref[...] = a_gb[...].sum(axis=0, keepdims=True)
            dw_ref[...] = a_w[...]
            dwb_ref[...] = a_wb[...].sum(axis=0, keepdims=True)
            dg_ref[...] = a_g[...].sum(axis=0, keepdims=True)
            db_ref[...] = a_b[...].sum(axis=0, keepdims=True)

    vec = lambda n: pl.BlockSpec((1, n), lambda i: (0, 0))
    nxt = lambda i: (jnp.minimum((i + 1) * hb, nhb - 1), 0)
    prv = lambda i: (jnp.maximum(i * hb - 1, 0), 0)
    return pl.pallas_call(
        body,
        name="conv_bwd",
        grid=(steps,),
        in_specs=[pl.BlockSpec((tm, DC), lambda i: (i, 0)), pl.BlockSpec((HALO, DC), nxt),
                  pl.BlockSpec((tm, DC), lambda i: (i, 0)), pl.BlockSpec((HALO, DC), nxt),
                  pl.BlockSpec((tm, 2 * DC), lambda i: (i, 0)), pl.BlockSpec((HALO, 2 * DC), prv),
                  vec(2 * DC), pl.BlockSpec((CW, DC), lambda i: (0, 0)), vec(DC), vec(DC)],
        out_specs=[pl.BlockSpec((tm, 2 * DC), lambda i: (i, 0)), vec(2 * DC),
                   pl.BlockSpec((CW * 8, DC), lambda i: (0, 0)), vec(DC), vec(DC), vec(DC)],
        out_shape=[jax.ShapeDtypeStruct((t, 2 * DC), BF16), jax.ShapeDtypeStruct((1, 2 * DC), F32),
                   jax.ShapeDtypeStruct((CW * 8, DC), F32), jax.ShapeDtypeStruct((1, DC), F32),
                   jax.ShapeDtypeStruct((1, DC), F32), jax.ShapeDtypeStruct((1, DC), F32)],
        scratch_shapes=[pltpu.VMEM((HALO + tm + 8, DC), F32), pltpu.VMEM((tm + HALO + 8, DC), F32),
                        pltpu.VMEM((7, HALO + tm, DC), F32), pltpu.VMEM((7, HALO + tm, DC), F32),
                        pltpu.VMEM((8, 2 * DC), F32), pltpu.VMEM((CW * 8, DC), F32),
                        pltpu.VMEM((8, DC), F32), pltpu.VMEM((8, DC), F32), pltpu.VMEM((8, DC), F32)],
        compiler_params=_params(("arbitrary",)),
    )(dcs, dcs, pre, pre, cin, cin, glu_b, dw_w, ln_g, ln_b)


def _place():
    x, y, c = lax.axis_index("x"), lax.axis_index("y"), lax.axis_index("c")
    return x, y, c


def _peers(x, y, c):
    out = []
    for k in range(1, NDEV):
        fx, fy, fc = (k >> 2) & 1, (k >> 1) & 1, k & 1
        px = 1 - x if fx else x
        py = 1 - y if fy else y
        pc = 1 - c if fc else c
        out.append((px, py, pc))
    return out


def _job_out_shapes(job):
    kind, arrays = job
    if kind == "gather":
        return [jax.ShapeDtypeStruct((a.shape[0], NDEV) + a.shape[1:], a.dtype) for a in arrays]
    return [jax.ShapeDtypeStruct((NDEV, a.shape[0]) + a.shape[2:], a.dtype) for a in arrays]


def _job_scratch(job):
    n = len(job[1])
    return [pltpu.SemaphoreType.DMA((n, NDEV - 1)), pltpu.SemaphoreType.DMA((n, NDEV - 1)),
            pltpu.SemaphoreType.DMA((n,))]


def _gather_parts(ins, outs, send_sems, recv_sems, local_sems):
    x, y, c = _place()
    me, sib = (x, y, c), (x, y, 1 - c)
    chips = [(1 - x, y), (x, 1 - y), (1 - x, 1 - y)]

    def copy(a, k, block, to, src=None):
        px, py, pc = block
        dst = outs[a].at[:, 4 * px + 2 * py + pc]
        return pltpu.make_async_remote_copy(
            src_ref=dst if src is None else src, dst_ref=dst,
            send_sem=send_sems.at[a, k], recv_sem=recv_sems.at[a, k], device_id=to, device_id_type=MESH)

    n = len(ins)
    local = [pltpu.make_async_copy(ins[a], outs[a].at[:, 4 * x + 2 * y + c], local_sems.at[a]) for a in range(n)]
    first = [[copy(a, 0, me, sib, src=ins[a])] + [copy(a, 1 + j, me, (*chip, c), src=ins[a])
                                                   for j, chip in enumerate(chips)] for a in range(n)]

    def start():
        for a in range(n):
            local[a].start()
            for cp in first[a]:
                cp.start()

    def finish():
        passed = []
        for j, chip in enumerate(chips):
            for a in range(n):
                copy(a, 1 + j, (*chip, c), me).wait_recv()
                cp = copy(a, 4 + j, (*chip, c), sib)
                cp.start()
                passed.append(cp)
        for a in range(n):
            copy(a, 0, sib, me).wait_recv()
            for j, chip in enumerate(chips):
                copy(a, 4 + j, (*chip, 1 - c), me).wait_recv()
        for a in range(n):
            for cp in first[a]:
                cp.wait_send()
            local[a].wait()
        for cp in passed:
            cp.wait_send()

    return start, finish


def _exchange_parts(ins, outs, send_sems, recv_sems, local_sems):
    x, y, c = _place()
    me = 4 * x + 2 * y + c
    n = len(ins)
    peers = _peers(x, y, c)
    local = [pltpu.make_async_copy(ins[a].at[:, me], outs[a].at[me], local_sems.at[a]) for a in range(n)]

    def copy(a, k):
        px, py, pc = peers[k]
        return pltpu.make_async_remote_copy(
            src_ref=ins[a].at[:, 4 * px + 2 * py + pc], dst_ref=outs[a].at[me],
            send_sem=send_sems.at[a, k], recv_sem=recv_sems.at[a, k], device_id=peers[k], device_id_type=MESH)

    def arrival(a, k):
        px, py, pc = peers[k]
        return pltpu.make_async_remote_copy(
            src_ref=ins[a].at[:, me], dst_ref=outs[a].at[4 * px + 2 * py + pc],
            send_sem=send_sems.at[a, k], recv_sem=recv_sems.at[a, k], device_id=peers[k], device_id_type=MESH)

    def start():
        for a in range(n):
            local[a].start()
            for k in range(NDEV - 1):
                copy(a, k).start()

    def finish():
        for a in range(n):
            for k in range(NDEV - 1):
                arrival(a, k).wait_recv()
        for a in range(n):
            for k in range(NDEV - 1):
                copy(a, k).wait_send()
            local[a].wait()

    return start, finish


def _call(body, *, name, grid, in_specs, args, out_specs, out_shape, scratch=(), sem=None, aliases=None, job=None):
    aliases = dict(aliases or {})
    if job is None:
        res = pl.pallas_call(
            body, name=name, grid=grid, in_specs=list(in_specs), out_specs=list(out_specs),
            out_shape=list(out_shape), scratch_shapes=list(scratch), input_output_aliases=aliases,
            compiler_params=_params(sem))(*args)
        return list(res), []
    kind, arrays = job
    n_in, n_out, n_scr, nj = len(args), len(out_shape), len(scratch), len(arrays)

    def wrapped(*refs):
        ins = refs[:n_in]
        jin = refs[n_in:n_in + nj]
        o0 = n_in + nj
        outs = refs[o0:o0 + n_out]
        jout = refs[o0 + n_out:o0 + n_out + nj]
        s0 = o0 + n_out + nj
        scr = refs[s0:s0 + n_scr]
        sems = refs[s0 + n_scr:]
        parts = _gather_parts if kind == "gather" else _exchange_parts
        start, finish = parts(jin, jout, *sems)
        if not grid:
            start()
            body(*ins, *outs, *scr)
            finish()
            return
        first = last = None
        for d, g in enumerate(grid):
            f, l = pl.program_id(d) == 0, pl.program_id(d) == g - 1
            first = f if first is None else jnp.logical_and(first, f)
            last = l if last is None else jnp.logical_and(last, l)
        pl.when(first)(start)
        body(*ins, *outs, *scr)
        pl.when(last)(finish)

    res = pl.pallas_call(
        wrapped, name=name, grid=grid, in_specs=list(in_specs) + [ANY] * nj,
        out_specs=list(out_specs) + [ANY] * nj, out_shape=list(out_shape) + _job_out_shapes(job),
        scratch_shapes=list(scratch) + _job_scratch(job), input_output_aliases=aliases,
        compiler_params=pltpu.CompilerParams(
            dimension_semantics=None if not grid else ("arbitrary",) * len(grid),
            vmem_limit_bytes=VMEM_LIMIT, has_side_effects=True))(*args, *arrays)
    return list(res[:n_out]), list(res[n_out:])


def _comm_only(name, job):
    return _call(lambda: None, name=name, grid=(), in_specs=[], args=[], out_specs=[], out_shape=[], job=job)[1]


def _sum_devices(name, parts, tr):
    _, r, c = parts.shape

    def body(p_ref, o_ref):
        acc = p_ref[0].astype(F32)
        for d in range(1, NDEV):
            acc = acc + p_ref[d].astype(F32)
        o_ref[...] = acc

    return pl.pallas_call(
        body,
        name=name,
        grid=(r // tr,),
        in_specs=[pl.BlockSpec((NDEV, tr, c), lambda i: (0, i, 0))],
        out_specs=pl.BlockSpec((tr, c), lambda i: (i, 0)),
        out_shape=jax.ShapeDtypeStruct((r, c), F32),
        compiler_params=_params(("parallel",)),
    )(parts)


def _adamw(name, w, g, m, v, tr=None):
    r, c = w.shape
    tr = r if tr is None else tr

    def body(w_ref, g_ref, m_ref, v_ref, d_ref, nm_ref, nv_ref):
        gv = g_ref[...]
        nm = ADAM_B1 * m_ref[...] + (1.0 - ADAM_B1) * gv
        nv = ADAM_B2 * v_ref[...] + (1.0 - ADAM_B2) * (gv * gv)
        m_hat = nm / (1.0 - ADAM_B1 ** ADAM_STEP)
        v_hat = nv / (1.0 - ADAM_B2 ** ADAM_STEP)
        d_ref[...] = -ADAM_LR * (m_hat / (jnp.sqrt(v_hat) + ADAM_EPS) + ADAM_WD * w_ref[...])
        nm_ref[...] = nm
        nv_ref[...] = nv

    spec = pl.BlockSpec((tr, c), lambda i: (i, 0))
    return pl.pallas_call(
        body,
        name=name,
        grid=(r // tr,),
        in_specs=[spec] * 4,
        out_specs=[spec] * 3,
        out_shape=[jax.ShapeDtypeStruct((r, c), F32)] * 3,
        compiler_params=_params(("parallel",)),
    )(w, g, m, v)


SMALL = ["ffn1_norm_pre", "ffn1_norm_post", "mix_norm_pre", "gate_bias", "rel_table", "conv_glu_bias",
         "conv_dw_b", "conv_ln_g", "conv_ln_b", "mix_norm_post", "ffn2_norm_pre", "ffn2_norm_post"]
SMALL_ROWS = 24
DW_ROWS = 32


def _pack_small(vals, extra=None):
    rows = []
    for name in SMALL:
        v = vals[name]
        if name == "rel_table":
            v = v.reshape(NH, -1)
            rows.append(jnp.pad(v, ((0, 0), (0, D - v.shape[1]))))
        else:
            v = v.reshape(-1)
            v = jnp.pad(v, (0, (-v.shape[0]) % D))
            rows.append(v.reshape(-1, D))
    if extra is not None:
        rows.append(extra)
    out = jnp.concatenate(rows, axis=0)
    return jnp.pad(out, ((0, SMALL_ROWS - out.shape[0]), (0, 0)))


LOSS_ROW = 20


def _small_update(parts, wp, mp, vp, shapes):
    layout = []
    r = 0
    for name in SMALL:
        n = 1
        for s in shapes[name]:
            n *= s
        if name == "rel_table":
            layout.append((name, r, NH, D))
            r += NH
        else:
            nr = -(-n // D)
            layout.append((name, r, nr, n // nr))
            r += nr

    def body(p_ref, w_ref, m_ref, v_ref, *o_refs):
        g = p_ref[0]
        for d in range(1, NDEV):
            g = g + p_ref[d]
        nm = ADAM_B1 * m_ref[...] + (1.0 - ADAM_B1) * g
        nv = ADAM_B2 * v_ref[...] + (1.0 - ADAM_B2) * (g * g)
        m_hat = nm / (1.0 - ADAM_B1 ** ADAM_STEP)
        v_hat = nv / (1.0 - ADAM_B2 ** ADAM_STEP)
        delta = -ADAM_LR * (m_hat / (jnp.sqrt(v_hat) + ADAM_EPS) + ADAM_WD * w_ref[...])
        k = 0
        for _, r0, nr, nc in layout:
            for val in (g, delta, nm, nv):
                o = o_refs[k]
                k += 1
                if nr == NH:
                    o[...] = val[r0:r0 + NH, :]
                else:
                    for q in range(nr):
                        o[:, q * nc:(q + 1) * nc] = val[r0 + q:r0 + q + 1, 0:nc]
        o_refs[k][...] = g[LOSS_ROW:LOSS_ROW + 1, :]

    out_shape = []
    for _, r0, nr, nc in layout:
        shp = (NH, D) if nr == NH else (1, nr * nc)
        out_shape += [jax.ShapeDtypeStruct(shp, F32)] * 4
    out_shape.append(jax.ShapeDtypeStruct((1, D), F32))
    vm = pl.BlockSpec(memory_space=pltpu.VMEM)
    res = pl.pallas_call(body, name="small_update", out_shape=out_shape, in_specs=[vm] * 4,
                         out_specs=[vm] * len(out_shape), compiler_params=_params())(parts, wp, mp, vp)
    out = {}
    for q, (name, r0, nr, nc) in enumerate(layout):
        vals = res[4 * q:4 * q + 4]
        if nr == NH:
            vals = [v[:, :NREL] for v in vals]
        out[name] = [v.reshape(shapes[name]) for v in vals]
    return out, res[-1]


def kernel(x, ffn1_norm_pre, ffn1_w_gate, ffn1_w_up, ffn1_w_down, ffn1_norm_post, mix_norm_pre, w_in, gate_bias, rel_table, w_attn_out, conv_glu_bias, conv_dw_w, conv_dw_b, conv_ln_g, conv_ln_b, conv_w_out, w_out, mix_norm_post, ffn2_norm_pre, ffn2_w_gate, ffn2_w_up, ffn2_w_down, ffn2_norm_post, loss_target, m_ffn1_norm_pre, m_ffn1_w_gate, m_ffn1_w_up, m_ffn1_w_down, m_ffn1_norm_post, m_mix_norm_pre, m_w_in, m_gate_bias, m_rel_table, m_w_attn_out, m_conv_glu_bias, m_conv_dw_w, m_conv_dw_b, m_conv_ln_g, m_conv_ln_b, m_conv_w_out, m_w_out, m_mix_norm_post, m_ffn2_norm_pre, m_ffn2_w_gate, m_ffn2_w_up, m_ffn2_w_down, m_ffn2_norm_post, v_ffn1_norm_pre, v_ffn1_w_gate, v_ffn1_w_up, v_ffn1_w_down, v_ffn1_norm_post, v_mix_norm_pre, v_w_in, v_gate_bias, v_rel_table, v_w_attn_out, v_conv_glu_bias, v_conv_dw_w, v_conv_dw_b, v_conv_ln_g, v_conv_ln_b, v_conv_w_out, v_w_out, v_mix_norm_post, v_ffn2_norm_pre, v_ffn2_w_gate, v_ffn2_w_up, v_ffn2_w_down, v_ffn2_norm_post):
    return _step(dict(locals()))


WEIGHTS = ["ffn1_norm_pre", "ffn1_w_gate", "ffn1_w_up", "ffn1_w_down", "ffn1_norm_post", "mix_norm_pre", "w_in",
           "gate_bias", "rel_table", "w_attn_out", "conv_glu_bias", "conv_dw_w", "conv_dw_b", "conv_ln_g",
           "conv_ln_b", "conv_w_out", "w_out", "mix_norm_post", "ffn2_norm_pre", "ffn2_w_gate", "ffn2_w_up",
           "ffn2_w_down", "ffn2_norm_post"]
FS = FF // NDEV
PS = (3 * DA + 2 * DC + 2 * D) // NDEV
OS = D // NDEV


def _local_step(xs, target, w, rel_table):
    t = xs.shape[0]
    vec = lambda n: w[n].reshape(1, -1)
    g_pre1, g_post1, g_mix, g_mixp = vec("ffn1_norm_pre"), vec("ffn1_norm_post"), vec("mix_norm_pre"), vec("mix_norm_post")
    g_pre2, g_post2 = vec("ffn2_norm_pre"), vec("ffn2_norm_post")
    gate_b, glu_b = vec("gate_bias"), vec("conv_glu_bias")
    dw_b, ln_g, ln_b = vec("conv_dw_b"), vec("conv_ln_g"), vec("conv_ln_b")

    tr = lambda a: jnp.transpose(a[0]).astype(BF16)
    sh_gu1 = jnp.stack([tr(w["ffn1_w_gate"]), tr(w["ffn1_w_up"])])
    sh_mid = [w["ffn1_w_down"].astype(BF16), tr(w["w_in"])[None], w["w_out"].astype(BF16),
              jnp.stack([tr(w["w_attn_out"]), tr(w["conv_w_out"])]),
              jnp.pad(w["conv_dw_w"][0, :, 0, :], ((0, DW_ROWS - CW), (0, 0)))[None]]
    sh_2 = jnp.stack([tr(w["ffn2_w_gate"]), tr(w["ffn2_w_up"]), w["ffn2_w_down"][0].astype(BF16)])

    (n1,), (w_gu1,) = _rowwise("pre1", lambda xv, g: ((_rms(xv)[0] * g),), [xs], [g_pre1], [(D, BF16)],
                               job=("gather", [sh_gu1]))
    w_gu1 = w_gu1.reshape(2, FF, D)
    (a1, b1, s1), (w_d1, wb, wc, wd, we) = _ffn_up("ffn1_up", n1, w_gu1, 0, job=("gather", sh_mid))
    w_d1, wb, wc, wd = w_d1.reshape(1, FF, D), wb.reshape(NDEV * PS, D), wc.reshape(D, D), wd.reshape(2, D, DA)
    dw_full = jnp.transpose(we[0], (1, 0, 2)).reshape(DW_ROWS, DC)[:CW]
    row = lambda i, j, q: (i, 0)
    top = lambda i, j, q: (0, 0)
    tmr = min(512, t)

    def post1(fv, xv, gp, gm):
        h = xv + 0.5 * (_rms(fv)[0] * gp)
        return fv, h, _rms(h)[0] * gm

    f1, h1, u = _mm_rows("ffn1_down", t, tmr, [(s1, (tmr, FF), row)], [(w_d1, (None, FF, D), lambda i, j, q: (0, 0, 0))],
                         [[(0, 0)]], NN, post1, [xs], [g_post1, g_mix], [(D, F32), (D, F32), (D, BF16)])

    qkv = _mm_simple("proj_qkv", u, wb, NT, BF16, 1024, 512, b_row0=0, b_rows=3 * DA)
    cin = _mm_simple("proj_conv", u, wb, NT, F32, 1024, 512, b_row0=3 * DA, b_rows=2 * DC)
    gg = _mm_simple("proj_gate", u, wb, NT, BF16, 1024, 512, b_row0=3 * DA + 2 * DC, b_rows=2 * D)

    bias = _relbias_fwd(jnp.pad(rel_table[0], ((0, 0), (0, 384 - NREL))))
    kvp = jnp.pad(qkv[:, DA:], ((KPAD, 0), (0, 0)))
    att, (w_2,) = _attn_fwd(qkv, kvp, bias, job=("gather", [sh_2]))
    w_2 = w_2.reshape(3, FF, D)
    cs, pre = _conv_fwd(cin, glu_b, dw_full, dw_b, ln_g, ln_b)
    ya = _mm_simple("attn_out", att, wd[0], NT, BF16, 1024, 1024)

    def merge(ybv, yav, gv, gb):
        gates = _sigmoid(gv + gb)
        return ybv, gates[:, :D] * yav + gates[:, D:] * ybv

    yb, merged = _mm_rows("conv_out", t, tmr, [(cs, (tmr, DC), row)], [(wd[1], (D, DC), top)], [[(0, 0)]], NT,
                          merge, [ya, gg], [gate_b], [(D, BF16), (D, BF16)])

    def postm(mv, hv, gp, g2):
        h = hv + _rms(mv)[0] * gp
        return mv, h, _rms(h)[0] * g2

    mm_, h2, n2 = _mm_rows("mix_out", t, tmr, [(merged, (tmr, D), row)], [(wc, (D, D), top)], [[(0, 0)]], NN,
                           postm, [h1], [g_mixp, g_pre2], [(D, F32), (D, F32), (D, BF16)])
    a2, b2, s2 = _ffn_up("ffn2_up", n2, w_2, 0)

    def post2(fv, hv, tv, gp):
        fh, r = _rms(fv)
        yv = hv + 0.5 * (fh * gp)
        err = yv - tv
        dy = err * (1.0 / D)
        df, dg = _rms_bwd(fh, r, gp, 0.5 * dy)
        return dy, df, (0.5 / D) * (err * err), dg

    dy, df2, loss_row, d_post2 = _mm_rows(
        "ffn2_down", t, tmr, [(s2, (tmr, FF), row)], [(w_2, (None, FF, D), lambda i, j, q: (2, 0, 0))], [[(0, 0)]],
        NN, post2, [h2, target], [g_post2], [(D, F32), (D, BF16)], [D, D])

    tmw = 1408
    nfi = FF // tmw
    tkf = min(2048, t)
    tkp = min(1024, t)

    def wgrad_down(name, s, df, shape, part, carry_buf, job=None):
        return _wgrad(name, (s, (tkf, tmw), lambda i, j, q: (q, i)), df,
                      (None, tmw, D), lambda i, j, q: (part, i, 0), nfi, carry_buf, shape, tk=tkf, job=job)

    def wgrad_gate_up(name, dab, nrm, shape, carry_buf, job=None):
        return _wgrad(name, (dab, (None, tkf, tmw), lambda i, j, q: (i // nfi, q, i % nfi)), nrm,
                      (None, tmw, D), lambda i, j, q: (i // nfi, i % nfi, 0), 2 * nfi, carry_buf, shape,
                      tk=tkf, job=job)

    g_2 = wgrad_down("ffn2_wgrad_d", s2, df2, (3, FF, D), 2, None)
    dab2 = _ffn_bwd_act("ffn2_bwd_act", df2, w_2, 2, a2, b2)
    g_2 = wgrad_gate_up("ffn2_wgrad_gu", dab2, n2, (3, FF, D), g_2)

    def bwd_pre2(dnv, hv, dyv, mv, g2, gp):
        hh, r = _rms(hv)
        dx, dg2 = _rms_bwd(hh, r, g2, dnv)
        dh = dyv + dx
        mh, rm = _rms(mv)
        dm, dgp = _rms_bwd(mh, rm, gp, dh)
        return dh, dm, dg2, dgp

    dh2, dm, d_pre2, d_mixp = _mm_rows(
        "ffn2_bwd_in", t, tmr, [(dab2, (None, tmr, FF), lambda i, j, q: (q, i, 0))],
        [(w_2, (None, FF, D), lambda i, j, q: (q, 0, 0))], [[(0, 0)]], NN, bwd_pre2, [h2, dy, mm_],
        [g_pre2, g_mixp], [(D, F32), (D, BF16)], [D, D], nk=2)

    def merge_bwd(dmv, yav, ybv, gv, gb):
        gates = _sigmoid(gv + gb)
        ga, gbb = gates[:, :D], gates[:, D:]
        dgg = jnp.concatenate([dmv * yav * ga * (1.0 - ga), dmv * ybv * gbb * (1.0 - gbb)], axis=1)
        return dmv * ga, dmv * gbb, dgg, dgg

    dya, dyb, dgg, d_gate_b = _mm_rows(
        "mix_out_bwd", t, tmr, [(dm, (tmr, D), row)], [(wc, (D, D), top)], [[(0, 0)]], NT, merge_bwd,
        [ya, yb, gg], [gate_b], [(D, BF16), (D, BF16), (2 * D, BF16)], [2 * D])
    datt = _mm_simple("attn_out_bwd", dya, wd[0], NN, BF16, 1024, 512)
    dcs = _mm_simple("conv_out_bwd", dyb, wd[1], NN, F32, 1024, 512)
    g_c = _wgrad("mix_out_wgrad", (merged, (tkp, D), lambda i, j, q: (q, 0)), dm,
                 (D, D), lambda i, j, q: (0, 0), 1, None, (D, D), tk=tkp)
    g_d = _wgrad("attn_out_wgrad", (dya, (tkp, D), lambda i, j, q: (q, 0)), att,
                 (None, D, DA), lambda i, j, q: (0, 0, 0), 1, None, (2, D, DA), tk=tkp)
    g_d = _wgrad("conv_out_wgrad", (dyb, (tkp, D), lambda i, j, q: (q, 0)), cs,
                 (None, D, DA), lambda i, j, q: (1, 0, 0), 1, g_d, (2, D, DA), tk=tkp)
    (dq, dkvp, dbias), (x_2, x_c, x_d) = _attn_bwd(
        qkv, kvp, bias, datt,
        job=("exchange", [g_2.reshape(3, NDEV, FS, D), g_c.reshape(1, NDEV, OS, D), g_d.reshape(2, NDEV, OS, DA)]))
    d_rel = _relbias_bwd(dbias)
    dcin, d_glu_b, d_dw8, d_dw_b, d_ln_g, d_ln_b = _conv_bwd(dcs, pre, cin, glu_b, dw_full, ln_g, ln_b)
    g_dw = jnp.pad(d_dw8, ((0, 8 * (DW_ROWS - CW)), (0, 0)))
    g_dw = g_dw.reshape(DW_ROWS * 8, NDEV, DC // NDEV).transpose(1, 0, 2)

    top = lambda i, j, q: (0, 0)
    g_b = jnp.concatenate([
        _wgrad("proj_wgrad_q", (dq, (tkp, DA), lambda i, j, q: (q, 0)), u, (DA, D), top, 1, None, (DA, D), tk=tkp),
        _wgrad("proj_wgrad_kv", (dkvp, (KPAD, 2 * DA), lambda i, j, q: (q + 1, 0)), u, (2 * DA, D), top, 1, None,
               (2 * DA, D), tk=KPAD),
        _wgrad("proj_wgrad_c", (dcin, (tkp, 2 * DC), lambda i, j, q: (q, 0)), u, (2 * DC, D), top, 1, None,
               (2 * DC, D), tk=tkp),
        _wgrad("proj_wgrad_g", (dgg, (tkp, 2 * D), lambda i, j, q: (q, 0)), u, (2 * D, D), top, 1, None,
               (2 * D, D), tk=tkp)], axis=0)

    tmu = 256
    a_ops = [(dq, (tmu, DA), row),
             (dkvp, (tmu, 2 * DA), lambda i, j, q: (i + KPAD // tmu, 0)),
             (dcin, (tmu, 2 * DC), row),
             (dgg, (tmu, 2 * D), row)]
    b_ops = [(wb[:DA], (DA, D), top), (wb[DA:3 * DA], (2 * DA, D), top),
             (wb[3 * DA:3 * DA + 2 * DC], (2 * DC, D), top), (wb[3 * DA + 2 * DC:], (2 * D, D), top)]

    def bwd_mix(duv, hv, dhv, fv, gm, gp):
        hh, r = _rms(hv)
        dx, dgm = _rms_bwd(hh, r, gm, duv)
        dh = dhv + dx
        fh, rf = _rms(fv)
        df, dgp = _rms_bwd(fh, rf, gp, 0.5 * dh)
        return dh, df, dgm, dgp

    dh1, df1, d_mix, d_post1 = _mm_rows(
        "proj_bwd", t, tmu, a_ops, b_ops, [[(0, 0), (1, 1), (2, 2), (3, 3)]], NN, bwd_mix, [h1, dh2, f1],
        [g_mix, g_post1], [(D, F32), (D, BF16)], [D, D])
    g_d1 = wgrad_down("ffn1_wgrad_d", s1, df1, (1, FF, D), 0, None)
    dab1, (x_b, x_dw) = _ffn_bwd_act("ffn1_bwd_act", df1, w_d1, 0, a1, b1,
                                     job=("exchange", [g_b.reshape(1, NDEV, PS, D), g_dw[None]]))
    g_gu1, (x_d1,) = wgrad_gate_up("ffn1_wgrad_gu", dab1, n1, (2, FF, D), None,
                                   job=("exchange", [g_d1.reshape(1, NDEV, FS, D)]))
    dn1, (x_gu1,) = _ffn_bwd_in("ffn1_bwd_in", dab1, w_gu1, 0, job=("exchange", [g_gu1.reshape(2, NDEV, FS, D)]))

    def bwd_pre1(xv, dnv, dhv, g1):
        xh, r = _rms(xv)
        dx, dg1 = _rms_bwd(xh, r, g1, dnv)
        return dhv + dx, dg1

    dx, d_pre1 = _rowwise("bwd_pre1", bwd_pre1, [xs, dn1, dh1], [g_pre1], [(D, F32)], [D])

    small_g = {"ffn1_norm_pre": d_pre1, "ffn1_norm_post": d_post1, "mix_norm_pre": d_mix, "gate_bias": d_gate_b,
               "rel_table": d_rel[:, :NREL], "conv_glu_bias": d_glu_b, "conv_dw_b": d_dw_b, "conv_ln_g": d_ln_g,
               "conv_ln_b": d_ln_b, "mix_norm_post": d_mixp, "ffn2_norm_pre": d_pre2, "ffn2_norm_post": d_post2}
    return loss_row, dx, (x_gu1, x_d1, x_2, x_b, x_c, x_d, x_dw), small_g


def _step(args):
    names = WEIGHTS
    w = {n: args[n] for n in names}
    fs, ps, os_ = FS, PS, OS
    conv_dw_w = args["conv_dw_w"]
    loss_row, dx, (x_gu1, x_d1, x_2, x_b, x_c, x_d, x_dw), small_g = _local_step(
        args["x"][0], args["loss_target"][0], w, args["rel_table"])

    g_small = _pack_small(small_g, extra=loss_row)
    (x_s,) = _comm_only("gather_small_grads", ("gather", [g_small[None]]))

    s_gu1 = _sum_devices("sum_ffn1_gu", x_gu1.reshape(NDEV, 2 * fs, D), fs).reshape(2, fs, D)
    s_d1 = _sum_devices("sum_ffn1_d", x_d1.reshape(NDEV, fs, D), fs)
    s_2 = _sum_devices("sum_ffn2", x_2.reshape(NDEV, 3 * fs, D), fs).reshape(3, fs, D)
    s_b = _sum_devices("sum_proj", x_b.reshape(NDEV, ps, D), ps)
    s_c = _sum_devices("sum_mix", x_c.reshape(NDEV, os_, D), os_)
    s_d = _sum_devices("sum_out", x_d.reshape(NDEV, 2 * os_, DA), 2 * os_).reshape(2, os_, DA)
    s_dw = _sum_devices("sum_dw", x_dw.reshape(NDEV, DW_ROWS * 8, DC // NDEV), DW_ROWS * 8)

    grads = {
        "ffn1_w_gate": jnp.transpose(s_gu1[0])[None], "ffn1_w_up": jnp.transpose(s_gu1[1])[None], "ffn1_w_down": s_d1[None],
        "ffn2_w_gate": jnp.transpose(s_2[0])[None], "ffn2_w_up": jnp.transpose(s_2[1])[None], "ffn2_w_down": s_2[2][None],
        "w_in": jnp.transpose(s_b)[None], "w_out": s_c[None],
        "w_attn_out": jnp.transpose(s_d[0])[None], "conv_w_out": jnp.transpose(s_d[1])[None],
    }
    shapes = {n: w[n].shape for n in SMALL}
    deltas, new_m, new_v = {}, {}, {}
    small, loss_terms = _small_update(
        x_s.reshape(NDEV, SMALL_ROWS, D), _pack_small({n: w[n] for n in SMALL}),
        _pack_small({n: args["m_" + n] for n in SMALL}), _pack_small({n: args["v_" + n] for n in SMALL}), shapes)
    for n in SMALL:
        grads[n], deltas[n], new_m[n], new_v[n] = small[n]

    big = ["ffn1_w_gate", "ffn1_w_up", "ffn1_w_down", "w_in", "w_attn_out", "conv_w_out", "w_out",
           "ffn2_w_gate", "ffn2_w_up", "ffn2_w_down"]
    for n in big:
        shp = w[n].shape
        two = lambda a: a.reshape(shp[1], shp[2])
        rows = shp[1]
        tr_ = rows // 2 if rows % 16 == 0 else rows
        d_, m_, v_ = _adamw("adamw_" + n, two(w[n]), two(grads[n]), two(args["m_" + n]), two(args["v_" + n]), tr_)
        deltas[n], new_m[n], new_v[n] = d_.reshape(shp), m_.reshape(shp), v_.reshape(shp)

    g_dw_own = _fold8("fold_dw", s_dw)[:CW]
    grads["conv_dw_w"] = g_dw_own.reshape(1, CW, 1, DC // NDEV)
    flat = lambda a: a.reshape(CW, DC // NDEV)
    d_, m_, v_ = _adamw("adamw_dw", flat(conv_dw_w), g_dw_own, flat(args["m_conv_dw_w"]), flat(args["v_conv_dw_w"]))
    shp = conv_dw_w.shape
    deltas["conv_dw_w"], new_m["conv_dw_w"], new_v["conv_dw_w"] = d_.reshape(shp), m_.reshape(shp), v_.reshape(shp)

    loss = jnp.sum(loss_terms)
    return (loss, dx[None], *[grads[n] for n in names], *[deltas[n] for n in names],
            *[new_m[n] for n in names], *[new_v[n] for n in names])


def _fold8(name, a):
    r8, c = a.shape

    def body(a_ref, o_ref):
        o_ref[...] = a_ref[...].reshape(r8 // 8, 8, c).sum(axis=1)

    return pl.pallas_call(
        body,
        name=name,
        out_shape=jax.ShapeDtypeStruct((r8 // 8, c), F32),
        in_specs=[pl.BlockSpec(memory_space=pltpu.VMEM)],
        out_specs=pl.BlockSpec(memory_space=pltpu.VMEM),
        compiler_params=_params(),
    )(a)
```

```python
import functools

import jax
import jax.numpy as jnp
from jax import lax
from jax.experimental import pallas as pl
from jax.experimental.pallas import tpu as pltpu

F32 = jnp.float32
BF16 = jnp.bfloat16

D = 1024
FF = 2816
DA = 512
DC = 512
NH = 8
CHUNK = 64
LEFT = 8
CW = 31
NREL = 257
EPS = 1e-6
NDEV = 8

QB = 4 * CHUNK
KW = LEFT * CHUNK + QB
KPAD = LEFT * CHUNK
RELW = KW + QB
HALO = 32

TM = 512
VMEM_LIMIT = 56 * 1024 * 1024

ADAM_LR, ADAM_B1, ADAM_B2, ADAM_EPS, ADAM_WD, ADAM_STEP = 0.001, 0.9, 0.999, 1e-08, 0.01, 10

NT = (((1,), (1,)), ((), ()))
NN = (((1,), (0,)), ((), ()))
TN = (((0,), (0,)), ((), ()))

MESH = pl.DeviceIdType.MESH
ANY = pl.BlockSpec(memory_space=pl.ANY)


def _params(sem=None, vmem=VMEM_LIMIT):
    return pltpu.CompilerParams(dimension_semantics=sem, vmem_limit_bytes=vmem)


def _sigmoid(x):
    return 0.5 * jnp.tanh(0.5 * x) + 0.5


def _mm(name, grid, a_ops, b_ops, groups, dims, epi, outs, extras=(), carry=None, job=None, params=(),
        partials=()):
    nk = grid[2]
    na, nb, ne, no, ng = len(a_ops), len(b_ops), len(extras), len(outs), len(groups)
    npar, npart = len(params), len(partials)
    nc = 0 if carry is None else 1

    def body(*refs):
        a_refs = refs[:na]
        b_refs = refs[na:na + nb]
        e_refs = refs[na + nb:na + nb + ne]
        p_refs = refs[na + nb + ne:na + nb + ne + npar]
        o0 = na + nb + ne + npar + nc
        o_refs = refs[o0:o0 + no]
        s_refs = refs[o0 + no:o0 + no + npart]
        acc_refs = refs[o0 + no + npart:o0 + no + npart + (ng if nk > 1 else 0)]
        part_refs = refs[len(refs) - npart:] if npart else ()
        k = pl.program_id(2)
        prods = []
        for grp in groups:
            p = None
            for ai, bi in grp:
                t = lax.dot_general(a_refs[ai][...], b_refs[bi][...], dims, preferred_element_type=F32)
                p = t if p is None else p + t
            prods.append(p)

        def finish(vals):
            res = epi(vals, [e[...] for e in e_refs] + [p[...] for p in p_refs])
            for o, r in zip(o_refs, res[:no]):
                o[...] = r.astype(o.dtype)
            if npart:
                i, j = pl.program_id(0), pl.program_id(1)

                @pl.when(jnp.logical_and(i == 0, j == 0))
                def _():
                    for acc in part_refs:
                        acc[...] = jnp.zeros_like(acc)

                for acc, r in zip(part_refs, res[no:]):
                    acc[...] += r.reshape(r.shape[0] // 8, 8, r.shape[-1]).sum(axis=0)

                @pl.when(jnp.logical_and(i == grid[0] - 1, j == grid[1] - 1))
                def _():
                    for s, acc in zip(s_refs, part_refs):
                        s[...] = acc[...].sum(axis=0, keepdims=True)

        if nk == 1:
            finish(prods)
        else:
            @pl.when(k == 0)
            def _():
                for acc, p in zip(acc_refs, prods):
                    acc[...] = p

            @pl.when(k > 0)
            def _():
                for acc, p in zip(acc_refs, prods):
                    acc[...] += p

            @pl.when(k == nk - 1)
            def _():
                finish([acc[...] for acc in acc_refs])

    in_specs = [pl.BlockSpec(blk, im) for _, blk, im in list(a_ops) + list(b_ops) + list(extras)]
    in_specs += [pl.BlockSpec(p.shape, lambda i, j, q: (0, 0)) for p in params]
    args = [arr for arr, _, _ in list(a_ops) + list(b_ops) + list(extras)] + list(params)
    aliases = {}
    if carry is not None:
        in_specs.append(ANY)
        args.append(carry[0])
        aliases = {len(args) - 1: carry[1]}
    scratch = []
    if nk > 1:
        for _ in range(ng):
            blk = tuple(b for b in outs[0][2] if b is not None)
            scratch.append(pltpu.VMEM(blk, F32))
    scratch += [pltpu.VMEM((8, c), F32) for c in partials]
    res, jres = _call(
        body, name=name, grid=grid, in_specs=in_specs, args=args,
        out_specs=[pl.BlockSpec(blk, im) for _, _, blk, im in outs]
        + [pl.BlockSpec((1, c), lambda i, j, q: (0, 0)) for c in partials],
        out_shape=[jax.ShapeDtypeStruct(shp, dt) for shp, dt, _, _ in outs]
        + [jax.ShapeDtypeStruct((1, c), F32) for c in partials],
        scratch=scratch, sem=("arbitrary",) * 3 if partials else ("parallel", "parallel", "arbitrary"),
        aliases=aliases, job=job)
    return res if job is None else (res, jres)


def _first(accs, extras):
    return (accs[0],)


def _mm_rows(name, t, tm, a_ops, b_ops, groups, dims, fn, extras, params, outs, partials=(), nk=1, job=None):
    ne = len(extras)

    def epi(accs, rest):
        return fn(accs[0], *[r.astype(F32) for r in rest[:ne]], *rest[ne:])

    e_ops = [(arr, (tm, arr.shape[1]), lambda i, j, q: (i, 0)) for arr in extras]
    o_ops = [((t, c), dt, (tm, c), lambda i, j, q: (i, 0)) for c, dt in outs]
    return _mm(name, (t // tm, 1, nk), a_ops, b_ops, groups, dims, epi, o_ops, e_ops, job=job, params=params,
               partials=partials)


def _mm_simple(name, a, b, dims, out_dtype, tm, tn, b_row0=0, b_rows=None):
    m, kk = a.shape
    tm = min(tm, m)
    if dims is NT:
        n = b.shape[0] if b_rows is None else b_rows
        assert b.shape[1] == kk and b_row0 % tn == 0
        b_op = (b, (tn, kk), lambda i, j, q: (j + b_row0 // tn, 0))
    else:
        assert b.shape[0] == kk
        n = b.shape[1]
        b_op = (b, (kk, tn), lambda i, j, q: (0, j))
    a_op = (a, (tm, kk), lambda i, j, q: (i, 0))
    out = ((m, n), out_dtype, (tm, tn), lambda i, j, q: (i, j))
    return _mm(name, (m // tm, n // tn, 1), [a_op], [b_op], [[(0, 0)]], dims, _first, [out])[0]


def _rowwise(name, fn, tiled, params, outs, partials=(), tm=TM, job=None):
    t = tiled[0].shape[0]
    steps = t // tm
    nt, npar, no, npart = len(tiled), len(params), len(outs), len(partials)

    def body(*refs):
        t_refs = refs[:nt]
        p_refs = refs[nt:nt + npar]
        o_refs = refs[nt + npar:nt + npar + no]
        s_refs = refs[nt + npar + no:nt + npar + no + npart]
        acc_refs = refs[nt + npar + no + npart:]
        i = pl.program_id(0)
        res = fn(*[r[...].astype(F32) for r in t_refs], *[r[...] for r in p_refs])
        for o, r in zip(o_refs, res[:no]):
            o[...] = r.astype(o.dtype)

        @pl.when(i == 0)
        def _():
            for acc in acc_refs:
                acc[...] = jnp.zeros_like(acc)

        for acc, r in zip(acc_refs, res[no:]):
            acc[...] += r.reshape(tm // 8, 8, r.shape[-1]).sum(axis=0)

        @pl.when(i == steps - 1)
        def _():
            for s, acc in zip(s_refs, acc_refs):
                s[...] = acc[...].sum(axis=0, keepdims=True)

    in_specs = [pl.BlockSpec((tm, a.shape[1]), lambda i: (i, 0)) for a in tiled]
    in_specs += [pl.BlockSpec(p.shape, lambda i: (0, 0)) for p in params]
    out_specs = [pl.BlockSpec((tm, c), lambda i: (i, 0)) for c, _ in outs]
    out_specs += [pl.BlockSpec((1, c), lambda i: (0, 0)) for c in partials]
    out_shape = [jax.ShapeDtypeStruct((t, c), dt) for c, dt in outs]
    out_shape += [jax.ShapeDtypeStruct((1, c), F32) for c in partials]
    res, jres = _call(body, name=name, grid=(steps,), in_specs=in_specs, args=[*tiled, *params], out_specs=out_specs,
                      out_shape=out_shape, scratch=[pltpu.VMEM((8, c), F32) for c in partials], sem=("arbitrary",),
                      job=job)
    return res if job is None else (res, jres)


def _rms(x):
    r = lax.rsqrt(jnp.mean(x * x, axis=-1, keepdims=True) + EPS)
    return x * r, r


def _rms_bwd(xhat, r, g, dy):
    dxh = dy * g
    dx = r * (dxh - xhat * jnp.mean(dxh * xhat, axis=-1, keepdims=True))
    return dx, dy * xhat


def _ffn_up(name, n, wa, part, tm=512, tf=1408, job=None):
    t = n.shape[0]

    def epi(accs, extras):
        a, b = accs
        sg = _sigmoid(a)
        silu = a * sg
        return silu, b * (sg + silu * (1.0 - sg)), silu * b

    a_op = (n, (tm, D), lambda f, i, q: (i, 0))
    b_ops = [(wa, (None, tf, D), lambda f, i, q: (part, f, 0)),
             (wa, (None, tf, D), lambda f, i, q: (part + 1, f, 0))]
    outs = [((t, FF), BF16, (tm, tf), lambda f, i, q: (i, f))] * 3
    return _mm(name, (FF // tf, t // tm, 1), [a_op], b_ops, [[(0, 0)], [(0, 1)]], NT, epi, outs, job=job)


def _ffn_down(name, s, wa, part, tm=512):
    t = s.shape[0]
    a_op = (s, (tm, FF), lambda i, j, q: (i, 0))
    b_op = (wa, (None, FF, D), lambda i, j, q: (part, 0, 0))
    out = ((t, D), F32, (tm, D), lambda i, j, q: (i, 0))
    return _mm(name, (t // tm, 1, 1), [a_op], [b_op], [[(0, 0)]], NN, _first, [out])[0]


def _ffn_bwd_act(name, df, wa, part, a, b, tm=512, tf=1408, job=None):
    t = df.shape[0]

    def epi(accs, extras):
        ds = accs[0]
        return (jnp.stack([ds * extras[1].astype(F32), ds * extras[0].astype(F32)]),)

    a_op = (df, (tm, D), lambda f, i, q: (i, 0))
    b_op = (wa, (None, tf, D), lambda f, i, q: (part, f, 0))
    extras = [(a, (tm, tf), lambda f, i, q: (i, f)), (b, (tm, tf), lambda f, i, q: (i, f))]
    out = ((2, t, FF), BF16, (2, tm, tf), lambda f, i, q: (0, i, f))
    res = _mm(name, (FF // tf, t // tm, 1), [a_op], [b_op], [[(0, 0)]], NT, epi, [out], extras, job=job)
    return res[0] if job is None else (res[0][0], res[1])


def _ffn_bwd_in(name, dab, wa, part, tm=1024, job=None):
    t = dab.shape[1]
    tm = min(tm, t)
    a_op = (dab, (None, tm, FF), lambda i, j, q: (q, i, 0))
    b_op = (wa, (None, FF, D), lambda i, j, q: (part + q, 0, 0))
    out = ((t, D), F32, (tm, D), lambda i, j, q: (i, 0))
    res = _mm(name, (t // tm, 1, 2), [a_op], [b_op], [[(0, 0)]], NN, _first, [out], job=job)
    return res[0] if job is None else (res[0][0], res[1])


def _wgrad(name, dy_op, x, out_block, out_map, gi, carry_buf, out_shape, tk=512, job=None):
    t, c = x.shape
    b_op = (x, (tk, c), lambda i, j, q: (q, 0))
    out = (out_shape, BF16, out_block, out_map)
    carry = None if carry_buf is None else (carry_buf, 0)
    res = _mm(name, (gi, 1, t // tk), [dy_op], [b_op], [[(0, 0)]], TN, _first, [out], carry=carry, job=job)
    return res[0] if job is None else (res[0][0], res[1])


def _rel_onehot():
    j = lax.broadcasted_iota(jnp.int32, (384, RELW), 0)
    xx = lax.broadcasted_iota(jnp.int32, (384, RELW), 1)
    idx = jnp.clip(KPAD + QB - xx, -128, 128) + 128
    return (j == idx).astype(F32)


def _relbias_fwd(table):
    def body(t_ref, o_ref):
        rev = jnp.dot(t_ref[...], _rel_onehot(), precision=lax.Precision.HIGHEST, preferred_element_type=F32)
        for r in range(QB):
            row = pltpu.roll(rev, (RELW - (QB - r)) % RELW, 1)[:, :KW]
            rr = lax.broadcasted_iota(jnp.int32, (NH, KW), 1) >> 6
            ok = (rr >= (r // CHUNK)) & (rr <= (r // CHUNK) + LEFT)
            row = jnp.where(ok, row, -1e30)
            for h in range(NH):
                o_ref[h * QB + r:h * QB + r + 1, :] = row[h:h + 1, :]

    return pl.pallas_call(
        body,
        name="relbias_fwd",
        out_shape=jax.ShapeDtypeStruct((NH * QB, KW), F32),
        in_specs=[pl.BlockSpec(memory_space=pltpu.VMEM)],
        out_specs=pl.BlockSpec(memory_space=pltpu.VMEM),
        compiler_params=_params(),
    )(table)


def _relbias_bwd(dbias):
    def body(d_ref, o_ref, acc_ref):
        for h in range(NH):
            acc = jnp.zeros((1, RELW), F32)
            for r in range(QB):
                row = d_ref[h * QB + r:h * QB + r + 1, :]
                wide = jnp.concatenate([row, jnp.zeros((1, RELW - KW), F32)], axis=1)
                acc = acc + pltpu.roll(wide, QB - r, 1)
            acc_ref[h:h + 1, :] = acc
        o_ref[...] = lax.dot_general(acc_ref[...], _rel_onehot(), NT, precision=lax.Precision.HIGHEST,
                                     preferred_element_type=F32)

    return pl.pallas_call(
        body,
        name="relbias_bwd",
        out_shape=jax.ShapeDtypeStruct((NH, 384), F32),
        in_specs=[pl.BlockSpec(memory_space=pltpu.VMEM)],
        out_specs=pl.BlockSpec(memory_space=pltpu.VMEM),
        scratch_shapes=[pltpu.VMEM((NH, RELW), F32)],
        compiler_params=_params(),
    )(dbias)


def _stack_heads(x_pair):
    first = lax.broadcasted_iota(jnp.int32, (1, 128), 1) < 64
    zero = jnp.zeros_like(x_pair)
    return jnp.concatenate([jnp.where(first, x_pair, zero), jnp.where(first, zero, x_pair)], axis=0), first


def _pair_probs(qs, kw, bias, key_ok):
    s = lax.dot_general(qs, kw, NT, preferred_element_type=F32) + bias
    if key_ok is not None:
        s = jnp.where(key_ok, s, -1e30)
    e = jnp.exp(s - jnp.max(s, axis=-1, keepdims=True))
    return e * (1.0 / jnp.sum(e, axis=-1, keepdims=True))


def _attn_fwd(qkv, kvp, bias, job=None):
    t = qkv.shape[0]

    def body(q_ref, kv_ref, b_ref, o_ref):
        i = pl.program_id(0)

        def run(masked):
            start = pl.multiple_of(i * QB, QB)
            col = lax.broadcasted_iota(jnp.int32, (1, KW), 1)
            key_ok = (col >= KPAD - i * QB) if masked else None
            for pair in range(4):
                lo = pair * 128
                kw = kv_ref[pl.ds(start, KW), lo:lo + 128]
                vw = kv_ref[pl.ds(start, KW), DA + lo:DA + lo + 128]
                qs, first = _stack_heads(q_ref[:, lo:lo + 128])
                p = _pair_probs(qs * 0.125, kw, b_ref[2 * pair * QB:(2 * pair + 2) * QB, :], key_ok)
                o = jnp.dot(p.astype(BF16), vw, preferred_element_type=F32)
                o_ref[:, lo:lo + 128] = jnp.where(first, o[:QB], o[QB:]).astype(BF16)

        pl.when(i < KPAD // QB)(lambda: run(True))
        pl.when(i >= KPAD // QB)(lambda: run(False))

    res, jres = _call(
        body, name="attn_fwd", grid=(t // QB,),
        in_specs=[pl.BlockSpec((QB, DA), lambda i: (i, 0)),
                  pl.BlockSpec(memory_space=pltpu.VMEM),
                  pl.BlockSpec(memory_space=pltpu.VMEM)],
        args=[qkv, kvp, bias],
        out_specs=[pl.BlockSpec((QB, DA), lambda i: (i, 0))],
        out_shape=[jax.ShapeDtypeStruct((t, DA), BF16)],
        sem=("arbitrary",), job=job)
    return res[0], jres


def _attn_bwd(qkv, kvp, bias, datt, job=None):
    t = qkv.shape[0]
    nb = t // QB
    flush = (KW - QB) // QB

    def body(q_ref, kv_ref, b_ref, do_ref, dq_ref, dkv_ref, db_out, acc_ref, db_ref):
        i = pl.program_id(0)

        @pl.when(i == 0)
        def _():
            acc_ref[...] = jnp.zeros_like(acc_ref)
            db_ref[...] = jnp.zeros_like(db_ref)

        def run(masked):
            start = pl.multiple_of(i * QB, QB)
            col = lax.broadcasted_iota(jnp.int32, (1, KW), 1)
            key_ok = (col >= KPAD - i * QB) if masked else None
            for pair in range(4):
                lo = pair * 128
                kw = kv_ref[pl.ds(start, KW), lo:lo + 128]
                vw = kv_ref[pl.ds(start, KW), DA + lo:DA + lo + 128]
                qs, first = _stack_heads(q_ref[:, lo:lo + 128])
                qs = qs * 0.125
                dos, _ = _stack_heads(do_ref[:, lo:lo + 128])
                rows = slice(2 * pair * QB, (2 * pair + 2) * QB)
                p = _pair_probs(qs, kw, b_ref[rows, :], key_ok)
                dp = lax.dot_general(dos, vw, NT, preferred_element_type=F32)
                ds = p * (dp - jnp.sum(p * dp, axis=-1, keepdims=True))
                db_ref[rows, :] += ds
                dsb = ds.astype(BF16)
                dq = jnp.dot(dsb, kw, preferred_element_type=F32)
                dq_ref[:, lo:lo + 128] = (jnp.where(first, dq[:QB], dq[QB:]) * 0.125).astype(BF16)
                acc_ref[:, lo:lo + 128] += lax.dot_general(dsb, qs, TN, preferred_element_type=F32)
                acc_ref[:, DA + lo:DA + lo + 128] += lax.dot_general(p.astype(BF16), dos, TN,
                                                                     preferred_element_type=F32)

        pl.when(i < KPAD // QB)(lambda: run(True))
        pl.when(jnp.logical_and(i >= KPAD // QB, i < nb))(lambda: run(False))

        dkv_ref[...] = acc_ref[0:QB, :].astype(BF16)
        rest = acc_ref[QB:KW, :]
        acc_ref[0:KW - QB, :] = rest
        acc_ref[KW - QB:KW, :] = jnp.zeros((QB, 2 * DA), F32)

        @pl.when(i == nb + flush - 1)
        def _():
            pltpu.sync_copy(db_ref, db_out)

    last = nb - 1
    res, jres = _call(
        body, name="attn_bwd", grid=(nb + flush,),
        in_specs=[pl.BlockSpec((QB, DA), lambda i: (jnp.minimum(i, last), 0)),
                  pl.BlockSpec(memory_space=pltpu.VMEM),
                  pl.BlockSpec(memory_space=pltpu.VMEM),
                  pl.BlockSpec((QB, DA), lambda i: (jnp.minimum(i, last), 0))],
        args=[qkv, kvp, bias, datt],
        out_specs=[pl.BlockSpec((QB, DA), lambda i: (jnp.minimum(i, last), 0)),
                   pl.BlockSpec((QB, 2 * DA), lambda i: (i, 0)),
                   ANY],
        out_shape=[jax.ShapeDtypeStruct((t, DA), BF16),
                   jax.ShapeDtypeStruct((t + KPAD, 2 * DA), BF16),
                   jax.ShapeDtypeStruct((NH * QB, KW), F32)],
        scratch=[pltpu.VMEM((KW, 2 * DA), F32), pltpu.VMEM((NH * QB, KW), F32)], sem=("arbitrary",), job=job)
    return res, jres


def _glu(c, gb):
    cb = c + gb
    return cb[:, :DC] * _sigmoid(cb[:, DC:])


def _ln_swish(pre, g, b):
    mu = jnp.mean(pre, axis=-1, keepdims=True)
    xc = pre - mu
    r = lax.rsqrt(jnp.mean(xc * xc, axis=-1, keepdims=True) + EPS)
    xhat = xc * r
    y = xhat * g + b
    return xhat, r, y


RT = 32


def _shifted_copies(src_ref, sh_ref, rows):
    for b in range(1, 8):
        sh_ref[b - 1, :, :] = src_ref[pl.ds(b, rows), :]


def _tap(src_ref, sh_ref, off, r0, rows=RT):
    a, b = divmod(off, 8)
    ref = src_ref if b == 0 else sh_ref.at[b - 1]
    if isinstance(r0, int):
        return ref[r0 + 8 * a:r0 + 8 * a + rows, :]
    return ref[pl.ds(pl.multiple_of(r0 + 8 * a, 8), rows), :]


def _conv_fwd(cin, glu_b, dw_w, dw_b, ln_g, ln_b, tm=TM):
    t = cin.shape[0]
    hb = tm // HALO

    def body(c_ref, h_ref, gb_ref, w_ref, wb_ref, g_ref, b_ref, cs_ref, pre_ref, ext_ref, sh_ref):
        i = pl.program_id(0)
        halo = _glu(h_ref[...], gb_ref[...])
        ext_ref[0:HALO, :] = jnp.where(i > 0, halo, jnp.zeros_like(halo))
        ext_ref[HALO:HALO + tm, :] = _glu(c_ref[...], gb_ref[...])
        ext_ref[HALO + tm:HALO + tm + 8, :] = jnp.zeros((8, DC), F32)
        _shifted_copies(ext_ref, sh_ref, HALO + tm)

        def tile(rt, carry):
            r0 = pl.multiple_of(rt * RT, RT)
            acc = jnp.zeros((RT, DC), F32) + wb_ref[...]
            for j in range(CW):
                acc = acc + w_ref[j:j + 1, :] * _tap(ext_ref, sh_ref, HALO - (CW - 1) + j, r0)
            pre_ref[pl.ds(r0, RT), :] = acc
            return carry

        lax.fori_loop(0, tm // RT, tile, 0, unroll=2)
        _, _, y = _ln_swish(pre_ref[...], g_ref[...], b_ref[...])
        cs_ref[...] = (y * _sigmoid(y)).astype(BF16)

    vec = lambda n: pl.BlockSpec((1, n), lambda i: (0, 0))
    return pl.pallas_call(
        body,
        name="conv_fwd",
        grid=(t // tm,),
        in_specs=[pl.BlockSpec((tm, 2 * DC), lambda i: (i, 0)),
                  pl.BlockSpec((HALO, 2 * DC), lambda i: (jnp.maximum(i * hb - 1, 0), 0)),
                  vec(2 * DC), pl.BlockSpec((CW, DC), lambda i: (0, 0)), vec(DC), vec(DC), vec(DC)],
        out_specs=[pl.BlockSpec((tm, DC), lambda i: (i, 0)), pl.BlockSpec((tm, DC), lambda i: (i, 0))],
        out_shape=[jax.ShapeDtypeStruct((t, DC), BF16), jax.ShapeDtypeStruct((t, DC), F32)],
        scratch_shapes=[pltpu.VMEM((HALO + tm + 8, DC), F32), pltpu.VMEM((7, HALO + tm, DC), F32)],
        compiler_params=_params(("arbitrary",)),
    )(cin, cin, glu_b, dw_w, dw_b, ln_g, ln_b)


def _conv_bwd(dcs, pre, cin, glu_b, dw_w, ln_g, ln_b, tm=TM):
    t = cin.shape[0]
    hb = tm // HALO
    steps = t // tm
    nhb = t // HALO

    def dpre_of(dcs_v, pre_v, g, b):
        xhat, r, y = _ln_swish(pre_v, g, b)
        sg = _sigmoid(y)
        dy = dcs_v * (sg * (1.0 + y * (1.0 - sg)))
        dxh = dy * g
        dpre = r * (dxh - jnp.mean(dxh, axis=-1, keepdims=True)
                    - xhat * jnp.mean(dxh * xhat, axis=-1, keepdims=True))
        return dpre, dy * xhat, dy

    def body(dcs_ref, dcsn_ref, pre_ref, pren_ref, c_ref, ch_ref, gb_ref, w_ref, g_ref, b_ref,
             dc_ref, dgb_ref, dw_ref, dwb_ref, dg_ref, db_ref,
             gext_ref, dext_ref, shg_ref, shd_ref, a_gb, a_w, a_wb, a_g, a_b):
        i = pl.program_id(0)

        @pl.when(i == 0)
        def _():
            for a in (a_gb, a_w, a_wb, a_g, a_b):
                a[...] = jnp.zeros_like(a)

        fold = lambda v: v.reshape(v.shape[0] // 8, 8, v.shape[-1]).sum(axis=0)
        g, b = g_ref[...], b_ref[...]
        dpre, dg_t, db_t = dpre_of(dcs_ref[...], pre_ref[...], g, b)
        dpre_n, _, _ = dpre_of(dcsn_ref[...], pren_ref[...], g, b)
        dext_ref[0:tm, :] = dpre
        dext_ref[tm:tm + HALO, :] = jnp.where(i < steps - 1, dpre_n, jnp.zeros_like(dpre_n))
        dext_ref[tm + HALO:tm + HALO + 8, :] = jnp.zeros((8, DC), F32)
        a_wb[...] += fold(dpre)
        a_g[...] += fold(dg_t)
        a_b[...] += fold(db_t)
        halo = _glu(ch_ref[...], gb_ref[...])
        gext_ref[0:HALO, :] = jnp.where(i > 0, halo, jnp.zeros_like(halo))
        gext_ref[HALO:HALO + tm, :] = _glu(c_ref[...], gb_ref[...])
        gext_ref[HALO + tm:HALO + tm + 8, :] = jnp.zeros((8, DC), F32)
        _shifted_copies(gext_ref, shg_ref, HALO + tm)
        _shifted_copies(dext_ref, shd_ref, HALO + tm)

        for j in range(CW):
            a_w[8 * j:8 * j + 8, :] += fold(dext_ref[0:tm, :] * _tap(gext_ref, shg_ref, HALO - (CW - 1) + j, 0, tm))

        def tile(rt, carry):
            r0 = pl.multiple_of(rt * RT, RT)
            dglu = jnp.zeros((RT, DC), F32)
            for j in range(CW):
                dglu = dglu + w_ref[j:j + 1, :] * _tap(dext_ref, shd_ref, CW - 1 - j, r0)
            gext_ref[pl.ds(r0, RT), :] = dglu
            return carry

        lax.fori_loop(0, tm // RT, tile, 0, unroll=2)
        dglu = gext_ref[0:tm, :]
        cb = c_ref[...] + gb_ref[...]
        sg = _sigmoid(cb[:, DC:])
        dc = jnp.concatenate([dglu * sg, dglu * cb[:, :DC] * sg * (1.0 - sg)], axis=1)
        dc_ref[...] = dc.astype(BF16)
        a_gb[...] += fold(dc)

        @pl.when(i == steps - 1)
        def _():
            dgb_ref[...] = a_gb[...].sum(axis=0, keepdims=True)
            dw_ref[...] = a_w[...]
            dwb_ref[...] = a_wb[...].sum(axis=0, keepdims=True)
            dg_ref[...] = a_g[...].sum(axis=0, keepdims=True)
            db_ref[...] = a_b[...].sum(axis=0, keepdims=True)

    vec = lambda n: pl.BlockSpec((1, n), lambda i: (0, 0))
    nxt = lambda i: (jnp.minimum((i + 1) * hb, nhb - 1), 0)
    prv = lambda i: (jnp.maximum(i * hb - 1, 0), 0)
    return pl.pallas_call(
        body,
        name="conv_bwd",
        grid=(steps,),
        in_specs=[pl.BlockSpec((tm, DC), lambda i: (i, 0)), pl.BlockSpec((HALO, DC), nxt),
                  pl.BlockSpec((tm, DC), lambda i: (i, 0)), pl.BlockSpec((HALO, DC), nxt),
                  pl.BlockSpec((tm, 2 * DC), lambda i: (i, 0)), pl.BlockSpec((HALO, 2 * DC), prv),
                  vec(2 * DC), pl.BlockSpec((CW, DC), lambda i: (0, 0)), vec(DC), vec(DC)],
        out_specs=[pl.BlockSpec((tm, 2 * DC), lambda i: (i, 0)), vec(2 * DC),
                   pl.BlockSpec((CW * 8, DC), lambda i: (0, 0)), vec(DC), vec(DC), vec(DC)],
        out_shape=[jax.ShapeDtypeStruct((t, 2 * DC), BF16), jax.ShapeDtypeStruct((1, 2 * DC), F32),
                   jax.ShapeDtypeStruct((CW * 8, DC), F32), jax.ShapeDtypeStruct((1, DC), F32),
                   jax.ShapeDtypeStruct((1, DC), F32), jax.ShapeDtypeStruct((1, DC), F32)],
        scratch_shapes=[pltpu.VMEM((HALO + tm + 8, DC), F32), pltpu.VMEM((tm + HALO + 8, DC), F32),
                        pltpu.VMEM((7, HALO + tm, DC), F32), pltpu.VMEM((7, HALO + tm, DC), F32),
                        pltpu.VMEM((8, 2 * DC), F32), pltpu.VMEM((CW * 8, DC), F32),
                        pltpu.VMEM((8, DC), F32), pltpu.VMEM((8, DC), F32), pltpu.VMEM((8, DC), F32)],
        compiler_params=_params(("arbitrary",)),
    )(dcs, dcs, pre, pre, cin, cin, glu_b, dw_w, ln_g, ln_b)


def _place():
    x, y, c = lax.axis_index("x"), lax.axis_index("y"), lax.axis_index("c")
    return x, y, c


def _peers(x, y, c):
    out = []
    for k in range(1, NDEV):
        fx, fy, fc = (k >> 2) & 1, (k >> 1) & 1, k & 1
        px = 1 - x if fx else x
        py = 1 - y if fy else y
        pc = 1 - c if fc else c
        out.append((px, py, pc))
    return out


def _job_out_shapes(job):
    kind, arrays = job
    if kind == "gather":
        return [jax.ShapeDtypeStruct((a.shape[0], NDEV) + a.shape[1:], a.dtype) for a in arrays]
    return [jax.ShapeDtypeStruct((NDEV, a.shape[0]) + a.shape[2:], a.dtype) for a in arrays]


def _job_scratch(job):
    n = len(job[1])
    return [pltpu.SemaphoreType.DMA((n, NDEV - 1)), pltpu.SemaphoreType.DMA((n, NDEV - 1)),
            pltpu.SemaphoreType.DMA((n,))]


def _gather_parts(ins, outs, send_sems, recv_sems, local_sems):
    x, y, c = _place()
    me, sib = (x, y, c), (x, y, 1 - c)
    chips = [(1 - x, y), (x, 1 - y), (1 - x, 1 - y)]

    def copy(a, k, block, to, src=None):
        px, py, pc = block
        dst = outs[a].at[:, 4 * px + 2 * py + pc]
        return pltpu.make_async_remote_copy(
            src_ref=dst if src is None else src, dst_ref=dst,
            send_sem=send_sems.at[a, k], recv_sem=recv_sems.at[a, k], device_id=to, device_id_type=MESH)

    n = len(ins)
    local = [pltpu.make_async_copy(ins[a], outs[a].at[:, 4 * x + 2 * y + c], local_sems.at[a]) for a in range(n)]
    first = [[copy(a, 0, me, sib, src=ins[a])] + [copy(a, 1 + j, me, (*chip, c), src=ins[a])
                                                   for j, chip in enumerate(chips)] for a in range(n)]

    def start():
        for a in range(n):
            local[a].start()
            for cp in first[a]:
                cp.start()

    def finish():
        passed = []
        for j, chip in enumerate(chips):
            for a in range(n):
                copy(a, 1 + j, (*chip, c), me).wait_recv()
                cp = copy(a, 4 + j, (*chip, c), sib)
                cp.start()
                passed.append(cp)
        for a in range(n):
            copy(a, 0, sib, me).wait_recv()
            for j, chip in enumerate(chips):
                copy(a, 4 + j, (*chip, 1 - c), me).wait_recv()
        for a in range(n):
            for cp in first[a]:
                cp.wait_send()
            local[a].wait()
        for cp in passed:
            cp.wait_send()

    return start, finish


def _exchange_parts(ins, outs, send_sems, recv_sems, local_sems):
    x, y, c = _place()
    me = 4 * x + 2 * y + c
    n = len(ins)
    peers = _peers(x, y, c)
    local = [pltpu.make_async_copy(ins[a].at[:, me], outs[a].at[me], local_sems.at[a]) for a in range(n)]

    def copy(a, k):
        px, py, pc = peers[k]
        return pltpu.make_async_remote_copy(
            src_ref=ins[a].at[:, 4 * px + 2 * py + pc], dst_ref=outs[a].at[me],
            send_sem=send_sems.at[a, k], recv_sem=recv_sems.at[a, k], device_id=peers[k], device_id_type=MESH)

    def arrival(a, k):
        px, py, pc = peers[k]
        return pltpu.make_async_remote_copy(
            src_ref=ins[a].at[:, me], dst_ref=outs[a].at[4 * px + 2 * py + pc],
            send_sem=send_sems.at[a, k], recv_sem=recv_sems.at[a, k], device_id=peers[k], device_id_type=MESH)

    def start():
        for a in range(n):
            local[a].start()
            for k in range(NDEV - 1):
                copy(a, k).start()

    def finish():
        for a in range(n):
            for k in range(NDEV - 1):
                arrival(a, k).wait_recv()
        for a in range(n):
            for k in range(NDEV - 1):
                copy(a, k).wait_send()
            local[a].wait()

    return start, finish


def _call(body, *, name, grid, in_specs, args, out_specs, out_shape, scratch=(), sem=None, aliases=None, job=None):
    aliases = dict(aliases or {})
    if job is None:
        res = pl.pallas_call(
            body, name=name, grid=grid, in_specs=list(in_specs), out_specs=list(out_specs),
            out_shape=list(out_shape), scratch_shapes=list(scratch), input_output_aliases=aliases,
            compiler_params=_params(sem))(*args)
        return list(res), []
    kind, arrays = job
    n_in, n_out, n_scr, nj = len(args), len(out_shape), len(scratch), len(arrays)

    def wrapped(*refs):
        ins = refs[:n_in]
        jin = refs[n_in:n_in + nj]
        o0 = n_in + nj
        outs = refs[o0:o0 + n_out]
        jout = refs[o0 + n_out:o0 + n_out + nj]
        s0 = o0 + n_out + nj
        scr = refs[s0:s0 + n_scr]
        sems = refs[s0 + n_scr:]
        parts = _gather_parts if kind == "gather" else _exchange_parts
        start, finish = parts(jin, jout, *sems)
        if not grid:
            start()
            body(*ins, *outs, *scr)
            finish()
            return
        first = last = None
        for d, g in enumerate(grid):
            f, l = pl.program_id(d) == 0, pl.program_id(d) == g - 1
            first = f if first is None else jnp.logical_and(first, f)
            last = l if last is None else jnp.logical_and(last, l)
        pl.when(first)(start)
        body(*ins, *outs, *scr)
        pl.when(last)(finish)

    res = pl.pallas_call(
        wrapped, name=name, grid=grid, in_specs=list(in_specs) + [ANY] * nj,
        out_specs=list(out_specs) + [ANY] * nj, out_shape=list(out_shape) + _job_out_shapes(job),
        scratch_shapes=list(scratch) + _job_scratch(job), input_output_aliases=aliases,
        compiler_params=pltpu.CompilerParams(
            dimension_semantics=None if not grid else ("arbitrary",) * len(grid),
            vmem_limit_bytes=VMEM_LIMIT, has_side_effects=True))(*args, *arrays)
    return list(res[:n_out]), list(res[n_out:])


def _comm_only(name, job):
    return _call(lambda: None, name=name, grid=(), in_specs=[], args=[], out_specs=[], out_shape=[], job=job)[1]


def _sum_devices(name, parts, tr):
    _, r, c = parts.shape

    def body(p_ref, o_ref):
        acc = p_ref[0].astype(F32)
        for d in range(1, NDEV):
            acc = acc + p_ref[d].astype(F32)
        o_ref[...] = acc

    return pl.pallas_call(
        body,
        name=name,
        grid=(r // tr,),
        in_specs=[pl.BlockSpec((NDEV, tr, c), lambda i: (0, i, 0))],
        out_specs=pl.BlockSpec((tr, c), lambda i: (i, 0)),
        out_shape=jax.ShapeDtypeStruct((r, c), F32),
        compiler_params=_params(("parallel",)),
    )(parts)


def _adamw(name, w, g, m, v, tr=None):
    r, c = w.shape
    tr = r if tr is None else tr

    def body(w_ref, g_ref, m_ref, v_ref, d_ref, nm_ref, nv_ref):
        gv = g_ref[...]
        nm = ADAM_B1 * m_ref[...] + (1.0 - ADAM_B1) * gv
        nv = ADAM_B2 * v_ref[...] + (1.0 - ADAM_B2) * (gv * gv)
        m_hat = nm / (1.0 - ADAM_B1 ** ADAM_STEP)
        v_hat = nv / (1.0 - ADAM_B2 ** ADAM_STEP)
        d_ref[...] = -ADAM_LR * (m_hat / (jnp.sqrt(v_hat) + ADAM_EPS) + ADAM_WD * w_ref[...])
        nm_ref[...] = nm
        nv_ref[...] = nv

    spec = pl.BlockSpec((tr, c), lambda i: (i, 0))
    return pl.pallas_call(
        body,
        name=name,
        grid=(r // tr,),
        in_specs=[spec] * 4,
        out_specs=[spec] * 3,
        out_shape=[jax.ShapeDtypeStruct((r, c), F32)] * 3,
        compiler_params=_params(("parallel",)),
    )(w, g, m, v)


SMALL = ["ffn1_norm_pre", "ffn1_norm_post", "mix_norm_pre", "gate_bias", "rel_table", "conv_glu_bias",
         "conv_dw_b", "conv_ln_g", "conv_ln_b", "mix_norm_post", "ffn2_norm_pre", "ffn2_norm_post"]
SMALL_ROWS = 24
DW_ROWS = 32


LOSS_ROW = 20
RELP = 384
SMALL_LAYOUT = {}
_r = 0
for _name, _n in zip(SMALL, [D, D, D, 2 * D, None, 2 * DC, DC, DC, DC, D, D, D]):
    if _n is None:
        SMALL_LAYOUT[_name] = (_r, NH, RELP)
        _r += NH
    else:
        SMALL_LAYOUT[_name] = (_r, -(-_n // D), min(_n, D))
        _r += -(-_n // D)
assert _r == LOSS_ROW


def _small_pieces(name):
    r0, nr, nc = SMALL_LAYOUT[name]
    if name == "rel_table":
        return [(slice(r0, r0 + NH), slice(0, nc), slice(0, NH), slice(0, nc))]
    return [(slice(r0 + k, r0 + k + 1), slice(0, nc), slice(0, 1), slice(k * nc, (k + 1) * nc)) for k in range(nr)]


def _pack_small(vals, loss_row):
    def body(*refs):
        o = refs[-1]
        o[...] = jnp.zeros_like(o)
        for ref, name in zip(refs, SMALL):
            for prow, pcol, arow, acol in _small_pieces(name):
                o[prow, pcol] = ref[arow, acol]
        o[LOSS_ROW:LOSS_ROW + 1, :] = refs[len(SMALL)][...]

    vm = pl.BlockSpec(memory_space=pltpu.VMEM)
    return pl.pallas_call(body, name="pack_small", out_shape=jax.ShapeDtypeStruct((SMALL_ROWS, D), F32),
                          in_specs=[vm] * (len(SMALL) + 1), out_specs=vm,
                          compiler_params=_params())(*[vals[n] for n in SMALL], loss_row)


def _small_update(parts, w, m, v):
    ns = len(SMALL)

    def body(p_ref, *refs):
        ins, outs = refs[:3 * ns], refs[3 * ns:]

        def total(prow, pcol):
            g = p_ref[0, prow, pcol]
            for d in range(1, NDEV):
                g = g + p_ref[d, prow, pcol]
            return g

        for q, name in enumerate(SMALL):
            w_ref, m_ref, v_ref = ins[3 * q:3 * q + 3]
            o_g, o_d, o_m, o_v = outs[4 * q:4 * q + 4]
            for prow, pcol, arow, acol in _small_pieces(name):
                g = total(prow, pcol)
                nm = ADAM_B1 * m_ref[arow, acol] + (1.0 - ADAM_B1) * g
                nv = ADAM_B2 * v_ref[arow, acol] + (1.0 - ADAM_B2) * (g * g)
                m_hat = nm / (1.0 - ADAM_B1 ** ADAM_STEP)
                v_hat = nv / (1.0 - ADAM_B2 ** ADAM_STEP)
                o_g[arow, acol] = g
                o_d[arow, acol] = -ADAM_LR * (m_hat / (jnp.sqrt(v_hat) + ADAM_EPS) + ADAM_WD * w_ref[arow, acol])
                o_m[arow, acol] = nm
                o_v[arow, acol] = nv
        outs[4 * ns][...] = total(slice(LOSS_ROW, LOSS_ROW + 1), slice(None))

    args, out_shape = [], []
    for name in SMALL:
        args += [w[name], m[name], v[name]]
        out_shape += [jax.ShapeDtypeStruct(w[name].shape, F32)] * 4
    out_shape.append(jax.ShapeDtypeStruct((1, D), F32))
    vm = pl.BlockSpec(memory_space=pltpu.VMEM)
    res = pl.pallas_call(body, name="small_update", out_shape=out_shape, in_specs=[vm] * (1 + 3 * ns),
                         out_specs=[vm] * len(out_shape), compiler_params=_params())(parts, *args)
    return {name: res[4 * q:4 * q + 4] for q, name in enumerate(SMALL)}, res[-1]


def kernel(x, ffn1_norm_pre, ffn1_w_gate, ffn1_w_up, ffn1_w_down, ffn1_norm_post, mix_norm_pre, w_in, gate_bias, rel_table, w_attn_out, conv_glu_bias, conv_dw_w, conv_dw_b, conv_ln_g, conv_ln_b, conv_w_out, w_out, mix_norm_post, ffn2_norm_pre, ffn2_w_gate, ffn2_w_up, ffn2_w_down, ffn2_norm_post, loss_target, m_ffn1_norm_pre, m_ffn1_w_gate, m_ffn1_w_up, m_ffn1_w_down, m_ffn1_norm_post, m_mix_norm_pre, m_w_in, m_gate_bias, m_rel_table, m_w_attn_out, m_conv_glu_bias, m_conv_dw_w, m_conv_dw_b, m_conv_ln_g, m_conv_ln_b, m_conv_w_out, m_w_out, m_mix_norm_post, m_ffn2_norm_pre, m_ffn2_w_gate, m_ffn2_w_up, m_ffn2_w_down, m_ffn2_norm_post, v_ffn1_norm_pre, v_ffn1_w_gate, v_ffn1_w_up, v_ffn1_w_down, v_ffn1_norm_post, v_mix_norm_pre, v_w_in, v_gate_bias, v_rel_table, v_w_attn_out, v_conv_glu_bias, v_conv_dw_w, v_conv_dw_b, v_conv_ln_g, v_conv_ln_b, v_conv_w_out, v_w_out, v_mix_norm_post, v_ffn2_norm_pre, v_ffn2_w_gate, v_ffn2_w_up, v_ffn2_w_down, v_ffn2_norm_post):
    return _step(dict(locals()))


WEIGHTS = ["ffn1_norm_pre", "ffn1_w_gate", "ffn1_w_up", "ffn1_w_down", "ffn1_norm_post", "mix_norm_pre", "w_in",
           "gate_bias", "rel_table", "w_attn_out", "conv_glu_bias", "conv_dw_w", "conv_dw_b", "conv_ln_g",
           "conv_ln_b", "conv_w_out", "w_out", "mix_norm_post", "ffn2_norm_pre", "ffn2_w_gate", "ffn2_w_up",
           "ffn2_w_down", "ffn2_norm_post"]
FS = FF // NDEV
PS = (3 * DA + 2 * DC + 2 * D) // NDEV
OS = D // NDEV


def _local_step(xs, target, w, rel_table):
    t = xs.shape[0]
    vec = lambda n: w[n].reshape(1, -1)
    g_pre1, g_post1, g_mix, g_mixp = vec("ffn1_norm_pre"), vec("ffn1_norm_post"), vec("mix_norm_pre"), vec("mix_norm_post")
    g_pre2, g_post2 = vec("ffn2_norm_pre"), vec("ffn2_norm_post")
    gate_b, glu_b = vec("gate_bias"), vec("conv_glu_bias")
    dw_b, ln_g, ln_b = vec("conv_dw_b"), vec("conv_ln_g"), vec("conv_ln_b")

    tr = lambda a: jnp.transpose(a[0]).astype(BF16)
    sh_gu1 = jnp.stack([tr(w["ffn1_w_gate"]), tr(w["ffn1_w_up"])])
    sh_mid = [w["ffn1_w_down"].astype(BF16), tr(w["w_in"])[None], w["w_out"].astype(BF16),
              jnp.stack([tr(w["w_attn_out"]), tr(w["conv_w_out"])]),
              jnp.pad(w["conv_dw_w"][0, :, 0, :], ((0, DW_ROWS - CW), (0, 0)))[None]]
    sh_2 = jnp.stack([tr(w["ffn2_w_gate"]), tr(w["ffn2_w_up"]), w["ffn2_w_down"][0].astype(BF16)])

    (n1,), (w_gu1,) = _rowwise("pre1", lambda xv, g: ((_rms(xv)[0] * g),), [xs], [g_pre1], [(D, BF16)],
                               job=("gather", [sh_gu1]))
    w_gu1 = w_gu1.reshape(2, FF, D)
    (a1, b1, s1), (w_d1, wb, wc, wd, we) = _ffn_up("ffn1_up", n1, w_gu1, 0, job=("gather", sh_mid))
    w_d1, wb, wc, wd = w_d1.reshape(1, FF, D), wb.reshape(NDEV * PS, D), wc.reshape(D, D), wd.reshape(2, D, DA)
    dw_full = jnp.transpose(we[0], (1, 0, 2)).reshape(DW_ROWS, DC)[:CW]
    row = lambda i, j, q: (i, 0)
    top = lambda i, j, q: (0, 0)
    tmr = min(512, t)

    def post1(fv, xv, gp, gm):
        h = xv + 0.5 * (_rms(fv)[0] * gp)
        return fv, h, _rms(h)[0] * gm

    f1, h1, u = _mm_rows("ffn1_down", t, tmr, [(s1, (tmr, FF), row)], [(w_d1, (None, FF, D), lambda i, j, q: (0, 0, 0))],
                         [[(0, 0)]], NN, post1, [xs], [g_post1, g_mix], [(D, F32), (D, F32), (D, BF16)])

    qkv = _mm_simple("proj_qkv", u, wb, NT, BF16, 1024, 512, b_row0=0, b_rows=3 * DA)
    cin = _mm_simple("proj_conv", u, wb, NT, F32, 1024, 512, b_row0=3 * DA, b_rows=2 * DC)
    gg = _mm_simple("proj_gate", u, wb, NT, BF16, 1024, 512, b_row0=3 * DA + 2 * DC, b_rows=2 * D)

    bias = _relbias_fwd(jnp.pad(rel_table[0], ((0, 0), (0, 384 - NREL))))
    kvp = jnp.pad(qkv[:, DA:], ((KPAD, 0), (0, 0)))
    att, (w_2,) = _attn_fwd(qkv, kvp, bias, job=("gather", [sh_2]))
    w_2 = w_2.reshape(3, FF, D)
    cs, pre = _conv_fwd(cin, glu_b, dw_full, dw_b, ln_g, ln_b)
    ya = _mm_simple("attn_out", att, wd[0], NT, BF16, 1024, 1024)

    def merge(ybv, yav, gv, gb):
        gates = _sigmoid(gv + gb)
        return ybv, gates[:, :D] * yav + gates[:, D:] * ybv

    yb, merged = _mm_rows("conv_out", t, tmr, [(cs, (tmr, DC), row)], [(wd[1], (D, DC), top)], [[(0, 0)]], NT,
                          merge, [ya, gg], [gate_b], [(D, BF16), (D, BF16)])

    def postm(mv, hv, gp, g2):
        h = hv + _rms(mv)[0] * gp
        return mv, h, _rms(h)[0] * g2

    mm_, h2, n2 = _mm_rows("mix_out", t, tmr, [(merged, (tmr, D), row)], [(wc, (D, D), top)], [[(0, 0)]], NN,
                           postm, [h1], [g_mixp, g_pre2], [(D, F32), (D, F32), (D, BF16)])
    a2, b2, s2 = _ffn_up("ffn2_up", n2, w_2, 0)

    def post2(fv, hv, tv, gp):
        fh, r = _rms(fv)
        yv = hv + 0.5 * (fh * gp)
        err = yv - tv
        dy = err * (1.0 / D)
        df, dg = _rms_bwd(fh, r, gp, 0.5 * dy)
        return dy, df, (0.5 / D) * (err * err), dg

    dy, df2, loss_row, d_post2 = _mm_rows(
        "ffn2_down", t, tmr, [(s2, (tmr, FF), row)], [(w_2, (None, FF, D), lambda i, j, q: (2, 0, 0))], [[(0, 0)]],
        NN, post2, [h2, target], [g_post2], [(D, F32), (D, BF16)], [D, D])

    tmw = 1408
    nfi = FF // tmw
    tkf = min(2048, t)
    tkp = min(1024, t)

    def wgrad_down(name, s, df, shape, part, carry_buf, job=None):
        return _wgrad(name, (s, (tkf, tmw), lambda i, j, q: (q, i)), df,
                      (None, tmw, D), lambda i, j, q: (part, i, 0), nfi, carry_buf, shape, tk=tkf, job=job)

    def wgrad_gate_up(name, dab, nrm, shape, carry_buf, job=None):
        return _wgrad(name, (dab, (None, tkf, tmw), lambda i, j, q: (i // nfi, q, i % nfi)), nrm,
                      (None, tmw, D), lambda i, j, q: (i // nfi, i % nfi, 0), 2 * nfi, carry_buf, shape,
                      tk=tkf, job=job)

    g_2 = wgrad_down("ffn2_wgrad_d", s2, df2, (3, FF, D), 2, None)
    dab2 = _ffn_bwd_act("ffn2_bwd_act", df2, w_2, 2, a2, b2)
    g_2 = wgrad_gate_up("ffn2_wgrad_gu", dab2, n2, (3, FF, D), g_2)

    def bwd_pre2(dnv, hv, dyv, mv, g2, gp):
        hh, r = _rms(hv)
        dx, dg2 = _rms_bwd(hh, r, g2, dnv)
        dh = dyv + dx
        mh, rm = _rms(mv)
        dm, dgp = _rms_bwd(mh, rm, gp, dh)
        return dh, dm, dg2, dgp

    tmb = 256
    dh2, dm, d_pre2, d_mixp = _mm_rows(
        "ffn2_bwd_in", t, tmb,
        [(dab2, (None, tmb, FF), lambda i, j, q: (0, i, 0)), (dab2, (None, tmb, FF), lambda i, j, q: (1, i, 0))],
        [(w_2, (None, FF, D), lambda i, j, q: (0, 0, 0)), (w_2, (None, FF, D), lambda i, j, q: (1, 0, 0))],
        [[(0, 0), (1, 1)]], NN, bwd_pre2, [h2, dy, mm_], [g_pre2, g_mixp], [(D, F32), (D, BF16)], [D, D])

    def merge_bwd(dmv, yav, ybv, gv, gb):
        gates = _sigmoid(gv + gb)
        ga, gbb = gates[:, :D], gates[:, D:]
        dgg = jnp.concatenate([dmv * yav * ga * (1.0 - ga), dmv * ybv * gbb * (1.0 - gbb)], axis=1)
        return dmv * ga, dmv * gbb, dgg, dgg

    dya, dyb, dgg, d_gate_b = _mm_rows(
        "mix_out_bwd", t, tmr, [(dm, (tmr, D), row)], [(wc, (D, D), top)], [[(0, 0)]], NT, merge_bwd,
        [ya, yb, gg], [gate_b], [(D, BF16), (D, BF16), (2 * D, BF16)], [2 * D])
    datt = _mm_simple("attn_out_bwd", dya, wd[0], NN, BF16, 1024, 512)
    dcs = _mm_simple("conv_out_bwd", dyb, wd[1], NN, F32, 1024, 512)
    g_c = _wgrad("mix_out_wgrad", (merged, (tkp, D), lambda i, j, q: (q, 0)), dm,
                 (D, D), lambda i, j, q: (0, 0), 1, None, (D, D), tk=tkp)
    g_d = _wgrad("attn_out_wgrad", (dya, (tkp, D), lambda i, j, q: (q, 0)), att,
                 (None, D, DA), lambda i, j, q: (0, 0, 0), 1, None, (2, D, DA), tk=tkp)
    g_d = _wgrad("conv_out_wgrad", (dyb, (tkp, D), lambda i, j, q: (q, 0)), cs,
                 (None, D, DA), lambda i, j, q: (1, 0, 0), 1, g_d, (2, D, DA), tk=tkp)
    (dq, dkvp, dbias), (x_2, x_c, x_d) = _attn_bwd(
        qkv, kvp, bias, datt,
        job=("exchange", [g_2.reshape(3, NDEV, FS, D), g_c.reshape(1, NDEV, OS, D), g_d.reshape(2, NDEV, OS, DA)]))
    d_rel = _relbias_bwd(dbias)
    dcin, d_glu_b, d_dw8, d_dw_b, d_ln_g, d_ln_b = _conv_bwd(dcs, pre, cin, glu_b, dw_full, ln_g, ln_b)
    g_dw = jnp.pad(d_dw8, ((0, 8 * (DW_ROWS - CW)), (0, 0)))
    g_dw = g_dw.reshape(DW_ROWS * 8, NDEV, DC // NDEV).transpose(1, 0, 2)

    top = lambda i, j, q: (0, 0)
    g_b = jnp.concatenate([
        _wgrad("proj_wgrad_q", (dq, (tkp, DA), lambda i, j, q: (q, 0)), u, (DA, D), top, 1, None, (DA, D), tk=tkp),
        _wgrad("proj_wgrad_kv", (dkvp, (KPAD, 2 * DA), lambda i, j, q: (q + 1, 0)), u, (2 * DA, D), top, 1, None,
               (2 * DA, D), tk=KPAD),
        _wgrad("proj_wgrad_c", (dcin, (tkp, 2 * DC), lambda i, j, q: (q, 0)), u, (2 * DC, D), top, 1, None,
               (2 * DC, D), tk=tkp),
        _wgrad("proj_wgrad_g", (dgg, (tkp, 2 * D), lambda i, j, q: (q, 0)), u, (2 * D, D), top, 1, None,
               (2 * D, D), tk=tkp)], axis=0)

    tmu = 256
    a_ops = [(dq, (tmu, DA), row),
             (dkvp, (tmu, 2 * DA), lambda i, j, q: (i + KPAD // tmu, 0)),
             (dcin, (tmu, 2 * DC), row),
             (dgg, (tmu, 2 * D), row)]
    b_ops = [(wb[:DA], (DA, D), top), (wb[DA:3 * DA], (2 * DA, D), top),
             (wb[3 * DA:3 * DA + 2 * DC], (2 * DC, D), top), (wb[3 * DA + 2 * DC:], (2 * D, D), top)]

    def bwd_mix(duv, hv, dhv, fv, gm, gp):
        hh, r = _rms(hv)
        dx, dgm = _rms_bwd(hh, r, gm, duv)
        dh = dhv + dx
        fh, rf = _rms(fv)
        df, dgp = _rms_bwd(fh, rf, gp, 0.5 * dh)
        return dh, df, dgm, dgp

    (dh1, df1, d_mix, d_post1), (x_b, x_dw) = _mm_rows(
        "proj_bwd", t, tmu, a_ops, b_ops, [[(0, 0), (1, 1), (2, 2), (3, 3)]], NN, bwd_mix, [h1, dh2, f1],
        [g_mix, g_post1], [(D, F32), (D, BF16)], [D, D],
        job=("exchange", [g_b.reshape(1, NDEV, PS, D), g_dw[None]]))
    g_d1 = wgrad_down("ffn1_wgrad_d", s1, df1, (1, FF, D), 0, None)
    dab1, (x_d1,) = _ffn_bwd_act("ffn1_bwd_act", df1, w_d1, 0, a1, b1,
                                 job=("exchange", [g_d1.reshape(1, NDEV, FS, D)]))

    def wgrad_half(name, p, job=None):
        return _wgrad(name, (dab1, (None, tkf, tmw), lambda i, j, q: (p, q, i)), n1,
                      (None, tmw, D), lambda i, j, q: (0, i, 0), nfi, None, (1, FF, D), tk=tkf, job=job)

    g_g1 = wgrad_half("ffn1_wgrad_g", 0)
    g_u1, (x_g1,) = wgrad_half("ffn1_wgrad_u", 1, job=("exchange", [g_g1.reshape(1, NDEV, FS, D)]))
    dn1, (x_u1,) = _ffn_bwd_in("ffn1_bwd_in", dab1, w_gu1, 0, job=("exchange", [g_u1.reshape(1, NDEV, FS, D)]))

    def bwd_pre1(xv, dnv, dhv, g1):
        xh, r = _rms(xv)
        dx, dg1 = _rms_bwd(xh, r, g1, dnv)
        return dhv + dx, dg1

    dx, d_pre1 = _rowwise("bwd_pre1", bwd_pre1, [xs, dn1, dh1], [g_pre1], [(D, F32)], [D])

    small_g = {"ffn1_norm_pre": d_pre1, "ffn1_norm_post": d_post1, "mix_norm_pre": d_mix, "gate_bias": d_gate_b,
               "rel_table": d_rel, "conv_glu_bias": d_glu_b, "conv_dw_b": d_dw_b, "conv_ln_g": d_ln_g,
               "conv_ln_b": d_ln_b, "mix_norm_post": d_mixp, "ffn2_norm_pre": d_pre2, "ffn2_norm_post": d_post2}
    return loss_row, dx, (x_g1, x_u1, x_d1, x_2, x_b, x_c, x_d, x_dw), small_g


def _step(args):
    names = WEIGHTS
    w = {n: args[n] for n in names}
    fs, ps, os_ = FS, PS, OS
    conv_dw_w = args["conv_dw_w"]
    loss_row, dx, (x_g1, x_u1, x_d1, x_2, x_b, x_c, x_d, x_dw), small_g = _local_step(
        args["x"][0], args["loss_target"][0], w, args["rel_table"])

    g_small = _pack_small(small_g, loss_row)
    (x_s,) = _comm_only("gather_small_grads", ("gather", [g_small[None]]))

    s_g1 = _sum_devices("sum_ffn1_g", x_g1.reshape(NDEV, fs, D), fs)
    s_u1 = _sum_devices("sum_ffn1_u", x_u1.reshape(NDEV, fs, D), fs)
    s_d1 = _sum_devices("sum_ffn1_d", x_d1.reshape(NDEV, fs, D), fs)
    s_2 = _sum_devices("sum_ffn2", x_2.reshape(NDEV, 3 * fs, D), fs).reshape(3, fs, D)
    s_b = _sum_devices("sum_proj", x_b.reshape(NDEV, ps, D), ps)
    s_c = _sum_devices("sum_mix", x_c.reshape(NDEV, os_, D), os_)
    s_d = _sum_devices("sum_out", x_d.reshape(NDEV, 2 * os_, DA), 2 * os_).reshape(2, os_, DA)
    s_dw = _sum_devices("sum_dw", x_dw.reshape(NDEV, DW_ROWS * 8, DC // NDEV), DW_ROWS * 8)

    grads = {
        "ffn1_w_gate": jnp.transpose(s_g1)[None], "ffn1_w_up": jnp.transpose(s_u1)[None], "ffn1_w_down": s_d1[None],
        "ffn2_w_gate": jnp.transpose(s_2[0])[None], "ffn2_w_up": jnp.transpose(s_2[1])[None], "ffn2_w_down": s_2[2][None],
        "w_in": jnp.transpose(s_b)[None], "w_out": s_c[None],
        "w_attn_out": jnp.transpose(s_d[0])[None], "conv_w_out": jnp.transpose(s_d[1])[None],
    }
    deltas, new_m, new_v = {}, {}, {}

    def flat2(a, n):
        return jnp.pad(a[0], ((0, 0), (0, RELP - NREL))) if n == "rel_table" else a.reshape(1, -1)

    small, loss_terms = _small_update(
        x_s.reshape(NDEV, SMALL_ROWS, D), {n: flat2(w[n], n) for n in SMALL},
        {n: flat2(args["m_" + n], n) for n in SMALL}, {n: flat2(args["v_" + n], n) for n in SMALL})
    for n in SMALL:
        vals = [a[:, :NREL] if n == "rel_table" else a for a in small[n]]
        grads[n], deltas[n], new_m[n], new_v[n] = [a.reshape(w[n].shape) for a in vals]

    big = ["ffn1_w_gate", "ffn1_w_up", "ffn1_w_down", "w_in", "w_attn_out", "conv_w_out", "w_out",
           "ffn2_w_gate", "ffn2_w_up", "ffn2_w_down"]
    for n in big:
        shp = w[n].shape
        two = lambda a: a.reshape(shp[1], shp[2])
        rows = shp[1]
        tr_ = rows // 2 if rows % 16 == 0 else rows
        d_, m_, v_ = _adamw("adamw_" + n, two(w[n]), two(grads[n]), two(args["m_" + n]), two(args["v_" + n]), tr_)
        deltas[n], new_m[n], new_v[n] = d_.reshape(shp), m_.reshape(shp), v_.reshape(shp)

    g_dw_own = _fold8("fold_dw", s_dw)[:CW]
    grads["conv_dw_w"] = g_dw_own.reshape(1, CW, 1, DC // NDEV)
    flat = lambda a: a.reshape(CW, DC // NDEV)
    d_, m_, v_ = _adamw("adamw_dw", flat(conv_dw_w), g_dw_own, flat(args["m_conv_dw_w"]), flat(args["v_conv_dw_w"]))
    shp = conv_dw_w.shape
    deltas["conv_dw_w"], new_m["conv_dw_w"], new_v["conv_dw_w"] = d_.reshape(shp), m_.reshape(shp), v_.reshape(shp)

    loss = jnp.sum(loss_terms)
    return (loss, dx[None], *[grads[n] for n in names], *[deltas[n] for n in names],
            *[new_m[n] for n in names], *[new_v[n] for n in names])


def _fold8(name, a):
    r8, c = a.shape

    def body(a_ref, o_ref):
        o_ref[...] = a_ref[...].reshape(r8 // 8, 8, c).sum(axis=1)

    return pl.pallas_call(
        body,
        name=name,
        out_shape=jax.ShapeDtypeStruct((r8 // 8, c), F32),
        in_specs=[pl.BlockSpec(memory_space=pltpu.VMEM)],
        out_specs=pl.BlockSpec(memory_space=pltpu.VMEM),
        compiler_params=_params(),
    )(a)
```

```python
import functools

import jax
import jax.numpy as jnp
from jax import lax
from jax.experimental import pallas as pl
from jax.experimental.pallas import tpu as pltpu

F32 = jnp.float32
BF16 = jnp.bfloat16

D = 1024
FF = 2816
DA = 512
DC = 512
NH = 8
CHUNK = 64
LEFT = 8
CW = 31
NREL = 257
EPS = 1e-6
NDEV = 8

QB = 4 * CHUNK
KW = LEFT * CHUNK + QB
KPAD = LEFT * CHUNK
RELW = KW + QB
HALO = 32

TM = 512
VMEM_LIMIT = 56 * 1024 * 1024

ADAM_LR, ADAM_B1, ADAM_B2, ADAM_EPS, ADAM_WD, ADAM_STEP = 0.001, 0.9, 0.999, 1e-08, 0.01, 10

NT = (((1,), (1,)), ((), ()))
NN = (((1,), (0,)), ((), ()))
TN = (((0,), (0,)), ((), ()))

MESH = pl.DeviceIdType.MESH
ANY = pl.BlockSpec(memory_space=pl.ANY)


def _params(sem=None, vmem=VMEM_LIMIT):
    return pltpu.CompilerParams(dimension_semantics=sem, vmem_limit_bytes=vmem)


def _sigmoid(x):
    return 0.5 * jnp.tanh(0.5 * x) + 0.5


def _mm(name, grid, a_ops, b_ops, groups, dims, epi, outs, extras=(), carry=None, job=None, params=(),
        partials=()):
    nk = grid[2]
    na, nb, ne, no, ng = len(a_ops), len(b_ops), len(extras), len(outs), len(groups)
    npar, npart = len(params), len(partials)
    nc = 0 if carry is None else 1

    def body(*refs):
        a_refs = refs[:na]
        b_refs = refs[na:na + nb]
        e_refs = refs[na + nb:na + nb + ne]
        p_refs = refs[na + nb + ne:na + nb + ne + npar]
        o0 = na + nb + ne + npar + nc
        o_refs = refs[o0:o0 + no]
        s_refs = refs[o0 + no:o0 + no + npart]
        acc_refs = refs[o0 + no + npart:o0 + no + npart + (ng if nk > 1 else 0)]
        part_refs = refs[len(refs) - npart:] if npart else ()
        k = pl.program_id(2)
        prods = []
        for grp in groups:
            p = None
            for ai, bi in grp:
                t = lax.dot_general(a_refs[ai][...], b_refs[bi][...], dims, preferred_element_type=F32)
                p = t if p is None else p + t
            prods.append(p)

        def finish(vals):
            res = epi(vals, [e[...] for e in e_refs] + [p[...] for p in p_refs])
            for o, r in zip(o_refs, res[:no]):
                o[...] = r.astype(o.dtype)
            if npart:
                i, j = pl.program_id(0), pl.program_id(1)

                @pl.when(jnp.logical_and(i == 0, j == 0))
                def _():
                    for acc in part_refs:
                        acc[...] = jnp.zeros_like(acc)

                for acc, r in zip(part_refs, res[no:]):
                    acc[...] += r.reshape(r.shape[0] // 8, 8, r.shape[-1]).sum(axis=0)

                @pl.when(jnp.logical_and(i == grid[0] - 1, j == grid[1] - 1))
                def _():
                    for s, acc in zip(s_refs, part_refs):
                        s[...] = acc[...].sum(axis=0, keepdims=True)

        if nk == 1:
            finish(prods)
        else:
            @pl.when(k == 0)
            def _():
                for acc, p in zip(acc_refs, prods):
                    acc[...] = p

            @pl.when(k > 0)
            def _():
                for acc, p in zip(acc_refs, prods):
                    acc[...] += p

            @pl.when(k == nk - 1)
            def _():
                finish([acc[...] for acc in acc_refs])

    in_specs = [pl.BlockSpec(blk, im) for _, blk, im in list(a_ops) + list(b_ops) + list(extras)]
    in_specs += [pl.BlockSpec(p.shape, lambda i, j, q: (0, 0)) for p in params]
    args = [arr for arr, _, _ in list(a_ops) + list(b_ops) + list(extras)] + list(params)
    aliases = {}
    if carry is not None:
        in_specs.append(ANY)
        args.append(carry[0])
        aliases = {len(args) - 1: carry[1]}
    scratch = []
    if nk > 1:
        for _ in range(ng):
            blk = tuple(b for b in outs[0][2] if b is not None)
            scratch.append(pltpu.VMEM(blk, F32))
    scratch += [pltpu.VMEM((8, c), F32) for c in partials]
    res, jres = _call(
        body, name=name, grid=grid, in_specs=in_specs, args=args,
        out_specs=[pl.BlockSpec(blk, im) for _, _, blk, im in outs]
        + [pl.BlockSpec((1, c), lambda i, j, q: (0, 0)) for c in partials],
        out_shape=[jax.ShapeDtypeStruct(shp, dt) for shp, dt, _, _ in outs]
        + [jax.ShapeDtypeStruct((1, c), F32) for c in partials],
        scratch=scratch, sem=("arbitrary",) * 3 if partials else ("parallel", "parallel", "arbitrary"),
        aliases=aliases, job=job)
    return res if job is None else (res, jres)


def _first(accs, extras):
    return (accs[0],)


def _mm_rows(name, t, tm, a_ops, b_ops, groups, dims, fn, extras, params, outs, partials=(), nk=1, job=None):
    ne = len(extras)

    def epi(accs, rest):
        return fn(accs[0], *[r.astype(F32) for r in rest[:ne]], *rest[ne:])

    e_ops = [(arr, (tm, arr.shape[1]), lambda i, j, q: (i, 0)) for arr in extras]
    o_ops = [((t, c), dt, (tm, c), lambda i, j, q: (i, 0)) for c, dt in outs]
    return _mm(name, (t // tm, 1, nk), a_ops, b_ops, groups, dims, epi, o_ops, e_ops, job=job, params=params,
               partials=partials)


def _mm_simple(name, a, b, dims, out_dtype, tm, tn, b_row0=0, b_rows=None):
    m, kk = a.shape
    tm = min(tm, m)
    if dims is NT:
        n = b.shape[0] if b_rows is None else b_rows
        assert b.shape[1] == kk and b_row0 % tn == 0
        b_op = (b, (tn, kk), lambda i, j, q: (j + b_row0 // tn, 0))
    else:
        assert b.shape[0] == kk
        n = b.shape[1]
        b_op = (b, (kk, tn), lambda i, j, q: (0, j))
    a_op = (a, (tm, kk), lambda i, j, q: (i, 0))
    out = ((m, n), out_dtype, (tm, tn), lambda i, j, q: (i, j))
    return _mm(name, (m // tm, n // tn, 1), [a_op], [b_op], [[(0, 0)]], dims, _first, [out])[0]


def _rowwise(name, fn, tiled, params, outs, partials=(), tm=TM, job=None):
    t = tiled[0].shape[0]
    steps = t // tm
    nt, npar, no, npart = len(tiled), len(params), len(outs), len(partials)

    def body(*refs):
        t_refs = refs[:nt]
        p_refs = refs[nt:nt + npar]
        o_refs = refs[nt + npar:nt + npar + no]
        s_refs = refs[nt + npar + no:nt + npar + no + npart]
        acc_refs = refs[nt + npar + no + npart:]
        i = pl.program_id(0)
        res = fn(*[r[...].astype(F32) for r in t_refs], *[r[...] for r in p_refs])
        for o, r in zip(o_refs, res[:no]):
            o[...] = r.astype(o.dtype)

        @pl.when(i == 0)
        def _():
            for acc in acc_refs:
                acc[...] = jnp.zeros_like(acc)

        for acc, r in zip(acc_refs, res[no:]):
            acc[...] += r.reshape(tm // 8, 8, r.shape[-1]).sum(axis=0)

        @pl.when(i == steps - 1)
        def _():
            for s, acc in zip(s_refs, acc_refs):
                s[...] = acc[...].sum(axis=0, keepdims=True)

    in_specs = [pl.BlockSpec((tm, a.shape[1]), lambda i: (i, 0)) for a in tiled]
    in_specs += [pl.BlockSpec(p.shape, lambda i: (0, 0)) for p in params]
    out_specs = [pl.BlockSpec((tm, c), lambda i: (i, 0)) for c, _ in outs]
    out_specs += [pl.BlockSpec((1, c), lambda i: (0, 0)) for c in partials]
    out_shape = [jax.ShapeDtypeStruct((t, c), dt) for c, dt in outs]
    out_shape += [jax.ShapeDtypeStruct((1, c), F32) for c in partials]
    res, jres = _call(body, name=name, grid=(steps,), in_specs=in_specs, args=[*tiled, *params], out_specs=out_specs,
                      out_shape=out_shape, scratch=[pltpu.VMEM((8, c), F32) for c in partials], sem=("arbitrary",),
                      job=job)
    return res if job is None else (res, jres)


def _rms(x):
    r = lax.rsqrt(jnp.mean(x * x, axis=-1, keepdims=True) + EPS)
    return x * r, r


def _rms_bwd(xhat, r, g, dy):
    dxh = dy * g
    dx = r * (dxh - xhat * jnp.mean(dxh * xhat, axis=-1, keepdims=True))
    return dx, dy * xhat


def _ffn_up(name, n, wa, part, tm=512, tf=1408, job=None):
    t = n.shape[0]

    def epi(accs, extras):
        a, b = accs
        sg = _sigmoid(a)
        silu = a * sg
        return silu, b * (sg + silu * (1.0 - sg)), silu * b

    a_op = (n, (tm, D), lambda f, i, q: (i, 0))
    b_ops = [(wa, (None, tf, D), lambda f, i, q: (part, f, 0)),
             (wa, (None, tf, D), lambda f, i, q: (part + 1, f, 0))]
    outs = [((t, FF), BF16, (tm, tf), lambda f, i, q: (i, f))] * 3
    return _mm(name, (FF // tf, t // tm, 1), [a_op], b_ops, [[(0, 0)], [(0, 1)]], NT, epi, outs, job=job)


def _ffn_bwd_act(name, df, wa, part, a, b, tm=512, tf=1408, job=None):
    t = df.shape[0]

    def epi(accs, extras):
        ds = accs[0]
        return (jnp.stack([ds * extras[1].astype(F32), ds * extras[0].astype(F32)]),)

    a_op = (df, (tm, D), lambda f, i, q: (i, 0))
    b_op = (wa, (None, tf, D), lambda f, i, q: (part, f, 0))
    extras = [(a, (tm, tf), lambda f, i, q: (i, f)), (b, (tm, tf), lambda f, i, q: (i, f))]
    out = ((2, t, FF), BF16, (2, tm, tf), lambda f, i, q: (0, i, f))
    res = _mm(name, (FF // tf, t // tm, 1), [a_op], [b_op], [[(0, 0)]], NT, epi, [out], extras, job=job)
    return res[0] if job is None else (res[0][0], res[1])


def _ffn_bwd_in(name, dab, wa, part, tm=1024, job=None):
    t = dab.shape[1]
    tm = min(tm, t)
    a_op = (dab, (None, tm, FF), lambda i, j, q: (q, i, 0))
    b_op = (wa, (None, FF, D), lambda i, j, q: (part + q, 0, 0))
    out = ((t, D), F32, (tm, D), lambda i, j, q: (i, 0))
    res = _mm(name, (t // tm, 1, 2), [a_op], [b_op], [[(0, 0)]], NN, _first, [out], job=job)
    return res[0] if job is None else (res[0][0], res[1])


def _wgrad(name, dy_op, x, out_block, out_map, gi, carry_buf, out_shape, tk=512, job=None):
    t, c = x.shape
    b_op = (x, (tk, c), lambda i, j, q: (q, 0))
    out = (out_shape, BF16, out_block, out_map)
    carry = None if carry_buf is None else (carry_buf, 0)
    res = _mm(name, (gi, 1, t // tk), [dy_op], [b_op], [[(0, 0)]], TN, _first, [out], carry=carry, job=job)
    return res[0] if job is None else (res[0][0], res[1])


def _rel_onehot():
    j = lax.broadcasted_iota(jnp.int32, (384, RELW), 0)
    xx = lax.broadcasted_iota(jnp.int32, (384, RELW), 1)
    idx = jnp.clip(KPAD + QB - xx, -128, 128) + 128
    return (j == idx).astype(F32)


def _relbias_fwd(table):
    def body(t_ref, o_ref):
        rev = jnp.dot(t_ref[...], _rel_onehot(), precision=lax.Precision.HIGHEST, preferred_element_type=F32)
        for r in range(QB):
            row = pltpu.roll(rev, (RELW - (QB - r)) % RELW, 1)[:, :KW]
            rr = lax.broadcasted_iota(jnp.int32, (NH, KW), 1) >> 6
            ok = (rr >= (r // CHUNK)) & (rr <= (r // CHUNK) + LEFT)
            row = jnp.where(ok, row, -1e30)
            for h in range(NH):
                o_ref[h * QB + r:h * QB + r + 1, :] = row[h:h + 1, :]

    return pl.pallas_call(
        body,
        name="relbias_fwd",
        out_shape=jax.ShapeDtypeStruct((NH * QB, KW), F32),
        in_specs=[pl.BlockSpec(memory_space=pltpu.VMEM)],
        out_specs=pl.BlockSpec(memory_space=pltpu.VMEM),
        compiler_params=_params(),
    )(table)


def _relbias_bwd(dbias):
    def body(d_ref, o_ref, acc_ref):
        for h in range(NH):
            acc = jnp.zeros((1, RELW), F32)
            for r in range(QB):
                row = d_ref[h * QB + r:h * QB + r + 1, :]
                wide = jnp.concatenate([row, jnp.zeros((1, RELW - KW), F32)], axis=1)
                acc = acc + pltpu.roll(wide, QB - r, 1)
            acc_ref[h:h + 1, :] = acc
        o_ref[...] = lax.dot_general(acc_ref[...], _rel_onehot(), NT, precision=lax.Precision.HIGHEST,
                                     preferred_element_type=F32)

    return pl.pallas_call(
        body,
        name="relbias_bwd",
        out_shape=jax.ShapeDtypeStruct((NH, 384), F32),
        in_specs=[pl.BlockSpec(memory_space=pltpu.VMEM)],
        out_specs=pl.BlockSpec(memory_space=pltpu.VMEM),
        scratch_shapes=[pltpu.VMEM((NH, RELW), F32)],
        compiler_params=_params(),
    )(dbias)


HQ = QB // 2


def _stack_heads(x_pair):
    first = lax.broadcasted_iota(jnp.int32, (1, 128), 1) < 64
    zero = jnp.zeros_like(x_pair)
    a, b = jnp.where(first, x_pair, zero), jnp.where(first, zero, x_pair)
    return jnp.concatenate([a[:HQ], b[:HQ], a[HQ:], b[HQ:]], axis=0), first


def _unstack_heads(o, first):
    return jnp.concatenate([jnp.where(first, o[0:HQ], o[HQ:2 * HQ]),
                            jnp.where(first, o[2 * HQ:3 * HQ], o[3 * HQ:4 * HQ])], axis=0)


def _half_cols(hf):
    return slice(hf * HQ, hf * HQ + KW - HQ)


def _half_bias(b_ref, pair, hf):
    rows = lambda h: slice(h * QB + hf * HQ, h * QB + (hf + 1) * HQ)
    return jnp.concatenate([b_ref[rows(2 * pair), _half_cols(hf)], b_ref[rows(2 * pair + 1), _half_cols(hf)]], axis=0)


def _half_probs(s_full, b_ref, pair, hf, key_ok):
    s = s_full[2 * hf * HQ:2 * (hf + 1) * HQ, _half_cols(hf)] + _half_bias(b_ref, pair, hf)
    if key_ok is not None:
        s = jnp.where(key_ok[:, _half_cols(hf)], s, -1e30)
    e = jnp.exp(s - jnp.max(s, axis=-1, keepdims=True))
    return e * (1.0 / jnp.sum(e, axis=-1, keepdims=True))


def _widen(top, bottom):
    z = jnp.zeros((2 * HQ, HQ), top.dtype)
    return jnp.concatenate([jnp.concatenate([top, z], axis=1), jnp.concatenate([z, bottom], axis=1)], axis=0)


def _attn_fwd(qkv, kvp, bias, job=None):
    t = qkv.shape[0]

    def body(q_ref, kv_ref, b_ref, o_ref):
        i = pl.program_id(0)

        def run(masked):
            start = pl.multiple_of(i * QB, QB)
            col = lax.broadcasted_iota(jnp.int32, (1, KW), 1)
            key_ok = (col >= KPAD - i * QB) if masked else None
            for pair in range(4):
                lo = pair * 128
                kw = kv_ref[pl.ds(start, KW), lo:lo + 128]
                vw = kv_ref[pl.ds(start, KW), DA + lo:DA + lo + 128]
                qs, first = _stack_heads(q_ref[:, lo:lo + 128])
                s = lax.dot_general(qs * 0.125, kw, NT, preferred_element_type=F32)
                p = _widen(*[_half_probs(s, b_ref, pair, hf, key_ok).astype(BF16) for hf in range(2)])
                o = jnp.dot(p, vw, preferred_element_type=F32)
                o_ref[:, lo:lo + 128] = _unstack_heads(o, first).astype(BF16)

        pl.when(i < KPAD // QB)(lambda: run(True))
        pl.when(i >= KPAD // QB)(lambda: run(False))

    res, jres = _call(
        body, name="attn_fwd", grid=(t // QB,),
        in_specs=[pl.BlockSpec((QB, DA), lambda i: (i, 0)),
                  pl.BlockSpec(memory_space=pltpu.VMEM),
                  pl.BlockSpec(memory_space=pltpu.VMEM)],
        args=[qkv, kvp, bias],
        out_specs=[pl.BlockSpec((QB, DA), lambda i: (i, 0))],
        out_shape=[jax.ShapeDtypeStruct((t, DA), BF16)],
        sem=("arbitrary",), job=job)
    return res[0], jres


def _attn_bwd(qkv, kvp, bias, datt, job=None):
    t = qkv.shape[0]
    nb = t // QB
    flush = (KW - QB) // QB

    def body(q_ref, kv_ref, b_ref, do_ref, dq_ref, dkv_ref, db_out, acc_ref, db_ref):
        i = pl.program_id(0)

        @pl.when(i == 0)
        def _():
            acc_ref[...] = jnp.zeros_like(acc_ref)
            db_ref[...] = jnp.zeros_like(db_ref)

        def run(masked):
            start = pl.multiple_of(i * QB, QB)
            col = lax.broadcasted_iota(jnp.int32, (1, KW), 1)
            key_ok = (col >= KPAD - i * QB) if masked else None
            for pair in range(4):
                lo = pair * 128
                kw = kv_ref[pl.ds(start, KW), lo:lo + 128]
                vw = kv_ref[pl.ds(start, KW), DA + lo:DA + lo + 128]
                qs, first = _stack_heads(q_ref[:, lo:lo + 128])
                qs = qs * 0.125
                dos, _ = _stack_heads(do_ref[:, lo:lo + 128])
                s = lax.dot_general(qs, kw, NT, preferred_element_type=F32)
                dp = lax.dot_general(dos, vw, NT, preferred_element_type=F32)
                ps, dss = [], []
                for hf in range(2):
                    p = _half_probs(s, b_ref, pair, hf, key_ok)
                    dph = dp[2 * hf * HQ:2 * (hf + 1) * HQ, _half_cols(hf)]
                    ds = p * (dph - jnp.sum(p * dph, axis=-1, keepdims=True))
                    for k, h in enumerate((2 * pair, 2 * pair + 1)):
                        db_ref[h * QB + hf * HQ:h * QB + (hf + 1) * HQ, _half_cols(hf)] += ds[k * HQ:(k + 1) * HQ]
                    ps.append(p.astype(BF16))
                    dss.append(ds.astype(BF16))
                pb, dsb = _widen(*ps), _widen(*dss)
                dq = jnp.dot(dsb, kw, preferred_element_type=F32)
                dq_ref[:, lo:lo + 128] = (_unstack_heads(dq, first) * 0.125).astype(BF16)
                acc_ref[:, lo:lo + 128] += lax.dot_general(dsb, qs, TN, preferred_element_type=F32)
                acc_ref[:, DA + lo:DA + lo + 128] += lax.dot_general(pb, dos, TN, preferred_element_type=F32)

        pl.when(i < KPAD // QB)(lambda: run(True))
        pl.when(jnp.logical_and(i >= KPAD // QB, i < nb))(lambda: run(False))

        dkv_ref[...] = acc_ref[0:QB, :].astype(BF16)
        rest = acc_ref[QB:KW, :]
        acc_ref[0:KW - QB, :] = rest
        acc_ref[KW - QB:KW, :] = jnp.zeros((QB, 2 * DA), F32)

        @pl.when(i == nb + flush - 1)
        def _():
            pltpu.sync_copy(db_ref, db_out)

    last = nb - 1
    res, jres = _call(
        body, name="attn_bwd", grid=(nb + flush,),
        in_specs=[pl.BlockSpec((QB, DA), lambda i: (jnp.minimum(i, last), 0)),
                  pl.BlockSpec(memory_space=pltpu.VMEM),
                  pl.BlockSpec(memory_space=pltpu.VMEM),
                  pl.BlockSpec((QB, DA), lambda i: (jnp.minimum(i, last), 0))],
        args=[qkv, kvp, bias, datt],
        out_specs=[pl.BlockSpec((QB, DA), lambda i: (jnp.minimum(i, last), 0)),
                   pl.BlockSpec((QB, 2 * DA), lambda i: (i, 0)),
                   ANY],
        out_shape=[jax.ShapeDtypeStruct((t, DA), BF16),
                   jax.ShapeDtypeStruct((t + KPAD, 2 * DA), BF16),
                   jax.ShapeDtypeStruct((NH * QB, KW), F32)],
        scratch=[pltpu.VMEM((KW, 2 * DA), F32), pltpu.VMEM((NH * QB, KW), F32)], sem=("arbitrary",), job=job)
    return res, jres


def _glu(c, gb):
    cb = c + gb
    return cb[:, :DC] * _sigmoid(cb[:, DC:])


def _ln_swish(pre, g, b):
    mu = jnp.mean(pre, axis=-1, keepdims=True)
    xc = pre - mu
    r = lax.rsqrt(jnp.mean(xc * xc, axis=-1, keepdims=True) + EPS)
    xhat = xc * r
    y = xhat * g + b
    return xhat, r, y


RT = 32


def _shifted_copies(src_ref, sh_ref, rows):
    for b in range(1, 8):
        sh_ref[b - 1, :, :] = src_ref[pl.ds(b, rows), :]


def _tap(src_ref, sh_ref, off, r0, rows=RT):
    a, b = divmod(off, 8)
    ref = src_ref if b == 0 else sh_ref.at[b - 1]
    if isinstance(r0, int):
        return ref[r0 + 8 * a:r0 + 8 * a + rows, :]
    return ref[pl.ds(pl.multiple_of(r0 + 8 * a, 8), rows), :]


def _conv_fwd(cin, glu_b, dw_w, dw_b, ln_g, ln_b, tm=TM):
    t = cin.shape[0]
    hb = tm // HALO

    def body(c_ref, h_ref, gb_ref, w_ref, wb_ref, g_ref, b_ref, cs_ref, pre_ref, ext_ref, sh_ref):
        i = pl.program_id(0)
        halo = _glu(h_ref[...], gb_ref[...])
        ext_ref[0:HALO, :] = jnp.where(i > 0, halo, jnp.zeros_like(halo))
        ext_ref[HALO:HALO + tm, :] = _glu(c_ref[...], gb_ref[...])
        ext_ref[HALO + tm:HALO + tm + 8, :] = jnp.zeros((8, DC), F32)
        _shifted_copies(ext_ref, sh_ref, HALO + tm)

        def tile(rt, carry):
            r0 = pl.multiple_of(rt * RT, RT)
            acc = jnp.zeros((RT, DC), F32) + wb_ref[...]
            for j in range(CW):
                acc = acc + w_ref[j:j + 1, :] * _tap(ext_ref, sh_ref, HALO - (CW - 1) + j, r0)
            pre_ref[pl.ds(r0, RT), :] = acc
            return carry

        lax.fori_loop(0, tm // RT, tile, 0, unroll=2)
        _, _, y = _ln_swish(pre_ref[...], g_ref[...], b_ref[...])
        cs_ref[...] = (y * _sigmoid(y)).astype(BF16)

    vec = lambda n: pl.BlockSpec((1, n), lambda i: (0, 0))
    return pl.pallas_call(
        body,
        name="conv_fwd",
        grid=(t // tm,),
        in_specs=[pl.BlockSpec((tm, 2 * DC), lambda i: (i, 0)),
                  pl.BlockSpec((HALO, 2 * DC), lambda i: (jnp.maximum(i * hb - 1, 0), 0)),
                  vec(2 * DC), pl.BlockSpec((CW, DC), lambda i: (0, 0)), vec(DC), vec(DC), vec(DC)],
        out_specs=[pl.BlockSpec((tm, DC), lambda i: (i, 0)), pl.BlockSpec((tm, DC), lambda i: (i, 0))],
        out_shape=[jax.ShapeDtypeStruct((t, DC), BF16), jax.ShapeDtypeStruct((t, DC), F32)],
        scratch_shapes=[pltpu.VMEM((HALO + tm + 8, DC), F32), pltpu.VMEM((7, HALO + tm, DC), F32)],
        compiler_params=_params(("arbitrary",)),
    )(cin, cin, glu_b, dw_w, dw_b, ln_g, ln_b)


def _conv_bwd(dcs, pre, cin, glu_b, dw_w, ln_g, ln_b, tm=TM):
    t = cin.shape[0]
    hb = tm // HALO
    steps = t // tm
    nhb = t // HALO

    def dpre_of(dcs_v, pre_v, g, b):
        xhat, r, y = _ln_swish(pre_v, g, b)
        sg = _sigmoid(y)
        dy = dcs_v * (sg * (1.0 + y * (1.0 - sg)))
        dxh = dy * g
        dpre = r * (dxh - jnp.mean(dxh, axis=-1, keepdims=True)
                    - xhat * jnp.mean(dxh * xhat, axis=-1, keepdims=True))
        return dpre, dy * xhat, dy

    def body(dcs_ref, dcsn_ref, pre_ref, pren_ref, c_ref, ch_ref, gb_ref, w_ref, g_ref, b_ref,
             dc_ref, dgb_ref, dw_ref, dwb_ref, dg_ref, db_ref,
             gext_ref, dext_ref, shg_ref, shd_ref, a_gb, a_w, a_wb, a_g, a_b):
        i = pl.program_id(0)

        @pl.when(i == 0)
        def _():
            for a in (a_gb, a_w, a_wb, a_g, a_b):
                a[...] = jnp.zeros_like(a)

        fold = lambda v: v.reshape(v.shape[0] // 8, 8, v.shape[-1]).sum(axis=0)
        g, b = g_ref[...], b_ref[...]
        dpre, dg_t, db_t = dpre_of(dcs_ref[...], pre_ref[...], g, b)
        dpre_n, _, _ = dpre_of(dcsn_ref[...], pren_ref[...], g, b)
        dext_ref[0:tm, :] = dpre
        dext_ref[tm:tm + HALO, :] = jnp.where(i < steps - 1, dpre_n, jnp.zeros_like(dpre_n))
        dext_ref[tm + HALO:tm + HALO + 8, :] = jnp.zeros((8, DC), F32)
        a_wb[...] += fold(dpre)
        a_g[...] += fold(dg_t)
        a_b[...] += fold(db_t)
        halo = _glu(ch_ref[...], gb_ref[...])
        gext_ref[0:HALO, :] = jnp.where(i > 0, halo, jnp.zeros_like(halo))
        gext_ref[HALO:HALO + tm, :] = _glu(c_ref[...], gb_ref[...])
        gext_ref[HALO + tm:HALO + tm + 8, :] = jnp.zeros((8, DC), F32)
        _shifted_copies(gext_ref, shg_ref, HALO + tm)
        _shifted_copies(dext_ref, shd_ref, HALO + tm)

        for j in range(CW):
            a_w[8 * j:8 * j + 8, :] += fold(dext_ref[0:tm, :] * _tap(gext_ref, shg_ref, HALO - (CW - 1) + j, 0, tm))

        def tile(rt, carry):
            r0 = pl.multiple_of(rt * RT, RT)
            dglu = jnp.zeros((RT, DC), F32)
            for j in range(CW):
                dglu = dglu + w_ref[j:j + 1, :] * _tap(dext_ref, shd_ref, CW - 1 - j, r0)
            gext_ref[pl.ds(r0, RT), :] = dglu
            return carry

        lax.fori_loop(0, tm // RT, tile, 0, unroll=2)
        dglu = gext_ref[0:tm, :]
        cb = c_ref[...] + gb_ref[...]
        sg = _sigmoid(cb[:, DC:])
        dc = jnp.concatenate([dglu * sg, dglu * cb[:, :DC] * sg * (1.0 - sg)], axis=1)
        dc_ref[...] = dc.astype(BF16)
        a_gb[...] += fold(dc)

        @pl.when(i == steps - 1)
        def _():
            dgb_ref[...] = a_gb[...].sum(axis=0, keepdims=True)
            dw_ref[...] = a_w[...]
            dwb_ref[...] = a_wb[...].sum(axis=0, keepdims=True)
            dg_ref[...] = a_g[...].sum(axis=0, keepdims=True)
            db_ref[...] = a_b[...].sum(axis=0, keepdims=True)

    vec = lambda n: pl.BlockSpec((1, n), lambda i: (0, 0))
    nxt = lambda i: (jnp.minimum((i + 1) * hb, nhb - 1), 0)
    prv = lambda i: (jnp.maximum(i * hb - 1, 0), 0)
    return pl.pallas_call(
        body,
        name="conv_bwd",
        grid=(steps,),
        in_specs=[pl.BlockSpec((tm, DC), lambda i: (i, 0)), pl.BlockSpec((HALO, DC), nxt),
                  pl.BlockSpec((tm, DC), lambda i: (i, 0)), pl.BlockSpec((HALO, DC), nxt),
                  pl.BlockSpec((tm, 2 * DC), lambda i: (i, 0)), pl.BlockSpec((HALO, 2 * DC), prv),
                  vec(2 * DC), pl.BlockSpec((CW, DC), lambda i: (0, 0)), vec(DC), vec(DC)],
        out_specs=[pl.BlockSpec((tm, 2 * DC), lambda i: (i, 0)), vec(2 * DC),
                   pl.BlockSpec((CW * 8, DC), lambda i: (0, 0)), vec(DC), vec(DC), vec(DC)],
        out_shape=[jax.ShapeDtypeStruct((t, 2 * DC), BF16), jax.ShapeDtypeStruct((1, 2 * DC), F32),
                   jax.ShapeDtypeStruct((CW * 8, DC), F32), jax.ShapeDtypeStruct((1, DC), F32),
                   jax.ShapeDtypeStruct((1, DC), F32), jax.ShapeDtypeStruct((1, DC), F32)],
        scratch_shapes=[pltpu.VMEM((HALO + tm + 8, DC), F32), pltpu.VMEM((tm + HALO + 8, DC), F32),
                        pltpu.VMEM((7, HALO + tm, DC), F32), pltpu.VMEM((7, HALO + tm, DC), F32),
                        pltpu.VMEM((8, 2 * DC), F32), pltpu.VMEM((CW * 8, DC), F32),
                        pltpu.VMEM((8, DC), F32), pltpu.VMEM((8, DC), F32), pltpu.VMEM((8, DC), F32)],
        compiler_params=_params(("arbitrary",)),
    )(dcs, dcs, pre, pre, cin, cin, glu_b, dw_w, ln_g, ln_b)


def _place():
    x, y, c = lax.axis_index("x"), lax.axis_index("y"), lax.axis_index("c")
    return x, y, c


def _peers(x, y, c):
    out = []
    for k in range(1, NDEV):
        fx, fy, fc = (k >> 2) & 1, (k >> 1) & 1, k & 1
        px = 1 - x if fx else x
        py = 1 - y if fy else y
        pc = 1 - c if fc else c
        out.append((px, py, pc))
    return out


def _job_out_shapes(job):
    kind, arrays = job
    if kind == "gather":
        return [jax.ShapeDtypeStruct((a.shape[0], NDEV) + a.shape[1:], a.dtype) for a in arrays]
    return [jax.ShapeDtypeStruct((NDEV, a.shape[0]) + a.shape[2:], a.dtype) for a in arrays]


def _job_scratch(job):
    n = len(job[1])
    return [pltpu.SemaphoreType.DMA((n, NDEV - 1)), pltpu.SemaphoreType.DMA((n, NDEV - 1)),
            pltpu.SemaphoreType.DMA((n,))]


def _gather_parts(ins, outs, send_sems, recv_sems, local_sems):
    x, y, c = _place()
    me, sib = (x, y, c), (x, y, 1 - c)
    chips = [(1 - x, y), (x, 1 - y), (1 - x, 1 - y)]

    def copy(a, k, block, to, src=None):
        px, py, pc = block
        dst = outs[a].at[:, 4 * px + 2 * py + pc]
        return pltpu.make_async_remote_copy(
            src_ref=dst if src is None else src, dst_ref=dst,
            send_sem=send_sems.at[a, k], recv_sem=recv_sems.at[a, k], device_id=to, device_id_type=MESH)

    n = len(ins)
    local = [pltpu.make_async_copy(ins[a], outs[a].at[:, 4 * x + 2 * y + c], local_sems.at[a]) for a in range(n)]
    first = [[copy(a, 0, me, sib, src=ins[a])] + [copy(a, 1 + j, me, (*chip, c), src=ins[a])
                                                   for j, chip in enumerate(chips)] for a in range(n)]

    def start():
        for a in range(n):
            local[a].start()
            for cp in first[a]:
                cp.start()

    def finish():
        passed = []
        for j, chip in enumerate(chips):
            for a in range(n):
                copy(a, 1 + j, (*chip, c), me).wait_recv()
                cp = copy(a, 4 + j, (*chip, c), sib)
                cp.start()
                passed.append(cp)
        for a in range(n):
            copy(a, 0, sib, me).wait_recv()
            for j, chip in enumerate(chips):
                copy(a, 4 + j, (*chip, 1 - c), me).wait_recv()
        for a in range(n):
            for cp in first[a]:
                cp.wait_send()
            local[a].wait()
        for cp in passed:
            cp.wait_send()

    return start, finish


def _exchange_parts(ins, outs, send_sems, recv_sems, local_sems):
    x, y, c = _place()
    me = 4 * x + 2 * y + c
    n = len(ins)
    peers = _peers(x, y, c)
    local = [pltpu.make_async_copy(ins[a].at[:, me], outs[a].at[me], local_sems.at[a]) for a in range(n)]

    def copy(a, k):
        px, py, pc = peers[k]
        return pltpu.make_async_remote_copy(
            src_ref=ins[a].at[:, 4 * px + 2 * py + pc], dst_ref=outs[a].at[me],
            send_sem=send_sems.at[a, k], recv_sem=recv_sems.at[a, k], device_id=peers[k], device_id_type=MESH)

    def arrival(a, k):
        px, py, pc = peers[k]
        return pltpu.make_async_remote_copy(
            src_ref=ins[a].at[:, me], dst_ref=outs[a].at[4 * px + 2 * py + pc],
            send_sem=send_sems.at[a, k], recv_sem=recv_sems.at[a, k], device_id=peers[k], device_id_type=MESH)

    def start():
        for a in range(n):
            local[a].start()
            for k in range(NDEV - 1):
                copy(a, k).start()

    def finish():
        for a in range(n):
            for k in range(NDEV - 1):
                arrival(a, k).wait_recv()
        for a in range(n):
            for k in range(NDEV - 1):
                copy(a, k).wait_send()
            local[a].wait()

    return start, finish


def _call(body, *, name, grid, in_specs, args, out_specs, out_shape, scratch=(), sem=None, aliases=None, job=None):
    aliases = dict(aliases or {})
    if job is None:
        res = pl.pallas_call(
            body, name=name, grid=grid, in_specs=list(in_specs), out_specs=list(out_specs),
            out_shape=list(out_shape), scratch_shapes=list(scratch), input_output_aliases=aliases,
            compiler_params=_params(sem))(*args)
        return list(res), []
    kind, arrays = job
    n_in, n_out, n_scr, nj = len(args), len(out_shape), len(scratch), len(arrays)

    def wrapped(*refs):
        ins = refs[:n_in]
        jin = refs[n_in:n_in + nj]
        o0 = n_in + nj
        outs = refs[o0:o0 + n_out]
        jout = refs[o0 + n_out:o0 + n_out + nj]
        s0 = o0 + n_out + nj
        scr = refs[s0:s0 + n_scr]
        sems = refs[s0 + n_scr:]
        parts = _gather_parts if kind == "gather" else _exchange_parts
        start, finish = parts(jin, jout, *sems)
        if not grid:
            start()
            body(*ins, *outs, *scr)
            finish()
            return
        first = last = None
        for d, g in enumerate(grid):
            f, l = pl.program_id(d) == 0, pl.program_id(d) == g - 1
            first = f if first is None else jnp.logical_and(first, f)
            last = l if last is None else jnp.logical_and(last, l)
        pl.when(first)(start)
        body(*ins, *outs, *scr)
        pl.when(last)(finish)

    res = pl.pallas_call(
        wrapped, name=name, grid=grid, in_specs=list(in_specs) + [ANY] * nj,
        out_specs=list(out_specs) + [ANY] * nj, out_shape=list(out_shape) + _job_out_shapes(job),
        scratch_shapes=list(scratch) + _job_scratch(job), input_output_aliases=aliases,
        compiler_params=pltpu.CompilerParams(
            dimension_semantics=None if not grid else ("arbitrary",) * len(grid),
            vmem_limit_bytes=VMEM_LIMIT, has_side_effects=True))(*args, *arrays)
    return list(res[:n_out]), list(res[n_out:])


def _comm_only(name, job):
    return _call(lambda: None, name=name, grid=(), in_specs=[], args=[], out_specs=[], out_shape=[], job=job)[1]


def _sum_devices(name, parts, tr):
    _, r, c = parts.shape

    def body(p_ref, o_ref):
        acc = p_ref[0].astype(F32)
        for d in range(1, NDEV):
            acc = acc + p_ref[d].astype(F32)
        o_ref[...] = acc

    return pl.pallas_call(
        body,
        name=name,
        grid=(r // tr,),
        in_specs=[pl.BlockSpec((NDEV, tr, c), lambda i: (0, i, 0))],
        out_specs=pl.BlockSpec((tr, c), lambda i: (i, 0)),
        out_shape=jax.ShapeDtypeStruct((r, c), F32),
        compiler_params=_params(("parallel",)),
    )(parts)


def _adamw(name, w, g, m, v, tr=None):
    r, c = w.shape
    tr = r if tr is None else tr

    def body(w_ref, g_ref, m_ref, v_ref, d_ref, nm_ref, nv_ref):
        gv = g_ref[...]
        nm = ADAM_B1 * m_ref[...] + (1.0 - ADAM_B1) * gv
        nv = ADAM_B2 * v_ref[...] + (1.0 - ADAM_B2) * (gv * gv)
        m_hat = nm / (1.0 - ADAM_B1 ** ADAM_STEP)
        v_hat = nv / (1.0 - ADAM_B2 ** ADAM_STEP)
        d_ref[...] = -ADAM_LR * (m_hat / (jnp.sqrt(v_hat) + ADAM_EPS) + ADAM_WD * w_ref[...])
        nm_ref[...] = nm
        nv_ref[...] = nv

    spec = pl.BlockSpec((tr, c), lambda i: (i, 0))
    return pl.pallas_call(
        body,
        name=name,
        grid=(r // tr,),
        in_specs=[spec] * 4,
        out_specs=[spec] * 3,
        out_shape=[jax.ShapeDtypeStruct((r, c), F32)] * 3,
        compiler_params=_params(("parallel",)),
    )(w, g, m, v)


SMALL = ["ffn1_norm_pre", "ffn1_norm_post", "mix_norm_pre", "gate_bias", "rel_table", "conv_glu_bias",
         "conv_dw_b", "conv_ln_g", "conv_ln_b", "mix_norm_post", "ffn2_norm_pre", "ffn2_norm_post"]
SMALL_ROWS = 24
DW_ROWS = 32


LOSS_ROW = 20
RELP = 384
SMALL_LAYOUT = {}
_r = 0
for _name, _n in zip(SMALL, [D, D, D, 2 * D, None, 2 * DC, DC, DC, DC, D, D, D]):
    if _n is None:
        SMALL_LAYOUT[_name] = (_r, NH, RELP)
        _r += NH
    else:
        SMALL_LAYOUT[_name] = (_r, -(-_n // D), min(_n, D))
        _r += -(-_n // D)
assert _r == LOSS_ROW


def _small_pieces(name):
    r0, nr, nc = SMALL_LAYOUT[name]
    if name == "rel_table":
        return [(slice(r0, r0 + NH), slice(0, nc), slice(0, NH), slice(0, nc))]
    return [(slice(r0 + k, r0 + k + 1), slice(0, nc), slice(0, 1), slice(k * nc, (k + 1) * nc)) for k in range(nr)]


def _pack_small(vals, loss_row):
    def body(*refs):
        o = refs[-1]
        o[...] = jnp.zeros_like(o)
        for ref, name in zip(refs, SMALL):
            for prow, pcol, arow, acol in _small_pieces(name):
                o[prow, pcol] = ref[arow, acol]
        o[LOSS_ROW:LOSS_ROW + 1, :] = refs[len(SMALL)][...]

    vm = pl.BlockSpec(memory_space=pltpu.VMEM)
    return pl.pallas_call(body, name="pack_small", out_shape=jax.ShapeDtypeStruct((SMALL_ROWS, D), F32),
                          in_specs=[vm] * (len(SMALL) + 1), out_specs=vm,
                          compiler_params=_params())(*[vals[n] for n in SMALL], loss_row)


def _small_update(parts, w, m, v):
    ns = len(SMALL)

    def body(p_ref, *refs):
        ins, outs = refs[:3 * ns], refs[3 * ns:]

        def total(prow, pcol):
            g = p_ref[0, prow, pcol]
            for d in range(1, NDEV):
                g = g + p_ref[d, prow, pcol]
            return g

        for q, name in enumerate(SMALL):
            w_ref, m_ref, v_ref = ins[3 * q:3 * q + 3]
            o_g, o_d, o_m, o_v = outs[4 * q:4 * q + 4]
            for prow, pcol, arow, acol in _small_pieces(name):
                g = total(prow, pcol)
                nm = ADAM_B1 * m_ref[arow, acol] + (1.0 - ADAM_B1) * g
                nv = ADAM_B2 * v_ref[arow, acol] + (1.0 - ADAM_B2) * (g * g)
                m_hat = nm / (1.0 - ADAM_B1 ** ADAM_STEP)
                v_hat = nv / (1.0 - ADAM_B2 ** ADAM_STEP)
                o_g[arow, acol] = g
                o_d[arow, acol] = -ADAM_LR * (m_hat / (jnp.sqrt(v_hat) + ADAM_EPS) + ADAM_WD * w_ref[arow, acol])
                o_m[arow, acol] = nm
                o_v[arow, acol] = nv
        outs[4 * ns][...] = total(slice(LOSS_ROW, LOSS_ROW + 1), slice(None))

    args, out_shape = [], []
    for name in SMALL:
        args += [w[name], m[name], v[name]]
        out_shape += [jax.ShapeDtypeStruct(w[name].shape, F32)] * 4
    out_shape.append(jax.ShapeDtypeStruct((1, D), F32))
    vm = pl.BlockSpec(memory_space=pltpu.VMEM)
    res = pl.pallas_call(body, name="small_update", out_shape=out_shape, in_specs=[vm] * (1 + 3 * ns),
                         out_specs=[vm] * len(out_shape), compiler_params=_params())(parts, *args)
    return {name: res[4 * q:4 * q + 4] for q, name in enumerate(SMALL)}, res[-1]


def kernel(x, ffn1_norm_pre, ffn1_w_gate, ffn1_w_up, ffn1_w_down, ffn1_norm_post, mix_norm_pre, w_in, gate_bias, rel_table, w_attn_out, conv_glu_bias, conv_dw_w, conv_dw_b, conv_ln_g, conv_ln_b, conv_w_out, w_out, mix_norm_post, ffn2_norm_pre, ffn2_w_gate, ffn2_w_up, ffn2_w_down, ffn2_norm_post, loss_target, m_ffn1_norm_pre, m_ffn1_w_gate, m_ffn1_w_up, m_ffn1_w_down, m_ffn1_norm_post, m_mix_norm_pre, m_w_in, m_gate_bias, m_rel_table, m_w_attn_out, m_conv_glu_bias, m_conv_dw_w, m_conv_dw_b, m_conv_ln_g, m_conv_ln_b, m_conv_w_out, m_w_out, m_mix_norm_post, m_ffn2_norm_pre, m_ffn2_w_gate, m_ffn2_w_up, m_ffn2_w_down, m_ffn2_norm_post, v_ffn1_norm_pre, v_ffn1_w_gate, v_ffn1_w_up, v_ffn1_w_down, v_ffn1_norm_post, v_mix_norm_pre, v_w_in, v_gate_bias, v_rel_table, v_w_attn_out, v_conv_glu_bias, v_conv_dw_w, v_conv_dw_b, v_conv_ln_g, v_conv_ln_b, v_conv_w_out, v_w_out, v_mix_norm_post, v_ffn2_norm_pre, v_ffn2_w_gate, v_ffn2_w_up, v_ffn2_w_down, v_ffn2_norm_post):
    return _step(dict(locals()))


WEIGHTS = ["ffn1_norm_pre", "ffn1_w_gate", "ffn1_w_up", "ffn1_w_down", "ffn1_norm_post", "mix_norm_pre", "w_in",
           "gate_bias", "rel_table", "w_attn_out", "conv_glu_bias", "conv_dw_w", "conv_dw_b", "conv_ln_g",
           "conv_ln_b", "conv_w_out", "w_out", "mix_norm_post", "ffn2_norm_pre", "ffn2_w_gate", "ffn2_w_up",
           "ffn2_w_down", "ffn2_norm_post"]
FS = FF // NDEV
PS = (3 * DA + 2 * DC + 2 * D) // NDEV
OS = D // NDEV


def _local_step(xs, target, w, rel_table):
    t = xs.shape[0]
    vec = lambda n: w[n].reshape(1, -1)
    g_pre1, g_post1, g_mix, g_mixp = vec("ffn1_norm_pre"), vec("ffn1_norm_post"), vec("mix_norm_pre"), vec("mix_norm_post")
    g_pre2, g_post2 = vec("ffn2_norm_pre"), vec("ffn2_norm_post")
    gate_b, glu_b = vec("gate_bias"), vec("conv_glu_bias")
    dw_b, ln_g, ln_b = vec("conv_dw_b"), vec("conv_ln_g"), vec("conv_ln_b")

    tr = lambda a: jnp.transpose(a[0]).astype(BF16)
    sh_gu1 = jnp.stack([tr(w["ffn1_w_gate"]), tr(w["ffn1_w_up"])])
    sh_mid = [w["ffn1_w_down"].astype(BF16), tr(w["w_in"])[None], w["w_out"].astype(BF16),
              jnp.stack([tr(w["w_attn_out"]), tr(w["conv_w_out"])]),
              jnp.pad(w["conv_dw_w"][0, :, 0, :], ((0, DW_ROWS - CW), (0, 0)))[None]]
    sh_2 = jnp.stack([tr(w["ffn2_w_gate"]), tr(w["ffn2_w_up"]), w["ffn2_w_down"][0].astype(BF16)])

    (n1,), (w_gu1,) = _rowwise("pre1", lambda xv, g: ((_rms(xv)[0] * g),), [xs], [g_pre1], [(D, BF16)],
                               job=("gather", [sh_gu1]))
    w_gu1 = w_gu1.reshape(2, FF, D)
    (a1, b1, s1), (w_d1, wb, wc, wd, we) = _ffn_up("ffn1_up", n1, w_gu1, 0, job=("gather", sh_mid))
    w_d1, wb, wc, wd = w_d1.reshape(1, FF, D), wb.reshape(NDEV * PS, D), wc.reshape(D, D), wd.reshape(2, D, DA)
    dw_full = jnp.transpose(we[0], (1, 0, 2)).reshape(DW_ROWS, DC)[:CW]
    row = lambda i, j, q: (i, 0)
    top = lambda i, j, q: (0, 0)
    tmr = min(512, t)

    def post1(fv, xv, gp, gm):
        h = xv + 0.5 * (_rms(fv)[0] * gp)
        return fv, h, _rms(h)[0] * gm

    f1, h1, u = _mm_rows("ffn1_down", t, tmr, [(s1, (tmr, FF), row)], [(w_d1, (None, FF, D), lambda i, j, q: (0, 0, 0))],
                         [[(0, 0)]], NN, post1, [xs], [g_post1, g_mix], [(D, F32), (D, F32), (D, BF16)])

    qkv = _mm_simple("proj_q", u, wb, NT, BF16, 1024, 512, b_row0=0, b_rows=DA)
    kvp = _mm("proj_kv", (t // KPAD + 1, 2, 1), [(u, (KPAD, D), lambda i, j, q: (jnp.maximum(i - 1, 0), 0))],
              [(wb, (DA, D), lambda i, j, q: (j + 1, 0))], [[(0, 0)]], NT,
              lambda accs, extras: (jnp.where(pl.program_id(0) > 0, accs[0], 0.0),),
              [((t + KPAD, 2 * DA), BF16, (KPAD, DA), lambda i, j, q: (i, j))])[0]
    cin = _mm_simple("proj_conv", u, wb, NT, F32, 1024, 512, b_row0=3 * DA, b_rows=2 * DC)
    gg = _mm_simple("proj_gate", u, wb, NT, BF16, 1024, 512, b_row0=3 * DA + 2 * DC, b_rows=2 * D)

    bias = _relbias_fwd(jnp.pad(rel_table[0], ((0, 0), (0, 384 - NREL))))
    att, (w_2,) = _attn_fwd(qkv, kvp, bias, job=("gather", [sh_2]))
    w_2 = w_2.reshape(3, FF, D)
    cs, pre = _conv_fwd(cin, glu_b, dw_full, dw_b, ln_g, ln_b)
    ya = _mm_simple("attn_out", att, wd[0], NT, BF16, 1024, 1024)

    def merge(ybv, yav, gv, gb):
        gates = _sigmoid(gv + gb)
        return ybv, gates[:, :D] * yav + gates[:, D:] * ybv

    yb, merged = _mm_rows("conv_out", t, tmr, [(cs, (tmr, DC), row)], [(wd[1], (D, DC), top)], [[(0, 0)]], NT,
                          merge, [ya, gg], [gate_b], [(D, BF16), (D, BF16)])

    def postm(mv, hv, gp, g2):
        h = hv + _rms(mv)[0] * gp
        return mv, h, _rms(h)[0] * g2

    mm_, h2, n2 = _mm_rows("mix_out", t, tmr, [(merged, (tmr, D), row)], [(wc, (D, D), top)], [[(0, 0)]], NN,
                           postm, [h1], [g_mixp, g_pre2], [(D, F32), (D, F32), (D, BF16)])
    a2, b2, s2 = _ffn_up("ffn2_up", n2, w_2, 0)

    def post2(fv, hv, tv, gp):
        fh, r = _rms(fv)
        yv = hv + 0.5 * (fh * gp)
        err = yv - tv
        dy = err * (1.0 / D)
        df, dg = _rms_bwd(fh, r, gp, 0.5 * dy)
        return dy, df, (0.5 / D) * (err * err), dg

    dy, df2, loss_row, d_post2 = _mm_rows(
        "ffn2_down", t, tmr, [(s2, (tmr, FF), row)], [(w_2, (None, FF, D), lambda i, j, q: (2, 0, 0))], [[(0, 0)]],
        NN, post2, [h2, target], [g_post2], [(D, F32), (D, BF16)], [D, D])

    tmw = 1408
    nfi = FF // tmw
    tkf = min(2048, t)
    tkp = min(1024, t)

    def wgrad_down(name, s, df, shape, part, carry_buf, job=None):
        return _wgrad(name, (s, (tkf, tmw), lambda i, j, q: (q, i)), df,
                      (None, tmw, D), lambda i, j, q: (part, i, 0), nfi, carry_buf, shape, tk=tkf, job=job)

    def wgrad_gate_up(name, dab, nrm, shape, carry_buf, job=None):
        return _wgrad(name, (dab, (None, tkf, tmw), lambda i, j, q: (i // nfi, q, i % nfi)), nrm,
                      (None, tmw, D), lambda i, j, q: (i // nfi, i % nfi, 0), 2 * nfi, carry_buf, shape,
                      tk=tkf, job=job)

    g_2 = wgrad_down("ffn2_wgrad_d", s2, df2, (3, FF, D), 2, None)
    dab2 = _ffn_bwd_act("ffn2_bwd_act", df2, w_2, 2, a2, b2)
    g_2 = wgrad_gate_up("ffn2_wgrad_gu", dab2, n2, (3, FF, D), g_2)

    def bwd_pre2(dnv, hv, dyv, mv, g2, gp):
        hh, r = _rms(hv)
        dx, dg2 = _rms_bwd(hh, r, g2, dnv)
        dh = dyv + dx
        mh, rm = _rms(mv)
        dm, dgp = _rms_bwd(mh, rm, gp, dh)
        return dh, dm, dg2, dgp

    tmb = 256
    dh2, dm, d_pre2, d_mixp = _mm_rows(
        "ffn2_bwd_in", t, tmb,
        [(dab2, (None, tmb, FF), lambda i, j, q: (0, i, 0)), (dab2, (None, tmb, FF), lambda i, j, q: (1, i, 0))],
        [(w_2, (None, FF, D), lambda i, j, q: (0, 0, 0)), (w_2, (None, FF, D), lambda i, j, q: (1, 0, 0))],
        [[(0, 0), (1, 1)]], NN, bwd_pre2, [h2, dy, mm_], [g_pre2, g_mixp], [(D, F32), (D, BF16)], [D, D])

    def merge_bwd(dmv, yav, ybv, gv, gb):
        gates = _sigmoid(gv + gb)
        ga, gbb = gates[:, :D], gates[:, D:]
        dgg = jnp.concatenate([dmv * yav * ga * (1.0 - ga), dmv * ybv * gbb * (1.0 - gbb)], axis=1)
        return dmv * ga, dmv * gbb, dgg, dgg

    dya, dyb, dgg, d_gate_b = _mm_rows(
        "mix_out_bwd", t, tmr, [(dm, (tmr, D), row)], [(wc, (D, D), top)], [[(0, 0)]], NT, merge_bwd,
        [ya, yb, gg], [gate_b], [(D, BF16), (D, BF16), (2 * D, BF16)], [2 * D])
    datt = _mm_simple("attn_out_bwd", dya, wd[0], NN, BF16, 1024, 512)
    dcs = _mm_simple("conv_out_bwd", dyb, wd[1], NN, F32, 1024, 512)
    g_c = _wgrad("mix_out_wgrad", (merged, (tkp, D), lambda i, j, q: (q, 0)), dm,
                 (D, D), lambda i, j, q: (0, 0), 1, None, (D, D), tk=tkp)
    g_d = _wgrad("attn_out_wgrad", (dya, (tkp, D), lambda i, j, q: (q, 0)), att,
                 (None, D, DA), lambda i, j, q: (0, 0, 0), 1, None, (2, D, DA), tk=tkp)
    g_d = _wgrad("conv_out_wgrad", (dyb, (tkp, D), lambda i, j, q: (q, 0)), cs,
                 (None, D, DA), lambda i, j, q: (1, 0, 0), 1, g_d, (2, D, DA), tk=tkp)
    (dq, dkvp, dbias), (x_2, x_c, x_d) = _attn_bwd(
        qkv, kvp, bias, datt,
        job=("exchange", [g_2.reshape(3, NDEV, FS, D), g_c.reshape(1, NDEV, OS, D), g_d.reshape(2, NDEV, OS, DA)]))
    d_rel = _relbias_bwd(dbias)
    dcin, d_glu_b, d_dw8, d_dw_b, d_ln_g, d_ln_b = _conv_bwd(dcs, pre, cin, glu_b, dw_full, ln_g, ln_b)
    g_dw = jnp.pad(d_dw8, ((0, 8 * (DW_ROWS - CW)), (0, 0)))
    g_dw = g_dw.reshape(DW_ROWS * 8, NDEV, DC // NDEV).transpose(1, 0, 2)

    top = lambda i, j, q: (0, 0)
    g_b = jnp.concatenate([
        _wgrad("proj_wgrad_q", (dq, (tkp, DA), lambda i, j, q: (q, 0)), u, (DA, D), top, 1, None, (DA, D), tk=tkp),
        _wgrad("proj_wgrad_kv", (dkvp, (KPAD, 2 * DA), lambda i, j, q: (q + 1, 0)), u, (2 * DA, D), top, 1, None,
               (2 * DA, D), tk=KPAD),
        _wgrad("proj_wgrad_c", (dcin, (tkp, 2 * DC), lambda i, j, q: (q, 0)), u, (2 * DC, D), top, 1, None,
               (2 * DC, D), tk=tkp),
        _wgrad("proj_wgrad_g", (dgg, (tkp, 2 * D), lambda i, j, q: (q, 0)), u, (2 * D, D), top, 1, None,
               (2 * D, D), tk=tkp)], axis=0)

    tmu = 256
    a_ops = [(dq, (tmu, DA), row),
             (dkvp, (tmu, 2 * DA), lambda i, j, q: (i + KPAD // tmu, 0)),
             (dcin, (tmu, 2 * DC), row),
             (dgg, (tmu, 2 * D), row)]
    b_ops = [(wb[:DA], (DA, D), top), (wb[DA:3 * DA], (2 * DA, D), top),
             (wb[3 * DA:3 * DA + 2 * DC], (2 * DC, D), top), (wb[3 * DA + 2 * DC:], (2 * D, D), top)]

    def bwd_mix(duv, hv, dhv, fv, gm, gp):
        hh, r = _rms(hv)
        dx, dgm = _rms_bwd(hh, r, gm, duv)
        dh = dhv + dx
        fh, rf = _rms(fv)
        df, dgp = _rms_bwd(fh, rf, gp, 0.5 * dh)
        return dh, df, dgm, dgp

    (dh1, df1, d_mix, d_post1), (x_b, x_dw) = _mm_rows(
        "proj_bwd", t, tmu, a_ops, b_ops, [[(0, 0), (1, 1), (2, 2), (3, 3)]], NN, bwd_mix, [h1, dh2, f1],
        [g_mix, g_post1], [(D, F32), (D, BF16)], [D, D],
        job=("exchange", [g_b.reshape(1, NDEV, PS, D), g_dw[None]]))
    g_d1 = wgrad_down("ffn1_wgrad_d", s1, df1, (1, FF, D), 0, None)
    dab1, (x_d1,) = _ffn_bwd_act("ffn1_bwd_act", df1, w_d1, 0, a1, b1,
                                 job=("exchange", [g_d1.reshape(1, NDEV, FS, D)]))

    def wgrad_half(name, p, job=None):
        return _wgrad(name, (dab1, (None, tkf, tmw), lambda i, j, q: (p, q, i)), n1,
                      (None, tmw, D), lambda i, j, q: (0, i, 0), nfi, None, (1, FF, D), tk=tkf, job=job)

    g_g1 = wgrad_half("ffn1_wgrad_g", 0)
    g_u1, (x_g1,) = wgrad_half("ffn1_wgrad_u", 1, job=("exchange", [g_g1.reshape(1, NDEV, FS, D)]))
    dn1, (x_u1,) = _ffn_bwd_in("ffn1_bwd_in", dab1, w_gu1, 0, job=("exchange", [g_u1.reshape(1, NDEV, FS, D)]))

    def bwd_pre1(xv, dnv, dhv, g1):
        xh, r = _rms(xv)
        dx, dg1 = _rms_bwd(xh, r, g1, dnv)
        return dhv + dx, dg1

    dx, d_pre1 = _rowwise("bwd_pre1", bwd_pre1, [xs, dn1, dh1], [g_pre1], [(D, F32)], [D])

    small_g = {"ffn1_norm_pre": d_pre1, "ffn1_norm_post": d_post1, "mix_norm_pre": d_mix, "gate_bias": d_gate_b,
               "rel_table": d_rel, "conv_glu_bias": d_glu_b, "conv_dw_b": d_dw_b, "conv_ln_g": d_ln_g,
               "conv_ln_b": d_ln_b, "mix_norm_post": d_mixp, "ffn2_norm_pre": d_pre2, "ffn2_norm_post": d_post2}
    return loss_row, dx, (x_g1, x_u1, x_d1, x_2, x_b, x_c, x_d, x_dw), small_g


def _step(args):
    names = WEIGHTS
    w = {n: args[n] for n in names}
    fs, ps, os_ = FS, PS, OS
    conv_dw_w = args["conv_dw_w"]
    loss_row, dx, (x_g1, x_u1, x_d1, x_2, x_b, x_c, x_d, x_dw), small_g = _local_step(
        args["x"][0], args["loss_target"][0], w, args["rel_table"])

    g_small = _pack_small(small_g, loss_row)
    (x_s,) = _comm_only("gather_small_grads", ("gather", [g_small[None]]))

    s_g1 = _sum_devices("sum_ffn1_g", x_g1.reshape(NDEV, fs, D), fs)
    s_u1 = _sum_devices("sum_ffn1_u", x_u1.reshape(NDEV, fs, D), fs)
    s_d1 = _sum_devices("sum_ffn1_d", x_d1.reshape(NDEV, fs, D), fs)
    s_2 = _sum_devices("sum_ffn2", x_2.reshape(NDEV, 3 * fs, D), fs).reshape(3, fs, D)
    s_b = _sum_devices("sum_proj", x_b.reshape(NDEV, ps, D), ps)
    s_c = _sum_devices("sum_mix", x_c.reshape(NDEV, os_, D), os_)
    s_d = _sum_devices("sum_out", x_d.reshape(NDEV, 2 * os_, DA), 2 * os_).reshape(2, os_, DA)
    s_dw = _sum_devices("sum_dw", x_dw.reshape(NDEV, DW_ROWS * 8, DC // NDEV), DW_ROWS * 8)

    grads = {
        "ffn1_w_gate": jnp.transpose(s_g1)[None], "ffn1_w_up": jnp.transpose(s_u1)[None], "ffn1_w_down": s_d1[None],
        "ffn2_w_gate": jnp.transpose(s_2[0])[None], "ffn2_w_up": jnp.transpose(s_2[1])[None], "ffn2_w_down": s_2[2][None],
        "w_in": jnp.transpose(s_b)[None], "w_out": s_c[None],
        "w_attn_out": jnp.transpose(s_d[0])[None], "conv_w_out": jnp.transpose(s_d[1])[None],
    }
    deltas, new_m, new_v = {}, {}, {}

    def flat2(a, n):
        return jnp.pad(a[0], ((0, 0), (0, RELP - NREL))) if n == "rel_table" else a.reshape(1, -1)

    small, loss_terms = _small_update(
        x_s.reshape(NDEV, SMALL_ROWS, D), {n: flat2(w[n], n) for n in SMALL},
        {n: flat2(args["m_" + n], n) for n in SMALL}, {n: flat2(args["v_" + n], n) for n in SMALL})
    for n in SMALL:
        vals = [a[:, :NREL] if n == "rel_table" else a for a in small[n]]
        grads[n], deltas[n], new_m[n], new_v[n] = [a.reshape(w[n].shape) for a in vals]

    big = ["ffn1_w_gate", "ffn1_w_up", "ffn1_w_down", "w_in", "w_attn_out", "conv_w_out", "w_out",
           "ffn2_w_gate", "ffn2_w_up", "ffn2_w_down"]
    for n in big:
        shp = w[n].shape
        two = lambda a: a.reshape(shp[1], shp[2])
        rows = shp[1]
        tr_ = rows // 2 if rows % 16 == 0 else rows
        d_, m_, v_ = _adamw("adamw_" + n, two(w[n]), two(grads[n]), two(args["m_" + n]), two(args["v_" + n]), tr_)
        deltas[n], new_m[n], new_v[n] = d_.reshape(shp), m_.reshape(shp), v_.reshape(shp)

    g_dw_own = _fold8("fold_dw", s_dw)[:CW]
    grads["conv_dw_w"] = g_dw_own.reshape(1, CW, 1, DC // NDEV)
    flat = lambda a: a.reshape(CW, DC // NDEV)
    d_, m_, v_ = _adamw("adamw_dw", flat(conv_dw_w), g_dw_own, flat(args["m_conv_dw_w"]), flat(args["v_conv_dw_w"]))
    shp = conv_dw_w.shape
    deltas["conv_dw_w"], new_m["conv_dw_w"], new_v["conv_dw_w"] = d_.reshape(shp), m_.reshape(shp), v_.reshape(shp)

    loss = jnp.sum(loss_terms)
    return (loss, dx[None], *[grads[n] for n in names], *[deltas[n] for n in names],
            *[new_m[n] for n in names], *[new_v[n] for n in names])


def _fold8(name, a):
    r8, c = a.shape

    def body(a_ref, o_ref):
        o_ref[...] = a_ref[...].reshape(r8 // 8, 8, c).sum(axis=1)

    return pl.pallas_call(
        body,
        name=name,
        out_shape=jax.ShapeDtypeStruct((r8 // 8, c), F32),
        in_specs=[pl.BlockSpec(memory_space=pltpu.VMEM)],
        out_specs=pl.BlockSpec(memory_space=pltpu.VMEM),
        compiler_params=_params(),
    )(a)
```

```python
import functools

import jax
import jax.numpy as jnp
from jax import lax
from jax.experimental import pallas as pl
from jax.experimental.pallas import tpu as pltpu

F32 = jnp.float32
BF16 = jnp.bfloat16

D = 1024
FF = 2816
DA = 512
DC = 512
NH = 8
CHUNK = 64
LEFT = 8
CW = 31
NREL = 257
EPS = 1e-6
NDEV = 8

QB = 4 * CHUNK
KW = LEFT * CHUNK + QB
KPAD = LEFT * CHUNK
RELW = KW + QB
HALO = 32

TM = 512
VMEM_LIMIT = 56 * 1024 * 1024

ADAM_LR, ADAM_B1, ADAM_B2, ADAM_EPS, ADAM_WD, ADAM_STEP = 0.001, 0.9, 0.999, 1e-08, 0.01, 10

NT = (((1,), (1,)), ((), ()))
NN = (((1,), (0,)), ((), ()))
TN = (((0,), (0,)), ((), ()))

MESH = pl.DeviceIdType.MESH
ANY = pl.BlockSpec(memory_space=pl.ANY)


def _params(sem=None, vmem=VMEM_LIMIT):
    return pltpu.CompilerParams(dimension_semantics=sem, vmem_limit_bytes=vmem)


def _sigmoid(x):
    return 0.5 * jnp.tanh(0.5 * x) + 0.5


def _mm(name, grid, a_ops, b_ops, groups, dims, epi, outs, extras=(), carry=None, job=None, params=(),
        partials=()):
    nk = grid[2]
    na, nb, ne, no, ng = len(a_ops), len(b_ops), len(extras), len(outs), len(groups)
    npar, npart = len(params), len(partials)
    nc = 0 if carry is None else 1

    def body(*refs):
        a_refs = refs[:na]
        b_refs = refs[na:na + nb]
        e_refs = refs[na + nb:na + nb + ne]
        p_refs = refs[na + nb + ne:na + nb + ne + npar]
        o0 = na + nb + ne + npar + nc
        o_refs = refs[o0:o0 + no]
        s_refs = refs[o0 + no:o0 + no + npart]
        acc_refs = refs[o0 + no + npart:o0 + no + npart + (ng if nk > 1 else 0)]
        part_refs = refs[len(refs) - npart:] if npart else ()
        k = pl.program_id(2)
        prods = []
        for grp in groups:
            p = None
            for ai, bi in grp:
                t = lax.dot_general(a_refs[ai][...], b_refs[bi][...], dims, preferred_element_type=F32)
                p = t if p is None else p + t
            prods.append(p)

        def finish(vals):
            res = epi(vals, [e[...] for e in e_refs] + [p[...] for p in p_refs])
            for o, r in zip(o_refs, res[:no]):
                o[...] = r.astype(o.dtype)
            if npart:
                i, j = pl.program_id(0), pl.program_id(1)

                @pl.when(jnp.logical_and(i == 0, j == 0))
                def _():
                    for acc in part_refs:
                        acc[...] = jnp.zeros_like(acc)

                for acc, r in zip(part_refs, res[no:]):
                    acc[...] += r.reshape(r.shape[0] // 8, 8, r.shape[-1]).sum(axis=0)

                @pl.when(jnp.logical_and(i == grid[0] - 1, j == grid[1] - 1))
                def _():
                    for s, acc in zip(s_refs, part_refs):
                        s[...] = acc[...].sum(axis=0, keepdims=True)

        if nk == 1:
            finish(prods)
        else:
            @pl.when(k == 0)
            def _():
                for acc, p in zip(acc_refs, prods):
                    acc[...] = p

            @pl.when(k > 0)
            def _():
                for acc, p in zip(acc_refs, prods):
                    acc[...] += p

            @pl.when(k == nk - 1)
            def _():
                finish([acc[...] for acc in acc_refs])

    in_specs = [pl.BlockSpec(blk, im) for _, blk, im in list(a_ops) + list(b_ops) + list(extras)]
    in_specs += [pl.BlockSpec(p.shape, lambda i, j, q: (0, 0)) for p in params]
    args = [arr for arr, _, _ in list(a_ops) + list(b_ops) + list(extras)] + list(params)
    aliases = {}
    if carry is not None:
        in_specs.append(ANY)
        args.append(carry[0])
        aliases = {len(args) - 1: carry[1]}
    scratch = []
    if nk > 1:
        for _ in range(ng):
            blk = tuple(b for b in outs[0][2] if b is not None)
            scratch.append(pltpu.VMEM(blk, F32))
    scratch += [pltpu.VMEM((8, c), F32) for c in partials]
    res, jres = _call(
        body, name=name, grid=grid, in_specs=in_specs, args=args,
        out_specs=[pl.BlockSpec(blk, im) for _, _, blk, im in outs]
        + [pl.BlockSpec((1, c), lambda i, j, q: (0, 0)) for c in partials],
        out_shape=[jax.ShapeDtypeStruct(shp, dt) for shp, dt, _, _ in outs]
        + [jax.ShapeDtypeStruct((1, c), F32) for c in partials],
        scratch=scratch, sem=("arbitrary",) * 3 if partials else ("parallel", "parallel", "arbitrary"),
        aliases=aliases, job=job)
    return res if job is None else (res, jres)


def _first(accs, extras):
    return (accs[0],)


def _mm_rows(name, t, tm, a_ops, b_ops, groups, dims, fn, extras, params, outs, partials=(), nk=1, job=None):
    ne = len(extras)

    def epi(accs, rest):
        return fn(accs[0], *[r.astype(F32) for r in rest[:ne]], *rest[ne:])

    e_ops = [(arr, (tm, arr.shape[1]), lambda i, j, q: (i, 0)) for arr in extras]
    o_ops = [((t, c), dt, (tm, c), lambda i, j, q: (i, 0)) for c, dt in outs]
    return _mm(name, (t // tm, 1, nk), a_ops, b_ops, groups, dims, epi, o_ops, e_ops, job=job, params=params,
               partials=partials)


def _mm_simple(name, a, b, dims, out_dtype, tm, tn, b_row0=0, b_rows=None):
    m, kk = a.shape
    tm = min(tm, m)
    if dims is NT:
        n = b.shape[0] if b_rows is None else b_rows
        assert b.shape[1] == kk and b_row0 % tn == 0
        b_op = (b, (tn, kk), lambda i, j, q: (j + b_row0 // tn, 0))
    else:
        assert b.shape[0] == kk
        n = b.shape[1]
        b_op = (b, (kk, tn), lambda i, j, q: (0, j))
    a_op = (a, (tm, kk), lambda i, j, q: (i, 0))
    out = ((m, n), out_dtype, (tm, tn), lambda i, j, q: (i, j))
    return _mm(name, (m // tm, n // tn, 1), [a_op], [b_op], [[(0, 0)]], dims, _first, [out])[0]


def _rowwise(name, fn, tiled, params, outs, partials=(), tm=TM, job=None):
    t = tiled[0].shape[0]
    steps = t // tm
    nt, npar, no, npart = len(tiled), len(params), len(outs), len(partials)

    def body(*refs):
        t_refs = refs[:nt]
        p_refs = refs[nt:nt + npar]
        o_refs = refs[nt + npar:nt + npar + no]
        s_refs = refs[nt + npar + no:nt + npar + no + npart]
        acc_refs = refs[nt + npar + no + npart:]
        i = pl.program_id(0)
        res = fn(*[r[...].astype(F32) for r in t_refs], *[r[...] for r in p_refs])
        for o, r in zip(o_refs, res[:no]):
            o[...] = r.astype(o.dtype)

        @pl.when(i == 0)
        def _():
            for acc in acc_refs:
                acc[...] = jnp.zeros_like(acc)

        for acc, r in zip(acc_refs, res[no:]):
            acc[...] += r.reshape(tm // 8, 8, r.shape[-1]).sum(axis=0)

        @pl.when(i == steps - 1)
        def _():
            for s, acc in zip(s_refs, acc_refs):
                s[...] = acc[...].sum(axis=0, keepdims=True)

    in_specs = [pl.BlockSpec((tm, a.shape[1]), lambda i: (i, 0)) for a in tiled]
    in_specs += [pl.BlockSpec(p.shape, lambda i: (0, 0)) for p in params]
    out_specs = [pl.BlockSpec((tm, c), lambda i: (i, 0)) for c, _ in outs]
    out_specs += [pl.BlockSpec((1, c), lambda i: (0, 0)) for c in partials]
    out_shape = [jax.ShapeDtypeStruct((t, c), dt) for c, dt in outs]
    out_shape += [jax.ShapeDtypeStruct((1, c), F32) for c in partials]
    res, jres = _call(body, name=name, grid=(steps,), in_specs=in_specs, args=[*tiled, *params], out_specs=out_specs,
                      out_shape=out_shape, scratch=[pltpu.VMEM((8, c), F32) for c in partials], sem=("arbitrary",),
                      job=job)
    return res if job is None else (res, jres)


def _rms(x):
    r = lax.rsqrt(jnp.mean(x * x, axis=-1, keepdims=True) + EPS)
    return x * r, r


def _rms_bwd(xhat, r, g, dy):
    dxh = dy * g
    dx = r * (dxh - xhat * jnp.mean(dxh * xhat, axis=-1, keepdims=True))
    return dx, dy * xhat


def _ffn_up(name, n, wa, part, tm=512, tf=1408, job=None):
    t = n.shape[0]

    def epi(accs, extras):
        a, b = accs
        sg = _sigmoid(a)
        silu = a * sg
        return silu, b * (sg + silu * (1.0 - sg)), silu * b

    a_op = (n, (tm, D), lambda f, i, q: (i, 0))
    b_ops = [(wa, (None, tf, D), lambda f, i, q: (part, f, 0)),
             (wa, (None, tf, D), lambda f, i, q: (part + 1, f, 0))]
    outs = [((t, FF), BF16, (tm, tf), lambda f, i, q: (i, f))] * 3
    return _mm(name, (FF // tf, t // tm, 1), [a_op], b_ops, [[(0, 0)], [(0, 1)]], NT, epi, outs, job=job)


def _ffn_bwd_act(name, df, wa, part, a, b, tm=512, tf=1408, job=None):
    t = df.shape[0]

    def epi(accs, extras):
        ds = accs[0]
        return (jnp.stack([ds * extras[1].astype(F32), ds * extras[0].astype(F32)]),)

    a_op = (df, (tm, D), lambda f, i, q: (i, 0))
    b_op = (wa, (None, tf, D), lambda f, i, q: (part, f, 0))
    extras = [(a, (tm, tf), lambda f, i, q: (i, f)), (b, (tm, tf), lambda f, i, q: (i, f))]
    out = ((2, t, FF), BF16, (2, tm, tf), lambda f, i, q: (0, i, f))
    res = _mm(name, (FF // tf, t // tm, 1), [a_op], [b_op], [[(0, 0)]], NT, epi, [out], extras, job=job)
    return res[0] if job is None else (res[0][0], res[1])


def _ffn_bwd_in(name, dab, wa, part, tm=1024, job=None):
    t = dab.shape[1]
    tm = min(tm, t)
    a_op = (dab, (None, tm, FF), lambda i, j, q: (q, i, 0))
    b_op = (wa, (None, FF, D), lambda i, j, q: (part + q, 0, 0))
    out = ((t, D), F32, (tm, D), lambda i, j, q: (i, 0))
    res = _mm(name, (t // tm, 1, 2), [a_op], [b_op], [[(0, 0)]], NN, _first, [out], job=job)
    return res[0] if job is None else (res[0][0], res[1])


def _wgrad(name, dy_op, x, out_block, out_map, gi, carry_buf, out_shape, tk=512, job=None):
    t, c = x.shape
    b_op = (x, (tk, c), lambda i, j, q: (q, 0))
    out = (out_shape, BF16, out_block, out_map)
    carry = None if carry_buf is None else (carry_buf, 0)
    res = _mm(name, (gi, 1, t // tk), [dy_op], [b_op], [[(0, 0)]], TN, _first, [out], carry=carry, job=job)
    return res[0] if job is None else (res[0][0], res[1])


def _rel_onehot():
    j = lax.broadcasted_iota(jnp.int32, (384, RELW), 0)
    xx = lax.broadcasted_iota(jnp.int32, (384, RELW), 1)
    idx = jnp.clip(KPAD + QB - xx, -128, 128) + 128
    return (j == idx).astype(F32)


def _relbias_fwd(table):
    def body(t_ref, o_ref):
        rev = jnp.dot(t_ref[...], _rel_onehot(), precision=lax.Precision.HIGHEST, preferred_element_type=F32)
        for r in range(QB):
            row = pltpu.roll(rev, (RELW - (QB - r)) % RELW, 1)[:, :KW]
            rr = lax.broadcasted_iota(jnp.int32, (NH, KW), 1) >> 6
            ok = (rr >= (r // CHUNK)) & (rr <= (r // CHUNK) + LEFT)
            row = jnp.where(ok, row, -1e30)
            for h in range(NH):
                o_ref[h * QB + r:h * QB + r + 1, :] = row[h:h + 1, :]

    return pl.pallas_call(
        body,
        name="relbias_fwd",
        out_shape=jax.ShapeDtypeStruct((NH * QB, KW), F32),
        in_specs=[pl.BlockSpec(memory_space=pltpu.VMEM)],
        out_specs=pl.BlockSpec(memory_space=pltpu.VMEM),
        compiler_params=_params(),
    )(table)


def _relbias_bwd(dbias):
    def body(d_ref, o_ref, acc_ref):
        for h in range(NH):
            acc = jnp.zeros((1, RELW), F32)
            for r in range(QB):
                row = d_ref[h * QB + r:h * QB + r + 1, :]
                wide = jnp.concatenate([row, jnp.zeros((1, RELW - KW), F32)], axis=1)
                acc = acc + pltpu.roll(wide, QB - r, 1)
            acc_ref[h:h + 1, :] = acc
        o_ref[...] = lax.dot_general(acc_ref[...], _rel_onehot(), NT, precision=lax.Precision.HIGHEST,
                                     preferred_element_type=F32)

    return pl.pallas_call(
        body,
        name="relbias_bwd",
        out_shape=jax.ShapeDtypeStruct((NH, 384), F32),
        in_specs=[pl.BlockSpec(memory_space=pltpu.VMEM)],
        out_specs=pl.BlockSpec(memory_space=pltpu.VMEM),
        scratch_shapes=[pltpu.VMEM((NH, RELW), F32)],
        compiler_params=_params(),
    )(dbias)


HQ = QB // 2


def _stack_heads(x_pair):
    first = lax.broadcasted_iota(jnp.int32, (1, 128), 1) < 64
    zero = jnp.zeros_like(x_pair)
    a, b = jnp.where(first, x_pair, zero), jnp.where(first, zero, x_pair)
    return jnp.concatenate([a[:HQ], b[:HQ], a[HQ:], b[HQ:]], axis=0), first


def _unstack_heads(o, first):
    return jnp.concatenate([jnp.where(first, o[0:HQ], o[HQ:2 * HQ]),
                            jnp.where(first, o[2 * HQ:3 * HQ], o[3 * HQ:4 * HQ])], axis=0)


def _half_cols(hf):
    return slice(hf * HQ, hf * HQ + KW - HQ)


def _half_bias(b_ref, pair, hf):
    rows = lambda h: slice(h * QB + hf * HQ, h * QB + (hf + 1) * HQ)
    return jnp.concatenate([b_ref[rows(2 * pair), _half_cols(hf)], b_ref[rows(2 * pair + 1), _half_cols(hf)]], axis=0)


def _half_probs(s_full, b_ref, pair, hf, key_ok):
    s = s_full[2 * hf * HQ:2 * (hf + 1) * HQ, _half_cols(hf)] + _half_bias(b_ref, pair, hf)
    if key_ok is not None:
        s = jnp.where(key_ok[:, _half_cols(hf)], s, -1e30)
    e = jnp.exp(s - jnp.max(s, axis=-1, keepdims=True))
    return e * (1.0 / jnp.sum(e, axis=-1, keepdims=True))


def _widen(top, bottom):
    z = jnp.zeros((2 * HQ, HQ), top.dtype)
    return jnp.concatenate([jnp.concatenate([top, z], axis=1), jnp.concatenate([z, bottom], axis=1)], axis=0)


def _attn_fwd(qkv, kvp, bias, job=None):
    t = qkv.shape[0]

    def body(q_ref, kv_ref, b_ref, o_ref):
        i = pl.program_id(0)

        def run(masked):
            start = pl.multiple_of(i * QB, QB)
            col = lax.broadcasted_iota(jnp.int32, (1, KW), 1)
            key_ok = (col >= KPAD - i * QB) if masked else None
            for pair in range(4):
                lo = pair * 128
                kw = kv_ref[pl.ds(start, KW), lo:lo + 128]
                vw = kv_ref[pl.ds(start, KW), DA + lo:DA + lo + 128]
                qs, first = _stack_heads(q_ref[:, lo:lo + 128])
                s = lax.dot_general(qs * 0.125, kw, NT, preferred_element_type=F32)
                p = _widen(*[_half_probs(s, b_ref, pair, hf, key_ok).astype(BF16) for hf in range(2)])
                o = jnp.dot(p, vw, preferred_element_type=F32)
                o_ref[:, lo:lo + 128] = _unstack_heads(o, first).astype(BF16)

        pl.when(i < KPAD // QB)(lambda: run(True))
        pl.when(i >= KPAD // QB)(lambda: run(False))

    res, jres = _call(
        body, name="attn_fwd", grid=(t // QB,),
        in_specs=[pl.BlockSpec((QB, DA), lambda i: (i, 0)),
                  pl.BlockSpec(memory_space=pltpu.VMEM),
                  pl.BlockSpec(memory_space=pltpu.VMEM)],
        args=[qkv, kvp, bias],
        out_specs=[pl.BlockSpec((QB, DA), lambda i: (i, 0))],
        out_shape=[jax.ShapeDtypeStruct((t, DA), BF16)],
        sem=("arbitrary",), job=job)
    return res[0], jres


def _attn_bwd(qkv, kvp, bias, datt, job=None):
    t = qkv.shape[0]
    nb = t // QB
    flush = (KW - QB) // QB

    def body(q_ref, kv_ref, b_ref, do_ref, dq_ref, dkv_ref, db_out, acc_ref, db_ref):
        i = pl.program_id(0)

        @pl.when(i == 0)
        def _():
            acc_ref[...] = jnp.zeros_like(acc_ref)
            db_ref[...] = jnp.zeros_like(db_ref)

        def run(masked):
            start = pl.multiple_of(i * QB, QB)
            col = lax.broadcasted_iota(jnp.int32, (1, KW), 1)
            key_ok = (col >= KPAD - i * QB) if masked else None
            for pair in range(4):
                lo = pair * 128
                kw = kv_ref[pl.ds(start, KW), lo:lo + 128]
                vw = kv_ref[pl.ds(start, KW), DA + lo:DA + lo + 128]
                qs, first = _stack_heads(q_ref[:, lo:lo + 128])
                qs = qs * 0.125
                dos, _ = _stack_heads(do_ref[:, lo:lo + 128])
                s = lax.dot_general(qs, kw, NT, preferred_element_type=F32)
                dp = lax.dot_general(dos, vw, NT, preferred_element_type=F32)
                ps, dss = [], []
                for hf in range(2):
                    p = _half_probs(s, b_ref, pair, hf, key_ok)
                    dph = dp[2 * hf * HQ:2 * (hf + 1) * HQ, _half_cols(hf)]
                    ds = p * (dph - jnp.sum(p * dph, axis=-1, keepdims=True))
                    for k, h in enumerate((2 * pair, 2 * pair + 1)):
                        db_ref[h * QB + hf * HQ:h * QB + (hf + 1) * HQ, _half_cols(hf)] += ds[k * HQ:(k + 1) * HQ]
                    ps.append(p.astype(BF16))
                    dss.append(ds.astype(BF16))
                pb, dsb = _widen(*ps), _widen(*dss)
                dq = jnp.dot(dsb, kw, preferred_element_type=F32)
                dq_ref[:, lo:lo + 128] = (_unstack_heads(dq, first) * 0.125).astype(BF16)
                acc_ref[:, lo:lo + 128] += lax.dot_general(dsb, qs, TN, preferred_element_type=F32)
                acc_ref[:, DA + lo:DA + lo + 128] += lax.dot_general(pb, dos, TN, preferred_element_type=F32)

        pl.when(i < KPAD // QB)(lambda: run(True))
        pl.when(jnp.logical_and(i >= KPAD // QB, i < nb))(lambda: run(False))

        dkv_ref[...] = acc_ref[0:QB, :].astype(BF16)
        rest = acc_ref[QB:KW, :]
        acc_ref[0:KW - QB, :] = rest
        acc_ref[KW - QB:KW, :] = jnp.zeros((QB, 2 * DA), F32)

        @pl.when(i == nb + flush - 1)
        def _():
            pltpu.sync_copy(db_ref, db_out)

    last = nb - 1
    res, jres = _call(
        body, name="attn_bwd", grid=(nb + flush,),
        in_specs=[pl.BlockSpec((QB, DA), lambda i: (jnp.minimum(i, last), 0)),
                  pl.BlockSpec(memory_space=pltpu.VMEM),
                  pl.BlockSpec(memory_space=pltpu.VMEM),
                  pl.BlockSpec((QB, DA), lambda i: (jnp.minimum(i, last), 0))],
        args=[qkv, kvp, bias, datt],
        out_specs=[pl.BlockSpec((QB, DA), lambda i: (jnp.minimum(i, last), 0)),
                   pl.BlockSpec((QB, 2 * DA), lambda i: (i, 0)),
                   ANY],
        out_shape=[jax.ShapeDtypeStruct((t, DA), BF16),
                   jax.ShapeDtypeStruct((t + KPAD, 2 * DA), BF16),
                   jax.ShapeDtypeStruct((NH * QB, KW), F32)],
        scratch=[pltpu.VMEM((KW, 2 * DA), F32), pltpu.VMEM((NH * QB, KW), F32)], sem=("arbitrary",), job=job)
    return res, jres


def _glu(c, gb):
    cb = c + gb
    return cb[:, :DC] * _sigmoid(cb[:, DC:])


def _ln_swish(pre, g, b):
    mu = jnp.mean(pre, axis=-1, keepdims=True)
    xc = pre - mu
    r = lax.rsqrt(jnp.mean(xc * xc, axis=-1, keepdims=True) + EPS)
    xhat = xc * r
    y = xhat * g + b
    return xhat, r, y


RT = 32


def _shifted_copies(src_ref, sh_ref, rows):
    for b in range(1, 8):
        sh_ref[b - 1, :, :] = src_ref[pl.ds(b, rows), :]


def _tap(src_ref, sh_ref, off, r0, rows=RT):
    a, b = divmod(off, 8)
    ref = src_ref if b == 0 else sh_ref.at[b - 1]
    if isinstance(r0, int):
        return ref[r0 + 8 * a:r0 + 8 * a + rows, :]
    return ref[pl.ds(pl.multiple_of(r0 + 8 * a, 8), rows), :]


def _conv_fwd(cin, glu_b, dw_w, dw_b, ln_g, ln_b, tm=TM):
    t = cin.shape[0]
    hb = tm // HALO

    def body(c_ref, h_ref, gb_ref, w_ref, wb_ref, g_ref, b_ref, cs_ref, pre_ref, ext_ref, sh_ref):
        i = pl.program_id(0)
        halo = _glu(h_ref[...], gb_ref[...])
        ext_ref[0:HALO, :] = jnp.where(i > 0, halo, jnp.zeros_like(halo))
        ext_ref[HALO:HALO + tm, :] = _glu(c_ref[...], gb_ref[...])
        ext_ref[HALO + tm:HALO + tm + 8, :] = jnp.zeros((8, DC), F32)
        _shifted_copies(ext_ref, sh_ref, HALO + tm)

        def tile(rt, carry):
            r0 = pl.multiple_of(rt * RT, RT)
            acc = jnp.zeros((RT, DC), F32) + wb_ref[...]
            for j in range(CW):
                acc = acc + w_ref[j:j + 1, :] * _tap(ext_ref, sh_ref, HALO - (CW - 1) + j, r0)
            pre_ref[pl.ds(r0, RT), :] = acc
            return carry

        lax.fori_loop(0, tm // RT, tile, 0, unroll=2)
        _, _, y = _ln_swish(pre_ref[...], g_ref[...], b_ref[...])
        cs_ref[...] = (y * _sigmoid(y)).astype(BF16)

    vec = lambda n: pl.BlockSpec((1, n), lambda i: (0, 0))
    return pl.pallas_call(
        body,
        name="conv_fwd",
        grid=(t // tm,),
        in_specs=[pl.BlockSpec((tm, 2 * DC), lambda i: (i, 0)),
                  pl.BlockSpec((HALO, 2 * DC), lambda i: (jnp.maximum(i * hb - 1, 0), 0)),
                  vec(2 * DC), pl.BlockSpec((CW, DC), lambda i: (0, 0)), vec(DC), vec(DC), vec(DC)],
        out_specs=[pl.BlockSpec((tm, DC), lambda i: (i, 0)), pl.BlockSpec((tm, DC), lambda i: (i, 0))],
        out_shape=[jax.ShapeDtypeStruct((t, DC), BF16), jax.ShapeDtypeStruct((t, DC), F32)],
        scratch_shapes=[pltpu.VMEM((HALO + tm + 8, DC), F32), pltpu.VMEM((7, HALO + tm, DC), F32)],
        compiler_params=_params(("arbitrary",)),
    )(cin, cin, glu_b, dw_w, dw_b, ln_g, ln_b)


def _conv_bwd(dcs, pre, cin, glu_b, dw_w, ln_g, ln_b, tm=TM):
    t = cin.shape[0]
    hb = tm // HALO
    steps = t // tm
    nhb = t // HALO

    def dpre_of(dcs_v, pre_v, g, b):
        xhat, r, y = _ln_swish(pre_v, g, b)
        sg = _sigmoid(y)
        dy = dcs_v * (sg * (1.0 + y * (1.0 - sg)))
        dxh = dy * g
        dpre = r * (dxh - jnp.mean(dxh, axis=-1, keepdims=True)
                    - xhat * jnp.mean(dxh * xhat, axis=-1, keepdims=True))
        return dpre, dy * xhat, dy

    def body(dcs_ref, dcsn_ref, pre_ref, pren_ref, c_ref, ch_ref, gb_ref, w_ref, g_ref, b_ref,
             dc_ref, dgb_ref, dw_ref, dwb_ref, dg_ref, db_ref,
             gext_ref, dext_ref, shg_ref, shd_ref, a_gb, a_w, a_wb, a_g, a_b):
        i = pl.program_id(0)

        @pl.when(i == 0)
        def _():
            for a in (a_gb, a_w, a_wb, a_g, a_b):
                a[...] = jnp.zeros_like(a)

        fold = lambda v: v.reshape(v.shape[0] // 8, 8, v.shape[-1]).sum(axis=0)
        g, b = g_ref[...], b_ref[...]
        dpre, dg_t, db_t = dpre_of(dcs_ref[...], pre_ref[...], g, b)
        dpre_n, _, _ = dpre_of(dcsn_ref[...], pren_ref[...], g, b)
        dext_ref[0:tm, :] = dpre
        dext_ref[tm:tm + HALO, :] = jnp.where(i < steps - 1, dpre_n, jnp.zeros_like(dpre_n))
        dext_ref[tm + HALO:tm + HALO + 8, :] = jnp.zeros((8, DC), F32)
        a_wb[...] += fold(dpre)
        a_g[...] += fold(dg_t)
        a_b[...] += fold(db_t)
        halo = _glu(ch_ref[...], gb_ref[...])
        gext_ref[0:HALO, :] = jnp.where(i > 0, halo, jnp.zeros_like(halo))
        gext_ref[HALO:HALO + tm, :] = _glu(c_ref[...], gb_ref[...])
        gext_ref[HALO + tm:HALO + tm + 8, :] = jnp.zeros((8, DC), F32)
        _shifted_copies(gext_ref, shg_ref, HALO + tm)
        _shifted_copies(dext_ref, shd_ref, HALO + tm)

        for j in range(CW):
            a_w[8 * j:8 * j + 8, :] += fold(dext_ref[0:tm, :] * _tap(gext_ref, shg_ref, HALO - (CW - 1) + j, 0, tm))

        def tile(rt, carry):
            r0 = pl.multiple_of(rt * RT, RT)
            dglu = jnp.zeros((RT, DC), F32)
            for j in range(CW):
                dglu = dglu + w_ref[j:j + 1, :] * _tap(dext_ref, shd_ref, CW - 1 - j, r0)
            gext_ref[pl.ds(r0, RT), :] = dglu
            return carry

        lax.fori_loop(0, tm // RT, tile, 0, unroll=2)
        dglu = gext_ref[0:tm, :]
        cb = c_ref[...] + gb_ref[...]
        sg = _sigmoid(cb[:, DC:])
        dc = jnp.concatenate([dglu * sg, dglu * cb[:, :DC] * sg * (1.0 - sg)], axis=1)
        dc_ref[...] = dc.astype(BF16)
        a_gb[...] += fold(dc)

        @pl.when(i == steps - 1)
        def _():
            dgb_ref[...] = a_gb[...].sum(axis=0, keepdims=True)
            dw_ref[...] = a_w[...]
            dwb_ref[...] = a_wb[...].sum(axis=0, keepdims=True)
            dg_ref[...] = a_g[...].sum(axis=0, keepdims=True)
            db_ref[...] = a_b[...].sum(axis=0, keepdims=True)

    vec = lambda n: pl.BlockSpec((1, n), lambda i: (0, 0))
    nxt = lambda i: (jnp.minimum((i + 1) * hb, nhb - 1), 0)
    prv = lambda i: (jnp.maximum(i * hb - 1, 0), 0)
    return pl.pallas_call(
        body,
        name="conv_bwd",
        grid=(steps,),
        in_specs=[pl.BlockSpec((tm, DC), lambda i: (i, 0)), pl.BlockSpec((HALO, DC), nxt),
                  pl.BlockSpec((tm, DC), lambda i: (i, 0)), pl.BlockSpec((HALO, DC), nxt),
                  pl.BlockSpec((tm, 2 * DC), lambda i: (i, 0)), pl.BlockSpec((HALO, 2 * DC), prv),
                  vec(2 * DC), pl.BlockSpec((CW, DC), lambda i: (0, 0)), vec(DC), vec(DC)],
        out_specs=[pl.BlockSpec((tm, 2 * DC), lambda i: (i, 0)), vec(2 * DC),
                   pl.BlockSpec((CW * 8, DC), lambda i: (0, 0)), vec(DC), vec(DC), vec(DC)],
        out_shape=[jax.ShapeDtypeStruct((t, 2 * DC), BF16), jax.ShapeDtypeStruct((1, 2 * DC), F32),
                   jax.ShapeDtypeStruct((CW * 8, DC), F32), jax.ShapeDtypeStruct((1, DC), F32),
                   jax.ShapeDtypeStruct((1, DC), F32), jax.ShapeDtypeStruct((1, DC), F32)],
        scratch_shapes=[pltpu.VMEM((HALO + tm + 8, DC), F32), pltpu.VMEM((tm + HALO + 8, DC), F32),
                        pltpu.VMEM((7, HALO + tm, DC), F32), pltpu.VMEM((7, HALO + tm, DC), F32),
                        pltpu.VMEM((8, 2 * DC), F32), pltpu.VMEM((CW * 8, DC), F32),
                        pltpu.VMEM((8, DC), F32), pltpu.VMEM((8, DC), F32), pltpu.VMEM((8, DC), F32)],
        compiler_params=_params(("arbitrary",)),
    )(dcs, dcs, pre, pre, cin, cin, glu_b, dw_w, ln_g, ln_b)


def _place():
    x, y, c = lax.axis_index("x"), lax.axis_index("y"), lax.axis_index("c")
    return x, y, c


def _peers(x, y, c):
    out = []
    for k in range(1, NDEV):
        fx, fy, fc = (k >> 2) & 1, (k >> 1) & 1, k & 1
        px = 1 - x if fx else x
        py = 1 - y if fy else y
        pc = 1 - c if fc else c
        out.append((px, py, pc))
    return out


def _job_out_shapes(job):
    kind, arrays = job
    if kind == "gather":
        return [jax.ShapeDtypeStruct((a.shape[0], NDEV) + a.shape[1:], a.dtype) for a in arrays]
    return [jax.ShapeDtypeStruct((NDEV, a.shape[0]) + a.shape[2:], a.dtype) for a in arrays]


def _job_scratch(job):
    n = len(job[1])
    return [pltpu.SemaphoreType.DMA((n, NDEV - 1)), pltpu.SemaphoreType.DMA((n, NDEV - 1)),
            pltpu.SemaphoreType.DMA((n,))]


def _gather_parts(ins, outs, send_sems, recv_sems, local_sems):
    x, y, c = _place()
    me, sib = (x, y, c), (x, y, 1 - c)
    chips = [(1 - x, y), (x, 1 - y), (1 - x, 1 - y)]

    def copy(a, k, block, to, src=None):
        px, py, pc = block
        dst = outs[a].at[:, 4 * px + 2 * py + pc]
        return pltpu.make_async_remote_copy(
            src_ref=dst if src is None else src, dst_ref=dst,
            send_sem=send_sems.at[a, k], recv_sem=recv_sems.at[a, k], device_id=to, device_id_type=MESH)

    n = len(ins)
    local = [pltpu.make_async_copy(ins[a], outs[a].at[:, 4 * x + 2 * y + c], local_sems.at[a]) for a in range(n)]
    first = [[copy(a, 0, me, sib, src=ins[a])] + [copy(a, 1 + j, me, (*chip, c), src=ins[a])
                                                   for j, chip in enumerate(chips)] for a in range(n)]

    def start():
        for a in range(n):
            local[a].start()
            for cp in first[a]:
                cp.start()

    def relay():
        for j, chip in enumerate(chips):
            for a in range(n):
                copy(a, 1 + j, (*chip, c), me).wait_recv()
                copy(a, 4 + j, (*chip, c), sib).start()

    def finish():
        for a in range(n):
            copy(a, 0, sib, me).wait_recv()
            for j, chip in enumerate(chips):
                copy(a, 4 + j, (*chip, 1 - c), me).wait_recv()
        for a in range(n):
            for cp in first[a]:
                cp.wait_send()
            local[a].wait()
            for j, chip in enumerate(chips):
                copy(a, 4 + j, (*chip, c), sib).wait_send()

    return start, relay, finish


def _exchange_parts(ins, outs, send_sems, recv_sems, local_sems):
    x, y, c = _place()
    me = 4 * x + 2 * y + c
    n = len(ins)
    peers = _peers(x, y, c)
    local = [pltpu.make_async_copy(ins[a].at[:, me], outs[a].at[me], local_sems.at[a]) for a in range(n)]

    def copy(a, k):
        px, py, pc = peers[k]
        return pltpu.make_async_remote_copy(
            src_ref=ins[a].at[:, 4 * px + 2 * py + pc], dst_ref=outs[a].at[me],
            send_sem=send_sems.at[a, k], recv_sem=recv_sems.at[a, k], device_id=peers[k], device_id_type=MESH)

    def arrival(a, k):
        px, py, pc = peers[k]
        return pltpu.make_async_remote_copy(
            src_ref=ins[a].at[:, me], dst_ref=outs[a].at[4 * px + 2 * py + pc],
            send_sem=send_sems.at[a, k], recv_sem=recv_sems.at[a, k], device_id=peers[k], device_id_type=MESH)

    def start():
        for a in range(n):
            local[a].start()
            for k in range(NDEV - 1):
                copy(a, k).start()

    def finish():
        for a in range(n):
            for k in range(NDEV - 1):
                arrival(a, k).wait_recv()
        for a in range(n):
            for k in range(NDEV - 1):
                copy(a, k).wait_send()
            local[a].wait()

    return start, None, finish


def _call(body, *, name, grid, in_specs, args, out_specs, out_shape, scratch=(), sem=None, aliases=None, job=None):
    aliases = dict(aliases or {})
    if job is None:
        res = pl.pallas_call(
            body, name=name, grid=grid, in_specs=list(in_specs), out_specs=list(out_specs),
            out_shape=list(out_shape), scratch_shapes=list(scratch), input_output_aliases=aliases,
            compiler_params=_params(sem))(*args)
        return list(res), []
    kind, arrays = job
    n_in, n_out, n_scr, nj = len(args), len(out_shape), len(scratch), len(arrays)

    def wrapped(*refs):
        ins = refs[:n_in]
        jin = refs[n_in:n_in + nj]
        o0 = n_in + nj
        outs = refs[o0:o0 + n_out]
        jout = refs[o0 + n_out:o0 + n_out + nj]
        s0 = o0 + n_out + nj
        scr = refs[s0:s0 + n_scr]
        sems = refs[s0 + n_scr:]
        parts = _gather_parts if kind == "gather" else _exchange_parts
        start, relay, finish = parts(jin, jout, *sems)
        if not grid:
            start()
            body(*ins, *outs, *scr)
            if relay is not None:
                relay()
            finish()
            return
        first = last = None
        step, steps = 0, 1
        for d, g in enumerate(grid):
            f, l = pl.program_id(d) == 0, pl.program_id(d) == g - 1
            first = f if first is None else jnp.logical_and(first, f)
            last = l if last is None else jnp.logical_and(last, l)
            step, steps = step * g + pl.program_id(d), steps * g
        pl.when(first)(start)
        body(*ins, *outs, *scr)
        if relay is not None:
            pl.when(step == (3 * steps) // 4)(relay)
        pl.when(last)(finish)

    res = pl.pallas_call(
        wrapped, name=name, grid=grid, in_specs=list(in_specs) + [ANY] * nj,
        out_specs=list(out_specs) + [ANY] * nj, out_shape=list(out_shape) + _job_out_shapes(job),
        scratch_shapes=list(scratch) + _job_scratch(job), input_output_aliases=aliases,
        compiler_params=pltpu.CompilerParams(
            dimension_semantics=None if not grid else ("arbitrary",) * len(grid),
            vmem_limit_bytes=VMEM_LIMIT, has_side_effects=True))(*args, *arrays)
    return list(res[:n_out]), list(res[n_out:])


def _comm_only(name, job):
    return _call(lambda: None, name=name, grid=(), in_specs=[], args=[], out_specs=[], out_shape=[], job=job)[1]


def _sum_devices(name, parts, tr):
    _, r, c = parts.shape

    def body(p_ref, o_ref):
        acc = p_ref[0].astype(F32)
        for d in range(1, NDEV):
            acc = acc + p_ref[d].astype(F32)
        o_ref[...] = acc

    return pl.pallas_call(
        body,
        name=name,
        grid=(r // tr,),
        in_specs=[pl.BlockSpec((NDEV, tr, c), lambda i: (0, i, 0))],
        out_specs=pl.BlockSpec((tr, c), lambda i: (i, 0)),
        out_shape=jax.ShapeDtypeStruct((r, c), F32),
        compiler_params=_params(("parallel",)),
    )(parts)


def _adamw(name, w, g, m, v, tr=None):
    r, c = w.shape
    tr = r if tr is None else tr

    def body(w_ref, g_ref, m_ref, v_ref, d_ref, nm_ref, nv_ref):
        gv = g_ref[...]
        nm = ADAM_B1 * m_ref[...] + (1.0 - ADAM_B1) * gv
        nv = ADAM_B2 * v_ref[...] + (1.0 - ADAM_B2) * (gv * gv)
        m_hat = nm / (1.0 - ADAM_B1 ** ADAM_STEP)
        v_hat = nv / (1.0 - ADAM_B2 ** ADAM_STEP)
        d_ref[...] = -ADAM_LR * (m_hat / (jnp.sqrt(v_hat) + ADAM_EPS) + ADAM_WD * w_ref[...])
        nm_ref[...] = nm
        nv_ref[...] = nv

    spec = pl.BlockSpec((tr, c), lambda i: (i, 0))
    return pl.pallas_call(
        body,
        name=name,
        grid=(r // tr,),
        in_specs=[spec] * 4,
        out_specs=[spec] * 3,
        out_shape=[jax.ShapeDtypeStruct((r, c), F32)] * 3,
        compiler_params=_params(("parallel",)),
    )(w, g, m, v)


SMALL = ["ffn1_norm_pre", "ffn1_norm_post", "mix_norm_pre", "gate_bias", "rel_table", "conv_glu_bias",
         "conv_dw_b", "conv_ln_g", "conv_ln_b", "mix_norm_post", "ffn2_norm_pre", "ffn2_norm_post"]
SMALL_ROWS = 24
DW_ROWS = 32


LOSS_ROW = 20
RELP = 384
SMALL_LAYOUT = {}
_r = 0
for _name, _n in zip(SMALL, [D, D, D, 2 * D, None, 2 * DC, DC, DC, DC, D, D, D]):
    if _n is None:
        SMALL_LAYOUT[_name] = (_r, NH, RELP)
        _r += NH
    else:
        SMALL_LAYOUT[_name] = (_r, -(-_n // D), min(_n, D))
        _r += -(-_n // D)
assert _r == LOSS_ROW


def _small_pieces(name):
    r0, nr, nc = SMALL_LAYOUT[name]
    if name == "rel_table":
        return [(slice(r0, r0 + NH), slice(0, nc), slice(0, NH), slice(0, nc))]
    return [(slice(r0 + k, r0 + k + 1), slice(0, nc), slice(0, 1), slice(k * nc, (k + 1) * nc)) for k in range(nr)]


def _pack_small(vals, loss_row):
    def body(*refs):
        o = refs[-1]
        o[...] = jnp.zeros_like(o)
        for ref, name in zip(refs, SMALL):
            for prow, pcol, arow, acol in _small_pieces(name):
                o[prow, pcol] = ref[arow, acol]
        o[LOSS_ROW:LOSS_ROW + 1, :] = refs[len(SMALL)][...]

    vm = pl.BlockSpec(memory_space=pltpu.VMEM)
    return pl.pallas_call(body, name="pack_small", out_shape=jax.ShapeDtypeStruct((SMALL_ROWS, D), F32),
                          in_specs=[vm] * (len(SMALL) + 1), out_specs=vm,
                          compiler_params=_params())(*[vals[n] for n in SMALL], loss_row)


def _small_update(parts, w, m, v):
    ns = len(SMALL)

    def body(p_ref, *refs):
        ins, outs = refs[:3 * ns], refs[3 * ns:]

        def total(prow, pcol):
            g = p_ref[0, prow, pcol]
            for d in range(1, NDEV):
                g = g + p_ref[d, prow, pcol]
            return g

        for q, name in enumerate(SMALL):
            w_ref, m_ref, v_ref = ins[3 * q:3 * q + 3]
            o_g, o_d, o_m, o_v = outs[4 * q:4 * q + 4]
            for prow, pcol, arow, acol in _small_pieces(name):
                g = total(prow, pcol)
                nm = ADAM_B1 * m_ref[arow, acol] + (1.0 - ADAM_B1) * g
                nv = ADAM_B2 * v_ref[arow, acol] + (1.0 - ADAM_B2) * (g * g)
                m_hat = nm / (1.0 - ADAM_B1 ** ADAM_STEP)
                v_hat = nv / (1.0 - ADAM_B2 ** ADAM_STEP)
                o_g[arow, acol] = g
                o_d[arow, acol] = -ADAM_LR * (m_hat / (jnp.sqrt(v_hat) + ADAM_EPS) + ADAM_WD * w_ref[arow, acol])
                o_m[arow, acol] = nm
                o_v[arow, acol] = nv
        outs[4 * ns][...] = total(slice(LOSS_ROW, LOSS_ROW + 1), slice(None))

    args, out_shape = [], []
    for name in SMALL:
        args += [w[name], m[name], v[name]]
        out_shape += [jax.ShapeDtypeStruct(w[name].shape, F32)] * 4
    out_shape.append(jax.ShapeDtypeStruct((1, D), F32))
    vm = pl.BlockSpec(memory_space=pltpu.VMEM)
    res = pl.pallas_call(body, name="small_update", out_shape=out_shape, in_specs=[vm] * (1 + 3 * ns),
                         out_specs=[vm] * len(out_shape), compiler_params=_params())(parts, *args)
    return {name: res[4 * q:4 * q + 4] for q, name in enumerate(SMALL)}, res[-1]


def kernel(x, ffn1_norm_pre, ffn1_w_gate, ffn1_w_up, ffn1_w_down, ffn1_norm_post, mix_norm_pre, w_in, gate_bias, rel_table, w_attn_out, conv_glu_bias, conv_dw_w, conv_dw_b, conv_ln_g, conv_ln_b, conv_w_out, w_out, mix_norm_post, ffn2_norm_pre, ffn2_w_gate, ffn2_w_up, ffn2_w_down, ffn2_norm_post, loss_target, m_ffn1_norm_pre, m_ffn1_w_gate, m_ffn1_w_up, m_ffn1_w_down, m_ffn1_norm_post, m_mix_norm_pre, m_w_in, m_gate_bias, m_rel_table, m_w_attn_out, m_conv_glu_bias, m_conv_dw_w, m_conv_dw_b, m_conv_ln_g, m_conv_ln_b, m_conv_w_out, m_w_out, m_mix_norm_post, m_ffn2_norm_pre, m_ffn2_w_gate, m_ffn2_w_up, m_ffn2_w_down, m_ffn2_norm_post, v_ffn1_norm_pre, v_ffn1_w_gate, v_ffn1_w_up, v_ffn1_w_down, v_ffn1_norm_post, v_mix_norm_pre, v_w_in, v_gate_bias, v_rel_table, v_w_attn_out, v_conv_glu_bias, v_conv_dw_w, v_conv_dw_b, v_conv_ln_g, v_conv_ln_b, v_conv_w_out, v_w_out, v_mix_norm_post, v_ffn2_norm_pre, v_ffn2_w_gate, v_ffn2_w_up, v_ffn2_w_down, v_ffn2_norm_post):
    return _step(dict(locals()))


WEIGHTS = ["ffn1_norm_pre", "ffn1_w_gate", "ffn1_w_up", "ffn1_w_down", "ffn1_norm_post", "mix_norm_pre", "w_in",
           "gate_bias", "rel_table", "w_attn_out", "conv_glu_bias", "conv_dw_w", "conv_dw_b", "conv_ln_g",
           "conv_ln_b", "conv_w_out", "w_out", "mix_norm_post", "ffn2_norm_pre", "ffn2_w_gate", "ffn2_w_up",
           "ffn2_w_down", "ffn2_norm_post"]
FS = FF // NDEV
PS = (3 * DA + 2 * DC + 2 * D) // NDEV
OS = D // NDEV


def _local_step(xs, target, w, rel_table):
    t = xs.shape[0]
    vec = lambda n: w[n].reshape(1, -1)
    g_pre1, g_post1, g_mix, g_mixp = vec("ffn1_norm_pre"), vec("ffn1_norm_post"), vec("mix_norm_pre"), vec("mix_norm_post")
    g_pre2, g_post2 = vec("ffn2_norm_pre"), vec("ffn2_norm_post")
    gate_b, glu_b = vec("gate_bias"), vec("conv_glu_bias")
    dw_b, ln_g, ln_b = vec("conv_dw_b"), vec("conv_ln_g"), vec("conv_ln_b")

    tr = lambda a: jnp.transpose(a[0]).astype(BF16)
    sh_gu1 = jnp.stack([tr(w["ffn1_w_gate"]), tr(w["ffn1_w_up"])])
    sh_mid = [w["ffn1_w_down"].astype(BF16), tr(w["w_in"])[None], w["w_out"].astype(BF16),
              jnp.stack([tr(w["w_attn_out"]), tr(w["conv_w_out"])]),
              jnp.pad(w["conv_dw_w"][0, :, 0, :], ((0, DW_ROWS - CW), (0, 0)))[None]]
    sh_2 = jnp.stack([tr(w["ffn2_w_gate"]), tr(w["ffn2_w_up"]), w["ffn2_w_down"][0].astype(BF16)])

    (n1,), (w_gu1,) = _rowwise("pre1", lambda xv, g: ((_rms(xv)[0] * g),), [xs], [g_pre1], [(D, BF16)],
                               job=("gather", [sh_gu1]))
    w_gu1 = w_gu1.reshape(2, FF, D)
    (a1, b1, s1), (w_d1, wb, wc, wd, we) = _ffn_up("ffn1_up", n1, w_gu1, 0, job=("gather", sh_mid))
    w_d1, wb, wc, wd = w_d1.reshape(1, FF, D), wb.reshape(NDEV * PS, D), wc.reshape(D, D), wd.reshape(2, D, DA)
    dw_full = jnp.transpose(we[0], (1, 0, 2)).reshape(DW_ROWS, DC)[:CW]
    row = lambda i, j, q: (i, 0)
    top = lambda i, j, q: (0, 0)
    tmr = min(512, t)

    def post1(fv, xv, gp, gm):
        h = xv + 0.5 * (_rms(fv)[0] * gp)
        return fv, h, _rms(h)[0] * gm

    f1, h1, u = _mm_rows("ffn1_down", t, tmr, [(s1, (tmr, FF), row)], [(w_d1, (None, FF, D), lambda i, j, q: (0, 0, 0))],
                         [[(0, 0)]], NN, post1, [xs], [g_post1, g_mix], [(D, F32), (D, F32), (D, BF16)])

    def proj_epi(accs, extras):
        cat = lambda parts: jnp.concatenate(parts, axis=1)
        return (accs[0], jnp.where(pl.program_id(0) > 0, cat(accs[1:3]), 0.0), cat(accs[3:5]), cat(accs[5:9]))

    prev = lambda i, j, q: (jnp.maximum(i - 1, 0), 0)
    qkv, kvp, cin, gg = _mm(
        "proj", (t // KPAD + 1, 1, 1), [(u, (KPAD, D), prev)],
        [(wb, (DA, D), (lambda i, j, q, blk=blk: (blk, 0))) for blk in range(NDEV * PS // DA)],
        [[(0, blk)] for blk in range(NDEV * PS // DA)], NT, proj_epi,
        [((t, DA), BF16, (KPAD, DA), prev), ((t + KPAD, 2 * DA), BF16, (KPAD, 2 * DA), lambda i, j, q: (i, 0)),
         ((t, 2 * DC), F32, (KPAD, 2 * DC), prev), ((t, 2 * D), BF16, (KPAD, 2 * D), prev)])

    bias = _relbias_fwd(jnp.pad(rel_table[0], ((0, 0), (0, 384 - NREL))))
    att, (w_2,) = _attn_fwd(qkv, kvp, bias, job=("gather", [sh_2]))
    w_2 = w_2.reshape(3, FF, D)
    cs, pre = _conv_fwd(cin, glu_b, dw_full, dw_b, ln_g, ln_b)
    ya = _mm_simple("attn_out", att, wd[0], NT, BF16, 1024, 1024)

    def merge(ybv, yav, gv, gb):
        gates = _sigmoid(gv + gb)
        return ybv, gates[:, :D] * yav + gates[:, D:] * ybv

    yb, merged = _mm_rows("conv_out", t, tmr, [(cs, (tmr, DC), row)], [(wd[1], (D, DC), top)], [[(0, 0)]], NT,
                          merge, [ya, gg], [gate_b], [(D, BF16), (D, BF16)])

    def postm(mv, hv, gp, g2):
        h = hv + _rms(mv)[0] * gp
        return mv, h, _rms(h)[0] * g2

    mm_, h2, n2 = _mm_rows("mix_out", t, tmr, [(merged, (tmr, D), row)], [(wc, (D, D), top)], [[(0, 0)]], NN,
                           postm, [h1], [g_mixp, g_pre2], [(D, F32), (D, F32), (D, BF16)])
    a2, b2, s2 = _ffn_up("ffn2_up", n2, w_2, 0)

    def post2(fv, hv, tv, gp):
        fh, r = _rms(fv)
        yv = hv + 0.5 * (fh * gp)
        err = yv - tv
        dy = err * (1.0 / D)
        df, dg = _rms_bwd(fh, r, gp, 0.5 * dy)
        return dy, df, (0.5 / D) * (err * err), dg

    dy, df2, loss_row, d_post2 = _mm_rows(
        "ffn2_down", t, tmr, [(s2, (tmr, FF), row)], [(w_2, (None, FF, D), lambda i, j, q: (2, 0, 0))], [[(0, 0)]],
        NN, post2, [h2, target], [g_post2], [(D, F32), (D, BF16)], [D, D])

    tmw = 1408
    nfi = FF // tmw
    tkf = min(2048, t)
    tkp = min(1024, t)

    def wgrad_down(name, s, df, shape, part, carry_buf, job=None):
        return _wgrad(name, (s, (tkf, tmw), lambda i, j, q: (q, i)), df,
                      (None, tmw, D), lambda i, j, q: (part, i, 0), nfi, carry_buf, shape, tk=tkf, job=job)

    def wgrad_gate_up(name, dab, nrm, shape, carry_buf, job=None):
        return _wgrad(name, (dab, (None, tkf, tmw), lambda i, j, q: (i // nfi, q, i % nfi)), nrm,
                      (None, tmw, D), lambda i, j, q: (i // nfi, i % nfi, 0), 2 * nfi, carry_buf, shape,
                      tk=tkf, job=job)

    g_2 = wgrad_down("ffn2_wgrad_d", s2, df2, (3, FF, D), 2, None)
    dab2 = _ffn_bwd_act("ffn2_bwd_act", df2, w_2, 2, a2, b2)
    g_2 = wgrad_gate_up("ffn2_wgrad_gu", dab2, n2, (3, FF, D), g_2)

    def bwd_pre2(dnv, hv, dyv, mv, g2, gp):
        hh, r = _rms(hv)
        dx, dg2 = _rms_bwd(hh, r, g2, dnv)
        dh = dyv + dx
        mh, rm = _rms(mv)
        dm, dgp = _rms_bwd(mh, rm, gp, dh)
        return dh, dm, dg2, dgp

    tmb = 256
    dh2, dm, d_pre2, d_mixp = _mm_rows(
        "ffn2_bwd_in", t, tmb,
        [(dab2, (None, tmb, FF), lambda i, j, q: (0, i, 0)), (dab2, (None, tmb, FF), lambda i, j, q: (1, i, 0))],
        [(w_2, (None, FF, D), lambda i, j, q: (0, 0, 0)), (w_2, (None, FF, D), lambda i, j, q: (1, 0, 0))],
        [[(0, 0), (1, 1)]], NN, bwd_pre2, [h2, dy, mm_], [g_pre2, g_mixp], [(D, F32), (D, BF16)], [D, D])

    def merge_bwd(dmv, yav, ybv, gv, gb):
        gates = _sigmoid(gv + gb)
        ga, gbb = gates[:, :D], gates[:, D:]
        dgg = jnp.concatenate([dmv * yav * ga * (1.0 - ga), dmv * ybv * gbb * (1.0 - gbb)], axis=1)
        return dmv * ga, dmv * gbb, dgg, dgg

    dya, dyb, dgg, d_gate_b = _mm_rows(
        "mix_out_bwd", t, tmr, [(dm, (tmr, D), row)], [(wc, (D, D), top)], [[(0, 0)]], NT, merge_bwd,
        [ya, yb, gg], [gate_b], [(D, BF16), (D, BF16), (2 * D, BF16)], [2 * D])
    datt = _mm_simple("attn_out_bwd", dya, wd[0], NN, BF16, 1024, 512)
    dcs = _mm_simple("conv_out_bwd", dyb, wd[1], NN, F32, 1024, 512)
    g_c = _wgrad("mix_out_wgrad", (merged, (tkp, D), lambda i, j, q: (q, 0)), dm,
                 (D, D), lambda i, j, q: (0, 0), 1, None, (D, D), tk=tkp)
    g_d = _wgrad("attn_out_wgrad", (dya, (tkp, D), lambda i, j, q: (q, 0)), att,
                 (None, D, DA), lambda i, j, q: (0, 0, 0), 1, None, (2, D, DA), tk=tkp)
    g_d = _wgrad("conv_out_wgrad", (dyb, (tkp, D), lambda i, j, q: (q, 0)), cs,
                 (None, D, DA), lambda i, j, q: (1, 0, 0), 1, g_d, (2, D, DA), tk=tkp)
    (dq, dkvp, dbias), (x_2, x_c, x_d) = _attn_bwd(
        qkv, kvp, bias, datt,
        job=("exchange", [g_2.reshape(3, NDEV, FS, D), g_c.reshape(1, NDEV, OS, D), g_d.reshape(2, NDEV, OS, DA)]))
    d_rel = _relbias_bwd(dbias)
    dcin, d_glu_b, d_dw8, d_dw_b, d_ln_g, d_ln_b = _conv_bwd(dcs, pre, cin, glu_b, dw_full, ln_g, ln_b)
    g_dw = jnp.pad(d_dw8, ((0, 8 * (DW_ROWS - CW)), (0, 0)))
    g_dw = g_dw.reshape(DW_ROWS * 8, NDEV, DC // NDEV).transpose(1, 0, 2)

    top = lambda i, j, q: (0, 0)
    g_b = jnp.concatenate([
        _wgrad("proj_wgrad_q", (dq, (tkp, DA), lambda i, j, q: (q, 0)), u, (DA, D), top, 1, None, (DA, D), tk=tkp),
        _wgrad("proj_wgrad_kv", (dkvp, (KPAD, 2 * DA), lambda i, j, q: (q + 1, 0)), u, (2 * DA, D), top, 1, None,
               (2 * DA, D), tk=KPAD),
        _wgrad("proj_wgrad_c", (dcin, (tkp, 2 * DC), lambda i, j, q: (q, 0)), u, (2 * DC, D), top, 1, None,
               (2 * DC, D), tk=tkp),
        _wgrad("proj_wgrad_g", (dgg, (tkp, 2 * D), lambda i, j, q: (q, 0)), u, (2 * D, D), top, 1, None,
               (2 * D, D), tk=tkp)], axis=0)

    tmu = 256
    a_ops = [(dq, (tmu, DA), row),
             (dkvp, (tmu, 2 * DA), lambda i, j, q: (i + KPAD // tmu, 0)),
             (dcin, (tmu, 2 * DC), row),
             (dgg, (tmu, 2 * D), row)]
    b_ops = [(wb[:DA], (DA, D), top), (wb[DA:3 * DA], (2 * DA, D), top),
             (wb[3 * DA:3 * DA + 2 * DC], (2 * DC, D), top), (wb[3 * DA + 2 * DC:], (2 * D, D), top)]

    def bwd_mix(duv, hv, dhv, fv, gm, gp):
        hh, r = _rms(hv)
        dx, dgm = _rms_bwd(hh, r, gm, duv)
        dh = dhv + dx
        fh, rf = _rms(fv)
        df, dgp = _rms_bwd(fh, rf, gp, 0.5 * dh)
        return dh, df, dgm, dgp

    (dh1, df1, d_mix, d_post1), (x_b, x_dw) = _mm_rows(
        "proj_bwd", t, tmu, a_ops, b_ops, [[(0, 0), (1, 1), (2, 2), (3, 3)]], NN, bwd_mix, [h1, dh2, f1],
        [g_mix, g_post1], [(D, F32), (D, BF16)], [D, D],
        job=("exchange", [g_b.reshape(1, NDEV, PS, D), g_dw[None]]))
    g_d1 = wgrad_down("ffn1_wgrad_d", s1, df1, (1, FF, D), 0, None)
    dab1, (x_d1,) = _ffn_bwd_act("ffn1_bwd_act", df1, w_d1, 0, a1, b1,
                                 job=("exchange", [g_d1.reshape(1, NDEV, FS, D)]))

    def wgrad_half(name, p, job=None):
        return _wgrad(name, (dab1, (None, tkf, tmw), lambda i, j, q: (p, q, i)), n1,
                      (None, tmw, D), lambda i, j, q: (0, i, 0), nfi, None, (1, FF, D), tk=tkf, job=job)

    g_g1 = wgrad_half("ffn1_wgrad_g", 0)
    g_u1, (x_g1,) = wgrad_half("ffn1_wgrad_u", 1, job=("exchange", [g_g1.reshape(1, NDEV, FS, D)]))
    dn1, (x_u1,) = _ffn_bwd_in("ffn1_bwd_in", dab1, w_gu1, 0, job=("exchange", [g_u1.reshape(1, NDEV, FS, D)]))

    def bwd_pre1(xv, dnv, dhv, g1):
        xh, r = _rms(xv)
        dx, dg1 = _rms_bwd(xh, r, g1, dnv)
        return dhv + dx, dg1

    dx, d_pre1 = _rowwise("bwd_pre1", bwd_pre1, [xs, dn1, dh1], [g_pre1], [(D, F32)], [D])

    small_g = {"ffn1_norm_pre": d_pre1, "ffn1_norm_post": d_post1, "mix_norm_pre": d_mix, "gate_bias": d_gate_b,
               "rel_table": d_rel, "conv_glu_bias": d_glu_b, "conv_dw_b": d_dw_b, "conv_ln_g": d_ln_g,
               "conv_ln_b": d_ln_b, "mix_norm_post": d_mixp, "ffn2_norm_pre": d_pre2, "ffn2_norm_post": d_post2}
    return loss_row, dx, (x_g1, x_u1, x_d1, x_2, x_b, x_c, x_d, x_dw), small_g


def _step(args):
    names = WEIGHTS
    w = {n: args[n] for n in names}
    fs, ps, os_ = FS, PS, OS
    conv_dw_w = args["conv_dw_w"]
    loss_row, dx, (x_g1, x_u1, x_d1, x_2, x_b, x_c, x_d, x_dw), small_g = _local_step(
        args["x"][0], args["loss_target"][0], w, args["rel_table"])

    g_small = _pack_small(small_g, loss_row)
    (x_s,) = _comm_only("gather_small_grads", ("gather", [g_small[None]]))

    s_g1 = _sum_devices("sum_ffn1_g", x_g1.reshape(NDEV, fs, D), fs)
    s_u1 = _sum_devices("sum_ffn1_u", x_u1.reshape(NDEV, fs, D), fs)
    s_d1 = _sum_devices("sum_ffn1_d", x_d1.reshape(NDEV, fs, D), fs)
    s_2 = _sum_devices("sum_ffn2", x_2.reshape(NDEV, 3 * fs, D), fs).reshape(3, fs, D)
    s_b = _sum_devices("sum_proj", x_b.reshape(NDEV, ps, D), ps)
    s_c = _sum_devices("sum_mix", x_c.reshape(NDEV, os_, D), os_)
    s_d = _sum_devices("sum_out", x_d.reshape(NDEV, 2 * os_, DA), 2 * os_).reshape(2, os_, DA)
    s_dw = _sum_devices("sum_dw", x_dw.reshape(NDEV, DW_ROWS * 8, DC // NDEV), DW_ROWS * 8)

    grads = {
        "ffn1_w_gate": jnp.transpose(s_g1)[None], "ffn1_w_up": jnp.transpose(s_u1)[None], "ffn1_w_down": s_d1[None],
        "ffn2_w_gate": jnp.transpose(s_2[0])[None], "ffn2_w_up": jnp.transpose(s_2[1])[None], "ffn2_w_down": s_2[2][None],
        "w_in": jnp.transpose(s_b)[None], "w_out": s_c[None],
        "w_attn_out": jnp.transpose(s_d[0])[None], "conv_w_out": jnp.transpose(s_d[1])[None],
    }
    deltas, new_m, new_v = {}, {}, {}

    def flat2(a, n):
        return jnp.pad(a[0], ((0, 0), (0, RELP - NREL))) if n == "rel_table" else a.reshape(1, -1)

    small, loss_terms = _small_update(
        x_s.reshape(NDEV, SMALL_ROWS, D), {n: flat2(w[n], n) for n in SMALL},
        {n: flat2(args["m_" + n], n) for n in SMALL}, {n: flat2(args["v_" + n], n) for n in SMALL})
    for n in SMALL:
        vals = [a[:, :NREL] if n == "rel_table" else a for a in small[n]]
        grads[n], deltas[n], new_m[n], new_v[n] = [a.reshape(w[n].shape) for a in vals]

    big = ["ffn1_w_gate", "ffn1_w_up", "ffn1_w_down", "w_in", "w_attn_out", "conv_w_out", "w_out",
           "ffn2_w_gate", "ffn2_w_up", "ffn2_w_down"]
    for n in big:
        shp = w[n].shape
        two = lambda a: a.reshape(shp[1], shp[2])
        rows = shp[1]
        tr_ = rows // 2 if rows % 16 == 0 else rows
        d_, m_, v_ = _adamw("adamw_" + n, two(w[n]), two(grads[n]), two(args["m_" + n]), two(args["v_" + n]), tr_)
        deltas[n], new_m[n], new_v[n] = d_.reshape(shp), m_.reshape(shp), v_.reshape(shp)

    g_dw_own = _fold8("fold_dw", s_dw)[:CW]
    grads["conv_dw_w"] = g_dw_own.reshape(1, CW, 1, DC // NDEV)
    flat = lambda a: a.reshape(CW, DC // NDEV)
    d_, m_, v_ = _adamw("adamw_dw", flat(conv_dw_w), g_dw_own, flat(args["m_conv_dw_w"]), flat(args["v_conv_dw_w"]))
    shp = conv_dw_w.shape
    deltas["conv_dw_w"], new_m["conv_dw_w"], new_v["conv_dw_w"] = d_.reshape(shp), m_.reshape(shp), v_.reshape(shp)

    loss = jnp.sum(loss_terms)
    return (loss, dx[None], *[grads[n] for n in names], *[deltas[n] for n in names],
            *[new_m[n] for n in names], *[new_v[n] for n in names])


def _fold8(name, a):
    r8, c = a.shape

    def body(a_ref, o_ref):
        o_ref[...] = a_ref[...].reshape(r8 // 8, 8, c).sum(axis=1)

    return pl.pallas_call(
        body,
        name=name,
        out_shape=jax.ShapeDtypeStruct((r8 // 8, c), F32),
        in_specs=[pl.BlockSpec(memory_space=pltpu.VMEM)],
        out_specs=pl.BlockSpec(memory_space=pltpu.VMEM),
        compiler_params=_params(),
    )(a)
```

```python
import functools

import jax
import jax.numpy as jnp
from jax import lax
from jax.experimental import pallas as pl
from jax.experimental.pallas import tpu as pltpu

F32 = jnp.float32
BF16 = jnp.bfloat16

D = 1024
FF = 2816
DA = 512
DC = 512
NH = 8
CHUNK = 64
LEFT = 8
CW = 31
NREL = 257
EPS = 1e-6
NDEV = 8

QB = 4 * CHUNK
KW = LEFT * CHUNK + QB
KPAD = LEFT * CHUNK
RELW = KW + QB
HALO = 32

TM = 512
VMEM_LIMIT = 56 * 1024 * 1024

ADAM_LR, ADAM_B1, ADAM_B2, ADAM_EPS, ADAM_WD, ADAM_STEP = 0.001, 0.9, 0.999, 1e-08, 0.01, 10

NT = (((1,), (1,)), ((), ()))
NN = (((1,), (0,)), ((), ()))
TN = (((0,), (0,)), ((), ()))

MESH = pl.DeviceIdType.MESH
ANY = pl.BlockSpec(memory_space=pl.ANY)


def _params(sem=None, vmem=VMEM_LIMIT):
    return pltpu.CompilerParams(dimension_semantics=sem, vmem_limit_bytes=vmem)


def _sigmoid(x):
    return 0.5 * jnp.tanh(0.5 * x) + 0.5


def _mm(name, grid, a_ops, b_ops, groups, dims, epi, outs, extras=(), carry=None, job=None, params=(),
        partials=()):
    nk = grid[2]
    na, nb, ne, no, ng = len(a_ops), len(b_ops), len(extras), len(outs), len(groups)
    npar, npart = len(params), len(partials)
    nc = 0 if carry is None else 1

    def body(*refs):
        a_refs = refs[:na]
        b_refs = refs[na:na + nb]
        e_refs = refs[na + nb:na + nb + ne]
        p_refs = refs[na + nb + ne:na + nb + ne + npar]
        o0 = na + nb + ne + npar + nc
        o_refs = refs[o0:o0 + no]
        s_refs = refs[o0 + no:o0 + no + npart]
        acc_refs = refs[o0 + no + npart:o0 + no + npart + (ng if nk > 1 else 0)]
        part_refs = refs[len(refs) - npart:] if npart else ()
        k = pl.program_id(2)
        prods = []
        for grp in groups:
            p = None
            for ai, bi in grp:
                t = lax.dot_general(a_refs[ai][...], b_refs[bi][...], dims, preferred_element_type=F32)
                p = t if p is None else p + t
            prods.append(p)

        def finish(vals):
            res = epi(vals, [e[...] for e in e_refs] + [p[...] for p in p_refs])
            for o, r in zip(o_refs, res[:no]):
                o[...] = r.astype(o.dtype)
            if npart:
                i, j = pl.program_id(0), pl.program_id(1)

                @pl.when(jnp.logical_and(i == 0, j == 0))
                def _():
                    for acc in part_refs:
                        acc[...] = jnp.zeros_like(acc)

                for acc, r in zip(part_refs, res[no:]):
                    acc[...] += r.reshape(r.shape[0] // 8, 8, r.shape[-1]).sum(axis=0)

                @pl.when(jnp.logical_and(i == grid[0] - 1, j == grid[1] - 1))
                def _():
                    for s, acc in zip(s_refs, part_refs):
                        s[...] = acc[...].sum(axis=0, keepdims=True)

        if nk == 1:
            finish(prods)
        else:
            @pl.when(k == 0)
            def _():
                for acc, p in zip(acc_refs, prods):
                    acc[...] = p

            @pl.when(k > 0)
            def _():
                for acc, p in zip(acc_refs, prods):
                    acc[...] += p

            @pl.when(k == nk - 1)
            def _():
                finish([acc[...] for acc in acc_refs])

    in_specs = [pl.BlockSpec(blk, im) for _, blk, im in list(a_ops) + list(b_ops) + list(extras)]
    in_specs += [pl.BlockSpec(p.shape, lambda i, j, q: (0, 0)) for p in params]
    args = [arr for arr, _, _ in list(a_ops) + list(b_ops) + list(extras)] + list(params)
    aliases = {}
    if carry is not None:
        in_specs.append(ANY)
        args.append(carry[0])
        aliases = {len(args) - 1: carry[1]}
    scratch = []
    if nk > 1:
        for _ in range(ng):
            blk = tuple(b for b in outs[0][2] if b is not None)
            scratch.append(pltpu.VMEM(blk, F32))
    scratch += [pltpu.VMEM((8, c), F32) for c in partials]
    res, jres = _call(
        body, name=name, grid=grid, in_specs=in_specs, args=args,
        out_specs=[pl.BlockSpec(blk, im) for _, _, blk, im in outs]
        + [pl.BlockSpec((1, c), lambda i, j, q: (0, 0)) for c in partials],
        out_shape=[jax.ShapeDtypeStruct(shp, dt) for shp, dt, _, _ in outs]
        + [jax.ShapeDtypeStruct((1, c), F32) for c in partials],
        scratch=scratch, sem=("arbitrary",) * 3 if partials else ("parallel", "parallel", "arbitrary"),
        aliases=aliases, job=job)
    return res if job is None else (res, jres)


def _first(accs, extras):
    return (accs[0],)


def _mm_rows(name, t, tm, a_ops, b_ops, groups, dims, fn, extras, params, outs, partials=(), nk=1, job=None):
    ne = len(extras)

    def epi(accs, rest):
        return fn(*accs, *[r.astype(F32) for r in rest[:ne]], *rest[ne:])

    e_ops = [(arr, (tm, arr.shape[1]), lambda i, j, q: (i, 0)) for arr in extras]
    o_ops = [((t, c), dt, (tm, c), lambda i, j, q: (i, 0)) for c, dt in outs]
    return _mm(name, (t // tm, 1, nk), a_ops, b_ops, groups, dims, epi, o_ops, e_ops, job=job, params=params,
               partials=partials)


def _mm_simple(name, a, b, dims, out_dtype, tm, tn, b_row0=0, b_rows=None):
    m, kk = a.shape
    tm = min(tm, m)
    if dims is NT:
        n = b.shape[0] if b_rows is None else b_rows
        assert b.shape[1] == kk and b_row0 % tn == 0
        b_op = (b, (tn, kk), lambda i, j, q: (j + b_row0 // tn, 0))
    else:
        assert b.shape[0] == kk
        n = b.shape[1]
        b_op = (b, (kk, tn), lambda i, j, q: (0, j))
    a_op = (a, (tm, kk), lambda i, j, q: (i, 0))
    out = ((m, n), out_dtype, (tm, tn), lambda i, j, q: (i, j))
    return _mm(name, (m // tm, n // tn, 1), [a_op], [b_op], [[(0, 0)]], dims, _first, [out])[0]


def _rowwise(name, fn, tiled, params, outs, partials=(), tm=TM, job=None):
    t = tiled[0].shape[0]
    steps = t // tm
    nt, npar, no, npart = len(tiled), len(params), len(outs), len(partials)

    def body(*refs):
        t_refs = refs[:nt]
        p_refs = refs[nt:nt + npar]
        o_refs = refs[nt + npar:nt + npar + no]
        s_refs = refs[nt + npar + no:nt + npar + no + npart]
        acc_refs = refs[nt + npar + no + npart:]
        i = pl.program_id(0)
        res = fn(*[r[...].astype(F32) for r in t_refs], *[r[...] for r in p_refs])
        for o, r in zip(o_refs, res[:no]):
            o[...] = r.astype(o.dtype)

        @pl.when(i == 0)
        def _():
            for acc in acc_refs:
                acc[...] = jnp.zeros_like(acc)

        for acc, r in zip(acc_refs, res[no:]):
            acc[...] += r.reshape(tm // 8, 8, r.shape[-1]).sum(axis=0)

        @pl.when(i == steps - 1)
        def _():
            for s, acc in zip(s_refs, acc_refs):
                s[...] = acc[...].sum(axis=0, keepdims=True)

    in_specs = [pl.BlockSpec((tm, a.shape[1]), lambda i: (i, 0)) for a in tiled]
    in_specs += [pl.BlockSpec(p.shape, lambda i: (0, 0)) for p in params]
    out_specs = [pl.BlockSpec((tm, c), lambda i: (i, 0)) for c, _ in outs]
    out_specs += [pl.BlockSpec((1, c), lambda i: (0, 0)) for c in partials]
    out_shape = [jax.ShapeDtypeStruct((t, c), dt) for c, dt in outs]
    out_shape += [jax.ShapeDtypeStruct((1, c), F32) for c in partials]
    res, jres = _call(body, name=name, grid=(steps,), in_specs=in_specs, args=[*tiled, *params], out_specs=out_specs,
                      out_shape=out_shape, scratch=[pltpu.VMEM((8, c), F32) for c in partials], sem=("arbitrary",),
                      job=job)
    return res if job is None else (res, jres)


def _rms(x):
    r = lax.rsqrt(jnp.mean(x * x, axis=-1, keepdims=True) + EPS)
    return x * r, r


def _rms_bwd(xhat, r, g, dy):
    dxh = dy * g
    dx = r * (dxh - xhat * jnp.mean(dxh * xhat, axis=-1, keepdims=True))
    return dx, dy * xhat


def _ffn_up(name, n, wa, part, tm=512, tf=1408, job=None):
    t = n.shape[0]

    def epi(accs, extras):
        a, b = accs
        sg = _sigmoid(a)
        silu = a * sg
        return silu, b * (sg + silu * (1.0 - sg)), silu * b

    a_op = (n, (tm, D), lambda f, i, q: (i, 0))
    b_ops = [(wa, (None, tf, D), lambda f, i, q: (part, f, 0)),
             (wa, (None, tf, D), lambda f, i, q: (part + 1, f, 0))]
    outs = [((t, FF), BF16, (tm, tf), lambda f, i, q: (i, f))] * 3
    return _mm(name, (FF // tf, t // tm, 1), [a_op], b_ops, [[(0, 0)], [(0, 1)]], NT, epi, outs, job=job)


def _ffn_bwd_act(name, df, wa, part, a, b, tm=512, tf=1408, job=None):
    t = df.shape[0]

    def epi(accs, extras):
        ds = accs[0]
        return (jnp.stack([ds * extras[1].astype(F32), ds * extras[0].astype(F32)]),)

    a_op = (df, (tm, D), lambda f, i, q: (i, 0))
    b_op = (wa, (None, tf, D), lambda f, i, q: (part, f, 0))
    extras = [(a, (tm, tf), lambda f, i, q: (i, f)), (b, (tm, tf), lambda f, i, q: (i, f))]
    out = ((2, t, FF), BF16, (2, tm, tf), lambda f, i, q: (0, i, f))
    res = _mm(name, (FF // tf, t // tm, 1), [a_op], [b_op], [[(0, 0)]], NT, epi, [out], extras, job=job)
    return res[0] if job is None else (res[0][0], res[1])


def _ffn_bwd_in(name, dab, wa, part, tm=1024, job=None):
    t = dab.shape[1]
    tm = min(tm, t)
    a_op = (dab, (None, tm, FF), lambda i, j, q: (q, i, 0))
    b_op = (wa, (None, FF, D), lambda i, j, q: (part + q, 0, 0))
    out = ((t, D), F32, (tm, D), lambda i, j, q: (i, 0))
    res = _mm(name, (t // tm, 1, 2), [a_op], [b_op], [[(0, 0)]], NN, _first, [out], job=job)
    return res[0] if job is None else (res[0][0], res[1])


def _wgrad(name, dy_op, x, out_block, out_map, gi, carry_buf, out_shape, tk=512, job=None):
    t, c = x.shape
    b_op = (x, (tk, c), lambda i, j, q: (q, 0))
    out = (out_shape, BF16, out_block, out_map)
    carry = None if carry_buf is None else (carry_buf, 0)
    res = _mm(name, (gi, 1, t // tk), [dy_op], [b_op], [[(0, 0)]], TN, _first, [out], carry=carry, job=job)
    return res[0] if job is None else (res[0][0], res[1])


def _rel_onehot():
    j = lax.broadcasted_iota(jnp.int32, (384, RELW), 0)
    xx = lax.broadcasted_iota(jnp.int32, (384, RELW), 1)
    idx = jnp.clip(KPAD + QB - xx, -128, 128) + 128
    return (j == idx).astype(F32)


def _relbias_fwd(table):
    def body(t_ref, o_ref):
        rev = jnp.dot(t_ref[...], _rel_onehot(), precision=lax.Precision.HIGHEST, preferred_element_type=F32)
        for r in range(QB):
            row = pltpu.roll(rev, (RELW - (QB - r)) % RELW, 1)[:, :KW]
            rr = lax.broadcasted_iota(jnp.int32, (NH, KW), 1) >> 6
            ok = (rr >= (r // CHUNK)) & (rr <= (r // CHUNK) + LEFT)
            row = jnp.where(ok, row, -1e30)
            for h in range(NH):
                o_ref[h * QB + r:h * QB + r + 1, :] = row[h:h + 1, :]

    return pl.pallas_call(
        body,
        name="relbias_fwd",
        out_shape=jax.ShapeDtypeStruct((NH * QB, KW), F32),
        in_specs=[pl.BlockSpec(memory_space=pltpu.VMEM)],
        out_specs=pl.BlockSpec(memory_space=pltpu.VMEM),
        compiler_params=_params(),
    )(table)


def _relbias_bwd(dbias):
    def body(d_ref, o_ref, acc_ref):
        for h in range(NH):
            acc = jnp.zeros((1, RELW), F32)
            for r in range(QB):
                row = d_ref[h * QB + r:h * QB + r + 1, :]
                wide = jnp.concatenate([row, jnp.zeros((1, RELW - KW), F32)], axis=1)
                acc = acc + pltpu.roll(wide, QB - r, 1)
            acc_ref[h:h + 1, :] = acc
        o_ref[...] = lax.dot_general(acc_ref[...], _rel_onehot(), NT, precision=lax.Precision.HIGHEST,
                                     preferred_element_type=F32)

    return pl.pallas_call(
        body,
        name="relbias_bwd",
        out_shape=jax.ShapeDtypeStruct((NH, 384), F32),
        in_specs=[pl.BlockSpec(memory_space=pltpu.VMEM)],
        out_specs=pl.BlockSpec(memory_space=pltpu.VMEM),
        scratch_shapes=[pltpu.VMEM((NH, RELW), F32)],
        compiler_params=_params(),
    )(dbias)


HQ = QB // 2


def _stack_heads(x_pair):
    first = lax.broadcasted_iota(jnp.int32, (1, 128), 1) < 64
    zero = jnp.zeros_like(x_pair)
    a, b = jnp.where(first, x_pair, zero), jnp.where(first, zero, x_pair)
    return jnp.concatenate([a[:HQ], b[:HQ], a[HQ:], b[HQ:]], axis=0), first


def _unstack_heads(o, first):
    return jnp.concatenate([jnp.where(first, o[0:HQ], o[HQ:2 * HQ]),
                            jnp.where(first, o[2 * HQ:3 * HQ], o[3 * HQ:4 * HQ])], axis=0)


def _half_cols(hf):
    return slice(hf * HQ, hf * HQ + KW - HQ)


def _half_bias(b_ref, pair, hf):
    rows = lambda h: slice(h * QB + hf * HQ, h * QB + (hf + 1) * HQ)
    return jnp.concatenate([b_ref[rows(2 * pair), _half_cols(hf)], b_ref[rows(2 * pair + 1), _half_cols(hf)]], axis=0)


def _half_probs(s_full, b_ref, pair, hf, key_ok):
    s = s_full[2 * hf * HQ:2 * (hf + 1) * HQ, _half_cols(hf)] + _half_bias(b_ref, pair, hf)
    if key_ok is not None:
        s = jnp.where(key_ok[:, _half_cols(hf)], s, -1e30)
    e = jnp.exp(s - jnp.max(s, axis=-1, keepdims=True))
    return e * (1.0 / jnp.sum(e, axis=-1, keepdims=True))


def _widen(top, bottom):
    z = jnp.zeros((2 * HQ, HQ), top.dtype)
    return jnp.concatenate([jnp.concatenate([top, z], axis=1), jnp.concatenate([z, bottom], axis=1)], axis=0)


def _attn_fwd(qkv, kvp, bias, job=None):
    t = qkv.shape[0]

    def body(q_ref, kv_ref, b_ref, o_ref):
        i = pl.program_id(0)

        def run(masked):
            start = pl.multiple_of(i * QB, QB)
            col = lax.broadcasted_iota(jnp.int32, (1, KW), 1)
            key_ok = (col >= KPAD - i * QB) if masked else None
            for pair in range(4):
                lo = pair * 128
                kw = kv_ref[pl.ds(start, KW), lo:lo + 128]
                vw = kv_ref[pl.ds(start, KW), DA + lo:DA + lo + 128]
                qs, first = _stack_heads(q_ref[:, lo:lo + 128])
                s = lax.dot_general(qs * 0.125, kw, NT, preferred_element_type=F32)
                p = _widen(*[_half_probs(s, b_ref, pair, hf, key_ok).astype(BF16) for hf in range(2)])
                o = jnp.dot(p, vw, preferred_element_type=F32)
                o_ref[:, lo:lo + 128] = _unstack_heads(o, first).astype(BF16)

        pl.when(i < KPAD // QB)(lambda: run(True))
        pl.when(i >= KPAD // QB)(lambda: run(False))

    res, jres = _call(
        body, name="attn_fwd", grid=(t // QB,),
        in_specs=[pl.BlockSpec((QB, DA), lambda i: (i, 0)),
                  pl.BlockSpec(memory_space=pltpu.VMEM),
                  pl.BlockSpec(memory_space=pltpu.VMEM)],
        args=[qkv, kvp, bias],
        out_specs=[pl.BlockSpec((QB, DA), lambda i: (i, 0))],
        out_shape=[jax.ShapeDtypeStruct((t, DA), BF16)],
        sem=("arbitrary",), job=job)
    return res[0], jres


def _attn_bwd(qkv, kvp, bias, datt, job=None):
    t = qkv.shape[0]
    nb = t // QB
    flush = (KW - QB) // QB

    def body(q_ref, kv_ref, b_ref, do_ref, dq_ref, dkv_ref, db_out, acc_ref, db_ref):
        i = pl.program_id(0)

        @pl.when(i == 0)
        def _():
            acc_ref[...] = jnp.zeros_like(acc_ref)
            db_ref[...] = jnp.zeros_like(db_ref)

        def run(masked):
            start = pl.multiple_of(i * QB, QB)
            col = lax.broadcasted_iota(jnp.int32, (1, KW), 1)
            key_ok = (col >= KPAD - i * QB) if masked else None
            for pair in range(4):
                lo = pair * 128
                kw = kv_ref[pl.ds(start, KW), lo:lo + 128]
                vw = kv_ref[pl.ds(start, KW), DA + lo:DA + lo + 128]
                qs, first = _stack_heads(q_ref[:, lo:lo + 128])
                qs = qs * 0.125
                dos, _ = _stack_heads(do_ref[:, lo:lo + 128])
                s = lax.dot_general(qs, kw, NT, preferred_element_type=F32)
                dp = lax.dot_general(dos, vw, NT, preferred_element_type=F32)
                ps, dss = [], []
                for hf in range(2):
                    p = _half_probs(s, b_ref, pair, hf, key_ok)
                    dph = dp[2 * hf * HQ:2 * (hf + 1) * HQ, _half_cols(hf)]
                    ds = p * (dph - jnp.sum(p * dph, axis=-1, keepdims=True))
                    for k, h in enumerate((2 * pair, 2 * pair + 1)):
                        db_ref[h * QB + hf * HQ:h * QB + (hf + 1) * HQ, _half_cols(hf)] += ds[k * HQ:(k + 1) * HQ]
                    ps.append(p.astype(BF16))
                    dss.append(ds.astype(BF16))
                pb, dsb = _widen(*ps), _widen(*dss)
                dq = jnp.dot(dsb, kw, preferred_element_type=F32)
                dq_ref[:, lo:lo + 128] = (_unstack_heads(dq, first) * 0.125).astype(BF16)
                acc_ref[:, lo:lo + 128] += lax.dot_general(dsb, qs, TN, preferred_element_type=F32)
                acc_ref[:, DA + lo:DA + lo + 128] += lax.dot_general(pb, dos, TN, preferred_element_type=F32)

        pl.when(i < KPAD // QB)(lambda: run(True))
        pl.when(jnp.logical_and(i >= KPAD // QB, i < nb))(lambda: run(False))

        dkv_ref[...] = acc_ref[0:QB, :].astype(BF16)
        rest = acc_ref[QB:KW, :]
        acc_ref[0:KW - QB, :] = rest
        acc_ref[KW - QB:KW, :] = jnp.zeros((QB, 2 * DA), F32)

        @pl.when(i == nb + flush - 1)
        def _():
            pltpu.sync_copy(db_ref, db_out)

    last = nb - 1
    res, jres = _call(
        body, name="attn_bwd", grid=(nb + flush,),
        in_specs=[pl.BlockSpec((QB, DA), lambda i: (jnp.minimum(i, last), 0)),
                  pl.BlockSpec(memory_space=pltpu.VMEM),
                  pl.BlockSpec(memory_space=pltpu.VMEM),
                  pl.BlockSpec((QB, DA), lambda i: (jnp.minimum(i, last), 0))],
        args=[qkv, kvp, bias, datt],
        out_specs=[pl.BlockSpec((QB, DA), lambda i: (jnp.minimum(i, last), 0)),
                   pl.BlockSpec((QB, 2 * DA), lambda i: (i, 0)),
                   ANY],
        out_shape=[jax.ShapeDtypeStruct((t, DA), BF16),
                   jax.ShapeDtypeStruct((t + KPAD, 2 * DA), BF16),
                   jax.ShapeDtypeStruct((NH * QB, KW), F32)],
        scratch=[pltpu.VMEM((KW, 2 * DA), F32), pltpu.VMEM((NH * QB, KW), F32)], sem=("arbitrary",), job=job)
    return res, jres


def _glu(c, gb):
    cb = c + gb
    return cb[:, :DC] * _sigmoid(cb[:, DC:])


def _ln_swish(pre, g, b):
    mu = jnp.mean(pre, axis=-1, keepdims=True)
    xc = pre - mu
    r = lax.rsqrt(jnp.mean(xc * xc, axis=-1, keepdims=True) + EPS)
    xhat = xc * r
    y = xhat * g + b
    return xhat, r, y


RT = 32


def _shifted_copies(src_ref, sh_ref, rows):
    for b in range(1, 8):
        sh_ref[b - 1, :, :] = src_ref[pl.ds(b, rows), :]


def _tap(src_ref, sh_ref, off, r0, rows=RT):
    a, b = divmod(off, 8)
    ref = src_ref if b == 0 else sh_ref.at[b - 1]
    if isinstance(r0, int):
        return ref[r0 + 8 * a:r0 + 8 * a + rows, :]
    return ref[pl.ds(pl.multiple_of(r0 + 8 * a, 8), rows), :]


def _conv_fwd(cin, glu_b, dw_w, dw_b, ln_g, ln_b, tm=TM):
    t = cin.shape[0]
    hb = tm // HALO

    def body(c_ref, h_ref, gb_ref, w_ref, wb_ref, g_ref, b_ref, cs_ref, pre_ref, ext_ref, sh_ref):
        i = pl.program_id(0)
        halo = _glu(h_ref[...], gb_ref[...])
        ext_ref[0:HALO, :] = jnp.where(i > 0, halo, jnp.zeros_like(halo))
        ext_ref[HALO:HALO + tm, :] = _glu(c_ref[...], gb_ref[...])
        ext_ref[HALO + tm:HALO + tm + 8, :] = jnp.zeros((8, DC), F32)
        _shifted_copies(ext_ref, sh_ref, HALO + tm)

        def tile(rt, carry):
            r0 = pl.multiple_of(rt * RT, RT)
            acc = jnp.zeros((RT, DC), F32) + wb_ref[...]
            for j in range(CW):
                acc = acc + w_ref[j:j + 1, :] * _tap(ext_ref, sh_ref, HALO - (CW - 1) + j, r0)
            pre_ref[pl.ds(r0, RT), :] = acc
            return carry

        lax.fori_loop(0, tm // RT, tile, 0, unroll=2)
        _, _, y = _ln_swish(pre_ref[...], g_ref[...], b_ref[...])
        cs_ref[...] = (y * _sigmoid(y)).astype(BF16)

    vec = lambda n: pl.BlockSpec((1, n), lambda i: (0, 0))
    return pl.pallas_call(
        body,
        name="conv_fwd",
        grid=(t // tm,),
        in_specs=[pl.BlockSpec((tm, 2 * DC), lambda i: (i, 0)),
                  pl.BlockSpec((HALO, 2 * DC), lambda i: (jnp.maximum(i * hb - 1, 0), 0)),
                  vec(2 * DC), pl.BlockSpec((CW, DC), lambda i: (0, 0)), vec(DC), vec(DC), vec(DC)],
        out_specs=[pl.BlockSpec((tm, DC), lambda i: (i, 0)), pl.BlockSpec((tm, DC), lambda i: (i, 0))],
        out_shape=[jax.ShapeDtypeStruct((t, DC), BF16), jax.ShapeDtypeStruct((t, DC), F32)],
        scratch_shapes=[pltpu.VMEM((HALO + tm + 8, DC), F32), pltpu.VMEM((7, HALO + tm, DC), F32)],
        compiler_params=_params(("arbitrary",)),
    )(cin, cin, glu_b, dw_w, dw_b, ln_g, ln_b)


def _conv_bwd(dcs, pre, cin, glu_b, dw_w, ln_g, ln_b, tm=TM):
    t = cin.shape[0]
    hb = tm // HALO
    steps = t // tm
    nhb = t // HALO

    def dpre_of(dcs_v, pre_v, g, b):
        xhat, r, y = _ln_swish(pre_v, g, b)
        sg = _sigmoid(y)
        dy = dcs_v * (sg * (1.0 + y * (1.0 - sg)))
        dxh = dy * g
        dpre = r * (dxh - jnp.mean(dxh, axis=-1, keepdims=True)
                    - xhat * jnp.mean(dxh * xhat, axis=-1, keepdims=True))
        return dpre, dy * xhat, dy

    def body(dcs_ref, dcsn_ref, pre_ref, pren_ref, c_ref, ch_ref, gb_ref, w_ref, g_ref, b_ref,
             dc_ref, dgb_ref, dw_ref, dwb_ref, dg_ref, db_ref,
             gext_ref, dext_ref, shg_ref, shd_ref, a_gb, a_w, a_wb, a_g, a_b):
        i = pl.program_id(0)

        @pl.when(i == 0)
        def _():
            for a in (a_gb, a_w, a_wb, a_g, a_b):
                a[...] = jnp.zeros_like(a)

        fold = lambda v: v.reshape(v.shape[0] // 8, 8, v.shape[-1]).sum(axis=0)
        g, b = g_ref[...], b_ref[...]
        dpre, dg_t, db_t = dpre_of(dcs_ref[...], pre_ref[...], g, b)
        dpre_n, _, _ = dpre_of(dcsn_ref[...], pren_ref[...], g, b)
        dext_ref[0:tm, :] = dpre
        dext_ref[tm:tm + HALO, :] = jnp.where(i < steps - 1, dpre_n, jnp.zeros_like(dpre_n))
        dext_ref[tm + HALO:tm + HALO + 8, :] = jnp.zeros((8, DC), F32)
        a_wb[...] += fold(dpre)
        a_g[...] += fold(dg_t)
        a_b[...] += fold(db_t)
        halo = _glu(ch_ref[...], gb_ref[...])
        gext_ref[0:HALO, :] = jnp.where(i > 0, halo, jnp.zeros_like(halo))
        gext_ref[HALO:HALO + tm, :] = _glu(c_ref[...], gb_ref[...])
        gext_ref[HALO + tm:HALO + tm + 8, :] = jnp.zeros((8, DC), F32)
        _shifted_copies(gext_ref, shg_ref, HALO + tm)
        _shifted_copies(dext_ref, shd_ref, HALO + tm)

        for j in range(CW):
            a_w[8 * j:8 * j + 8, :] += fold(dext_ref[0:tm, :] * _tap(gext_ref, shg_ref, HALO - (CW - 1) + j, 0, tm))

        def tile(rt, carry):
            r0 = pl.multiple_of(rt * RT, RT)
            dglu = jnp.zeros((RT, DC), F32)
            for j in range(CW):
                dglu = dglu + w_ref[j:j + 1, :] * _tap(dext_ref, shd_ref, CW - 1 - j, r0)
            gext_ref[pl.ds(r0, RT), :] = dglu
            return carry

        lax.fori_loop(0, tm // RT, tile, 0, unroll=2)
        dglu = gext_ref[0:tm, :]
        cb = c_ref[...] + gb_ref[...]
        sg = _sigmoid(cb[:, DC:])
        dc = jnp.concatenate([dglu * sg, dglu * cb[:, :DC] * sg * (1.0 - sg)], axis=1)
        dc_ref[...] = dc.astype(BF16)
        a_gb[...] += fold(dc)

        @pl.when(i == steps - 1)
        def _():
            dgb_ref[...] = a_gb[...].sum(axis=0, keepdims=True)
            dw_ref[...] = a_w[...]
            dwb_ref[...] = a_wb[...].sum(axis=0, keepdims=True)
            dg_ref[...] = a_g[...].sum(axis=0, keepdims=True)
            db_ref[...] = a_b[...].sum(axis=0, keepdims=True)

    vec = lambda n: pl.BlockSpec((1, n), lambda i: (0, 0))
    nxt = lambda i: (jnp.minimum((i + 1) * hb, nhb - 1), 0)
    prv = lambda i: (jnp.maximum(i * hb - 1, 0), 0)
    return pl.pallas_call(
        body,
        name="conv_bwd",
        grid=(steps,),
        in_specs=[pl.BlockSpec((tm, DC), lambda i: (i, 0)), pl.BlockSpec((HALO, DC), nxt),
                  pl.BlockSpec((tm, DC), lambda i: (i, 0)), pl.BlockSpec((HALO, DC), nxt),
                  pl.BlockSpec((tm, 2 * DC), lambda i: (i, 0)), pl.BlockSpec((HALO, 2 * DC), prv),
                  vec(2 * DC), pl.BlockSpec((CW, DC), lambda i: (0, 0)), vec(DC), vec(DC)],
        out_specs=[pl.BlockSpec((tm, 2 * DC), lambda i: (i, 0)), vec(2 * DC),
                   pl.BlockSpec((CW * 8, DC), lambda i: (0, 0)), vec(DC), vec(DC), vec(DC)],
        out_shape=[jax.ShapeDtypeStruct((t, 2 * DC), BF16), jax.ShapeDtypeStruct((1, 2 * DC), F32),
                   jax.ShapeDtypeStruct((CW * 8, DC), F32), jax.ShapeDtypeStruct((1, DC), F32),
                   jax.ShapeDtypeStruct((1, DC), F32), jax.ShapeDtypeStruct((1, DC), F32)],
        scratch_shapes=[pltpu.VMEM((HALO + tm + 8, DC), F32), pltpu.VMEM((tm + HALO + 8, DC), F32),
                        pltpu.VMEM((7, HALO + tm, DC), F32), pltpu.VMEM((7, HALO + tm, DC), F32),
                        pltpu.VMEM((8, 2 * DC), F32), pltpu.VMEM((CW * 8, DC), F32),
                        pltpu.VMEM((8, DC), F32), pltpu.VMEM((8, DC), F32), pltpu.VMEM((8, DC), F32)],
        compiler_params=_params(("arbitrary",)),
    )(dcs, dcs, pre, pre, cin, cin, glu_b, dw_w, ln_g, ln_b)


def _place():
    x, y, c = lax.axis_index("x"), lax.axis_index("y"), lax.axis_index("c")
    return x, y, c


def _peers(x, y, c):
    out = []
    for k in range(1, NDEV):
        fx, fy, fc = (k >> 2) & 1, (k >> 1) & 1, k & 1
        px = 1 - x if fx else x
        py = 1 - y if fy else y
        pc = 1 - c if fc else c
        out.append((px, py, pc))
    return out


def _job_out_shapes(job):
    kind, arrays = job
    if kind == "gather":
        return [jax.ShapeDtypeStruct((a.shape[0], NDEV) + a.shape[1:], a.dtype) for a in arrays]
    return [jax.ShapeDtypeStruct((NDEV, a.shape[0]) + a.shape[2:], a.dtype) for a in arrays]


def _job_scratch(job):
    n = len(job[1])
    return [pltpu.SemaphoreType.DMA((n, NDEV - 1)), pltpu.SemaphoreType.DMA((n, NDEV - 1)),
            pltpu.SemaphoreType.DMA((n,))]


def _gather_parts(ins, outs, send_sems, recv_sems, local_sems):
    x, y, c = _place()
    me, sib = (x, y, c), (x, y, 1 - c)
    chips = [(1 - x, y), (x, 1 - y), (1 - x, 1 - y)]

    def copy(a, k, block, to, src=None):
        px, py, pc = block
        dst = outs[a].at[:, 4 * px + 2 * py + pc]
        return pltpu.make_async_remote_copy(
            src_ref=dst if src is None else src, dst_ref=dst,
            send_sem=send_sems.at[a, k], recv_sem=recv_sems.at[a, k], device_id=to, device_id_type=MESH)

    n = len(ins)
    local = [pltpu.make_async_copy(ins[a], outs[a].at[:, 4 * x + 2 * y + c], local_sems.at[a]) for a in range(n)]
    first = [[copy(a, 0, me, sib, src=ins[a])] + [copy(a, 1 + j, me, (*chip, c), src=ins[a])
                                                   for j, chip in enumerate(chips)] for a in range(n)]

    def start():
        for a in range(n):
            local[a].start()
            for cp in first[a]:
                cp.start()

    def relay():
        for j, chip in enumerate(chips):
            for a in range(n):
                copy(a, 1 + j, (*chip, c), me).wait_recv()
                copy(a, 4 + j, (*chip, c), sib).start()

    def finish():
        for a in range(n):
            copy(a, 0, sib, me).wait_recv()
            for j, chip in enumerate(chips):
                copy(a, 4 + j, (*chip, 1 - c), me).wait_recv()
        for a in range(n):
            for cp in first[a]:
                cp.wait_send()
            local[a].wait()
            for j, chip in enumerate(chips):
                copy(a, 4 + j, (*chip, c), sib).wait_send()

    return start, relay, finish


def _exchange_parts(ins, outs, send_sems, recv_sems, local_sems):
    x, y, c = _place()
    me = 4 * x + 2 * y + c
    n = len(ins)
    peers = _peers(x, y, c)
    local = [pltpu.make_async_copy(ins[a].at[:, me], outs[a].at[me], local_sems.at[a]) for a in range(n)]

    def copy(a, k):
        px, py, pc = peers[k]
        return pltpu.make_async_remote_copy(
            src_ref=ins[a].at[:, 4 * px + 2 * py + pc], dst_ref=outs[a].at[me],
            send_sem=send_sems.at[a, k], recv_sem=recv_sems.at[a, k], device_id=peers[k], device_id_type=MESH)

    def arrival(a, k):
        px, py, pc = peers[k]
        return pltpu.make_async_remote_copy(
            src_ref=ins[a].at[:, me], dst_ref=outs[a].at[4 * px + 2 * py + pc],
            send_sem=send_sems.at[a, k], recv_sem=recv_sems.at[a, k], device_id=peers[k], device_id_type=MESH)

    def start():
        for a in range(n):
            local[a].start()
            for k in range(NDEV - 1):
                copy(a, k).start()

    def finish():
        for a in range(n):
            for k in range(NDEV - 1):
                arrival(a, k).wait_recv()
        for a in range(n):
            for k in range(NDEV - 1):
                copy(a, k).wait_send()
            local[a].wait()

    return start, None, finish


def _call(body, *, name, grid, in_specs, args, out_specs, out_shape, scratch=(), sem=None, aliases=None, job=None):
    aliases = dict(aliases or {})
    if job is None:
        res = pl.pallas_call(
            body, name=name, grid=grid, in_specs=list(in_specs), out_specs=list(out_specs),
            out_shape=list(out_shape), scratch_shapes=list(scratch), input_output_aliases=aliases,
            compiler_params=_params(sem))(*args)
        return list(res), []
    kind, arrays = job
    n_in, n_out, n_scr, nj = len(args), len(out_shape), len(scratch), len(arrays)

    def wrapped(*refs):
        ins = refs[:n_in]
        jin = refs[n_in:n_in + nj]
        o0 = n_in + nj
        outs = refs[o0:o0 + n_out]
        jout = refs[o0 + n_out:o0 + n_out + nj]
        s0 = o0 + n_out + nj
        scr = refs[s0:s0 + n_scr]
        sems = refs[s0 + n_scr:]
        parts = _gather_parts if kind == "gather" else _exchange_parts
        start, relay, finish = parts(jin, jout, *sems)
        if not grid:
            start()
            body(*ins, *outs, *scr)
            if relay is not None:
                relay()
            finish()
            return
        first = last = None
        step, steps = 0, 1
        for d, g in enumerate(grid):
            f, l = pl.program_id(d) == 0, pl.program_id(d) == g - 1
            first = f if first is None else jnp.logical_and(first, f)
            last = l if last is None else jnp.logical_and(last, l)
            step, steps = step * g + pl.program_id(d), steps * g
        pl.when(first)(start)
        body(*ins, *outs, *scr)
        if relay is not None:
            pl.when(step == (7 * steps) // 8)(relay)
        pl.when(last)(finish)

    res = pl.pallas_call(
        wrapped, name=name, grid=grid, in_specs=list(in_specs) + [ANY] * nj,
        out_specs=list(out_specs) + [ANY] * nj, out_shape=list(out_shape) + _job_out_shapes(job),
        scratch_shapes=list(scratch) + _job_scratch(job), input_output_aliases=aliases,
        compiler_params=pltpu.CompilerParams(
            dimension_semantics=None if not grid else ("arbitrary",) * len(grid),
            vmem_limit_bytes=VMEM_LIMIT, has_side_effects=True))(*args, *arrays)
    return list(res[:n_out]), list(res[n_out:])


def _comm_only(name, job):
    return _call(lambda: None, name=name, grid=(), in_specs=[], args=[], out_specs=[], out_shape=[], job=job)[1]


def _sum_devices(name, parts, tr):
    _, r, c = parts.shape

    def body(p_ref, o_ref):
        acc = p_ref[0].astype(F32)
        for d in range(1, NDEV):
            acc = acc + p_ref[d].astype(F32)
        o_ref[...] = acc

    return pl.pallas_call(
        body,
        name=name,
        grid=(r // tr,),
        in_specs=[pl.BlockSpec((NDEV, tr, c), lambda i: (0, i, 0))],
        out_specs=pl.BlockSpec((tr, c), lambda i: (i, 0)),
        out_shape=jax.ShapeDtypeStruct((r, c), F32),
        compiler_params=_params(("parallel",)),
    )(parts)


def _adamw(name, w, g, m, v, tr=None):
    r, c = w.shape
    tr = r if tr is None else tr

    def body(w_ref, g_ref, m_ref, v_ref, d_ref, nm_ref, nv_ref):
        gv = g_ref[...]
        nm = ADAM_B1 * m_ref[...] + (1.0 - ADAM_B1) * gv
        nv = ADAM_B2 * v_ref[...] + (1.0 - ADAM_B2) * (gv * gv)
        m_hat = nm / (1.0 - ADAM_B1 ** ADAM_STEP)
        v_hat = nv / (1.0 - ADAM_B2 ** ADAM_STEP)
        d_ref[...] = -ADAM_LR * (m_hat / (jnp.sqrt(v_hat) + ADAM_EPS) + ADAM_WD * w_ref[...])
        nm_ref[...] = nm
        nv_ref[...] = nv

    spec = pl.BlockSpec((tr, c), lambda i: (i, 0))
    return pl.pallas_call(
        body,
        name=name,
        grid=(r // tr,),
        in_specs=[spec] * 4,
        out_specs=[spec] * 3,
        out_shape=[jax.ShapeDtypeStruct((r, c), F32)] * 3,
        compiler_params=_params(("parallel",)),
    )(w, g, m, v)


SMALL = ["ffn1_norm_pre", "ffn1_norm_post", "mix_norm_pre", "gate_bias", "rel_table", "conv_glu_bias",
         "conv_dw_b", "conv_ln_g", "conv_ln_b", "mix_norm_post", "ffn2_norm_pre", "ffn2_norm_post"]
SMALL_ROWS = 24
DW_ROWS = 32


LOSS_ROW = 20
RELP = 384
SMALL_LAYOUT = {}
_r = 0
for _name, _n in zip(SMALL, [D, D, D, 2 * D, None, 2 * DC, DC, DC, DC, D, D, D]):
    if _n is None:
        SMALL_LAYOUT[_name] = (_r, NH, RELP)
        _r += NH
    else:
        SMALL_LAYOUT[_name] = (_r, -(-_n // D), min(_n, D))
        _r += -(-_n // D)
assert _r == LOSS_ROW


def _small_pieces(name):
    r0, nr, nc = SMALL_LAYOUT[name]
    if name == "rel_table":
        return [(slice(r0, r0 + NH), slice(0, nc), slice(0, NH), slice(0, nc))]
    return [(slice(r0 + k, r0 + k + 1), slice(0, nc), slice(0, 1), slice(k * nc, (k + 1) * nc)) for k in range(nr)]


def _pack_small(vals, loss_row):
    def body(*refs):
        o = refs[-1]
        o[...] = jnp.zeros_like(o)
        for ref, name in zip(refs, SMALL):
            for prow, pcol, arow, acol in _small_pieces(name):
                o[prow, pcol] = ref[arow, acol]
        o[LOSS_ROW:LOSS_ROW + 1, :] = refs[len(SMALL)][...]

    vm = pl.BlockSpec(memory_space=pltpu.VMEM)
    return pl.pallas_call(body, name="pack_small", out_shape=jax.ShapeDtypeStruct((SMALL_ROWS, D), F32),
                          in_specs=[vm] * (len(SMALL) + 1), out_specs=vm,
                          compiler_params=_params())(*[vals[n] for n in SMALL], loss_row)


def _small_update(parts, w, m, v):
    ns = len(SMALL)

    def body(p_ref, *refs):
        ins, outs = refs[:3 * ns], refs[3 * ns:]

        def total(prow, pcol):
            g = p_ref[0, prow, pcol]
            for d in range(1, NDEV):
                g = g + p_ref[d, prow, pcol]
            return g

        for q, name in enumerate(SMALL):
            w_ref, m_ref, v_ref = ins[3 * q:3 * q + 3]
            o_g, o_d, o_m, o_v = outs[4 * q:4 * q + 4]
            for prow, pcol, arow, acol in _small_pieces(name):
                g = total(prow, pcol)
                nm = ADAM_B1 * m_ref[arow, acol] + (1.0 - ADAM_B1) * g
                nv = ADAM_B2 * v_ref[arow, acol] + (1.0 - ADAM_B2) * (g * g)
                m_hat = nm / (1.0 - ADAM_B1 ** ADAM_STEP)
                v_hat = nv / (1.0 - ADAM_B2 ** ADAM_STEP)
                o_g[arow, acol] = g
                o_d[arow, acol] = -ADAM_LR * (m_hat / (jnp.sqrt(v_hat) + ADAM_EPS) + ADAM_WD * w_ref[arow, acol])
                o_m[arow, acol] = nm
                o_v[arow, acol] = nv
        outs[4 * ns][...] = total(slice(LOSS_ROW, LOSS_ROW + 1), slice(None))

    args, out_shape = [], []
    for name in SMALL:
        args += [w[name], m[name], v[name]]
        out_shape += [jax.ShapeDtypeStruct(w[name].shape, F32)] * 4
    out_shape.append(jax.ShapeDtypeStruct((1, D), F32))
    vm = pl.BlockSpec(memory_space=pltpu.VMEM)
    res = pl.pallas_call(body, name="small_update", out_shape=out_shape, in_specs=[vm] * (1 + 3 * ns),
                         out_specs=[vm] * len(out_shape), compiler_params=_params())(parts, *args)
    return {name: res[4 * q:4 * q + 4] for q, name in enumerate(SMALL)}, res[-1]


def kernel(x, ffn1_norm_pre, ffn1_w_gate, ffn1_w_up, ffn1_w_down, ffn1_norm_post, mix_norm_pre, w_in, gate_bias, rel_table, w_attn_out, conv_glu_bias, conv_dw_w, conv_dw_b, conv_ln_g, conv_ln_b, conv_w_out, w_out, mix_norm_post, ffn2_norm_pre, ffn2_w_gate, ffn2_w_up, ffn2_w_down, ffn2_norm_post, loss_target, m_ffn1_norm_pre, m_ffn1_w_gate, m_ffn1_w_up, m_ffn1_w_down, m_ffn1_norm_post, m_mix_norm_pre, m_w_in, m_gate_bias, m_rel_table, m_w_attn_out, m_conv_glu_bias, m_conv_dw_w, m_conv_dw_b, m_conv_ln_g, m_conv_ln_b, m_conv_w_out, m_w_out, m_mix_norm_post, m_ffn2_norm_pre, m_ffn2_w_gate, m_ffn2_w_up, m_ffn2_w_down, m_ffn2_norm_post, v_ffn1_norm_pre, v_ffn1_w_gate, v_ffn1_w_up, v_ffn1_w_down, v_ffn1_norm_post, v_mix_norm_pre, v_w_in, v_gate_bias, v_rel_table, v_w_attn_out, v_conv_glu_bias, v_conv_dw_w, v_conv_dw_b, v_conv_ln_g, v_conv_ln_b, v_conv_w_out, v_w_out, v_mix_norm_post, v_ffn2_norm_pre, v_ffn2_w_gate, v_ffn2_w_up, v_ffn2_w_down, v_ffn2_norm_post):
    return _step(dict(locals()))


WEIGHTS = ["ffn1_norm_pre", "ffn1_w_gate", "ffn1_w_up", "ffn1_w_down", "ffn1_norm_post", "mix_norm_pre", "w_in",
           "gate_bias", "rel_table", "w_attn_out", "conv_glu_bias", "conv_dw_w", "conv_dw_b", "conv_ln_g",
           "conv_ln_b", "conv_w_out", "w_out", "mix_norm_post", "ffn2_norm_pre", "ffn2_w_gate", "ffn2_w_up",
           "ffn2_w_down", "ffn2_norm_post"]
FS = FF // NDEV
PS = (3 * DA + 2 * DC + 2 * D) // NDEV
OS = D // NDEV


def _local_step(xs, target, w, rel_table):
    t = xs.shape[0]
    vec = lambda n: w[n].reshape(1, -1)
    g_pre1, g_post1, g_mix, g_mixp = vec("ffn1_norm_pre"), vec("ffn1_norm_post"), vec("mix_norm_pre"), vec("mix_norm_post")
    g_pre2, g_post2 = vec("ffn2_norm_pre"), vec("ffn2_norm_post")
    gate_b, glu_b = vec("gate_bias"), vec("conv_glu_bias")
    dw_b, ln_g, ln_b = vec("conv_dw_b"), vec("conv_ln_g"), vec("conv_ln_b")

    tr = lambda a: jnp.transpose(a[0]).astype(BF16)
    sh_gu1 = jnp.stack([tr(w["ffn1_w_gate"]), tr(w["ffn1_w_up"])])
    sh_mid = [w["ffn1_w_down"].astype(BF16), tr(w["w_in"])[None], w["w_out"].astype(BF16),
              jnp.stack([tr(w["w_attn_out"]), tr(w["conv_w_out"])]),
              jnp.pad(w["conv_dw_w"][0, :, 0, :], ((0, DW_ROWS - CW), (0, 0)))[None]]
    sh_2 = jnp.stack([tr(w["ffn2_w_gate"]), tr(w["ffn2_w_up"]), w["ffn2_w_down"][0].astype(BF16)])

    (n1,), (w_gu1,) = _rowwise("pre1", lambda xv, g: ((_rms(xv)[0] * g),), [xs], [g_pre1], [(D, BF16)],
                               job=("gather", [sh_gu1]))
    w_gu1 = w_gu1.reshape(2, FF, D)
    (a1, b1, s1), (w_d1, wb, wc, wd, we) = _ffn_up("ffn1_up", n1, w_gu1, 0, job=("gather", sh_mid))
    w_d1, wb, wc, wd = w_d1.reshape(1, FF, D), wb.reshape(NDEV * PS, D), wc.reshape(D, D), wd.reshape(2, D, DA)
    dw_full = jnp.transpose(we[0], (1, 0, 2)).reshape(DW_ROWS, DC)[:CW]
    row = lambda i, j, q: (i, 0)
    top = lambda i, j, q: (0, 0)
    tmr = min(512, t)

    def post1(fv, xv, gp, gm):
        h = xv + 0.5 * (_rms(fv)[0] * gp)
        return fv, h, _rms(h)[0] * gm

    f1, h1, u = _mm_rows("ffn1_down", t, tmr, [(s1, (tmr, FF), row)], [(w_d1, (None, FF, D), lambda i, j, q: (0, 0, 0))],
                         [[(0, 0)]], NN, post1, [xs], [g_post1, g_mix], [(D, F32), (D, F32), (D, BF16)])

    def proj_epi(accs, extras):
        cat = lambda parts: jnp.concatenate(parts, axis=1)
        return (accs[0], jnp.where(pl.program_id(0) > 0, cat(accs[1:3]), 0.0), cat(accs[3:5]), cat(accs[5:9]))

    prev = lambda i, j, q: (jnp.maximum(i - 1, 0), 0)
    qkv, kvp, cin, gg = _mm(
        "proj", (t // KPAD + 1, 1, 1), [(u, (KPAD, D), prev)],
        [(wb, (DA, D), (lambda i, j, q, blk=blk: (blk, 0))) for blk in range(NDEV * PS // DA)],
        [[(0, blk)] for blk in range(NDEV * PS // DA)], NT, proj_epi,
        [((t, DA), BF16, (KPAD, DA), prev), ((t + KPAD, 2 * DA), BF16, (KPAD, 2 * DA), lambda i, j, q: (i, 0)),
         ((t, 2 * DC), F32, (KPAD, 2 * DC), prev), ((t, 2 * D), BF16, (KPAD, 2 * D), prev)])

    bias = _relbias_fwd(jnp.pad(rel_table[0], ((0, 0), (0, 384 - NREL))))
    att, (w_2,) = _attn_fwd(qkv, kvp, bias, job=("gather", [sh_2]))
    w_2 = w_2.reshape(3, FF, D)
    cs, pre = _conv_fwd(cin, glu_b, dw_full, dw_b, ln_g, ln_b)
    def merge(yav, ybv, gv, gb):
        gates = _sigmoid(gv + gb)
        return yav, ybv, gates[:, :D] * yav + gates[:, D:] * ybv

    wd_block = lambda p: (wd, (None, D, DC), lambda i, j, q: (p, 0, 0))
    ya, yb, merged = _mm_rows("branch_out", t, tmr, [(att, (tmr, DA), row), (cs, (tmr, DC), row)],
                              [wd_block(0), wd_block(1)], [[(0, 0)], [(1, 1)]], NT, merge, [gg], [gate_b],
                              [(D, BF16), (D, BF16), (D, BF16)])

    def postm(mv, hv, gp, g2):
        h = hv + _rms(mv)[0] * gp
        return mv, h, _rms(h)[0] * g2

    mm_, h2, n2 = _mm_rows("mix_out", t, tmr, [(merged, (tmr, D), row)], [(wc, (D, D), top)], [[(0, 0)]], NN,
                           postm, [h1], [g_mixp, g_pre2], [(D, F32), (D, F32), (D, BF16)])
    a2, b2, s2 = _ffn_up("ffn2_up", n2, w_2, 0)

    def post2(fv, hv, tv, gp):
        fh, r = _rms(fv)
        yv = hv + 0.5 * (fh * gp)
        err = yv - tv
        dy = err * (1.0 / D)
        df, dg = _rms_bwd(fh, r, gp, 0.5 * dy)
        return dy, df, (0.5 / D) * (err * err), dg

    dy, df2, loss_row, d_post2 = _mm_rows(
        "ffn2_down", t, tmr, [(s2, (tmr, FF), row)], [(w_2, (None, FF, D), lambda i, j, q: (2, 0, 0))], [[(0, 0)]],
        NN, post2, [h2, target], [g_post2], [(D, F32), (D, BF16)], [D, D])

    tmw = 1408
    nfi = FF // tmw
    tkf = min(2048, t)
    tkp = min(1024, t)

    def wgrad_down(name, s, df, shape, part, carry_buf, job=None):
        return _wgrad(name, (s, (tkf, tmw), lambda i, j, q: (q, i)), df,
                      (None, tmw, D), lambda i, j, q: (part, i, 0), nfi, carry_buf, shape, tk=tkf, job=job)

    def wgrad_gate_up(name, dab, nrm, shape, carry_buf, job=None):
        return _wgrad(name, (dab, (None, tkf, tmw), lambda i, j, q: (i // nfi, q, i % nfi)), nrm,
                      (None, tmw, D), lambda i, j, q: (i // nfi, i % nfi, 0), 2 * nfi, carry_buf, shape,
                      tk=tkf, job=job)

    g_2 = wgrad_down("ffn2_wgrad_d", s2, df2, (3, FF, D), 2, None)
    dab2 = _ffn_bwd_act("ffn2_bwd_act", df2, w_2, 2, a2, b2)
    g_2 = wgrad_gate_up("ffn2_wgrad_gu", dab2, n2, (3, FF, D), g_2)

    def bwd_pre2(dnv, hv, dyv, mv, g2, gp):
        hh, r = _rms(hv)
        dx, dg2 = _rms_bwd(hh, r, g2, dnv)
        dh = dyv + dx
        mh, rm = _rms(mv)
        dm, dgp = _rms_bwd(mh, rm, gp, dh)
        return dh, dm, dg2, dgp

    tmb = 256
    dh2, dm, d_pre2, d_mixp = _mm_rows(
        "ffn2_bwd_in", t, tmb,
        [(dab2, (None, tmb, FF), lambda i, j, q: (0, i, 0)), (dab2, (None, tmb, FF), lambda i, j, q: (1, i, 0))],
        [(w_2, (None, FF, D), lambda i, j, q: (0, 0, 0)), (w_2, (None, FF, D), lambda i, j, q: (1, 0, 0))],
        [[(0, 0), (1, 1)]], NN, bwd_pre2, [h2, dy, mm_], [g_pre2, g_mixp], [(D, F32), (D, BF16)], [D, D])

    def merge_bwd(dmv, yav, ybv, gv, gb, wao, wco):
        gates = _sigmoid(gv + gb)
        ga, gbb = gates[:, :D], gates[:, D:]
        dgg = jnp.concatenate([dmv * yav * ga * (1.0 - ga), dmv * ybv * gbb * (1.0 - gbb)], axis=1)
        dyav, dybv = (dmv * ga).astype(BF16), (dmv * gbb).astype(BF16)
        return (dyav, dybv, dgg, jnp.dot(dyav, wao, preferred_element_type=F32),
                jnp.dot(dybv, wco, preferred_element_type=F32), dgg)

    dya, dyb, dgg, datt, dcs, d_gate_b = _mm_rows(
        "mix_out_bwd", t, tmr, [(dm, (tmr, D), row)], [(wc, (D, D), top)], [[(0, 0)]], NT, merge_bwd,
        [ya, yb, gg], [gate_b, wd[0], wd[1]], [(D, BF16), (D, BF16), (2 * D, BF16), (DA, BF16), (DC, F32)], [2 * D])
    g_c = _wgrad("mix_out_wgrad", (merged, (tkp, D), lambda i, j, q: (q, 0)), dm,
                 (D, D), lambda i, j, q: (0, 0), 1, None, (D, D), tk=tkp)
    g_d = _wgrad("attn_out_wgrad", (dya, (tkp, D), lambda i, j, q: (q, 0)), att,
                 (None, D, DA), lambda i, j, q: (0, 0, 0), 1, None, (2, D, DA), tk=tkp)
    g_d = _wgrad("conv_out_wgrad", (dyb, (tkp, D), lambda i, j, q: (q, 0)), cs,
                 (None, D, DA), lambda i, j, q: (1, 0, 0), 1, g_d, (2, D, DA), tk=tkp)
    (dq, dkvp, dbias), (x_2, x_c, x_d) = _attn_bwd(
        qkv, kvp, bias, datt,
        job=("exchange", [g_2.reshape(3, NDEV, FS, D), g_c.reshape(1, NDEV, OS, D), g_d.reshape(2, NDEV, OS, DA)]))
    d_rel = _relbias_bwd(dbias)
    dcin, d_glu_b, d_dw8, d_dw_b, d_ln_g, d_ln_b = _conv_bwd(dcs, pre, cin, glu_b, dw_full, ln_g, ln_b)
    g_dw = jnp.pad(d_dw8, ((0, 8 * (DW_ROWS - CW)), (0, 0)))
    g_dw = g_dw.reshape(DW_ROWS * 8, NDEV, DC // NDEV).transpose(1, 0, 2)

    top = lambda i, j, q: (0, 0)
    g_b = jnp.concatenate([
        _wgrad("proj_wgrad_q", (dq, (tkp, DA), lambda i, j, q: (q, 0)), u, (DA, D), top, 1, None, (DA, D), tk=tkp),
        _wgrad("proj_wgrad_kv", (dkvp, (KPAD, 2 * DA), lambda i, j, q: (q + 1, 0)), u, (2 * DA, D), top, 1, None,
               (2 * DA, D), tk=KPAD),
        _wgrad("proj_wgrad_c", (dcin, (tkp, 2 * DC), lambda i, j, q: (q, 0)), u, (2 * DC, D), top, 1, None,
               (2 * DC, D), tk=tkp),
        _wgrad("proj_wgrad_g", (dgg, (tkp, 2 * D), lambda i, j, q: (q, 0)), u, (2 * D, D), top, 1, None,
               (2 * D, D), tk=tkp)], axis=0)

    tmu = 256
    a_ops = [(dq, (tmu, DA), row),
             (dkvp, (tmu, 2 * DA), lambda i, j, q: (i + KPAD // tmu, 0)),
             (dcin, (tmu, 2 * DC), row),
             (dgg, (tmu, 2 * D), row)]
    b_ops = [(wb[:DA], (DA, D), top), (wb[DA:3 * DA], (2 * DA, D), top),
             (wb[3 * DA:3 * DA + 2 * DC], (2 * DC, D), top), (wb[3 * DA + 2 * DC:], (2 * D, D), top)]

    def bwd_mix(duv, hv, dhv, fv, gm, gp):
        hh, r = _rms(hv)
        dx, dgm = _rms_bwd(hh, r, gm, duv)
        dh = dhv + dx
        fh, rf = _rms(fv)
        df, dgp = _rms_bwd(fh, rf, gp, 0.5 * dh)
        return dh, df, dgm, dgp

    (dh1, df1, d_mix, d_post1), (x_b, x_dw) = _mm_rows(
        "proj_bwd", t, tmu, a_ops, b_ops, [[(0, 0), (1, 1), (2, 2), (3, 3)]], NN, bwd_mix, [h1, dh2, f1],
        [g_mix, g_post1], [(D, F32), (D, BF16)], [D, D],
        job=("exchange", [g_b.reshape(1, NDEV, PS, D), g_dw[None]]))
    g_d1 = wgrad_down("ffn1_wgrad_d", s1, df1, (1, FF, D), 0, None)
    dab1, (x_d1,) = _ffn_bwd_act("ffn1_bwd_act", df1, w_d1, 0, a1, b1,
                                 job=("exchange", [g_d1.reshape(1, NDEV, FS, D)]))

    def wgrad_half(name, p, job=None):
        return _wgrad(name, (dab1, (None, tkf, tmw), lambda i, j, q: (p, q, i)), n1,
                      (None, tmw, D), lambda i, j, q: (0, i, 0), nfi, None, (1, FF, D), tk=tkf, job=job)

    g_g1 = wgrad_half("ffn1_wgrad_g", 0)
    g_u1, (x_g1,) = wgrad_half("ffn1_wgrad_u", 1, job=("exchange", [g_g1.reshape(1, NDEV, FS, D)]))
    dn1, (x_u1,) = _ffn_bwd_in("ffn1_bwd_in", dab1, w_gu1, 0, job=("exchange", [g_u1.reshape(1, NDEV, FS, D)]))

    def bwd_pre1(xv, dnv, dhv, g1):
        xh, r = _rms(xv)
        dx, dg1 = _rms_bwd(xh, r, g1, dnv)
        return dhv + dx, dg1

    dx, d_pre1 = _rowwise("bwd_pre1", bwd_pre1, [xs, dn1, dh1], [g_pre1], [(D, F32)], [D])

    small_g = {"ffn1_norm_pre": d_pre1, "ffn1_norm_post": d_post1, "mix_norm_pre": d_mix, "gate_bias": d_gate_b,
               "rel_table": d_rel, "conv_glu_bias": d_glu_b, "conv_dw_b": d_dw_b, "conv_ln_g": d_ln_g,
               "conv_ln_b": d_ln_b, "mix_norm_post": d_mixp, "ffn2_norm_pre": d_pre2, "ffn2_norm_post": d_post2}
    return loss_row, dx, (x_g1, x_u1, x_d1, x_2, x_b, x_c, x_d, x_dw), small_g


def _step(args):
    names = WEIGHTS
    w = {n: args[n] for n in names}
    fs, ps, os_ = FS, PS, OS
    conv_dw_w = args["conv_dw_w"]
    loss_row, dx, (x_g1, x_u1, x_d1, x_2, x_b, x_c, x_d, x_dw), small_g = _local_step(
        args["x"][0], args["loss_target"][0], w, args["rel_table"])

    g_small = _pack_small(small_g, loss_row)
    (x_s,) = _comm_only("gather_small_grads", ("gather", [g_small[None]]))

    s_g1 = _sum_devices("sum_ffn1_g", x_g1.reshape(NDEV, fs, D), fs)
    s_u1 = _sum_devices("sum_ffn1_u", x_u1.reshape(NDEV, fs, D), fs)
    s_d1 = _sum_devices("sum_ffn1_d", x_d1.reshape(NDEV, fs, D), fs)
    s_2 = _sum_devices("sum_ffn2", x_2.reshape(NDEV, 3 * fs, D), fs).reshape(3, fs, D)
    s_b = _sum_devices("sum_proj", x_b.reshape(NDEV, ps, D), ps)
    s_c = _sum_devices("sum_mix", x_c.reshape(NDEV, os_, D), os_)
    s_d = _sum_devices("sum_out", x_d.reshape(NDEV, 2 * os_, DA), 2 * os_).reshape(2, os_, DA)
    s_dw = _sum_devices("sum_dw", x_dw.reshape(NDEV, DW_ROWS * 8, DC // NDEV), DW_ROWS * 8)

    grads = {
        "ffn1_w_gate": jnp.transpose(s_g1)[None], "ffn1_w_up": jnp.transpose(s_u1)[None], "ffn1_w_down": s_d1[None],
        "ffn2_w_gate": jnp.transpose(s_2[0])[None], "ffn2_w_up": jnp.transpose(s_2[1])[None], "ffn2_w_down": s_2[2][None],
        "w_in": jnp.transpose(s_b)[None], "w_out": s_c[None],
        "w_attn_out": jnp.transpose(s_d[0])[None], "conv_w_out": jnp.transpose(s_d[1])[None],
    }
    deltas, new_m, new_v = {}, {}, {}

    def flat2(a, n):
        return jnp.pad(a[0], ((0, 0), (0, RELP - NREL))) if n == "rel_table" else a.reshape(1, -1)

    small, loss_terms = _small_update(
        x_s.reshape(NDEV, SMALL_ROWS, D), {n: flat2(w[n], n) for n in SMALL},
        {n: flat2(args["m_" + n], n) for n in SMALL}, {n: flat2(args["v_" + n], n) for n in SMALL})
    for n in SMALL:
        vals = [a[:, :NREL] if n == "rel_table" else a for a in small[n]]
        grads[n], deltas[n], new_m[n], new_v[n] = [a.reshape(w[n].shape) for a in vals]

    big = ["ffn1_w_gate", "ffn1_w_up", "ffn1_w_down", "w_in", "w_attn_out", "conv_w_out", "w_out",
           "ffn2_w_gate", "ffn2_w_up", "ffn2_w_down"]
    for n in big:
        shp = w[n].shape
        two = lambda a: a.reshape(shp[1], shp[2])
        rows = shp[1]
        tr_ = rows // 2 if rows % 16 == 0 else rows
        d_, m_, v_ = _adamw("adamw_" + n, two(w[n]), two(grads[n]), two(args["m_" + n]), two(args["v_" + n]), tr_)
        deltas[n], new_m[n], new_v[n] = d_.reshape(shp), m_.reshape(shp), v_.reshape(shp)

    g_dw_own = _fold8("fold_dw", s_dw)[:CW]
    grads["conv_dw_w"] = g_dw_own.reshape(1, CW, 1, DC // NDEV)
    flat = lambda a: a.reshape(CW, DC // NDEV)
    d_, m_, v_ = _adamw("adamw_dw", flat(conv_dw_w), g_dw_own, flat(args["m_conv_dw_w"]), flat(args["v_conv_dw_w"]))
    shp = conv_dw_w.shape
    deltas["conv_dw_w"], new_m["conv_dw_w"], new_v["conv_dw_w"] = d_.reshape(shp), m_.reshape(shp), v_.reshape(shp)

    loss = jnp.sum(loss_terms)
    return (loss, dx[None], *[grads[n] for n in names], *[deltas[n] for n in names],
            *[new_m[n] for n in names], *[new_v[n] for n in names])


def _fold8(name, a):
    r8, c = a.shape

    def body(a_ref, o_ref):
        o_ref[...] = a_ref[...].reshape(r8 // 8, 8, c).sum(axis=1)

    return pl.pallas_call(
        body,
        name=name,
        out_shape=jax.ShapeDtypeStruct((r8 // 8, c), F32),
        in_specs=[pl.BlockSpec(memory_space=pltpu.VMEM)],
        out_specs=pl.BlockSpec(memory_space=pltpu.VMEM),
        compiler_params=_params(),
    )(a)
```

```python
import functools

import jax
import jax.numpy as jnp
from jax import lax
from jax.experimental import pallas as pl
from jax.experimental.pallas import tpu as pltpu

F32 = jnp.float32
BF16 = jnp.bfloat16

D = 1024
FF = 2816
DA = 512
DC = 512
NH = 8
CHUNK = 64
LEFT = 8
CW = 31
NREL = 257
EPS = 1e-6
NDEV = 8

QB = 4 * CHUNK
KW = LEFT * CHUNK + QB
KPAD = LEFT * CHUNK
RELW = KW + QB
HALO = 32

TM = 512
VMEM_LIMIT = 56 * 1024 * 1024

ADAM_LR, ADAM_B1, ADAM_B2, ADAM_EPS, ADAM_WD, ADAM_STEP = 0.001, 0.9, 0.999, 1e-08, 0.01, 10

NT = (((1,), (1,)), ((), ()))
NN = (((1,), (0,)), ((), ()))
TN = (((0,), (0,)), ((), ()))

MESH = pl.DeviceIdType.MESH
ANY = pl.BlockSpec(memory_space=pl.ANY)


def _params(sem=None, vmem=VMEM_LIMIT):
    return pltpu.CompilerParams(dimension_semantics=sem, vmem_limit_bytes=vmem)


def _sigmoid(x):
    return 0.5 * jnp.tanh(0.5 * x) + 0.5


def _mm(name, grid, a_ops, b_ops, groups, dims, epi, outs, extras=(), carry=None, job=None, params=(),
        partials=()):
    nk = grid[2]
    na, nb, ne, no, ng = len(a_ops), len(b_ops), len(extras), len(outs), len(groups)
    npar, npart = len(params), len(partials)
    nc = 0 if carry is None else 1

    def body(*refs):
        a_refs = refs[:na]
        b_refs = refs[na:na + nb]
        e_refs = refs[na + nb:na + nb + ne]
        p_refs = refs[na + nb + ne:na + nb + ne + npar]
        o0 = na + nb + ne + npar + nc
        o_refs = refs[o0:o0 + no]
        s_refs = refs[o0 + no:o0 + no + npart]
        acc_refs = refs[o0 + no + npart:o0 + no + npart + (ng if nk > 1 else 0)]
        part_refs = refs[len(refs) - npart:] if npart else ()
        k = pl.program_id(2)
        prods = []
        for grp in groups:
            p = None
            for ai, bi in grp:
                t = lax.dot_general(a_refs[ai][...], b_refs[bi][...], dims, preferred_element_type=F32)
                p = t if p is None else p + t
            prods.append(p)

        def finish(vals):
            res = epi(vals, [e[...] for e in e_refs] + [p[...] for p in p_refs])
            for o, r in zip(o_refs, res[:no]):
                o[...] = r.astype(o.dtype)
            if npart:
                i, j = pl.program_id(0), pl.program_id(1)

                @pl.when(jnp.logical_and(i == 0, j == 0))
                def _():
                    for acc in part_refs:
                        acc[...] = jnp.zeros_like(acc)

                for acc, r in zip(part_refs, res[no:]):
                    acc[...] += r.reshape(r.shape[0] // 8, 8, r.shape[-1]).sum(axis=0)

                @pl.when(jnp.logical_and(i == grid[0] - 1, j == grid[1] - 1))
                def _():
                    for s, acc in zip(s_refs, part_refs):
                        s[...] = acc[...].sum(axis=0, keepdims=True)

        if nk == 1:
            finish(prods)
        else:
            @pl.when(k == 0)
            def _():
                for acc, p in zip(acc_refs, prods):
                    acc[...] = p

            @pl.when(k > 0)
            def _():
                for acc, p in zip(acc_refs, prods):
                    acc[...] += p

            @pl.when(k == nk - 1)
            def _():
                finish([acc[...] for acc in acc_refs])

    in_specs = [pl.BlockSpec(blk, im) for _, blk, im in list(a_ops) + list(b_ops) + list(extras)]
    in_specs += [pl.BlockSpec(p.shape, lambda i, j, q: (0, 0)) for p in params]
    args = [arr for arr, _, _ in list(a_ops) + list(b_ops) + list(extras)] + list(params)
    aliases = {}
    if carry is not None:
        in_specs.append(ANY)
        args.append(carry[0])
        aliases = {len(args) - 1: carry[1]}
    scratch = []
    if nk > 1:
        for _ in range(ng):
            blk = tuple(b for b in outs[0][2] if b is not None)
            scratch.append(pltpu.VMEM(blk, F32))
    scratch += [pltpu.VMEM((8, c), F32) for c in partials]
    res, jres = _call(
        body, name=name, grid=grid, in_specs=in_specs, args=args,
        out_specs=[pl.BlockSpec(blk, im) for _, _, blk, im in outs]
        + [pl.BlockSpec((1, c), lambda i, j, q: (0, 0)) for c in partials],
        out_shape=[jax.ShapeDtypeStruct(shp, dt) for shp, dt, _, _ in outs]
        + [jax.ShapeDtypeStruct((1, c), F32) for c in partials],
        scratch=scratch, sem=("arbitrary",) * 3 if partials else ("parallel", "parallel", "arbitrary"),
        aliases=aliases, job=job)
    return res if job is None else (res, jres)


def _first(accs, extras):
    return (accs[0],)


def _mm_rows(name, t, tm, a_ops, b_ops, groups, dims, fn, extras, params, outs, partials=(), nk=1, job=None):
    ne = len(extras)

    def epi(accs, rest):
        return fn(*accs, *[r.astype(F32) for r in rest[:ne]], *rest[ne:])

    e_ops = [(arr, (tm, arr.shape[1]), lambda i, j, q: (i, 0)) for arr in extras]
    o_ops = [((t, c), dt, (tm, c), lambda i, j, q: (i, 0)) for c, dt in outs]
    return _mm(name, (t // tm, 1, nk), a_ops, b_ops, groups, dims, epi, o_ops, e_ops, job=job, params=params,
               partials=partials)


def _mm_simple(name, a, b, dims, out_dtype, tm, tn, b_row0=0, b_rows=None):
    m, kk = a.shape
    tm = min(tm, m)
    if dims is NT:
        n = b.shape[0] if b_rows is None else b_rows
        assert b.shape[1] == kk and b_row0 % tn == 0
        b_op = (b, (tn, kk), lambda i, j, q: (j + b_row0 // tn, 0))
    else:
        assert b.shape[0] == kk
        n = b.shape[1]
        b_op = (b, (kk, tn), lambda i, j, q: (0, j))
    a_op = (a, (tm, kk), lambda i, j, q: (i, 0))
    out = ((m, n), out_dtype, (tm, tn), lambda i, j, q: (i, j))
    return _mm(name, (m // tm, n // tn, 1), [a_op], [b_op], [[(0, 0)]], dims, _first, [out])[0]


def _rowwise(name, fn, tiled, params, outs, partials=(), tm=TM, job=None):
    t = tiled[0].shape[0]
    steps = t // tm
    nt, npar, no, npart = len(tiled), len(params), len(outs), len(partials)

    def body(*refs):
        t_refs = refs[:nt]
        p_refs = refs[nt:nt + npar]
        o_refs = refs[nt + npar:nt + npar + no]
        s_refs = refs[nt + npar + no:nt + npar + no + npart]
        acc_refs = refs[nt + npar + no + npart:]
        i = pl.program_id(0)
        res = fn(*[r[...].astype(F32) for r in t_refs], *[r[...] for r in p_refs])
        for o, r in zip(o_refs, res[:no]):
            o[...] = r.astype(o.dtype)

        @pl.when(i == 0)
        def _():
            for acc in acc_refs:
                acc[...] = jnp.zeros_like(acc)

        for acc, r in zip(acc_refs, res[no:]):
            acc[...] += r.reshape(tm // 8, 8, r.shape[-1]).sum(axis=0)

        @pl.when(i == steps - 1)
        def _():
            for s, acc in zip(s_refs, acc_refs):
                s[...] = acc[...].sum(axis=0, keepdims=True)

    in_specs = [pl.BlockSpec((tm, a.shape[1]), lambda i: (i, 0)) for a in tiled]
    in_specs += [pl.BlockSpec(p.shape, lambda i: (0, 0)) for p in params]
    out_specs = [pl.BlockSpec((tm, c), lambda i: (i, 0)) for c, _ in outs]
    out_specs += [pl.BlockSpec((1, c), lambda i: (0, 0)) for c in partials]
    out_shape = [jax.ShapeDtypeStruct((t, c), dt) for c, dt in outs]
    out_shape += [jax.ShapeDtypeStruct((1, c), F32) for c in partials]
    res, jres = _call(body, name=name, grid=(steps,), in_specs=in_specs, args=[*tiled, *params], out_specs=out_specs,
                      out_shape=out_shape, scratch=[pltpu.VMEM((8, c), F32) for c in partials], sem=("arbitrary",),
                      job=job)
    return res if job is None else (res, jres)


def _rms(x):
    r = lax.rsqrt(jnp.mean(x * x, axis=-1, keepdims=True) + EPS)
    return x * r, r


def _rms_bwd(xhat, r, g, dy):
    dxh = dy * g
    dx = r * (dxh - xhat * jnp.mean(dxh * xhat, axis=-1, keepdims=True))
    return dx, dy * xhat


def _ffn_up(name, n, wa, part, tm=512, tf=1408, job=None):
    t = n.shape[0]

    def epi(accs, extras):
        a, b = accs
        sg = _sigmoid(a)
        silu = a * sg
        return silu, b * (sg + silu * (1.0 - sg)), silu * b

    a_op = (n, (tm, D), lambda f, i, q: (i, 0))
    b_ops = [(wa, (None, tf, D), lambda f, i, q: (part, f, 0)),
             (wa, (None, tf, D), lambda f, i, q: (part + 1, f, 0))]
    outs = [((t, FF), BF16, (tm, tf), lambda f, i, q: (i, f))] * 3
    return _mm(name, (FF // tf, t // tm, 1), [a_op], b_ops, [[(0, 0)], [(0, 1)]], NT, epi, outs, job=job)


def _ffn_bwd_act(name, df, wa, part, a, b, tm=512, tf=1408, job=None):
    t = df.shape[0]

    def epi(accs, extras):
        ds = accs[0]
        return (jnp.stack([ds * extras[1].astype(F32), ds * extras[0].astype(F32)]),)

    a_op = (df, (tm, D), lambda f, i, q: (i, 0))
    b_op = (wa, (None, tf, D), lambda f, i, q: (part, f, 0))
    extras = [(a, (tm, tf), lambda f, i, q: (i, f)), (b, (tm, tf), lambda f, i, q: (i, f))]
    out = ((2, t, FF), BF16, (2, tm, tf), lambda f, i, q: (0, i, f))
    res = _mm(name, (FF // tf, t // tm, 1), [a_op], [b_op], [[(0, 0)]], NT, epi, [out], extras, job=job)
    return res[0] if job is None else (res[0][0], res[1])


def _wgrad(name, dy_op, x, out_block, out_map, gi, carry_buf, out_shape, tk=512, job=None):
    t, c = x.shape
    b_op = (x, (tk, c), lambda i, j, q: (q, 0))
    out = (out_shape, BF16, out_block, out_map)
    carry = None if carry_buf is None else (carry_buf, 0)
    res = _mm(name, (gi, 1, t // tk), [dy_op], [b_op], [[(0, 0)]], TN, _first, [out], carry=carry, job=job)
    return res[0] if job is None else (res[0][0], res[1])


def _rel_onehot():
    j = lax.broadcasted_iota(jnp.int32, (384, RELW), 0)
    xx = lax.broadcasted_iota(jnp.int32, (384, RELW), 1)
    idx = jnp.clip(KPAD + QB - xx, -128, 128) + 128
    return (j == idx).astype(F32)


def _relbias_fwd(table):
    def body(t_ref, o_ref):
        rev = jnp.dot(t_ref[...], _rel_onehot(), precision=lax.Precision.HIGHEST, preferred_element_type=F32)
        for r in range(QB):
            row = pltpu.roll(rev, (RELW - (QB - r)) % RELW, 1)[:, :KW]
            rr = lax.broadcasted_iota(jnp.int32, (NH, KW), 1) >> 6
            ok = (rr >= (r // CHUNK)) & (rr <= (r // CHUNK) + LEFT)
            row = jnp.where(ok, row, -1e30)
            for h in range(NH):
                o_ref[h * QB + r:h * QB + r + 1, :] = row[h:h + 1, :]

    return pl.pallas_call(
        body,
        name="relbias_fwd",
        out_shape=jax.ShapeDtypeStruct((NH * QB, KW), F32),
        in_specs=[pl.BlockSpec(memory_space=pltpu.VMEM)],
        out_specs=pl.BlockSpec(memory_space=pltpu.VMEM),
        compiler_params=_params(),
    )(table)


def _relbias_bwd(dbias):
    def body(d_ref, o_ref, acc_ref):
        for h in range(NH):
            acc = jnp.zeros((1, RELW), F32)
            for r in range(QB):
                row = d_ref[h * QB + r:h * QB + r + 1, :]
                wide = jnp.concatenate([row, jnp.zeros((1, RELW - KW), F32)], axis=1)
                acc = acc + pltpu.roll(wide, QB - r, 1)
            acc_ref[h:h + 1, :] = acc
        o_ref[...] = lax.dot_general(acc_ref[...], _rel_onehot(), NT, precision=lax.Precision.HIGHEST,
                                     preferred_element_type=F32)

    return pl.pallas_call(
        body,
        name="relbias_bwd",
        out_shape=jax.ShapeDtypeStruct((NH, 384), F32),
        in_specs=[pl.BlockSpec(memory_space=pltpu.VMEM)],
        out_specs=pl.BlockSpec(memory_space=pltpu.VMEM),
        scratch_shapes=[pltpu.VMEM((NH, RELW), F32)],
        compiler_params=_params(),
    )(dbias)


HQ = QB // 2


def _stack_heads(x_pair):
    first = lax.broadcasted_iota(jnp.int32, (1, 128), 1) < 64
    zero = jnp.zeros_like(x_pair)
    a, b = jnp.where(first, x_pair, zero), jnp.where(first, zero, x_pair)
    return jnp.concatenate([a[:HQ], b[:HQ], a[HQ:], b[HQ:]], axis=0), first


def _unstack_heads(o, first):
    return jnp.concatenate([jnp.where(first, o[0:HQ], o[HQ:2 * HQ]),
                            jnp.where(first, o[2 * HQ:3 * HQ], o[3 * HQ:4 * HQ])], axis=0)


def _half_cols(hf):
    return slice(hf * HQ, hf * HQ + KW - HQ)


def _half_bias(b_ref, pair, hf):
    rows = lambda h: slice(h * QB + hf * HQ, h * QB + (hf + 1) * HQ)
    return jnp.concatenate([b_ref[rows(2 * pair), _half_cols(hf)], b_ref[rows(2 * pair + 1), _half_cols(hf)]], axis=0)


def _half_probs(s_full, b_ref, pair, hf, key_ok):
    s = s_full[2 * hf * HQ:2 * (hf + 1) * HQ, _half_cols(hf)] + _half_bias(b_ref, pair, hf)
    if key_ok is not None:
        s = jnp.where(key_ok[:, _half_cols(hf)], s, -1e30)
    e = jnp.exp(s - jnp.max(s, axis=-1, keepdims=True))
    return e * (1.0 / jnp.sum(e, axis=-1, keepdims=True))


def _widen(top, bottom):
    z = jnp.zeros((2 * HQ, HQ), top.dtype)
    return jnp.concatenate([jnp.concatenate([top, z], axis=1), jnp.concatenate([z, bottom], axis=1)], axis=0)


def _attn_fwd(qkv, kvp, bias, job=None):
    t = qkv.shape[0]

    def body(q_ref, kv_ref, b_ref, o_ref):
        i = pl.program_id(0)

        def run(masked):
            start = pl.multiple_of(i * QB, QB)
            col = lax.broadcasted_iota(jnp.int32, (1, KW), 1)
            key_ok = (col >= KPAD - i * QB) if masked else None
            for pair in range(4):
                lo = pair * 128
                kw = kv_ref[pl.ds(start, KW), lo:lo + 128]
                vw = kv_ref[pl.ds(start, KW), DA + lo:DA + lo + 128]
                qs, first = _stack_heads(q_ref[:, lo:lo + 128])
                s = lax.dot_general(qs * 0.125, kw, NT, preferred_element_type=F32)
                p = _widen(*[_half_probs(s, b_ref, pair, hf, key_ok).astype(BF16) for hf in range(2)])
                o = jnp.dot(p, vw, preferred_element_type=F32)
                o_ref[:, lo:lo + 128] = _unstack_heads(o, first).astype(BF16)

        pl.when(i < KPAD // QB)(lambda: run(True))
        pl.when(i >= KPAD // QB)(lambda: run(False))

    res, jres = _call(
        body, name="attn_fwd", grid=(t // QB,),
        in_specs=[pl.BlockSpec((QB, DA), lambda i: (i, 0)),
                  pl.BlockSpec(memory_space=pltpu.VMEM),
                  pl.BlockSpec(memory_space=pltpu.VMEM)],
        args=[qkv, kvp, bias],
        out_specs=[pl.BlockSpec((QB, DA), lambda i: (i, 0))],
        out_shape=[jax.ShapeDtypeStruct((t, DA), BF16)],
        sem=("arbitrary",), job=job)
    return res[0], jres


def _attn_bwd(qkv, kvp, bias, datt, job=None):
    t = qkv.shape[0]
    nb = t // QB
    flush = (KW - QB) // QB

    def body(q_ref, kv_ref, b_ref, do_ref, dq_ref, dkv_ref, db_out, acc_ref, db_ref):
        i = pl.program_id(0)

        @pl.when(i == 0)
        def _():
            acc_ref[...] = jnp.zeros_like(acc_ref)
            db_ref[...] = jnp.zeros_like(db_ref)

        def run(masked):
            start = pl.multiple_of(i * QB, QB)
            col = lax.broadcasted_iota(jnp.int32, (1, KW), 1)
            key_ok = (col >= KPAD - i * QB) if masked else None
            for pair in range(4):
                lo = pair * 128
                kw = kv_ref[pl.ds(start, KW), lo:lo + 128]
                vw = kv_ref[pl.ds(start, KW), DA + lo:DA + lo + 128]
                qs, first = _stack_heads(q_ref[:, lo:lo + 128])
                qs = qs * 0.125
                dos, _ = _stack_heads(do_ref[:, lo:lo + 128])
                s = lax.dot_general(qs, kw, NT, preferred_element_type=F32)
                dp = lax.dot_general(dos, vw, NT, preferred_element_type=F32)
                ps, dss = [], []
                for hf in range(2):
                    p = _half_probs(s, b_ref, pair, hf, key_ok)
                    dph = dp[2 * hf * HQ:2 * (hf + 1) * HQ, _half_cols(hf)]
                    ds = p * (dph - jnp.sum(p * dph, axis=-1, keepdims=True))
                    for k, h in enumerate((2 * pair, 2 * pair + 1)):
                        db_ref[h * QB + hf * HQ:h * QB + (hf + 1) * HQ, _half_cols(hf)] += ds[k * HQ:(k + 1) * HQ]
                    ps.append(p.astype(BF16))
                    dss.append(ds.astype(BF16))
                pb, dsb = _widen(*ps), _widen(*dss)
                dq = jnp.dot(dsb, kw, preferred_element_type=F32)
                dq_ref[:, lo:lo + 128] = (_unstack_heads(dq, first) * 0.125).astype(BF16)
                acc_ref[:, lo:lo + 128] += lax.dot_general(dsb, qs, TN, preferred_element_type=F32)
                acc_ref[:, DA + lo:DA + lo + 128] += lax.dot_general(pb, dos, TN, preferred_element_type=F32)

        pl.when(i < KPAD // QB)(lambda: run(True))
        pl.when(jnp.logical_and(i >= KPAD // QB, i < nb))(lambda: run(False))

        dkv_ref[...] = acc_ref[0:QB, :].astype(BF16)
        rest = acc_ref[QB:KW, :]
        acc_ref[0:KW - QB, :] = rest
        acc_ref[KW - QB:KW, :] = jnp.zeros((QB, 2 * DA), F32)

        @pl.when(i == nb + flush - 1)
        def _():
            pltpu.sync_copy(db_ref, db_out)

    last = nb - 1
    res, jres = _call(
        body, name="attn_bwd", grid=(nb + flush,),
        in_specs=[pl.BlockSpec((QB, DA), lambda i: (jnp.minimum(i, last), 0)),
                  pl.BlockSpec(memory_space=pltpu.VMEM),
                  pl.BlockSpec(memory_space=pltpu.VMEM),
                  pl.BlockSpec((QB, DA), lambda i: (jnp.minimum(i, last), 0))],
        args=[qkv, kvp, bias, datt],
        out_specs=[pl.BlockSpec((QB, DA), lambda i: (jnp.minimum(i, last), 0)),
                   pl.BlockSpec((QB, 2 * DA), lambda i: (i, 0)),
                   ANY],
        out_shape=[jax.ShapeDtypeStruct((t, DA), BF16),
                   jax.ShapeDtypeStruct((t + KPAD, 2 * DA), BF16),
                   jax.ShapeDtypeStruct((NH * QB, KW), F32)],
        scratch=[pltpu.VMEM((KW, 2 * DA), F32), pltpu.VMEM((NH * QB, KW), F32)], sem=("arbitrary",), job=job)
    return res, jres


def _glu(c, gb):
    cb = c + gb
    return cb[:, :DC] * _sigmoid(cb[:, DC:])


def _ln_swish(pre, g, b):
    mu = jnp.mean(pre, axis=-1, keepdims=True)
    xc = pre - mu
    r = lax.rsqrt(jnp.mean(xc * xc, axis=-1, keepdims=True) + EPS)
    xhat = xc * r
    y = xhat * g + b
    return xhat, r, y


RT = 32


def _shifted_copies(src_ref, sh_ref, rows):
    for b in range(1, 8):
        sh_ref[b - 1, :, :] = src_ref[pl.ds(b, rows), :]


def _tap(src_ref, sh_ref, off, r0, rows=RT):
    a, b = divmod(off, 8)
    ref = src_ref if b == 0 else sh_ref.at[b - 1]
    if isinstance(r0, int):
        return ref[r0 + 8 * a:r0 + 8 * a + rows, :]
    return ref[pl.ds(pl.multiple_of(r0 + 8 * a, 8), rows), :]


def _conv_fwd(cin, glu_b, dw_w, dw_b, ln_g, ln_b, tm=TM):
    t = cin.shape[0]
    hb = tm // HALO

    def body(c_ref, h_ref, gb_ref, w_ref, wb_ref, g_ref, b_ref, cs_ref, pre_ref, ext_ref, sh_ref):
        i = pl.program_id(0)
        halo = _glu(h_ref[...], gb_ref[...])
        ext_ref[0:HALO, :] = jnp.where(i > 0, halo, jnp.zeros_like(halo))
        ext_ref[HALO:HALO + tm, :] = _glu(c_ref[...], gb_ref[...])
        ext_ref[HALO + tm:HALO + tm + 8, :] = jnp.zeros((8, DC), F32)
        _shifted_copies(ext_ref, sh_ref, HALO + tm)

        def tile(rt, carry):
            r0 = pl.multiple_of(rt * RT, RT)
            acc = jnp.zeros((RT, DC), F32) + wb_ref[...]
            for j in range(CW):
                acc = acc + w_ref[j:j + 1, :] * _tap(ext_ref, sh_ref, HALO - (CW - 1) + j, r0)
            pre_ref[pl.ds(r0, RT), :] = acc
            return carry

        lax.fori_loop(0, tm // RT, tile, 0, unroll=2)
        _, _, y = _ln_swish(pre_ref[...], g_ref[...], b_ref[...])
        cs_ref[...] = (y * _sigmoid(y)).astype(BF16)

    vec = lambda n: pl.BlockSpec((1, n), lambda i: (0, 0))
    return pl.pallas_call(
        body,
        name="conv_fwd",
        grid=(t // tm,),
        in_specs=[pl.BlockSpec((tm, 2 * DC), lambda i: (i, 0)),
                  pl.BlockSpec((HALO, 2 * DC), lambda i: (jnp.maximum(i * hb - 1, 0), 0)),
                  vec(2 * DC), pl.BlockSpec((CW, DC), lambda i: (0, 0)), vec(DC), vec(DC), vec(DC)],
        out_specs=[pl.BlockSpec((tm, DC), lambda i: (i, 0)), pl.BlockSpec((tm, DC), lambda i: (i, 0))],
        out_shape=[jax.ShapeDtypeStruct((t, DC), BF16), jax.ShapeDtypeStruct((t, DC), F32)],
        scratch_shapes=[pltpu.VMEM((HALO + tm + 8, DC), F32), pltpu.VMEM((7, HALO + tm, DC), F32)],
        compiler_params=_params(("arbitrary",)),
    )(cin, cin, glu_b, dw_w, dw_b, ln_g, ln_b)


def _conv_bwd(dcs, pre, cin, glu_b, dw_w, ln_g, ln_b, tm=TM):
    t = cin.shape[0]
    hb = tm // HALO
    steps = t // tm
    nhb = t // HALO

    def dpre_of(dcs_v, pre_v, g, b):
        xhat, r, y = _ln_swish(pre_v, g, b)
        sg = _sigmoid(y)
        dy = dcs_v * (sg * (1.0 + y * (1.0 - sg)))
        dxh = dy * g
        dpre = r * (dxh - jnp.mean(dxh, axis=-1, keepdims=True)
                    - xhat * jnp.mean(dxh * xhat, axis=-1, keepdims=True))
        return dpre, dy * xhat, dy

    def body(dcs_ref, dcsn_ref, pre_ref, pren_ref, c_ref, ch_ref, gb_ref, w_ref, g_ref, b_ref,
             dc_ref, dgb_ref, dw_ref, dwb_ref, dg_ref, db_ref,
             gext_ref, dext_ref, shg_ref, shd_ref, a_gb, a_w, a_wb, a_g, a_b):
        i = pl.program_id(0)

        @pl.when(i == 0)
        def _():
            for a in (a_gb, a_w, a_wb, a_g, a_b):
                a[...] = jnp.zeros_like(a)

        fold = lambda v: v.reshape(v.shape[0] // 8, 8, v.shape[-1]).sum(axis=0)
        g, b = g_ref[...], b_ref[...]
        dpre, dg_t, db_t = dpre_of(dcs_ref[...], pre_ref[...], g, b)
        dpre_n, _, _ = dpre_of(dcsn_ref[...], pren_ref[...], g, b)
        dext_ref[0:tm, :] = dpre
        dext_ref[tm:tm + HALO, :] = jnp.where(i < steps - 1, dpre_n, jnp.zeros_like(dpre_n))
        dext_ref[tm + HALO:tm + HALO + 8, :] = jnp.zeros((8, DC), F32)
        a_wb[...] += fold(dpre)
        a_g[...] += fold(dg_t)
        a_b[...] += fold(db_t)
        halo = _glu(ch_ref[...], gb_ref[...])
        gext_ref[0:HALO, :] = jnp.where(i > 0, halo, jnp.zeros_like(halo))
        gext_ref[HALO:HALO + tm, :] = _glu(c_ref[...], gb_ref[...])
        gext_ref[HALO + tm:HALO + tm + 8, :] = jnp.zeros((8, DC), F32)
        _shifted_copies(gext_ref, shg_ref, HALO + tm)
        _shifted_copies(dext_ref, shd_ref, HALO + tm)

        for j in range(CW):
            a_w[8 * j:8 * j + 8, :] += fold(dext_ref[0:tm, :] * _tap(gext_ref, shg_ref, HALO - (CW - 1) + j, 0, tm))

        def tile(rt, carry):
            r0 = pl.multiple_of(rt * RT, RT)
            dglu = jnp.zeros((RT, DC), F32)
            for j in range(CW):
                dglu = dglu + w_ref[j:j + 1, :] * _tap(dext_ref, shd_ref, CW - 1 - j, r0)
            gext_ref[pl.ds(r0, RT), :] = dglu
            return carry

        lax.fori_loop(0, tm // RT, tile, 0, unroll=2)
        dglu = gext_ref[0:tm, :]
        cb = c_ref[...] + gb_ref[...]
        sg = _sigmoid(cb[:, DC:])
        dc = jnp.concatenate([dglu * sg, dglu * cb[:, :DC] * sg * (1.0 - sg)], axis=1)
        dc_ref[...] = dc.astype(BF16)
        a_gb[...] += fold(dc)

        @pl.when(i == steps - 1)
        def _():
            dgb_ref[...] = a_gb[...].sum(axis=0, keepdims=True)
            dw_ref[...] = a_w[...]
            dwb_ref[...] = a_wb[...].sum(axis=0, keepdims=True)
            dg_ref[...] = a_g[...].sum(axis=0, keepdims=True)
            db_ref[...] = a_b[...].sum(axis=0, keepdims=True)

    vec = lambda n: pl.BlockSpec((1, n), lambda i: (0, 0))
    nxt = lambda i: (jnp.minimum((i + 1) * hb, nhb - 1), 0)
    prv = lambda i: (jnp.maximum(i * hb - 1, 0), 0)
    return pl.pallas_call(
        body,
        name="conv_bwd",
        grid=(steps,),
        in_specs=[pl.BlockSpec((tm, DC), lambda i: (i, 0)), pl.BlockSpec((HALO, DC), nxt),
                  pl.BlockSpec((tm, DC), lambda i: (i, 0)), pl.BlockSpec((HALO, DC), nxt),
                  pl.BlockSpec((tm, 2 * DC), lambda i: (i, 0)), pl.BlockSpec((HALO, 2 * DC), prv),
                  vec(2 * DC), pl.BlockSpec((CW, DC), lambda i: (0, 0)), vec(DC), vec(DC)],
        out_specs=[pl.BlockSpec((tm, 2 * DC), lambda i: (i, 0)), vec(2 * DC),
                   pl.BlockSpec((CW * 8, DC), lambda i: (0, 0)), vec(DC), vec(DC), vec(DC)],
        out_shape=[jax.ShapeDtypeStruct((t, 2 * DC), BF16), jax.ShapeDtypeStruct((1, 2 * DC), F32),
                   jax.ShapeDtypeStruct((CW * 8, DC), F32), jax.ShapeDtypeStruct((1, DC), F32),
                   jax.ShapeDtypeStruct((1, DC), F32), jax.ShapeDtypeStruct((1, DC), F32)],
        scratch_shapes=[pltpu.VMEM((HALO + tm + 8, DC), F32), pltpu.VMEM((tm + HALO + 8, DC), F32),
                        pltpu.VMEM((7, HALO + tm, DC), F32), pltpu.VMEM((7, HALO + tm, DC), F32),
                        pltpu.VMEM((8, 2 * DC), F32), pltpu.VMEM((CW * 8, DC), F32),
                        pltpu.VMEM((8, DC), F32), pltpu.VMEM((8, DC), F32), pltpu.VMEM((8, DC), F32)],
        compiler_params=_params(("arbitrary",)),
    )(dcs, dcs, pre, pre, cin, cin, glu_b, dw_w, ln_g, ln_b)


def _place():
    x, y, c = lax.axis_index("x"), lax.axis_index("y"), lax.axis_index("c")
    return x, y, c


def _peers(x, y, c):
    out = []
    for k in range(1, NDEV):
        fx, fy, fc = (k >> 2) & 1, (k >> 1) & 1, k & 1
        px = 1 - x if fx else x
        py = 1 - y if fy else y
        pc = 1 - c if fc else c
        out.append((px, py, pc))
    return out


def _job_out_shapes(job):
    kind, arrays = job
    if kind == "gather":
        return [jax.ShapeDtypeStruct((a.shape[0], NDEV) + a.shape[1:], a.dtype) for a in arrays]
    return [jax.ShapeDtypeStruct((NDEV, a.shape[0]) + a.shape[2:], a.dtype) for a in arrays]


def _job_scratch(job):
    n = len(job[1])
    return [pltpu.SemaphoreType.DMA((n, NDEV - 1)), pltpu.SemaphoreType.DMA((n, NDEV - 1)),
            pltpu.SemaphoreType.DMA((n,))]


def _gather_parts(ins, outs, send_sems, recv_sems, local_sems):
    x, y, c = _place()
    me, sib = (x, y, c), (x, y, 1 - c)
    chips = [(1 - x, y), (x, 1 - y), (1 - x, 1 - y)]

    def copy(a, k, block, to, src=None):
        px, py, pc = block
        dst = outs[a].at[:, 4 * px + 2 * py + pc]
        return pltpu.make_async_remote_copy(
            src_ref=dst if src is None else src, dst_ref=dst,
            send_sem=send_sems.at[a, k], recv_sem=recv_sems.at[a, k], device_id=to, device_id_type=MESH)

    n = len(ins)
    local = [pltpu.make_async_copy(ins[a], outs[a].at[:, 4 * x + 2 * y + c], local_sems.at[a]) for a in range(n)]
    first = [[copy(a, 0, me, sib, src=ins[a])] + [copy(a, 1 + j, me, (*chip, c), src=ins[a])
                                                   for j, chip in enumerate(chips)] for a in range(n)]

    def start():
        for a in range(n):
            local[a].start()
            for cp in first[a]:
                cp.start()

    def relay():
        for j, chip in enumerate(chips):
            for a in range(n):
                copy(a, 1 + j, (*chip, c), me).wait_recv()
                copy(a, 4 + j, (*chip, c), sib).start()

    def finish():
        for a in range(n):
            copy(a, 0, sib, me).wait_recv()
            for j, chip in enumerate(chips):
                copy(a, 4 + j, (*chip, 1 - c), me).wait_recv()
        for a in range(n):
            for cp in first[a]:
                cp.wait_send()
            local[a].wait()
            for j, chip in enumerate(chips):
                copy(a, 4 + j, (*chip, c), sib).wait_send()

    return start, relay, finish


def _exchange_parts(ins, outs, send_sems, recv_sems, local_sems):
    x, y, c = _place()
    me = 4 * x + 2 * y + c
    n = len(ins)
    peers = _peers(x, y, c)
    local = [pltpu.make_async_copy(ins[a].at[:, me], outs[a].at[me], local_sems.at[a]) for a in range(n)]

    def copy(a, k):
        px, py, pc = peers[k]
        return pltpu.make_async_remote_copy(
            src_ref=ins[a].at[:, 4 * px + 2 * py + pc], dst_ref=outs[a].at[me],
            send_sem=send_sems.at[a, k], recv_sem=recv_sems.at[a, k], device_id=peers[k], device_id_type=MESH)

    def arrival(a, k):
        px, py, pc = peers[k]
        return pltpu.make_async_remote_copy(
            src_ref=ins[a].at[:, me], dst_ref=outs[a].at[4 * px + 2 * py + pc],
            send_sem=send_sems.at[a, k], recv_sem=recv_sems.at[a, k], device_id=peers[k], device_id_type=MESH)

    def start():
        for a in range(n):
            local[a].start()
            for k in range(NDEV - 1):
                copy(a, k).start()

    def finish():
        for a in range(n):
            for k in range(NDEV - 1):
                arrival(a, k).wait_recv()
        for a in range(n):
            for k in range(NDEV - 1):
                copy(a, k).wait_send()
            local[a].wait()

    return start, None, finish


def _call(body, *, name, grid, in_specs, args, out_specs, out_shape, scratch=(), sem=None, aliases=None, job=None):
    aliases = dict(aliases or {})
    if job is None:
        res = pl.pallas_call(
            body, name=name, grid=grid, in_specs=list(in_specs), out_specs=list(out_specs),
            out_shape=list(out_shape), scratch_shapes=list(scratch), input_output_aliases=aliases,
            compiler_params=_params(sem))(*args)
        return list(res), []
    kind, arrays = job
    n_in, n_out, n_scr, nj = len(args), len(out_shape), len(scratch), len(arrays)

    def wrapped(*refs):
        ins = refs[:n_in]
        jin = refs[n_in:n_in + nj]
        o0 = n_in + nj
        outs = refs[o0:o0 + n_out]
        jout = refs[o0 + n_out:o0 + n_out + nj]
        s0 = o0 + n_out + nj
        scr = refs[s0:s0 + n_scr]
        sems = refs[s0 + n_scr:]
        parts = _gather_parts if kind == "gather" else _exchange_parts
        start, relay, finish = parts(jin, jout, *sems)
        if not grid:
            start()
            body(*ins, *outs, *scr)
            if relay is not None:
                relay()
            finish()
            return
        first = last = None
        step, steps = 0, 1
        for d, g in enumerate(grid):
            f, l = pl.program_id(d) == 0, pl.program_id(d) == g - 1
            first = f if first is None else jnp.logical_and(first, f)
            last = l if last is None else jnp.logical_and(last, l)
            step, steps = step * g + pl.program_id(d), steps * g
        pl.when(first)(start)
        body(*ins, *outs, *scr)
        if relay is not None:
            pl.when(step == (7 * steps) // 8)(relay)
        pl.when(last)(finish)

    res = pl.pallas_call(
        wrapped, name=name, grid=grid, in_specs=list(in_specs) + [ANY] * nj,
        out_specs=list(out_specs) + [ANY] * nj, out_shape=list(out_shape) + _job_out_shapes(job),
        scratch_shapes=list(scratch) + _job_scratch(job), input_output_aliases=aliases,
        compiler_params=pltpu.CompilerParams(
            dimension_semantics=None if not grid else ("arbitrary",) * len(grid),
            vmem_limit_bytes=VMEM_LIMIT, has_side_effects=True))(*args, *arrays)
    return list(res[:n_out]), list(res[n_out:])


def _comm_only(name, job):
    return _call(lambda: None, name=name, grid=(), in_specs=[], args=[], out_specs=[], out_shape=[], job=job)[1]


def _sum_devices(name, parts, tr):
    _, r, c = parts.shape

    def body(p_ref, o_ref):
        acc = p_ref[0].astype(F32)
        for d in range(1, NDEV):
            acc = acc + p_ref[d].astype(F32)
        o_ref[...] = acc

    return pl.pallas_call(
        body,
        name=name,
        grid=(r // tr,),
        in_specs=[pl.BlockSpec((NDEV, tr, c), lambda i: (0, i, 0))],
        out_specs=pl.BlockSpec((tr, c), lambda i: (i, 0)),
        out_shape=jax.ShapeDtypeStruct((r, c), F32),
        compiler_params=_params(("parallel",)),
    )(parts)


def _adamw(name, w, g, m, v, tr=None):
    r, c = w.shape
    tr = r if tr is None else tr

    def body(w_ref, g_ref, m_ref, v_ref, d_ref, nm_ref, nv_ref):
        gv = g_ref[...]
        nm = ADAM_B1 * m_ref[...] + (1.0 - ADAM_B1) * gv
        nv = ADAM_B2 * v_ref[...] + (1.0 - ADAM_B2) * (gv * gv)
        m_hat = nm / (1.0 - ADAM_B1 ** ADAM_STEP)
        v_hat = nv / (1.0 - ADAM_B2 ** ADAM_STEP)
        d_ref[...] = -ADAM_LR * (m_hat / (jnp.sqrt(v_hat) + ADAM_EPS) + ADAM_WD * w_ref[...])
        nm_ref[...] = nm
        nv_ref[...] = nv

    spec = pl.BlockSpec((tr, c), lambda i: (i, 0))
    return pl.pallas_call(
        body,
        name=name,
        grid=(r // tr,),
        in_specs=[spec] * 4,
        out_specs=[spec] * 3,
        out_shape=[jax.ShapeDtypeStruct((r, c), F32)] * 3,
        compiler_params=_params(("parallel",)),
    )(w, g, m, v)


SMALL = ["ffn1_norm_pre", "ffn1_norm_post", "mix_norm_pre", "gate_bias", "rel_table", "conv_glu_bias",
         "conv_dw_b", "conv_ln_g", "conv_ln_b", "mix_norm_post", "ffn2_norm_pre", "ffn2_norm_post"]
SMALL_ROWS = 24
DW_ROWS = 32


LOSS_ROW = 20
RELP = 384
SMALL_LAYOUT = {}
_r = 0
for _name, _n in zip(SMALL, [D, D, D, 2 * D, None, 2 * DC, DC, DC, DC, D, D, D]):
    if _n is None:
        SMALL_LAYOUT[_name] = (_r, NH, RELP)
        _r += NH
    else:
        SMALL_LAYOUT[_name] = (_r, -(-_n // D), min(_n, D))
        _r += -(-_n // D)
assert _r == LOSS_ROW


def _small_pieces(name):
    r0, nr, nc = SMALL_LAYOUT[name]
    if name == "rel_table":
        return [(slice(r0, r0 + NH), slice(0, nc), slice(0, NH), slice(0, nc))]
    return [(slice(r0 + k, r0 + k + 1), slice(0, nc), slice(0, 1), slice(k * nc, (k + 1) * nc)) for k in range(nr)]


def _pack_small(vals, loss_row):
    def body(*refs):
        o = refs[-1]
        o[...] = jnp.zeros_like(o)
        for ref, name in zip(refs, SMALL):
            for prow, pcol, arow, acol in _small_pieces(name):
                o[prow, pcol] = ref[arow, acol]
        o[LOSS_ROW:LOSS_ROW + 1, :] = refs[len(SMALL)][...]

    vm = pl.BlockSpec(memory_space=pltpu.VMEM)
    return pl.pallas_call(body, name="pack_small", out_shape=jax.ShapeDtypeStruct((SMALL_ROWS, D), F32),
                          in_specs=[vm] * (len(SMALL) + 1), out_specs=vm,
                          compiler_params=_params())(*[vals[n] for n in SMALL], loss_row)


def _small_update(parts, w, m, v):
    ns = len(SMALL)

    def body(p_ref, *refs):
        ins, outs = refs[:3 * ns], refs[3 * ns:]

        def total(prow, pcol):
            g = p_ref[0, prow, pcol]
            for d in range(1, NDEV):
                g = g + p_ref[d, prow, pcol]
            return g

        for q, name in enumerate(SMALL):
            w_ref, m_ref, v_ref = ins[3 * q:3 * q + 3]
            o_g, o_d, o_m, o_v = outs[4 * q:4 * q + 4]
            for prow, pcol, arow, acol in _small_pieces(name):
                g = total(prow, pcol)
                nm = ADAM_B1 * m_ref[arow, acol] + (1.0 - ADAM_B1) * g
                nv = ADAM_B2 * v_ref[arow, acol] + (1.0 - ADAM_B2) * (g * g)
                m_hat = nm / (1.0 - ADAM_B1 ** ADAM_STEP)
                v_hat = nv / (1.0 - ADAM_B2 ** ADAM_STEP)
                o_g[arow, acol] = g
                o_d[arow, acol] = -ADAM_LR * (m_hat / (jnp.sqrt(v_hat) + ADAM_EPS) + ADAM_WD * w_ref[arow, acol])
                o_m[arow, acol] = nm
                o_v[arow, acol] = nv
        outs[4 * ns][...] = total(slice(LOSS_ROW, LOSS_ROW + 1), slice(None))

    args, out_shape = [], []
    for name in SMALL:
        args += [w[name], m[name], v[name]]
        out_shape += [jax.ShapeDtypeStruct(w[name].shape, F32)] * 4
    out_shape.append(jax.ShapeDtypeStruct((1, D), F32))
    vm = pl.BlockSpec(memory_space=pltpu.VMEM)
    res = pl.pallas_call(body, name="small_update", out_shape=out_shape, in_specs=[vm] * (1 + 3 * ns),
                         out_specs=[vm] * len(out_shape), compiler_params=_params())(parts, *args)
    return {name: res[4 * q:4 * q + 4] for q, name in enumerate(SMALL)}, res[-1]


def kernel(x, ffn1_norm_pre, ffn1_w_gate, ffn1_w_up, ffn1_w_down, ffn1_norm_post, mix_norm_pre, w_in, gate_bias, rel_table, w_attn_out, conv_glu_bias, conv_dw_w, conv_dw_b, conv_ln_g, conv_ln_b, conv_w_out, w_out, mix_norm_post, ffn2_norm_pre, ffn2_w_gate, ffn2_w_up, ffn2_w_down, ffn2_norm_post, loss_target, m_ffn1_norm_pre, m_ffn1_w_gate, m_ffn1_w_up, m_ffn1_w_down, m_ffn1_norm_post, m_mix_norm_pre, m_w_in, m_gate_bias, m_rel_table, m_w_attn_out, m_conv_glu_bias, m_conv_dw_w, m_conv_dw_b, m_conv_ln_g, m_conv_ln_b, m_conv_w_out, m_w_out, m_mix_norm_post, m_ffn2_norm_pre, m_ffn2_w_gate, m_ffn2_w_up, m_ffn2_w_down, m_ffn2_norm_post, v_ffn1_norm_pre, v_ffn1_w_gate, v_ffn1_w_up, v_ffn1_w_down, v_ffn1_norm_post, v_mix_norm_pre, v_w_in, v_gate_bias, v_rel_table, v_w_attn_out, v_conv_glu_bias, v_conv_dw_w, v_conv_dw_b, v_conv_ln_g, v_conv_ln_b, v_conv_w_out, v_w_out, v_mix_norm_post, v_ffn2_norm_pre, v_ffn2_w_gate, v_ffn2_w_up, v_ffn2_w_down, v_ffn2_norm_post):
    return _step(dict(locals()))


WEIGHTS = ["ffn1_norm_pre", "ffn1_w_gate", "ffn1_w_up", "ffn1_w_down", "ffn1_norm_post", "mix_norm_pre", "w_in",
           "gate_bias", "rel_table", "w_attn_out", "conv_glu_bias", "conv_dw_w", "conv_dw_b", "conv_ln_g",
           "conv_ln_b", "conv_w_out", "w_out", "mix_norm_post", "ffn2_norm_pre", "ffn2_w_gate", "ffn2_w_up",
           "ffn2_w_down", "ffn2_norm_post"]
FS = FF // NDEV
PS = (3 * DA + 2 * DC + 2 * D) // NDEV
OS = D // NDEV


def _local_step(xs, target, w, rel_table):
    t = xs.shape[0]
    vec = lambda n: w[n].reshape(1, -1)
    g_pre1, g_post1, g_mix, g_mixp = vec("ffn1_norm_pre"), vec("ffn1_norm_post"), vec("mix_norm_pre"), vec("mix_norm_post")
    g_pre2, g_post2 = vec("ffn2_norm_pre"), vec("ffn2_norm_post")
    gate_b, glu_b = vec("gate_bias"), vec("conv_glu_bias")
    dw_b, ln_g, ln_b = vec("conv_dw_b"), vec("conv_ln_g"), vec("conv_ln_b")

    tr = lambda a: jnp.transpose(a[0]).astype(BF16)
    sh_gu1 = jnp.stack([tr(w["ffn1_w_gate"]), tr(w["ffn1_w_up"])])
    sh_mid = [w["ffn1_w_down"].astype(BF16), tr(w["w_in"])[None], w["w_out"].astype(BF16),
              jnp.stack([tr(w["w_attn_out"]), tr(w["conv_w_out"])]),
              jnp.pad(w["conv_dw_w"][0, :, 0, :], ((0, DW_ROWS - CW), (0, 0)))[None]]
    sh_2 = jnp.stack([tr(w["ffn2_w_gate"]), tr(w["ffn2_w_up"]), w["ffn2_w_down"][0].astype(BF16)])

    (n1,), (w_gu1,) = _rowwise("pre1", lambda xv, g: ((_rms(xv)[0] * g),), [xs], [g_pre1], [(D, BF16)],
                               job=("gather", [sh_gu1]))
    w_gu1 = w_gu1.reshape(2, FF, D)
    (a1, b1, s1), (w_d1, wb, wc, wd, we) = _ffn_up("ffn1_up", n1, w_gu1, 0, job=("gather", sh_mid))
    w_d1, wb, wc, wd = w_d1.reshape(1, FF, D), wb.reshape(NDEV * PS, D), wc.reshape(D, D), wd.reshape(2, D, DA)
    dw_full = jnp.transpose(we[0], (1, 0, 2)).reshape(DW_ROWS, DC)[:CW]
    row = lambda i, j, q: (i, 0)
    top = lambda i, j, q: (0, 0)
    tmr = min(512, t)

    def post1(fv, xv, gp, gm):
        h = xv + 0.5 * (_rms(fv)[0] * gp)
        return fv, h, _rms(h)[0] * gm

    f1, h1, u = _mm_rows("ffn1_down", t, tmr, [(s1, (tmr, FF), row)], [(w_d1, (None, FF, D), lambda i, j, q: (0, 0, 0))],
                         [[(0, 0)]], NN, post1, [xs], [g_post1, g_mix], [(D, F32), (D, F32), (D, BF16)])

    def proj_epi(accs, extras):
        cat = lambda parts: jnp.concatenate(parts, axis=1)
        return (accs[0], jnp.where(pl.program_id(0) > 0, cat(accs[1:3]), 0.0), cat(accs[3:5]), cat(accs[5:9]))

    prev = lambda i, j, q: (jnp.maximum(i - 1, 0), 0)
    qkv, kvp, cin, gg = _mm(
        "proj", (t // KPAD + 1, 1, 1), [(u, (KPAD, D), prev)],
        [(wb, (DA, D), (lambda i, j, q, blk=blk: (blk, 0))) for blk in range(NDEV * PS // DA)],
        [[(0, blk)] for blk in range(NDEV * PS // DA)], NT, proj_epi,
        [((t, DA), BF16, (KPAD, DA), prev), ((t + KPAD, 2 * DA), BF16, (KPAD, 2 * DA), lambda i, j, q: (i, 0)),
         ((t, 2 * DC), F32, (KPAD, 2 * DC), prev), ((t, 2 * D), BF16, (KPAD, 2 * D), prev)])

    bias = _relbias_fwd(jnp.pad(rel_table[0], ((0, 0), (0, 384 - NREL))))
    att, (w_2,) = _attn_fwd(qkv, kvp, bias, job=("gather", [sh_2]))
    w_2 = w_2.reshape(3, FF, D)
    cs, pre = _conv_fwd(cin, glu_b, dw_full, dw_b, ln_g, ln_b)
    def merge(yav, ybv, gv, gb):
        gates = _sigmoid(gv + gb)
        return yav, ybv, gates[:, :D] * yav + gates[:, D:] * ybv

    wd_block = lambda p: (wd, (None, D, DC), lambda i, j, q: (p, 0, 0))
    ya, yb, merged = _mm_rows("branch_out", t, tmr, [(att, (tmr, DA), row), (cs, (tmr, DC), row)],
                              [wd_block(0), wd_block(1)], [[(0, 0)], [(1, 1)]], NT, merge, [gg], [gate_b],
                              [(D, BF16), (D, BF16), (D, BF16)])

    def postm(mv, hv, gp, g2):
        h = hv + _rms(mv)[0] * gp
        return mv, h, _rms(h)[0] * g2

    mm_, h2, n2 = _mm_rows("mix_out", t, tmr, [(merged, (tmr, D), row)], [(wc, (D, D), top)], [[(0, 0)]], NN,
                           postm, [h1], [g_mixp, g_pre2], [(D, F32), (D, F32), (D, BF16)])
    a2, b2, s2 = _ffn_up("ffn2_up", n2, w_2, 0)

    def post2(fv, hv, tv, gp):
        fh, r = _rms(fv)
        yv = hv + 0.5 * (fh * gp)
        err = yv - tv
        dy = err * (1.0 / D)
        df, dg = _rms_bwd(fh, r, gp, 0.5 * dy)
        return dy, df, (0.5 / D) * (err * err), dg

    dy, df2, loss_row, d_post2 = _mm_rows(
        "ffn2_down", t, tmr, [(s2, (tmr, FF), row)], [(w_2, (None, FF, D), lambda i, j, q: (2, 0, 0))], [[(0, 0)]],
        NN, post2, [h2, target], [g_post2], [(D, F32), (D, BF16)], [D, D])

    tmw = 1408
    nfi = FF // tmw
    tkf = min(2048, t)
    tkp = min(1024, t)

    def wgrad_down(name, s, df, shape, part, carry_buf, job=None):
        return _wgrad(name, (s, (tkf, tmw), lambda i, j, q: (q, i)), df,
                      (None, tmw, D), lambda i, j, q: (part, i, 0), nfi, carry_buf, shape, tk=tkf, job=job)

    def wgrad_gate_up(name, dab, nrm, shape, carry_buf, job=None):
        return _wgrad(name, (dab, (None, tkf, tmw), lambda i, j, q: (i // nfi, q, i % nfi)), nrm,
                      (None, tmw, D), lambda i, j, q: (i // nfi, i % nfi, 0), 2 * nfi, carry_buf, shape,
                      tk=tkf, job=job)

    g_2 = wgrad_down("ffn2_wgrad_d", s2, df2, (3, FF, D), 2, None)
    dab2 = _ffn_bwd_act("ffn2_bwd_act", df2, w_2, 2, a2, b2)
    g_2 = wgrad_gate_up("ffn2_wgrad_gu", dab2, n2, (3, FF, D), g_2)

    def bwd_pre2(dnv, hv, dyv, mv, g2, gp):
        hh, r = _rms(hv)
        dx, dg2 = _rms_bwd(hh, r, g2, dnv)
        dh = dyv + dx
        mh, rm = _rms(mv)
        dm, dgp = _rms_bwd(mh, rm, gp, dh)
        return dh, dm, dg2, dgp

    tmb = 256
    dh2, dm, d_pre2, d_mixp = _mm_rows(
        "ffn2_bwd_in", t, tmb,
        [(dab2, (None, tmb, FF), lambda i, j, q: (0, i, 0)), (dab2, (None, tmb, FF), lambda i, j, q: (1, i, 0))],
        [(w_2, (None, FF, D), lambda i, j, q: (0, 0, 0)), (w_2, (None, FF, D), lambda i, j, q: (1, 0, 0))],
        [[(0, 0), (1, 1)]], NN, bwd_pre2, [h2, dy, mm_], [g_pre2, g_mixp], [(D, F32), (D, BF16)], [D, D])

    def merge_bwd(dmv, yav, ybv, gv, gb, wao, wco):
        gates = _sigmoid(gv + gb)
        ga, gbb = gates[:, :D], gates[:, D:]
        dgg = jnp.concatenate([dmv * yav * ga * (1.0 - ga), dmv * ybv * gbb * (1.0 - gbb)], axis=1)
        dyav, dybv = (dmv * ga).astype(BF16), (dmv * gbb).astype(BF16)
        return (dyav, dybv, dgg, jnp.dot(dyav, wao, preferred_element_type=F32),
                jnp.dot(dybv, wco, preferred_element_type=F32), dgg)

    dya, dyb, dgg, datt, dcs, d_gate_b = _mm_rows(
        "mix_out_bwd", t, tmr, [(dm, (tmr, D), row)], [(wc, (D, D), top)], [[(0, 0)]], NT, merge_bwd,
        [ya, yb, gg], [gate_b, wd[0], wd[1]], [(D, BF16), (D, BF16), (2 * D, BF16), (DA, BF16), (DC, F32)], [2 * D])
    g_c = _wgrad("mix_out_wgrad", (merged, (tkp, D), lambda i, j, q: (q, 0)), dm,
                 (D, D), lambda i, j, q: (0, 0), 1, None, (D, D), tk=tkp)
    g_d = _wgrad("attn_out_wgrad", (dya, (tkp, D), lambda i, j, q: (q, 0)), att,
                 (None, D, DA), lambda i, j, q: (0, 0, 0), 1, None, (2, D, DA), tk=tkp)
    g_d = _wgrad("conv_out_wgrad", (dyb, (tkp, D), lambda i, j, q: (q, 0)), cs,
                 (None, D, DA), lambda i, j, q: (1, 0, 0), 1, g_d, (2, D, DA), tk=tkp)
    (dq, dkvp, dbias), (x_2, x_c, x_d) = _attn_bwd(
        qkv, kvp, bias, datt,
        job=("exchange", [g_2.reshape(3, NDEV, FS, D), g_c.reshape(1, NDEV, OS, D), g_d.reshape(2, NDEV, OS, DA)]))
    d_rel = _relbias_bwd(dbias)
    dcin, d_glu_b, d_dw8, d_dw_b, d_ln_g, d_ln_b = _conv_bwd(dcs, pre, cin, glu_b, dw_full, ln_g, ln_b)
    g_dw = jnp.pad(d_dw8, ((0, 8 * (DW_ROWS - CW)), (0, 0)))
    g_dw = g_dw.reshape(DW_ROWS * 8, NDEV, DC // NDEV).transpose(1, 0, 2)

    top = lambda i, j, q: (0, 0)
    g_b = jnp.concatenate([
        _wgrad("proj_wgrad_q", (dq, (tkp, DA), lambda i, j, q: (q, 0)), u, (DA, D), top, 1, None, (DA, D), tk=tkp),
        _wgrad("proj_wgrad_kv", (dkvp, (KPAD, 2 * DA), lambda i, j, q: (q + 1, 0)), u, (2 * DA, D), top, 1, None,
               (2 * DA, D), tk=KPAD),
        _wgrad("proj_wgrad_c", (dcin, (tkp, 2 * DC), lambda i, j, q: (q, 0)), u, (2 * DC, D), top, 1, None,
               (2 * DC, D), tk=tkp),
        _wgrad("proj_wgrad_g", (dgg, (tkp, 2 * D), lambda i, j, q: (q, 0)), u, (2 * D, D), top, 1, None,
               (2 * D, D), tk=tkp)], axis=0)

    tmu = 256
    a_ops = [(dq, (tmu, DA), row),
             (dkvp, (tmu, 2 * DA), lambda i, j, q: (i + KPAD // tmu, 0)),
             (dcin, (tmu, 2 * DC), row),
             (dgg, (tmu, 2 * D), row)]
    b_ops = [(wb[:DA], (DA, D), top), (wb[DA:3 * DA], (2 * DA, D), top),
             (wb[3 * DA:3 * DA + 2 * DC], (2 * DC, D), top), (wb[3 * DA + 2 * DC:], (2 * D, D), top)]

    def bwd_mix(duv, hv, dhv, fv, gm, gp):
        hh, r = _rms(hv)
        dx, dgm = _rms_bwd(hh, r, gm, duv)
        dh = dhv + dx
        fh, rf = _rms(fv)
        df, dgp = _rms_bwd(fh, rf, gp, 0.5 * dh)
        return dh, df, dgm, dgp

    (dh1, df1, d_mix, d_post1), (x_b, x_dw) = _mm_rows(
        "proj_bwd", t, tmu, a_ops, b_ops, [[(0, 0), (1, 1), (2, 2), (3, 3)]], NN, bwd_mix, [h1, dh2, f1],
        [g_mix, g_post1], [(D, F32), (D, BF16)], [D, D],
        job=("exchange", [g_b.reshape(1, NDEV, PS, D), g_dw[None]]))
    g_d1 = wgrad_down("ffn1_wgrad_d", s1, df1, (1, FF, D), 0, None)
    dab1, (x_d1,) = _ffn_bwd_act("ffn1_bwd_act", df1, w_d1, 0, a1, b1,
                                 job=("exchange", [g_d1.reshape(1, NDEV, FS, D)]))

    def wgrad_half(name, p, job=None):
        return _wgrad(name, (dab1, (None, tkf, tmw), lambda i, j, q: (p, q, i)), n1,
                      (None, tmw, D), lambda i, j, q: (0, i, 0), nfi, None, (1, FF, D), tk=tkf, job=job)

    g_g1 = wgrad_half("ffn1_wgrad_g", 0)
    g_u1, (x_g1,) = wgrad_half("ffn1_wgrad_u", 1, job=("exchange", [g_g1.reshape(1, NDEV, FS, D)]))

    def bwd_pre1(dnv, xv, dhv, g1):
        xh, r = _rms(xv)
        dx, dg1 = _rms_bwd(xh, r, g1, dnv)
        return dhv + dx, dg1

    (dx, d_pre1), (x_u1,) = _mm_rows(
        "ffn1_bwd_in", t, tmb,
        [(dab1, (None, tmb, FF), lambda i, j, q: (0, i, 0)), (dab1, (None, tmb, FF), lambda i, j, q: (1, i, 0))],
        [(w_gu1, (None, FF, D), lambda i, j, q: (0, 0, 0)), (w_gu1, (None, FF, D), lambda i, j, q: (1, 0, 0))],
        [[(0, 0), (1, 1)]], NN, bwd_pre1, [xs, dh1], [g_pre1], [(D, F32)], [D],
        job=("exchange", [g_u1.reshape(1, NDEV, FS, D)]))

    small_g = {"ffn1_norm_pre": d_pre1, "ffn1_norm_post": d_post1, "mix_norm_pre": d_mix, "gate_bias": d_gate_b,
               "rel_table": d_rel, "conv_glu_bias": d_glu_b, "conv_dw_b": d_dw_b, "conv_ln_g": d_ln_g,
               "conv_ln_b": d_ln_b, "mix_norm_post": d_mixp, "ffn2_norm_pre": d_pre2, "ffn2_norm_post": d_post2}
    return loss_row, dx, (x_g1, x_u1, x_d1, x_2, x_b, x_c, x_d, x_dw), small_g


def _step(args):
    names = WEIGHTS
    w = {n: args[n] for n in names}
    fs, ps, os_ = FS, PS, OS
    conv_dw_w = args["conv_dw_w"]
    loss_row, dx, (x_g1, x_u1, x_d1, x_2, x_b, x_c, x_d, x_dw), small_g = _local_step(
        args["x"][0], args["loss_target"][0], w, args["rel_table"])

    g_small = _pack_small(small_g, loss_row)
    (x_s,) = _comm_only("gather_small_grads", ("gather", [g_small[None]]))

    s_g1 = _sum_devices("sum_ffn1_g", x_g1.reshape(NDEV, fs, D), fs)
    s_u1 = _sum_devices("sum_ffn1_u", x_u1.reshape(NDEV, fs, D), fs)
    s_d1 = _sum_devices("sum_ffn1_d", x_d1.reshape(NDEV, fs, D), fs)
    s_2 = _sum_devices("sum_ffn2", x_2.reshape(NDEV, 3 * fs, D), fs).reshape(3, fs, D)
    s_b = _sum_devices("sum_proj", x_b.reshape(NDEV, ps, D), ps)
    s_c = _sum_devices("sum_mix", x_c.reshape(NDEV, os_, D), os_)
    s_d = _sum_devices("sum_out", x_d.reshape(NDEV, 2 * os_, DA), 2 * os_).reshape(2, os_, DA)
    s_dw = _sum_devices("sum_dw", x_dw.reshape(NDEV, DW_ROWS * 8, DC // NDEV), DW_ROWS * 8)

    grads = {
        "ffn1_w_gate": jnp.transpose(s_g1)[None], "ffn1_w_up": jnp.transpose(s_u1)[None], "ffn1_w_down": s_d1[None],
        "ffn2_w_gate": jnp.transpose(s_2[0])[None], "ffn2_w_up": jnp.transpose(s_2[1])[None], "ffn2_w_down": s_2[2][None],
        "w_in": jnp.transpose(s_b)[None], "w_out": s_c[None],
        "w_attn_out": jnp.transpose(s_d[0])[None], "conv_w_out": jnp.transpose(s_d[1])[None],
    }
    deltas, new_m, new_v = {}, {}, {}

    def flat2(a, n):
        return jnp.pad(a[0], ((0, 0), (0, RELP - NREL))) if n == "rel_table" else a.reshape(1, -1)

    small, loss_terms = _small_update(
        x_s.reshape(NDEV, SMALL_ROWS, D), {n: flat2(w[n], n) for n in SMALL},
        {n: flat2(args["m_" + n], n) for n in SMALL}, {n: flat2(args["v_" + n], n) for n in SMALL})
    for n in SMALL:
        vals = [a[:, :NREL] if n == "rel_table" else a for a in small[n]]
        grads[n], deltas[n], new_m[n], new_v[n] = [a.reshape(w[n].shape) for a in vals]

    big = ["ffn1_w_gate", "ffn1_w_up", "ffn1_w_down", "w_in", "w_attn_out", "conv_w_out", "w_out",
           "ffn2_w_gate", "ffn2_w_up", "ffn2_w_down"]
    for n in big:
        shp = w[n].shape
        two = lambda a: a.reshape(shp[1], shp[2])
        rows = shp[1]
        tr_ = rows // 2 if rows % 16 == 0 else rows
        d_, m_, v_ = _adamw("adamw_" + n, two(w[n]), two(grads[n]), two(args["m_" + n]), two(args["v_" + n]), tr_)
        deltas[n], new_m[n], new_v[n] = d_.reshape(shp), m_.reshape(shp), v_.reshape(shp)

    g_dw_own = _fold8("fold_dw", s_dw)[:CW]
    grads["conv_dw_w"] = g_dw_own.reshape(1, CW, 1, DC // NDEV)
    flat = lambda a: a.reshape(CW, DC // NDEV)
    d_, m_, v_ = _adamw("adamw_dw", flat(conv_dw_w), g_dw_own, flat(args["m_conv_dw_w"]), flat(args["v_conv_dw_w"]))
    shp = conv_dw_w.shape
    deltas["conv_dw_w"], new_m["conv_dw_w"], new_v["conv_dw_w"] = d_.reshape(shp), m_.reshape(shp), v_.reshape(shp)

    loss = jnp.sum(loss_terms)
    return (loss, dx[None], *[grads[n] for n in names], *[deltas[n] for n in names],
            *[new_m[n] for n in names], *[new_v[n] for n in names])


def _fold8(name, a):
    r8, c = a.shape

    def body(a_ref, o_ref):
        o_ref[...] = a_ref[...].reshape(r8 // 8, 8, c).sum(axis=1)

    return pl.pallas_call(
        body,
        name=name,
        out_shape=jax.ShapeDtypeStruct((r8 // 8, c), F32),
        in_specs=[pl.BlockSpec(memory_space=pltpu.VMEM)],
        out_specs=pl.BlockSpec(memory_space=pltpu.VMEM),
        compiler_params=_params(),
    )(a)
```

```python
import functools

import jax
import jax.numpy as jnp
from jax import lax
from jax.experimental import pallas as pl
from jax.experimental.pallas import tpu as pltpu

F32 = jnp.float32
BF16 = jnp.bfloat16

D = 1024
FF = 2816
DA = 512
DC = 512
NH = 8
CHUNK = 64
LEFT = 8
CW = 31
NREL = 257
EPS = 1e-6
NDEV = 8

QB = 4 * CHUNK
KW = LEFT * CHUNK + QB
KPAD = LEFT * CHUNK
RELW = KW + QB
HALO = 32

TM = 512
VMEM_LIMIT = 56 * 1024 * 1024

ADAM_LR, ADAM_B1, ADAM_B2, ADAM_EPS, ADAM_WD, ADAM_STEP = 0.001, 0.9, 0.999, 1e-08, 0.01, 10

NT = (((1,), (1,)), ((), ()))
NN = (((1,), (0,)), ((), ()))
TN = (((0,), (0,)), ((), ()))

MESH = pl.DeviceIdType.MESH
ANY = pl.BlockSpec(memory_space=pl.ANY)


def _params(sem=None, vmem=VMEM_LIMIT):
    return pltpu.CompilerParams(dimension_semantics=sem, vmem_limit_bytes=vmem)


def _sigmoid(x):
    return 0.5 * jnp.tanh(0.5 * x) + 0.5


def _mm(name, grid, a_ops, b_ops, groups, dims, epi, outs, extras=(), carry=None, job=None, params=(),
        partials=()):
    nk = grid[2]
    na, nb, ne, no, ng = len(a_ops), len(b_ops), len(extras), len(outs), len(groups)
    npar, npart = len(params), len(partials)
    nc = 0 if carry is None else 1

    def body(*refs):
        a_refs = refs[:na]
        b_refs = refs[na:na + nb]
        e_refs = refs[na + nb:na + nb + ne]
        p_refs = refs[na + nb + ne:na + nb + ne + npar]
        o0 = na + nb + ne + npar + nc
        o_refs = refs[o0:o0 + no]
        s_refs = refs[o0 + no:o0 + no + npart]
        acc_refs = refs[o0 + no + npart:o0 + no + npart + (ng if nk > 1 else 0)]
        part_refs = refs[len(refs) - npart:] if npart else ()
        k = pl.program_id(2)
        prods = []
        for grp in groups:
            p = None
            for ai, bi in grp:
                t = lax.dot_general(a_refs[ai][...], b_refs[bi][...], dims, preferred_element_type=F32)
                p = t if p is None else p + t
            prods.append(p)

        def finish(vals):
            res = epi(vals, [e[...] for e in e_refs] + [p[...] for p in p_refs])
            for o, r in zip(o_refs, res[:no]):
                o[...] = r.astype(o.dtype)
            if npart:
                i, j = pl.program_id(0), pl.program_id(1)

                @pl.when(jnp.logical_and(i == 0, j == 0))
                def _():
                    for acc in part_refs:
                        acc[...] = jnp.zeros_like(acc)

                for acc, r in zip(part_refs, res[no:]):
                    acc[...] += r.reshape(r.shape[0] // 8, 8, r.shape[-1]).sum(axis=0)

                @pl.when(jnp.logical_and(i == grid[0] - 1, j == grid[1] - 1))
                def _():
                    for s, acc in zip(s_refs, part_refs):
                        s[...] = acc[...].sum(axis=0, keepdims=True)

        if nk == 1:
            finish(prods)
        else:
            @pl.when(k == 0)
            def _():
                for acc, p in zip(acc_refs, prods):
                    acc[...] = p

            @pl.when(k > 0)
            def _():
                for acc, p in zip(acc_refs, prods):
                    acc[...] += p

            @pl.when(k == nk - 1)
            def _():
                finish([acc[...] for acc in acc_refs])

    in_specs = [pl.BlockSpec(blk, im) for _, blk, im in list(a_ops) + list(b_ops) + list(extras)]
    in_specs += [pl.BlockSpec(p.shape, lambda i, j, q: (0, 0)) for p in params]
    args = [arr for arr, _, _ in list(a_ops) + list(b_ops) + list(extras)] + list(params)
    aliases = {}
    if carry is not None:
        in_specs.append(ANY)
        args.append(carry[0])
        aliases = {len(args) - 1: carry[1]}
    scratch = []
    if nk > 1:
        for _ in range(ng):
            blk = tuple(b for b in outs[0][2] if b is not None)
            scratch.append(pltpu.VMEM(blk, F32))
    scratch += [pltpu.VMEM((8, c), F32) for c in partials]
    res, jres = _call(
        body, name=name, grid=grid, in_specs=in_specs, args=args,
        out_specs=[pl.BlockSpec(blk, im) for _, _, blk, im in outs]
        + [pl.BlockSpec((1, c), lambda i, j, q: (0, 0)) for c in partials],
        out_shape=[jax.ShapeDtypeStruct(shp, dt) for shp, dt, _, _ in outs]
        + [jax.ShapeDtypeStruct((1, c), F32) for c in partials],
        scratch=scratch, sem=("arbitrary",) * 3 if partials else ("parallel", "parallel", "arbitrary"),
        aliases=aliases, job=job)
    return res if job is None else (res, jres)


def _first(accs, extras):
    return (accs[0],)


def _mm_rows(name, t, tm, a_ops, b_ops, groups, dims, fn, extras, params, outs, partials=(), nk=1, job=None):
    ne = len(extras)

    def epi(accs, rest):
        return fn(*accs, *[r.astype(F32) for r in rest[:ne]], *rest[ne:])

    e_ops = [(arr, (tm, arr.shape[1]), lambda i, j, q: (i, 0)) for arr in extras]
    o_ops = [((t, c), dt, (tm, c), lambda i, j, q: (i, 0)) for c, dt in outs]
    return _mm(name, (t // tm, 1, nk), a_ops, b_ops, groups, dims, epi, o_ops, e_ops, job=job, params=params,
               partials=partials)


def _mm_simple(name, a, b, dims, out_dtype, tm, tn, b_row0=0, b_rows=None):
    m, kk = a.shape
    tm = min(tm, m)
    if dims is NT:
        n = b.shape[0] if b_rows is None else b_rows
        assert b.shape[1] == kk and b_row0 % tn == 0
        b_op = (b, (tn, kk), lambda i, j, q: (j + b_row0 // tn, 0))
    else:
        assert b.shape[0] == kk
        n = b.shape[1]
        b_op = (b, (kk, tn), lambda i, j, q: (0, j))
    a_op = (a, (tm, kk), lambda i, j, q: (i, 0))
    out = ((m, n), out_dtype, (tm, tn), lambda i, j, q: (i, j))
    return _mm(name, (m // tm, n // tn, 1), [a_op], [b_op], [[(0, 0)]], dims, _first, [out])[0]


def _rowwise(name, fn, tiled, params, outs, partials=(), tm=TM, job=None):
    t = tiled[0].shape[0]
    steps = t // tm
    nt, npar, no, npart = len(tiled), len(params), len(outs), len(partials)

    def body(*refs):
        t_refs = refs[:nt]
        p_refs = refs[nt:nt + npar]
        o_refs = refs[nt + npar:nt + npar + no]
        s_refs = refs[nt + npar + no:nt + npar + no + npart]
        acc_refs = refs[nt + npar + no + npart:]
        i = pl.program_id(0)
        res = fn(*[r[...].astype(F32) for r in t_refs], *[r[...] for r in p_refs])
        for o, r in zip(o_refs, res[:no]):
            o[...] = r.astype(o.dtype)

        @pl.when(i == 0)
        def _():
            for acc in acc_refs:
                acc[...] = jnp.zeros_like(acc)

        for acc, r in zip(acc_refs, res[no:]):
            acc[...] += r.reshape(tm // 8, 8, r.shape[-1]).sum(axis=0)

        @pl.when(i == steps - 1)
        def _():
            for s, acc in zip(s_refs, acc_refs):
                s[...] = acc[...].sum(axis=0, keepdims=True)

    in_specs = [pl.BlockSpec((tm, a.shape[1]), lambda i: (i, 0)) for a in tiled]
    in_specs += [pl.BlockSpec(p.shape, lambda i: (0, 0)) for p in params]
    out_specs = [pl.BlockSpec((tm, c), lambda i: (i, 0)) for c, _ in outs]
    out_specs += [pl.BlockSpec((1, c), lambda i: (0, 0)) for c in partials]
    out_shape = [jax.ShapeDtypeStruct((t, c), dt) for c, dt in outs]
    out_shape += [jax.ShapeDtypeStruct((1, c), F32) for c in partials]
    res, jres = _call(body, name=name, grid=(steps,), in_specs=in_specs, args=[*tiled, *params], out_specs=out_specs,
                      out_shape=out_shape, scratch=[pltpu.VMEM((8, c), F32) for c in partials], sem=("arbitrary",),
                      job=job)
    return res if job is None else (res, jres)


def _rms(x):
    r = lax.rsqrt(jnp.mean(x * x, axis=-1, keepdims=True) + EPS)
    return x * r, r


def _rms_bwd(xhat, r, g, dy):
    dxh = dy * g
    dx = r * (dxh - xhat * jnp.mean(dxh * xhat, axis=-1, keepdims=True))
    return dx, dy * xhat


def _ffn_up(name, n, wa, part, tm=512, tf=1408, job=None):
    t = n.shape[0]

    def epi(accs, extras):
        a, b = accs
        sg = _sigmoid(a)
        silu = a * sg
        return silu, b * (sg + silu * (1.0 - sg)), silu * b

    a_op = (n, (tm, D), lambda f, i, q: (i, 0))
    b_ops = [(wa, (None, tf, D), lambda f, i, q: (part, f, 0)),
             (wa, (None, tf, D), lambda f, i, q: (part + 1, f, 0))]
    outs = [((t, FF), BF16, (tm, tf), lambda f, i, q: (i, f))] * 3
    return _mm(name, (FF // tf, t // tm, 1), [a_op], b_ops, [[(0, 0)], [(0, 1)]], NT, epi, outs, job=job)


def _ffn_bwd_act(name, df, wa, part, a, b, tm=512, tf=1408, job=None):
    t = df.shape[0]

    def epi(accs, extras):
        ds = accs[0]
        return (jnp.stack([ds * extras[1].astype(F32), ds * extras[0].astype(F32)]),)

    a_op = (df, (tm, D), lambda f, i, q: (i, 0))
    b_op = (wa, (None, tf, D), lambda f, i, q: (part, f, 0))
    extras = [(a, (tm, tf), lambda f, i, q: (i, f)), (b, (tm, tf), lambda f, i, q: (i, f))]
    out = ((2, t, FF), BF16, (2, tm, tf), lambda f, i, q: (0, i, f))
    res = _mm(name, (FF // tf, t // tm, 1), [a_op], [b_op], [[(0, 0)]], NT, epi, [out], extras, job=job)
    return res[0] if job is None else (res[0][0], res[1])


def _wgrad(name, dy_op, x, out_block, out_map, gi, carry_buf, out_shape, tk=512, job=None):
    t, c = x.shape
    b_op = (x, (tk, c), lambda i, j, q: (q, 0))
    out = (out_shape, BF16, out_block, out_map)
    carry = None if carry_buf is None else (carry_buf, 0)
    res = _mm(name, (gi, 1, t // tk), [dy_op], [b_op], [[(0, 0)]], TN, _first, [out], carry=carry, job=job)
    return res[0] if job is None else (res[0][0], res[1])


def _rel_onehot():
    j = lax.broadcasted_iota(jnp.int32, (384, RELW), 0)
    xx = lax.broadcasted_iota(jnp.int32, (384, RELW), 1)
    idx = jnp.clip(KPAD + QB - xx, -128, 128) + 128
    return (j == idx).astype(F32)


def _relbias_fwd(table):
    def body(t_ref, o_ref):
        rev = jnp.dot(t_ref[...], _rel_onehot(), precision=lax.Precision.HIGHEST, preferred_element_type=F32)
        for r in range(QB):
            row = pltpu.roll(rev, (RELW - (QB - r)) % RELW, 1)[:, :KW]
            rr = lax.broadcasted_iota(jnp.int32, (NH, KW), 1) >> 6
            ok = (rr >= (r // CHUNK)) & (rr <= (r // CHUNK) + LEFT)
            row = jnp.where(ok, row, -1e30)
            for h in range(NH):
                o_ref[h * QB + r:h * QB + r + 1, :] = row[h:h + 1, :]

    return pl.pallas_call(
        body,
        name="relbias_fwd",
        out_shape=jax.ShapeDtypeStruct((NH * QB, KW), F32),
        in_specs=[pl.BlockSpec(memory_space=pltpu.VMEM)],
        out_specs=pl.BlockSpec(memory_space=pltpu.VMEM),
        compiler_params=_params(),
    )(table)


def _relbias_bwd(dbias):
    def body(d_ref, o_ref):
        acc = jnp.zeros((NH, RELW), F32)
        for r in range(QB):
            rows = jnp.concatenate([d_ref[h * QB + r:h * QB + r + 1, :] for h in range(NH)], axis=0)
            wide = jnp.concatenate([rows, jnp.zeros((NH, RELW - KW), F32)], axis=1)
            acc = acc + pltpu.roll(wide, QB - r, 1)
        o_ref[...] = lax.dot_general(acc, _rel_onehot(), NT, precision=lax.Precision.HIGHEST,
                                     preferred_element_type=F32)

    return pl.pallas_call(
        body,
        name="relbias_bwd",
        out_shape=jax.ShapeDtypeStruct((NH, 384), F32),
        in_specs=[pl.BlockSpec(memory_space=pltpu.VMEM)],
        out_specs=pl.BlockSpec(memory_space=pltpu.VMEM),
        compiler_params=_params(),
    )(dbias)


HQ = QB // 2


def _stack_heads(x_pair):
    first = lax.broadcasted_iota(jnp.int32, (1, 128), 1) < 64
    zero = jnp.zeros_like(x_pair)
    a, b = jnp.where(first, x_pair, zero), jnp.where(first, zero, x_pair)
    return jnp.concatenate([a[:HQ], b[:HQ], a[HQ:], b[HQ:]], axis=0), first


def _unstack_heads(o, first):
    return jnp.concatenate([jnp.where(first, o[0:HQ], o[HQ:2 * HQ]),
                            jnp.where(first, o[2 * HQ:3 * HQ], o[3 * HQ:4 * HQ])], axis=0)


def _half_cols(hf):
    return slice(hf * HQ, hf * HQ + KW - HQ)


def _half_bias(b_ref, pair, hf):
    rows = lambda h: slice(h * QB + hf * HQ, h * QB + (hf + 1) * HQ)
    return jnp.concatenate([b_ref[rows(2 * pair), _half_cols(hf)], b_ref[rows(2 * pair + 1), _half_cols(hf)]], axis=0)


def _half_probs(s_full, b_ref, pair, hf, key_ok):
    s = s_full[2 * hf * HQ:2 * (hf + 1) * HQ, _half_cols(hf)] + _half_bias(b_ref, pair, hf)
    if key_ok is not None:
        s = jnp.where(key_ok[:, _half_cols(hf)], s, -1e30)
    e = jnp.exp(s - jnp.max(s, axis=-1, keepdims=True))
    return e * (1.0 / jnp.sum(e, axis=-1, keepdims=True))


def _widen(top, bottom):
    z = jnp.zeros((2 * HQ, HQ), top.dtype)
    return jnp.concatenate([jnp.concatenate([top, z], axis=1), jnp.concatenate([z, bottom], axis=1)], axis=0)


def _attn_fwd(qkv, kvp, bias, job=None):
    t = qkv.shape[0]

    def body(q_ref, kv_ref, b_ref, o_ref):
        i = pl.program_id(0)

        def run(masked):
            start = pl.multiple_of(i * QB, QB)
            col = lax.broadcasted_iota(jnp.int32, (1, KW), 1)
            key_ok = (col >= KPAD - i * QB) if masked else None
            for pair in range(4):
                lo = pair * 128
                kw = kv_ref[pl.ds(start, KW), lo:lo + 128]
                vw = kv_ref[pl.ds(start, KW), DA + lo:DA + lo + 128]
                qs, first = _stack_heads(q_ref[:, lo:lo + 128])
                s = lax.dot_general(qs * 0.125, kw, NT, preferred_element_type=F32)
                p = _widen(*[_half_probs(s, b_ref, pair, hf, key_ok).astype(BF16) for hf in range(2)])
                o = jnp.dot(p, vw, preferred_element_type=F32)
                o_ref[:, lo:lo + 128] = _unstack_heads(o, first).astype(BF16)

        pl.when(i < KPAD // QB)(lambda: run(True))
        pl.when(i >= KPAD // QB)(lambda: run(False))

    res, jres = _call(
        body, name="attn_fwd", grid=(t // QB,),
        in_specs=[pl.BlockSpec((QB, DA), lambda i: (i, 0)),
                  pl.BlockSpec(memory_space=pltpu.VMEM),
                  pl.BlockSpec(memory_space=pltpu.VMEM)],
        args=[qkv, kvp, bias],
        out_specs=[pl.BlockSpec((QB, DA), lambda i: (i, 0))],
        out_shape=[jax.ShapeDtypeStruct((t, DA), BF16)],
        sem=("arbitrary",), job=job)
    return res[0], jres


def _attn_bwd(qkv, kvp, bias, datt, job=None):
    t = qkv.shape[0]
    nb = t // QB
    flush = (KW - QB) // QB

    def body(q_ref, kv_ref, b_ref, do_ref, dq_ref, dkv_ref, db_out, acc_ref, db_ref):
        i = pl.program_id(0)

        @pl.when(i == 0)
        def _():
            acc_ref[...] = jnp.zeros_like(acc_ref)
            db_ref[...] = jnp.zeros_like(db_ref)

        def run(masked):
            start = pl.multiple_of(i * QB, QB)
            col = lax.broadcasted_iota(jnp.int32, (1, KW), 1)
            key_ok = (col >= KPAD - i * QB) if masked else None
            for pair in range(4):
                lo = pair * 128
                kw = kv_ref[pl.ds(start, KW), lo:lo + 128]
                vw = kv_ref[pl.ds(start, KW), DA + lo:DA + lo + 128]
                qs, first = _stack_heads(q_ref[:, lo:lo + 128])
                qs = qs * 0.125
                dos, _ = _stack_heads(do_ref[:, lo:lo + 128])
                s = lax.dot_general(qs, kw, NT, preferred_element_type=F32)
                dp = lax.dot_general(dos, vw, NT, preferred_element_type=F32)
                ps, dss = [], []
                for hf in range(2):
                    p = _half_probs(s, b_ref, pair, hf, key_ok)
                    dph = dp[2 * hf * HQ:2 * (hf + 1) * HQ, _half_cols(hf)]
                    ds = p * (dph - jnp.sum(p * dph, axis=-1, keepdims=True))
                    for k, h in enumerate((2 * pair, 2 * pair + 1)):
                        db_ref[h * QB + hf * HQ:h * QB + (hf + 1) * HQ, _half_cols(hf)] += ds[k * HQ:(k + 1) * HQ]
                    ps.append(p.astype(BF16))
                    dss.append(ds.astype(BF16))
                pb, dsb = _widen(*ps), _widen(*dss)
                dq = jnp.dot(dsb, kw, preferred_element_type=F32)
                dq_ref[:, lo:lo + 128] = (_unstack_heads(dq, first) * 0.125).astype(BF16)
                acc_ref[:, lo:lo + 128] += lax.dot_general(dsb, qs, TN, preferred_element_type=F32)
                acc_ref[:, DA + lo:DA + lo + 128] += lax.dot_general(pb, dos, TN, preferred_element_type=F32)

        pl.when(i < KPAD // QB)(lambda: run(True))
        pl.when(jnp.logical_and(i >= KPAD // QB, i < nb))(lambda: run(False))

        dkv_ref[...] = acc_ref[0:QB, :].astype(BF16)
        rest = acc_ref[QB:KW, :]
        acc_ref[0:KW - QB, :] = rest
        acc_ref[KW - QB:KW, :] = jnp.zeros((QB, 2 * DA), F32)

        @pl.when(i == nb + flush - 1)
        def _():
            pltpu.sync_copy(db_ref, db_out)

    last = nb - 1
    res, jres = _call(
        body, name="attn_bwd", grid=(nb + flush,),
        in_specs=[pl.BlockSpec((QB, DA), lambda i: (jnp.minimum(i, last), 0)),
                  pl.BlockSpec(memory_space=pltpu.VMEM),
                  pl.BlockSpec(memory_space=pltpu.VMEM),
                  pl.BlockSpec((QB, DA), lambda i: (jnp.minimum(i, last), 0))],
        args=[qkv, kvp, bias, datt],
        out_specs=[pl.BlockSpec((QB, DA), lambda i: (jnp.minimum(i, last), 0)),
                   pl.BlockSpec((QB, 2 * DA), lambda i: (i, 0)),
                   ANY],
        out_shape=[jax.ShapeDtypeStruct((t, DA), BF16),
                   jax.ShapeDtypeStruct((t + KPAD, 2 * DA), BF16),
                   jax.ShapeDtypeStruct((NH * QB, KW), F32)],
        scratch=[pltpu.VMEM((KW, 2 * DA), F32), pltpu.VMEM((NH * QB, KW), F32)], sem=("arbitrary",), job=job)
    return res, jres


def _glu(c, gb):
    cb = c + gb
    return cb[:, :DC] * _sigmoid(cb[:, DC:])


def _ln_swish(pre, g, b):
    mu = jnp.mean(pre, axis=-1, keepdims=True)
    xc = pre - mu
    r = lax.rsqrt(jnp.mean(xc * xc, axis=-1, keepdims=True) + EPS)
    xhat = xc * r
    y = xhat * g + b
    return xhat, r, y


RT = 32


def _shifted_copies(src_ref, sh_ref, rows):
    for b in range(1, 8):
        sh_ref[b - 1, :, :] = src_ref[pl.ds(b, rows), :]


def _tap(src_ref, sh_ref, off, r0, rows=RT):
    a, b = divmod(off, 8)
    ref = src_ref if b == 0 else sh_ref.at[b - 1]
    if isinstance(r0, int):
        return ref[r0 + 8 * a:r0 + 8 * a + rows, :]
    return ref[pl.ds(pl.multiple_of(r0 + 8 * a, 8), rows), :]


def _conv_fwd(cin, glu_b, dw_w, dw_b, ln_g, ln_b, tm=TM):
    t = cin.shape[0]
    hb = tm // HALO

    def body(c_ref, h_ref, gb_ref, w_ref, wb_ref, g_ref, b_ref, cs_ref, pre_ref, ext_ref, sh_ref):
        i = pl.program_id(0)
        halo = _glu(h_ref[...], gb_ref[...])
        ext_ref[0:HALO, :] = jnp.where(i > 0, halo, jnp.zeros_like(halo))
        ext_ref[HALO:HALO + tm, :] = _glu(c_ref[...], gb_ref[...])
        ext_ref[HALO + tm:HALO + tm + 8, :] = jnp.zeros((8, DC), F32)
        _shifted_copies(ext_ref, sh_ref, HALO + tm)

        def tile(rt, carry):
            r0 = pl.multiple_of(rt * RT, RT)
            acc = jnp.zeros((RT, DC), F32) + wb_ref[...]
            for j in range(CW):
                acc = acc + w_ref[j:j + 1, :] * _tap(ext_ref, sh_ref, HALO - (CW - 1) + j, r0)
            pre_ref[pl.ds(r0, RT), :] = acc
            return carry

        lax.fori_loop(0, tm // RT, tile, 0, unroll=2)
        _, _, y = _ln_swish(pre_ref[...], g_ref[...], b_ref[...])
        cs_ref[...] = (y * _sigmoid(y)).astype(BF16)

    vec = lambda n: pl.BlockSpec((1, n), lambda i: (0, 0))
    return pl.pallas_call(
        body,
        name="conv_fwd",
        grid=(t // tm,),
        in_specs=[pl.BlockSpec((tm, 2 * DC), lambda i: (i, 0)),
                  pl.BlockSpec((HALO, 2 * DC), lambda i: (jnp.maximum(i * hb - 1, 0), 0)),
                  vec(2 * DC), pl.BlockSpec((CW, DC), lambda i: (0, 0)), vec(DC), vec(DC), vec(DC)],
        out_specs=[pl.BlockSpec((tm, DC), lambda i: (i, 0)), pl.BlockSpec((tm, DC), lambda i: (i, 0))],
        out_shape=[jax.ShapeDtypeStruct((t, DC), BF16), jax.ShapeDtypeStruct((t, DC), F32)],
        scratch_shapes=[pltpu.VMEM((HALO + tm + 8, DC), F32), pltpu.VMEM((7, HALO + tm, DC), F32)],
        compiler_params=_params(("arbitrary",)),
    )(cin, cin, glu_b, dw_w, dw_b, ln_g, ln_b)


def _conv_bwd(dcs, pre, cin, glu_b, dw_w, ln_g, ln_b, tm=TM):
    t = cin.shape[0]
    hb = tm // HALO
    steps = t // tm
    nhb = t // HALO

    def dpre_of(dcs_v, pre_v, g, b):
        xhat, r, y = _ln_swish(pre_v, g, b)
        sg = _sigmoid(y)
        dy = dcs_v * (sg * (1.0 + y * (1.0 - sg)))
        dxh = dy * g
        dpre = r * (dxh - jnp.mean(dxh, axis=-1, keepdims=True)
                    - xhat * jnp.mean(dxh * xhat, axis=-1, keepdims=True))
        return dpre, dy * xhat, dy

    def body(dcs_ref, dcsn_ref, pre_ref, pren_ref, c_ref, ch_ref, gb_ref, w_ref, g_ref, b_ref,
             dc_ref, dgb_ref, dw_ref, dwb_ref, dg_ref, db_ref,
             gext_ref, dext_ref, shg_ref, shd_ref, a_gb, a_w, a_wb, a_g, a_b):
        i = pl.program_id(0)

        @pl.when(i == 0)
        def _():
            for a in (a_gb, a_w, a_wb, a_g, a_b):
                a[...] = jnp.zeros_like(a)

        fold = lambda v: v.reshape(v.shape[0] // 8, 8, v.shape[-1]).sum(axis=0)
        g, b = g_ref[...], b_ref[...]
        dpre, dg_t, db_t = dpre_of(dcs_ref[...], pre_ref[...], g, b)
        dpre_n, _, _ = dpre_of(dcsn_ref[...], pren_ref[...], g, b)
        dext_ref[0:tm, :] = dpre
        dext_ref[tm:tm + HALO, :] = jnp.where(i < steps - 1, dpre_n, jnp.zeros_like(dpre_n))
        dext_ref[tm + HALO:tm + HALO + 8, :] = jnp.zeros((8, DC), F32)
        a_wb[...] += fold(dpre)
        a_g[...] += fold(dg_t)
        a_b[...] += fold(db_t)
        halo = _glu(ch_ref[...], gb_ref[...])
        gext_ref[0:HALO, :] = jnp.where(i > 0, halo, jnp.zeros_like(halo))
        gext_ref[HALO:HALO + tm, :] = _glu(c_ref[...], gb_ref[...])
        gext_ref[HALO + tm:HALO + tm + 8, :] = jnp.zeros((8, DC), F32)
        _shifted_copies(gext_ref, shg_ref, HALO + tm)
        _shifted_copies(dext_ref, shd_ref, HALO + tm)

        for j in range(CW):
            a_w[8 * j:8 * j + 8, :] += fold(dext_ref[0:tm, :] * _tap(gext_ref, shg_ref, HALO - (CW - 1) + j, 0, tm))

        def tile(rt, carry):
            r0 = pl.multiple_of(rt * RT, RT)
            dglu = jnp.zeros((RT, DC), F32)
            for j in range(CW):
                dglu = dglu + w_ref[j:j + 1, :] * _tap(dext_ref, shd_ref, CW - 1 - j, r0)
            gext_ref[pl.ds(r0, RT), :] = dglu
            return carry

        lax.fori_loop(0, tm // RT, tile, 0, unroll=2)
        dglu = gext_ref[0:tm, :]
        cb = c_ref[...] + gb_ref[...]
        sg = _sigmoid(cb[:, DC:])
        dc = jnp.concatenate([dglu * sg, dglu * cb[:, :DC] * sg * (1.0 - sg)], axis=1)
        dc_ref[...] = dc.astype(BF16)
        a_gb[...] += fold(dc)

        @pl.when(i == steps - 1)
        def _():
            dgb_ref[...] = a_gb[...].sum(axis=0, keepdims=True)
            dw_ref[...] = a_w[...]
            dwb_ref[...] = a_wb[...].sum(axis=0, keepdims=True)
            dg_ref[...] = a_g[...].sum(axis=0, keepdims=True)
            db_ref[...] = a_b[...].sum(axis=0, keepdims=True)

    vec = lambda n: pl.BlockSpec((1, n), lambda i: (0, 0))
    nxt = lambda i: (jnp.minimum((i + 1) * hb, nhb - 1), 0)
    prv = lambda i: (jnp.maximum(i * hb - 1, 0), 0)
    return pl.pallas_call(
        body,
        name="conv_bwd",
        grid=(steps,),
        in_specs=[pl.BlockSpec((tm, DC), lambda i: (i, 0)), pl.BlockSpec((HALO, DC), nxt),
                  pl.BlockSpec((tm, DC), lambda i: (i, 0)), pl.BlockSpec((HALO, DC), nxt),
                  pl.BlockSpec((tm, 2 * DC), lambda i: (i, 0)), pl.BlockSpec((HALO, 2 * DC), prv),
                  vec(2 * DC), pl.BlockSpec((CW, DC), lambda i: (0, 0)), vec(DC), vec(DC)],
        out_specs=[pl.BlockSpec((tm, 2 * DC), lambda i: (i, 0)), vec(2 * DC),
                   pl.BlockSpec((CW * 8, DC), lambda i: (0, 0)), vec(DC), vec(DC), vec(DC)],
        out_shape=[jax.ShapeDtypeStruct((t, 2 * DC), BF16), jax.ShapeDtypeStruct((1, 2 * DC), F32),
                   jax.ShapeDtypeStruct((CW * 8, DC), F32), jax.ShapeDtypeStruct((1, DC), F32),
                   jax.ShapeDtypeStruct((1, DC), F32), jax.ShapeDtypeStruct((1, DC), F32)],
        scratch_shapes=[pltpu.VMEM((HALO + tm + 8, DC), F32), pltpu.VMEM((tm + HALO + 8, DC), F32),
                        pltpu.VMEM((7, HALO + tm, DC), F32), pltpu.VMEM((7, HALO + tm, DC), F32),
                        pltpu.VMEM((8, 2 * DC), F32), pltpu.VMEM((CW * 8, DC), F32),
                        pltpu.VMEM((8, DC), F32), pltpu.VMEM((8, DC), F32), pltpu.VMEM((8, DC), F32)],
        compiler_params=_params(("arbitrary",)),
    )(dcs, dcs, pre, pre, cin, cin, glu_b, dw_w, ln_g, ln_b)


def _place():
    x, y, c = lax.axis_index("x"), lax.axis_index("y"), lax.axis_index("c")
    return x, y, c


def _peers(x, y, c):
    out = []
    for k in range(1, NDEV):
        fx, fy, fc = (k >> 2) & 1, (k >> 1) & 1, k & 1
        px = 1 - x if fx else x
        py = 1 - y if fy else y
        pc = 1 - c if fc else c
        out.append((px, py, pc))
    return out


def _job_out_shapes(job):
    kind, arrays = job
    if kind == "gather":
        return [jax.ShapeDtypeStruct((a.shape[0], NDEV) + a.shape[1:], a.dtype) for a in arrays]
    return [jax.ShapeDtypeStruct((NDEV, a.shape[0]) + a.shape[2:], a.dtype) for a in arrays]


def _job_scratch(job):
    n = len(job[1])
    return [pltpu.SemaphoreType.DMA((n, NDEV - 1)), pltpu.SemaphoreType.DMA((n, NDEV - 1)),
            pltpu.SemaphoreType.DMA((n,))]


def _gather_parts(ins, outs, send_sems, recv_sems, local_sems):
    x, y, c = _place()
    me, sib = (x, y, c), (x, y, 1 - c)
    chips = [(1 - x, y), (x, 1 - y), (1 - x, 1 - y)]

    def copy(a, k, block, to, src=None):
        px, py, pc = block
        dst = outs[a].at[:, 4 * px + 2 * py + pc]
        return pltpu.make_async_remote_copy(
            src_ref=dst if src is None else src, dst_ref=dst,
            send_sem=send_sems.at[a, k], recv_sem=recv_sems.at[a, k], device_id=to, device_id_type=MESH)

    n = len(ins)
    local = [pltpu.make_async_copy(ins[a], outs[a].at[:, 4 * x + 2 * y + c], local_sems.at[a]) for a in range(n)]
    first = [[copy(a, 0, me, sib, src=ins[a])] + [copy(a, 1 + j, me, (*chip, c), src=ins[a])
                                                   for j, chip in enumerate(chips)] for a in range(n)]

    def start():
        for a in range(n):
            local[a].start()
            for cp in first[a]:
                cp.start()

    def relay():
        for j, chip in enumerate(chips):
            for a in range(n):
                copy(a, 1 + j, (*chip, c), me).wait_recv()
                copy(a, 4 + j, (*chip, c), sib).start()

    def finish():
        for a in range(n):
            copy(a, 0, sib, me).wait_recv()
            for j, chip in enumerate(chips):
                copy(a, 4 + j, (*chip, 1 - c), me).wait_recv()
        for a in range(n):
            for cp in first[a]:
                cp.wait_send()
            local[a].wait()
            for j, chip in enumerate(chips):
                copy(a, 4 + j, (*chip, c), sib).wait_send()

    return start, relay, finish


def _exchange_parts(ins, outs, send_sems, recv_sems, local_sems):
    x, y, c = _place()
    me = 4 * x + 2 * y + c
    n = len(ins)
    peers = _peers(x, y, c)
    local = [pltpu.make_async_copy(ins[a].at[:, me], outs[a].at[me], local_sems.at[a]) for a in range(n)]

    def copy(a, k):
        px, py, pc = peers[k]
        return pltpu.make_async_remote_copy(
            src_ref=ins[a].at[:, 4 * px + 2 * py + pc], dst_ref=outs[a].at[me],
            send_sem=send_sems.at[a, k], recv_sem=recv_sems.at[a, k], device_id=peers[k], device_id_type=MESH)

    def arrival(a, k):
        px, py, pc = peers[k]
        return pltpu.make_async_remote_copy(
            src_ref=ins[a].at[:, me], dst_ref=outs[a].at[4 * px + 2 * py + pc],
            send_sem=send_sems.at[a, k], recv_sem=recv_sems.at[a, k], device_id=peers[k], device_id_type=MESH)

    def start():
        for a in range(n):
            local[a].start()
            for k in range(NDEV - 1):
                copy(a, k).start()

    def finish():
        for a in range(n):
            for k in range(NDEV - 1):
                arrival(a, k).wait_recv()
        for a in range(n):
            for k in range(NDEV - 1):
                copy(a, k).wait_send()
            local[a].wait()

    return start, None, finish


def _call(body, *, name, grid, in_specs, args, out_specs, out_shape, scratch=(), sem=None, aliases=None, job=None):
    aliases = dict(aliases or {})
    if job is None:
        res = pl.pallas_call(
            body, name=name, grid=grid, in_specs=list(in_specs), out_specs=list(out_specs),
            out_shape=list(out_shape), scratch_shapes=list(scratch), input_output_aliases=aliases,
            compiler_params=_params(sem))(*args)
        return list(res), []
    kind, arrays = job
    n_in, n_out, n_scr, nj = len(args), len(out_shape), len(scratch), len(arrays)

    def wrapped(*refs):
        ins = refs[:n_in]
        jin = refs[n_in:n_in + nj]
        o0 = n_in + nj
        outs = refs[o0:o0 + n_out]
        jout = refs[o0 + n_out:o0 + n_out + nj]
        s0 = o0 + n_out + nj
        scr = refs[s0:s0 + n_scr]
        sems = refs[s0 + n_scr:]
        parts = _gather_parts if kind == "gather" else _exchange_parts
        start, relay, finish = parts(jin, jout, *sems)
        if not grid:
            start()
            body(*ins, *outs, *scr)
            if relay is not None:
                relay()
            finish()
            return
        first = last = None
        step, steps = 0, 1
        for d, g in enumerate(grid):
            f, l = pl.program_id(d) == 0, pl.program_id(d) == g - 1
            first = f if first is None else jnp.logical_and(first, f)
            last = l if last is None else jnp.logical_and(last, l)
            step, steps = step * g + pl.program_id(d), steps * g
        pl.when(first)(start)
        body(*ins, *outs, *scr)
        if relay is not None:
            pl.when(step == (7 * steps) // 8)(relay)
        pl.when(last)(finish)

    res = pl.pallas_call(
        wrapped, name=name, grid=grid, in_specs=list(in_specs) + [ANY] * nj,
        out_specs=list(out_specs) + [ANY] * nj, out_shape=list(out_shape) + _job_out_shapes(job),
        scratch_shapes=list(scratch) + _job_scratch(job), input_output_aliases=aliases,
        compiler_params=pltpu.CompilerParams(
            dimension_semantics=None if not grid else ("arbitrary",) * len(grid),
            vmem_limit_bytes=VMEM_LIMIT, has_side_effects=True))(*args, *arrays)
    return list(res[:n_out]), list(res[n_out:])


def _comm_only(name, job):
    return _call(lambda: None, name=name, grid=(), in_specs=[], args=[], out_specs=[], out_shape=[], job=job)[1]


def _sum_devices(name, parts, tr):
    _, r, c = parts.shape

    def body(p_ref, o_ref):
        acc = p_ref[0].astype(F32)
        for d in range(1, NDEV):
            acc = acc + p_ref[d].astype(F32)
        o_ref[...] = acc

    return pl.pallas_call(
        body,
        name=name,
        grid=(r // tr,),
        in_specs=[pl.BlockSpec((NDEV, tr, c), lambda i: (0, i, 0))],
        out_specs=pl.BlockSpec((tr, c), lambda i: (i, 0)),
        out_shape=jax.ShapeDtypeStruct((r, c), F32),
        compiler_params=_params(("parallel",)),
    )(parts)


def _adamw_sum(name, w, parts, m, v, tr):
    r, c = w.shape

    def body(w_ref, p_ref, m_ref, v_ref, g_ref, d_ref, nm_ref, nv_ref):
        gv = p_ref[0].astype(F32)
        for d in range(1, NDEV):
            gv = gv + p_ref[d].astype(F32)
        nm = ADAM_B1 * m_ref[...] + (1.0 - ADAM_B1) * gv
        nv = ADAM_B2 * v_ref[...] + (1.0 - ADAM_B2) * (gv * gv)
        m_hat = nm / (1.0 - ADAM_B1 ** ADAM_STEP)
        v_hat = nv / (1.0 - ADAM_B2 ** ADAM_STEP)
        g_ref[...] = gv
        d_ref[...] = -ADAM_LR * (m_hat / (jnp.sqrt(v_hat) + ADAM_EPS) + ADAM_WD * w_ref[...])
        nm_ref[...] = nm
        nv_ref[...] = nv

    spec = pl.BlockSpec((tr, c), lambda i: (i, 0))
    return pl.pallas_call(
        body,
        name=name,
        grid=(r // tr,),
        in_specs=[spec, pl.BlockSpec((NDEV, tr, c), lambda i: (0, i, 0)), spec, spec],
        out_specs=[spec] * 4,
        out_shape=[jax.ShapeDtypeStruct((r, c), F32)] * 4,
        compiler_params=_params(("parallel",)),
    )(w, parts, m, v)


def _adamw(name, w, g, m, v, tr=None):
    r, c = w.shape
    tr = r if tr is None else tr

    def body(w_ref, g_ref, m_ref, v_ref, d_ref, nm_ref, nv_ref):
        gv = g_ref[...]
        nm = ADAM_B1 * m_ref[...] + (1.0 - ADAM_B1) * gv
        nv = ADAM_B2 * v_ref[...] + (1.0 - ADAM_B2) * (gv * gv)
        m_hat = nm / (1.0 - ADAM_B1 ** ADAM_STEP)
        v_hat = nv / (1.0 - ADAM_B2 ** ADAM_STEP)
        d_ref[...] = -ADAM_LR * (m_hat / (jnp.sqrt(v_hat) + ADAM_EPS) + ADAM_WD * w_ref[...])
        nm_ref[...] = nm
        nv_ref[...] = nv

    spec = pl.BlockSpec((tr, c), lambda i: (i, 0))
    return pl.pallas_call(
        body,
        name=name,
        grid=(r // tr,),
        in_specs=[spec] * 4,
        out_specs=[spec] * 3,
        out_shape=[jax.ShapeDtypeStruct((r, c), F32)] * 3,
        compiler_params=_params(("parallel",)),
    )(w, g, m, v)


SMALL = ["ffn1_norm_pre", "ffn1_norm_post", "mix_norm_pre", "gate_bias", "rel_table", "conv_glu_bias",
         "conv_dw_b", "conv_ln_g", "conv_ln_b", "mix_norm_post", "ffn2_norm_pre", "ffn2_norm_post"]
SMALL_ROWS = 24
DW_ROWS = 32


LOSS_ROW = 20
RELP = 384
SMALL_LAYOUT = {}
_r = 0
for _name, _n in zip(SMALL, [D, D, D, 2 * D, None, 2 * DC, DC, DC, DC, D, D, D]):
    if _n is None:
        SMALL_LAYOUT[_name] = (_r, NH, RELP)
        _r += NH
    else:
        SMALL_LAYOUT[_name] = (_r, -(-_n // D), min(_n, D))
        _r += -(-_n // D)
assert _r == LOSS_ROW


def _small_pieces(name):
    r0, nr, nc = SMALL_LAYOUT[name]
    if name == "rel_table":
        return [(slice(r0, r0 + NH), slice(0, nc), slice(0, NH), slice(0, nc))]
    return [(slice(r0 + k, r0 + k + 1), slice(0, nc), slice(0, 1), slice(k * nc, (k + 1) * nc)) for k in range(nr)]


def _pack_small(vals, loss_row):
    def body(*refs):
        o = refs[-1]
        o[...] = jnp.zeros_like(o)
        for ref, name in zip(refs, SMALL):
            for prow, pcol, arow, acol in _small_pieces(name):
                o[prow, pcol] = ref[arow, acol]
        o[LOSS_ROW:LOSS_ROW + 1, :] = refs[len(SMALL)][...]

    vm = pl.BlockSpec(memory_space=pltpu.VMEM)
    return pl.pallas_call(body, name="pack_small", out_shape=jax.ShapeDtypeStruct((SMALL_ROWS, D), F32),
                          in_specs=[vm] * (len(SMALL) + 1), out_specs=vm,
                          compiler_params=_params())(*[vals[n] for n in SMALL], loss_row)


def _small_update(parts, w, m, v):
    ns = len(SMALL)

    def body(p_ref, *refs):
        ins, outs = refs[:3 * ns], refs[3 * ns:]

        def total(prow, pcol):
            g = p_ref[0, prow, pcol]
            for d in range(1, NDEV):
                g = g + p_ref[d, prow, pcol]
            return g

        for q, name in enumerate(SMALL):
            w_ref, m_ref, v_ref = ins[3 * q:3 * q + 3]
            o_g, o_d, o_m, o_v = outs[4 * q:4 * q + 4]
            for prow, pcol, arow, acol in _small_pieces(name):
                g = total(prow, pcol)
                nm = ADAM_B1 * m_ref[arow, acol] + (1.0 - ADAM_B1) * g
                nv = ADAM_B2 * v_ref[arow, acol] + (1.0 - ADAM_B2) * (g * g)
                m_hat = nm / (1.0 - ADAM_B1 ** ADAM_STEP)
                v_hat = nv / (1.0 - ADAM_B2 ** ADAM_STEP)
                o_g[arow, acol] = g
                o_d[arow, acol] = -ADAM_LR * (m_hat / (jnp.sqrt(v_hat) + ADAM_EPS) + ADAM_WD * w_ref[arow, acol])
                o_m[arow, acol] = nm
                o_v[arow, acol] = nv
        outs[4 * ns][...] = total(slice(LOSS_ROW, LOSS_ROW + 1), slice(None))

    args, out_shape = [], []
    for name in SMALL:
        args += [w[name], m[name], v[name]]
        out_shape += [jax.ShapeDtypeStruct(w[name].shape, F32)] * 4
    out_shape.append(jax.ShapeDtypeStruct((1, D), F32))
    vm = pl.BlockSpec(memory_space=pltpu.VMEM)
    res = pl.pallas_call(body, name="small_update", out_shape=out_shape, in_specs=[vm] * (1 + 3 * ns),
                         out_specs=[vm] * len(out_shape), compiler_params=_params())(parts, *args)
    return {name: res[4 * q:4 * q + 4] for q, name in enumerate(SMALL)}, res[-1]


def kernel(x, ffn1_norm_pre, ffn1_w_gate, ffn1_w_up, ffn1_w_down, ffn1_norm_post, mix_norm_pre, w_in, gate_bias, rel_table, w_attn_out, conv_glu_bias, conv_dw_w, conv_dw_b, conv_ln_g, conv_ln_b, conv_w_out, w_out, mix_norm_post, ffn2_norm_pre, ffn2_w_gate, ffn2_w_up, ffn2_w_down, ffn2_norm_post, loss_target, m_ffn1_norm_pre, m_ffn1_w_gate, m_ffn1_w_up, m_ffn1_w_down, m_ffn1_norm_post, m_mix_norm_pre, m_w_in, m_gate_bias, m_rel_table, m_w_attn_out, m_conv_glu_bias, m_conv_dw_w, m_conv_dw_b, m_conv_ln_g, m_conv_ln_b, m_conv_w_out, m_w_out, m_mix_norm_post, m_ffn2_norm_pre, m_ffn2_w_gate, m_ffn2_w_up, m_ffn2_w_down, m_ffn2_norm_post, v_ffn1_norm_pre, v_ffn1_w_gate, v_ffn1_w_up, v_ffn1_w_down, v_ffn1_norm_post, v_mix_norm_pre, v_w_in, v_gate_bias, v_rel_table, v_w_attn_out, v_conv_glu_bias, v_conv_dw_w, v_conv_dw_b, v_conv_ln_g, v_conv_ln_b, v_conv_w_out, v_w_out, v_mix_norm_post, v_ffn2_norm_pre, v_ffn2_w_gate, v_ffn2_w_up, v_ffn2_w_down, v_ffn2_norm_post):
    return _step(dict(locals()))


WEIGHTS = ["ffn1_norm_pre", "ffn1_w_gate", "ffn1_w_up", "ffn1_w_down", "ffn1_norm_post", "mix_norm_pre", "w_in",
           "gate_bias", "rel_table", "w_attn_out", "conv_glu_bias", "conv_dw_w", "conv_dw_b", "conv_ln_g",
           "conv_ln_b", "conv_w_out", "w_out", "mix_norm_post", "ffn2_norm_pre", "ffn2_w_gate", "ffn2_w_up",
           "ffn2_w_down", "ffn2_norm_post"]
FS = FF // NDEV
PS = (3 * DA + 2 * DC + 2 * D) // NDEV
OS = D // NDEV


def _local_step(xs, target, w, rel_table):
    t = xs.shape[0]
    vec = lambda n: w[n].reshape(1, -1)
    g_pre1, g_post1, g_mix, g_mixp = vec("ffn1_norm_pre"), vec("ffn1_norm_post"), vec("mix_norm_pre"), vec("mix_norm_post")
    g_pre2, g_post2 = vec("ffn2_norm_pre"), vec("ffn2_norm_post")
    gate_b, glu_b = vec("gate_bias"), vec("conv_glu_bias")
    dw_b, ln_g, ln_b = vec("conv_dw_b"), vec("conv_ln_g"), vec("conv_ln_b")

    tr = lambda a: jnp.transpose(a[0]).astype(BF16)
    sh_gu1 = jnp.stack([tr(w["ffn1_w_gate"]), tr(w["ffn1_w_up"])])
    sh_mid = [w["ffn1_w_down"].astype(BF16), tr(w["w_in"])[None], w["w_out"].astype(BF16),
              jnp.stack([tr(w["w_attn_out"]), tr(w["conv_w_out"])]),
              jnp.pad(w["conv_dw_w"][0, :, 0, :], ((0, DW_ROWS - CW), (0, 0)))[None]]
    sh_2 = jnp.stack([tr(w["ffn2_w_gate"]), tr(w["ffn2_w_up"]), w["ffn2_w_down"][0].astype(BF16)])

    (n1,), (w_gu1,) = _rowwise("pre1", lambda xv, g: ((_rms(xv)[0] * g),), [xs], [g_pre1], [(D, BF16)],
                               job=("gather", [sh_gu1]))
    w_gu1 = w_gu1.reshape(2, FF, D)
    (a1, b1, s1), (w_d1, wb, wc, wd, we) = _ffn_up("ffn1_up", n1, w_gu1, 0, job=("gather", sh_mid))
    w_d1, wb, wc, wd = w_d1.reshape(1, FF, D), wb.reshape(NDEV * PS, D), wc.reshape(D, D), wd.reshape(2, D, DA)
    dw_full = jnp.transpose(we[0], (1, 0, 2)).reshape(DW_ROWS, DC)[:CW]
    row = lambda i, j, q: (i, 0)
    top = lambda i, j, q: (0, 0)
    tmr = min(512, t)

    def post1(fv, xv, gp, gm):
        h = xv + 0.5 * (_rms(fv)[0] * gp)
        return fv, h, _rms(h)[0] * gm

    f1, h1, u = _mm_rows("ffn1_down", t, tmr, [(s1, (tmr, FF), row)], [(w_d1, (None, FF, D), lambda i, j, q: (0, 0, 0))],
                         [[(0, 0)]], NN, post1, [xs], [g_post1, g_mix], [(D, F32), (D, F32), (D, BF16)])

    def proj_epi(accs, extras):
        cat = lambda parts: jnp.concatenate(parts, axis=1)
        return (accs[0], jnp.where(pl.program_id(0) > 0, cat(accs[1:3]), 0.0), cat(accs[3:5]), cat(accs[5:9]))

    prev = lambda i, j, q: (jnp.maximum(i - 1, 0), 0)
    qkv, kvp, cin, gg = _mm(
        "proj", (t // KPAD + 1, 1, 1), [(u, (KPAD, D), prev)],
        [(wb, (DA, D), (lambda i, j, q, blk=blk: (blk, 0))) for blk in range(NDEV * PS // DA)],
        [[(0, blk)] for blk in range(NDEV * PS // DA)], NT, proj_epi,
        [((t, DA), BF16, (KPAD, DA), prev), ((t + KPAD, 2 * DA), BF16, (KPAD, 2 * DA), lambda i, j, q: (i, 0)),
         ((t, 2 * DC), F32, (KPAD, 2 * DC), prev), ((t, 2 * D), BF16, (KPAD, 2 * D), prev)])

    bias = _relbias_fwd(jnp.pad(rel_table[0], ((0, 0), (0, 384 - NREL))))
    att, (w_2,) = _attn_fwd(qkv, kvp, bias, job=("gather", [sh_2]))
    w_2 = w_2.reshape(3, FF, D)
    cs, pre = _conv_fwd(cin, glu_b, dw_full, dw_b, ln_g, ln_b)
    def merge(yav, ybv, gv, gb):
        gates = _sigmoid(gv + gb)
        return yav, ybv, gates[:, :D] * yav + gates[:, D:] * ybv

    wd_block = lambda p: (wd, (None, D, DC), lambda i, j, q: (p, 0, 0))
    ya, yb, merged = _mm_rows("branch_out", t, tmr, [(att, (tmr, DA), row), (cs, (tmr, DC), row)],
                              [wd_block(0), wd_block(1)], [[(0, 0)], [(1, 1)]], NT, merge, [gg], [gate_b],
                              [(D, BF16), (D, BF16), (D, BF16)])

    def postm(mv, hv, gp, g2):
        h = hv + _rms(mv)[0] * gp
        return mv, h, _rms(h)[0] * g2

    mm_, h2, n2 = _mm_rows("mix_out", t, tmr, [(merged, (tmr, D), row)], [(wc, (D, D), top)], [[(0, 0)]], NN,
                           postm, [h1], [g_mixp, g_pre2], [(D, F32), (D, F32), (D, BF16)])
    a2, b2, s2 = _ffn_up("ffn2_up", n2, w_2, 0)

    def post2(fv, hv, tv, gp):
        fh, r = _rms(fv)
        yv = hv + 0.5 * (fh * gp)
        err = yv - tv
        dy = err * (1.0 / D)
        df, dg = _rms_bwd(fh, r, gp, 0.5 * dy)
        return dy, df, (0.5 / D) * (err * err), dg

    dy, df2, loss_row, d_post2 = _mm_rows(
        "ffn2_down", t, tmr, [(s2, (tmr, FF), row)], [(w_2, (None, FF, D), lambda i, j, q: (2, 0, 0))], [[(0, 0)]],
        NN, post2, [h2, target], [g_post2], [(D, F32), (D, BF16)], [D, D])

    tmw = 1408
    nfi = FF // tmw
    tkf = min(2048, t)
    tkp = min(1024, t)

    def wgrad_down(name, s, df, shape, part, carry_buf, job=None):
        return _wgrad(name, (s, (tkf, tmw), lambda i, j, q: (q, i)), df,
                      (None, tmw, D), lambda i, j, q: (part, i, 0), nfi, carry_buf, shape, tk=tkf, job=job)

    def wgrad_gate_up(name, dab, nrm, shape, carry_buf, job=None):
        return _wgrad(name, (dab, (None, tkf, tmw), lambda i, j, q: (i // nfi, q, i % nfi)), nrm,
                      (None, tmw, D), lambda i, j, q: (i // nfi, i % nfi, 0), 2 * nfi, carry_buf, shape,
                      tk=tkf, job=job)

    g_2 = wgrad_down("ffn2_wgrad_d", s2, df2, (3, FF, D), 2, None)
    dab2 = _ffn_bwd_act("ffn2_bwd_act", df2, w_2, 2, a2, b2)
    g_2 = wgrad_gate_up("ffn2_wgrad_gu", dab2, n2, (3, FF, D), g_2)

    def bwd_pre2(dnv, hv, dyv, mv, g2, gp):
        hh, r = _rms(hv)
        dx, dg2 = _rms_bwd(hh, r, g2, dnv)
        dh = dyv + dx
        mh, rm = _rms(mv)
        dm, dgp = _rms_bwd(mh, rm, gp, dh)
        return dh, dm, dg2, dgp

    tmb = 256
    dh2, dm, d_pre2, d_mixp = _mm_rows(
        "ffn2_bwd_in", t, tmb,
        [(dab2, (None, tmb, FF), lambda i, j, q: (0, i, 0)), (dab2, (None, tmb, FF), lambda i, j, q: (1, i, 0))],
        [(w_2, (None, FF, D), lambda i, j, q: (0, 0, 0)), (w_2, (None, FF, D), lambda i, j, q: (1, 0, 0))],
        [[(0, 0), (1, 1)]], NN, bwd_pre2, [h2, dy, mm_], [g_pre2, g_mixp], [(D, F32), (D, BF16)], [D, D])

    def merge_bwd(dmv, yav, ybv, gv, gb, wao, wco):
        gates = _sigmoid(gv + gb)
        ga, gbb = gates[:, :D], gates[:, D:]
        dgg = jnp.concatenate([dmv * yav * ga * (1.0 - ga), dmv * ybv * gbb * (1.0 - gbb)], axis=1)
        dyav, dybv = (dmv * ga).astype(BF16), (dmv * gbb).astype(BF16)
        return (dyav, dybv, dgg, jnp.dot(dyav, wao, preferred_element_type=F32),
                jnp.dot(dybv, wco, preferred_element_type=F32), dgg)

    dya, dyb, dgg, datt, dcs, d_gate_b = _mm_rows(
        "mix_out_bwd", t, tmr, [(dm, (tmr, D), row)], [(wc, (D, D), top)], [[(0, 0)]], NT, merge_bwd,
        [ya, yb, gg], [gate_b, wd[0], wd[1]], [(D, BF16), (D, BF16), (2 * D, BF16), (DA, BF16), (DC, F32)], [2 * D])
    g_c = _wgrad("mix_out_wgrad", (merged, (tkp, D), lambda i, j, q: (q, 0)), dm,
                 (D, D), lambda i, j, q: (0, 0), 1, None, (D, D), tk=tkp)
    g_d = _wgrad("attn_out_wgrad", (dya, (tkp, D), lambda i, j, q: (q, 0)), att,
                 (None, D, DA), lambda i, j, q: (0, 0, 0), 1, None, (2, D, DA), tk=tkp)
    g_d = _wgrad("conv_out_wgrad", (dyb, (tkp, D), lambda i, j, q: (q, 0)), cs,
                 (None, D, DA), lambda i, j, q: (1, 0, 0), 1, g_d, (2, D, DA), tk=tkp)
    (dq, dkvp, dbias), (x_2, x_c, x_d) = _attn_bwd(
        qkv, kvp, bias, datt,
        job=("exchange", [g_2.reshape(3, NDEV, FS, D), g_c.reshape(1, NDEV, OS, D), g_d.reshape(2, NDEV, OS, DA)]))
    d_rel = _relbias_bwd(dbias)
    dcin, d_glu_b, d_dw8, d_dw_b, d_ln_g, d_ln_b = _conv_bwd(dcs, pre, cin, glu_b, dw_full, ln_g, ln_b)
    g_dw = jnp.pad(d_dw8, ((0, 8 * (DW_ROWS - CW)), (0, 0)))
    g_dw = g_dw.reshape(DW_ROWS * 8, NDEV, DC // NDEV).transpose(1, 0, 2)

    top = lambda i, j, q: (0, 0)
    g_b = jnp.concatenate([
        _wgrad("proj_wgrad_q", (dq, (tkp, DA), lambda i, j, q: (q, 0)), u, (DA, D), top, 1, None, (DA, D), tk=tkp),
        _wgrad("proj_wgrad_kv", (dkvp, (KPAD, 2 * DA), lambda i, j, q: (q + 1, 0)), u, (2 * DA, D), top, 1, None,
               (2 * DA, D), tk=KPAD),
        _wgrad("proj_wgrad_c", (dcin, (tkp, 2 * DC), lambda i, j, q: (q, 0)), u, (2 * DC, D), top, 1, None,
               (2 * DC, D), tk=tkp),
        _wgrad("proj_wgrad_g", (dgg, (tkp, 2 * D), lambda i, j, q: (q, 0)), u, (2 * D, D), top, 1, None,
               (2 * D, D), tk=tkp)], axis=0)

    tmu = 256
    a_ops = [(dq, (tmu, DA), row),
             (dkvp, (tmu, 2 * DA), lambda i, j, q: (i + KPAD // tmu, 0)),
             (dcin, (tmu, 2 * DC), row),
             (dgg, (tmu, 2 * D), row)]
    b_ops = [(wb[:DA], (DA, D), top), (wb[DA:3 * DA], (2 * DA, D), top),
             (wb[3 * DA:3 * DA + 2 * DC], (2 * DC, D), top), (wb[3 * DA + 2 * DC:], (2 * D, D), top)]

    def bwd_mix(duv, hv, dhv, fv, gm, gp):
        hh, r = _rms(hv)
        dx, dgm = _rms_bwd(hh, r, gm, duv)
        dh = dhv + dx
        fh, rf = _rms(fv)
        df, dgp = _rms_bwd(fh, rf, gp, 0.5 * dh)
        return dh, df, dgm, dgp

    (dh1, df1, d_mix, d_post1), (x_b, x_dw) = _mm_rows(
        "proj_bwd", t, tmu, a_ops, b_ops, [[(0, 0), (1, 1), (2, 2), (3, 3)]], NN, bwd_mix, [h1, dh2, f1],
        [g_mix, g_post1], [(D, F32), (D, BF16)], [D, D],
        job=("exchange", [g_b.reshape(1, NDEV, PS, D), g_dw[None]]))
    g_d1 = wgrad_down("ffn1_wgrad_d", s1, df1, (1, FF, D), 0, None)
    dab1, (x_d1,) = _ffn_bwd_act("ffn1_bwd_act", df1, w_d1, 0, a1, b1,
                                 job=("exchange", [g_d1.reshape(1, NDEV, FS, D)]))

    def wgrad_half(name, p, job=None):
        return _wgrad(name, (dab1, (None, tkf, tmw), lambda i, j, q: (p, q, i)), n1,
                      (None, tmw, D), lambda i, j, q: (0, i, 0), nfi, None, (1, FF, D), tk=tkf, job=job)

    g_g1 = wgrad_half("ffn1_wgrad_g", 0)
    g_u1, (x_g1,) = wgrad_half("ffn1_wgrad_u", 1, job=("exchange", [g_g1.reshape(1, NDEV, FS, D)]))

    def bwd_pre1(dnv, xv, dhv, g1):
        xh, r = _rms(xv)
        dx, dg1 = _rms_bwd(xh, r, g1, dnv)
        return dhv + dx, dg1

    (dx, d_pre1), (x_u1,) = _mm_rows(
        "ffn1_bwd_in", t, tmb,
        [(dab1, (None, tmb, FF), lambda i, j, q: (0, i, 0)), (dab1, (None, tmb, FF), lambda i, j, q: (1, i, 0))],
        [(w_gu1, (None, FF, D), lambda i, j, q: (0, 0, 0)), (w_gu1, (None, FF, D), lambda i, j, q: (1, 0, 0))],
        [[(0, 0), (1, 1)]], NN, bwd_pre1, [xs, dh1], [g_pre1], [(D, F32)], [D],
        job=("exchange", [g_u1.reshape(1, NDEV, FS, D)]))

    small_g = {"ffn1_norm_pre": d_pre1, "ffn1_norm_post": d_post1, "mix_norm_pre": d_mix, "gate_bias": d_gate_b,
               "rel_table": d_rel, "conv_glu_bias": d_glu_b, "conv_dw_b": d_dw_b, "conv_ln_g": d_ln_g,
               "conv_ln_b": d_ln_b, "mix_norm_post": d_mixp, "ffn2_norm_pre": d_pre2, "ffn2_norm_post": d_post2}
    return loss_row, dx, (x_g1, x_u1, x_d1, x_2, x_b, x_c, x_d, x_dw), small_g


def _step(args):
    names = WEIGHTS
    w = {n: args[n] for n in names}
    fs, ps, os_ = FS, PS, OS
    conv_dw_w = args["conv_dw_w"]
    loss_row, dx, (x_g1, x_u1, x_d1, x_2, x_b, x_c, x_d, x_dw), small_g = _local_step(
        args["x"][0], args["loss_target"][0], w, args["rel_table"])

    g_small = _pack_small(small_g, loss_row)
    (x_s,) = _comm_only("gather_small_grads", ("gather", [g_small[None]]))

    s_g1 = _sum_devices("sum_ffn1_g", x_g1.reshape(NDEV, fs, D), fs)
    s_u1 = _sum_devices("sum_ffn1_u", x_u1.reshape(NDEV, fs, D), fs)
    s_2 = _sum_devices("sum_ffn2", x_2.reshape(NDEV, 3 * fs, D), fs).reshape(3, fs, D)
    s_b = _sum_devices("sum_proj", x_b.reshape(NDEV, ps, D), ps)
    s_d = _sum_devices("sum_out", x_d.reshape(NDEV, 2 * os_, DA), 2 * os_).reshape(2, os_, DA)
    s_dw = _sum_devices("sum_dw", x_dw.reshape(NDEV, DW_ROWS * 8, DC // NDEV), DW_ROWS * 8)

    grads = {
        "ffn1_w_gate": jnp.transpose(s_g1)[None], "ffn1_w_up": jnp.transpose(s_u1)[None],
        "ffn2_w_gate": jnp.transpose(s_2[0])[None], "ffn2_w_up": jnp.transpose(s_2[1])[None], "ffn2_w_down": s_2[2][None],
        "w_in": jnp.transpose(s_b)[None],
        "w_attn_out": jnp.transpose(s_d[0])[None], "conv_w_out": jnp.transpose(s_d[1])[None],
    }
    deltas, new_m, new_v = {}, {}, {}
    for n, parts in (("ffn1_w_down", x_d1.reshape(NDEV, fs, D)), ("w_out", x_c.reshape(NDEV, os_, D))):
        shp = w[n].shape
        two = lambda a: a.reshape(shp[1], shp[2])
        outs = _adamw_sum("adamw_" + n, two(w[n]), parts, two(args["m_" + n]), two(args["v_" + n]), shp[1] // 2)
        grads[n], deltas[n], new_m[n], new_v[n] = [a.reshape(shp) for a in outs]

    def flat2(a, n):
        return jnp.pad(a[0], ((0, 0), (0, RELP - NREL))) if n == "rel_table" else a.reshape(1, -1)

    small, loss_terms = _small_update(
        x_s.reshape(NDEV, SMALL_ROWS, D), {n: flat2(w[n], n) for n in SMALL},
        {n: flat2(args["m_" + n], n) for n in SMALL}, {n: flat2(args["v_" + n], n) for n in SMALL})
    for n in SMALL:
        vals = [a[:, :NREL] if n == "rel_table" else a for a in small[n]]
        grads[n], deltas[n], new_m[n], new_v[n] = [a.reshape(w[n].shape) for a in vals]

    big = ["ffn1_w_gate", "ffn1_w_up", "w_in", "w_attn_out", "conv_w_out", "ffn2_w_gate", "ffn2_w_up", "ffn2_w_down"]
    for n in big:
        shp = w[n].shape
        two = lambda a: a.reshape(shp[1], shp[2])
        rows = shp[1]
        tr_ = rows // 2 if rows % 16 == 0 else rows
        d_, m_, v_ = _adamw("adamw_" + n, two(w[n]), two(grads[n]), two(args["m_" + n]), two(args["v_" + n]), tr_)
        deltas[n], new_m[n], new_v[n] = d_.reshape(shp), m_.reshape(shp), v_.reshape(shp)

    g_dw_own = _fold8("fold_dw", s_dw)[:CW]
    grads["conv_dw_w"] = g_dw_own.reshape(1, CW, 1, DC // NDEV)
    flat = lambda a: a.reshape(CW, DC // NDEV)
    d_, m_, v_ = _adamw("adamw_dw", flat(conv_dw_w), g_dw_own, flat(args["m_conv_dw_w"]), flat(args["v_conv_dw_w"]))
    shp = conv_dw_w.shape
    deltas["conv_dw_w"], new_m["conv_dw_w"], new_v["conv_dw_w"] = d_.reshape(shp), m_.reshape(shp), v_.reshape(shp)

    loss = jnp.sum(loss_terms)
    return (loss, dx[None], *[grads[n] for n in names], *[deltas[n] for n in names],
            *[new_m[n] for n in names], *[new_v[n] for n in names])


def _fold8(name, a):
    r8, c = a.shape

    def body(a_ref, o_ref):
        o_ref[...] = a_ref[...].reshape(r8 // 8, 8, c).sum(axis=1)

    return pl.pallas_call(
        body,
        name=name,
        out_shape=jax.ShapeDtypeStruct((r8 // 8, c), F32),
        in_specs=[pl.BlockSpec(memory_space=pltpu.VMEM)],
        out_specs=pl.BlockSpec(memory_space=pltpu.VMEM),
        compiler_params=_params(),
    )(a)
```

```python
import functools

import jax
import jax.numpy as jnp
from jax import lax
from jax.experimental import pallas as pl
from jax.experimental.pallas import tpu as pltpu

F32 = jnp.float32
BF16 = jnp.bfloat16

D = 1024
FF = 2816
DA = 512
DC = 512
NH = 8
CHUNK = 64
LEFT = 8
CW = 31
NREL = 257
EPS = 1e-6
NDEV = 8

QB = 4 * CHUNK
KW = LEFT * CHUNK + QB
KPAD = LEFT * CHUNK
RELW = KW + QB
HALO = 32

TM = 512
VMEM_LIMIT = 56 * 1024 * 1024

ADAM_LR, ADAM_B1, ADAM_B2, ADAM_EPS, ADAM_WD, ADAM_STEP = 0.001, 0.9, 0.999, 1e-08, 0.01, 10

NT = (((1,), (1,)), ((), ()))
NN = (((1,), (0,)), ((), ()))
TN = (((0,), (0,)), ((), ()))

MESH = pl.DeviceIdType.MESH
ANY = pl.BlockSpec(memory_space=pl.ANY)


def _params(sem=None, vmem=VMEM_LIMIT):
    return pltpu.CompilerParams(dimension_semantics=sem, vmem_limit_bytes=vmem)


def _sigmoid(x):
    return 0.5 * jnp.tanh(0.5 * x) + 0.5


def _mm(name, grid, a_ops, b_ops, groups, dims, epi, outs, extras=(), carry=None, job=None, params=(),
        partials=()):
    nk = grid[2]
    na, nb, ne, no, ng = len(a_ops), len(b_ops), len(extras), len(outs), len(groups)
    npar, npart = len(params), len(partials)
    nc = 0 if carry is None else 1

    def body(*refs):
        a_refs = refs[:na]
        b_refs = refs[na:na + nb]
        e_refs = refs[na + nb:na + nb + ne]
        p_refs = refs[na + nb + ne:na + nb + ne + npar]
        o0 = na + nb + ne + npar + nc
        o_refs = refs[o0:o0 + no]
        s_refs = refs[o0 + no:o0 + no + npart]
        acc_refs = refs[o0 + no + npart:o0 + no + npart + (ng if nk > 1 else 0)]
        part_refs = refs[len(refs) - npart:] if npart else ()
        k = pl.program_id(2)
        prods = []
        for grp in groups:
            p = None
            for ai, bi in grp:
                t = lax.dot_general(a_refs[ai][...], b_refs[bi][...], dims, preferred_element_type=F32)
                p = t if p is None else p + t
            prods.append(p)

        def finish(vals):
            res = epi(vals, [e[...] for e in e_refs] + [p[...] for p in p_refs])
            for o, r in zip(o_refs, res[:no]):
                o[...] = r.astype(o.dtype)
            if npart:
                i, j = pl.program_id(0), pl.program_id(1)

                @pl.when(jnp.logical_and(i == 0, j == 0))
                def _():
                    for acc in part_refs:
                        acc[...] = jnp.zeros_like(acc)

                for acc, r in zip(part_refs, res[no:]):
                    acc[...] += r.reshape(r.shape[0] // 8, 8, r.shape[-1]).sum(axis=0)

                @pl.when(jnp.logical_and(i == grid[0] - 1, j == grid[1] - 1))
                def _():
                    for s, acc in zip(s_refs, part_refs):
                        s[...] = acc[...].sum(axis=0, keepdims=True)

        if nk == 1:
            finish(prods)
        else:
            @pl.when(k == 0)
            def _():
                for acc, p in zip(acc_refs, prods):
                    acc[...] = p

            @pl.when(k > 0)
            def _():
                for acc, p in zip(acc_refs, prods):
                    acc[...] += p

            @pl.when(k == nk - 1)
            def _():
                finish([acc[...] for acc in acc_refs])

    in_specs = [pl.BlockSpec(blk, im) for _, blk, im in list(a_ops) + list(b_ops) + list(extras)]
    in_specs += [pl.BlockSpec(p.shape, lambda i, j, q: (0, 0)) for p in params]
    args = [arr for arr, _, _ in list(a_ops) + list(b_ops) + list(extras)] + list(params)
    aliases = {}
    if carry is not None:
        in_specs.append(ANY)
        args.append(carry[0])
        aliases = {len(args) - 1: carry[1]}
    scratch = []
    if nk > 1:
        for _ in range(ng):
            blk = tuple(b for b in outs[0][2] if b is not None)
            scratch.append(pltpu.VMEM(blk, F32))
    scratch += [pltpu.VMEM((8, c), F32) for c in partials]
    res, jres = _call(
        body, name=name, grid=grid, in_specs=in_specs, args=args,
        out_specs=[pl.BlockSpec(blk, im) for _, _, blk, im in outs]
        + [pl.BlockSpec((1, c), lambda i, j, q: (0, 0)) for c in partials],
        out_shape=[jax.ShapeDtypeStruct(shp, dt) for shp, dt, _, _ in outs]
        + [jax.ShapeDtypeStruct((1, c), F32) for c in partials],
        scratch=scratch, sem=("arbitrary",) * 3 if partials else ("parallel", "parallel", "arbitrary"),
        aliases=aliases, job=job)
    return res if job is None else (res, jres)


def _first(accs, extras):
    return (accs[0],)


def _mm_rows(name, t, tm, a_ops, b_ops, groups, dims, fn, extras, params, outs, partials=(), nk=1, job=None):
    ne = len(extras)

    def epi(accs, rest):
        return fn(*accs, *[r.astype(F32) for r in rest[:ne]], *rest[ne:])

    e_ops = [(arr, (tm, arr.shape[1]), lambda i, j, q: (i, 0)) for arr in extras]
    o_ops = [((t, c), dt, (tm, c), lambda i, j, q: (i, 0)) for c, dt in outs]
    return _mm(name, (t // tm, 1, nk), a_ops, b_ops, groups, dims, epi, o_ops, e_ops, job=job, params=params,
               partials=partials)


def _mm_simple(name, a, b, dims, out_dtype, tm, tn, b_row0=0, b_rows=None):
    m, kk = a.shape
    tm = min(tm, m)
    if dims is NT:
        n = b.shape[0] if b_rows is None else b_rows
        assert b.shape[1] == kk and b_row0 % tn == 0
        b_op = (b, (tn, kk), lambda i, j, q: (j + b_row0 // tn, 0))
    else:
        assert b.shape[0] == kk
        n = b.shape[1]
        b_op = (b, (kk, tn), lambda i, j, q: (0, j))
    a_op = (a, (tm, kk), lambda i, j, q: (i, 0))
    out = ((m, n), out_dtype, (tm, tn), lambda i, j, q: (i, j))
    return _mm(name, (m // tm, n // tn, 1), [a_op], [b_op], [[(0, 0)]], dims, _first, [out])[0]


def _rowwise(name, fn, tiled, params, outs, partials=(), tm=TM, job=None):
    t = tiled[0].shape[0]
    steps = t // tm
    nt, npar, no, npart = len(tiled), len(params), len(outs), len(partials)

    def body(*refs):
        t_refs = refs[:nt]
        p_refs = refs[nt:nt + npar]
        o_refs = refs[nt + npar:nt + npar + no]
        s_refs = refs[nt + npar + no:nt + npar + no + npart]
        acc_refs = refs[nt + npar + no + npart:]
        i = pl.program_id(0)
        res = fn(*[r[...].astype(F32) for r in t_refs], *[r[...] for r in p_refs])
        for o, r in zip(o_refs, res[:no]):
            o[...] = r.astype(o.dtype)

        @pl.when(i == 0)
        def _():
            for acc in acc_refs:
                acc[...] = jnp.zeros_like(acc)

        for acc, r in zip(acc_refs, res[no:]):
            acc[...] += r.reshape(tm // 8, 8, r.shape[-1]).sum(axis=0)

        @pl.when(i == steps - 1)
        def _():
            for s, acc in zip(s_refs, acc_refs):
                s[...] = acc[...].sum(axis=0, keepdims=True)

    in_specs = [pl.BlockSpec((tm, a.shape[1]), lambda i: (i, 0)) for a in tiled]
    in_specs += [pl.BlockSpec(p.shape, lambda i: (0, 0)) for p in params]
    out_specs = [pl.BlockSpec((tm, c), lambda i: (i, 0)) for c, _ in outs]
    out_specs += [pl.BlockSpec((1, c), lambda i: (0, 0)) for c in partials]
    out_shape = [jax.ShapeDtypeStruct((t, c), dt) for c, dt in outs]
    out_shape += [jax.ShapeDtypeStruct((1, c), F32) for c in partials]
    res, jres = _call(body, name=name, grid=(steps,), in_specs=in_specs, args=[*tiled, *params], out_specs=out_specs,
                      out_shape=out_shape, scratch=[pltpu.VMEM((8, c), F32) for c in partials], sem=("arbitrary",),
                      job=job)
    return res if job is None else (res, jres)


def _rms(x):
    r = lax.rsqrt(jnp.mean(x * x, axis=-1, keepdims=True) + EPS)
    return x * r, r


def _rms_bwd(xhat, r, g, dy):
    dxh = dy * g
    dx = r * (dxh - xhat * jnp.mean(dxh * xhat, axis=-1, keepdims=True))
    return dx, dy * xhat


def _ffn_up(name, n, wa, part, tm=512, tf=1408, job=None):
    t = n.shape[0]

    def epi(accs, extras):
        a, b = accs
        sg = _sigmoid(a)
        silu = a * sg
        return silu, b * (sg + silu * (1.0 - sg)), silu * b

    a_op = (n, (tm, D), lambda f, i, q: (i, 0))
    b_ops = [(wa, (None, tf, D), lambda f, i, q: (part, f, 0)),
             (wa, (None, tf, D), lambda f, i, q: (part + 1, f, 0))]
    outs = [((t, FF), BF16, (tm, tf), lambda f, i, q: (i, f))] * 3
    return _mm(name, (FF // tf, t // tm, 1), [a_op], b_ops, [[(0, 0)], [(0, 1)]], NT, epi, outs, job=job)


def _ffn_bwd_act(name, df, wa, part, a, b, tm=512, tf=1408, job=None):
    t = df.shape[0]

    def epi(accs, extras):
        ds = accs[0]
        return (jnp.stack([ds * extras[1].astype(F32), ds * extras[0].astype(F32)]),)

    a_op = (df, (tm, D), lambda f, i, q: (i, 0))
    b_op = (wa, (None, tf, D), lambda f, i, q: (part, f, 0))
    extras = [(a, (tm, tf), lambda f, i, q: (i, f)), (b, (tm, tf), lambda f, i, q: (i, f))]
    out = ((2, t, FF), BF16, (2, tm, tf), lambda f, i, q: (0, i, f))
    res = _mm(name, (FF // tf, t // tm, 1), [a_op], [b_op], [[(0, 0)]], NT, epi, [out], extras, job=job)
    return res[0] if job is None else (res[0][0], res[1])


def _wgrad(name, dy_op, x, out_block, out_map, gi, carry_buf, out_shape, tk=512, job=None):
    t, c = x.shape
    b_op = (x, (tk, c), lambda i, j, q: (q, 0))
    out = (out_shape, BF16, out_block, out_map)
    carry = None if carry_buf is None else (carry_buf, 0)
    res = _mm(name, (gi, 1, t // tk), [dy_op], [b_op], [[(0, 0)]], TN, _first, [out], carry=carry, job=job)
    return res[0] if job is None else (res[0][0], res[1])


def _rel_onehot():
    j = lax.broadcasted_iota(jnp.int32, (384, RELW), 0)
    xx = lax.broadcasted_iota(jnp.int32, (384, RELW), 1)
    idx = jnp.clip(KPAD + QB - xx, -128, 128) + 128
    return (j == idx).astype(F32)


def _relbias_fwd(table):
    def body(t_ref, o_ref):
        rev = jnp.dot(t_ref[...], _rel_onehot(), precision=lax.Precision.HIGHEST, preferred_element_type=F32)
        for r in range(QB):
            row = pltpu.roll(rev, (RELW - (QB - r)) % RELW, 1)[:, :KW]
            rr = lax.broadcasted_iota(jnp.int32, (NH, KW), 1) >> 6
            ok = (rr >= (r // CHUNK)) & (rr <= (r // CHUNK) + LEFT)
            row = jnp.where(ok, row, -1e30)
            for h in range(NH):
                o_ref[h * QB + r:h * QB + r + 1, :] = row[h:h + 1, :]

    return pl.pallas_call(
        body,
        name="relbias_fwd",
        out_shape=jax.ShapeDtypeStruct((NH * QB, KW), F32),
        in_specs=[pl.BlockSpec(memory_space=pltpu.VMEM)],
        out_specs=pl.BlockSpec(memory_space=pltpu.VMEM),
        compiler_params=_params(),
    )(table)


def _relbias_bwd(dbias):
    def body(d_ref, o_ref):
        acc = jnp.zeros((NH, RELW), F32)
        for r in range(QB):
            rows = jnp.concatenate([d_ref[h * QB + r:h * QB + r + 1, :] for h in range(NH)], axis=0)
            wide = jnp.concatenate([rows, jnp.zeros((NH, RELW - KW), F32)], axis=1)
            acc = acc + pltpu.roll(wide, QB - r, 1)
        o_ref[...] = lax.dot_general(acc, _rel_onehot(), NT, precision=lax.Precision.HIGHEST,
                                     preferred_element_type=F32)

    return pl.pallas_call(
        body,
        name="relbias_bwd",
        out_shape=jax.ShapeDtypeStruct((NH, 384), F32),
        in_specs=[pl.BlockSpec(memory_space=pltpu.VMEM)],
        out_specs=pl.BlockSpec(memory_space=pltpu.VMEM),
        compiler_params=_params(),
    )(dbias)


HQ = QB // 2


def _stack_heads(x_pair):
    first = lax.broadcasted_iota(jnp.int32, (1, 128), 1) < 64
    zero = jnp.zeros_like(x_pair)
    a, b = jnp.where(first, x_pair, zero), jnp.where(first, zero, x_pair)
    return jnp.concatenate([a[:HQ], b[:HQ], a[HQ:], b[HQ:]], axis=0), first


def _unstack_heads(o, first):
    return jnp.concatenate([jnp.where(first, o[0:HQ], o[HQ:2 * HQ]),
                            jnp.where(first, o[2 * HQ:3 * HQ], o[3 * HQ:4 * HQ])], axis=0)


def _half_cols(hf):
    return slice(hf * HQ, hf * HQ + KW - HQ)


def _half_bias(b_ref, pair, hf):
    rows = lambda h: slice(h * QB + hf * HQ, h * QB + (hf + 1) * HQ)
    return jnp.concatenate([b_ref[rows(2 * pair), _half_cols(hf)], b_ref[rows(2 * pair + 1), _half_cols(hf)]], axis=0)


def _half_probs(s_full, b_ref, pair, hf, key_ok, normalize=True):
    s = s_full[2 * hf * HQ:2 * (hf + 1) * HQ, _half_cols(hf)] + _half_bias(b_ref, pair, hf)
    if key_ok is not None:
        s = jnp.where(key_ok[:, _half_cols(hf)], s, -1e30)
    e = jnp.exp(s - jnp.max(s, axis=-1, keepdims=True))
    inv = 1.0 / jnp.sum(e, axis=-1, keepdims=True)
    return e * inv if normalize else (e, inv)


def _widen(top, bottom):
    z = jnp.zeros((2 * HQ, HQ), top.dtype)
    return jnp.concatenate([jnp.concatenate([top, z], axis=1), jnp.concatenate([z, bottom], axis=1)], axis=0)


def _attn_fwd(qkv, kvp, bias, job=None):
    t = qkv.shape[0]

    def body(q_ref, kv_ref, b_ref, o_ref):
        i = pl.program_id(0)

        def run(masked):
            start = pl.multiple_of(i * QB, QB)
            col = lax.broadcasted_iota(jnp.int32, (1, KW), 1)
            key_ok = (col >= KPAD - i * QB) if masked else None
            for pair in range(4):
                lo = pair * 128
                kw = kv_ref[pl.ds(start, KW), lo:lo + 128]
                vw = kv_ref[pl.ds(start, KW), DA + lo:DA + lo + 128]
                qs, first = _stack_heads(q_ref[:, lo:lo + 128])
                s = lax.dot_general(qs * 0.125, kw, NT, preferred_element_type=F32)
                (e0, inv0), (e1, inv1) = [_half_probs(s, b_ref, pair, hf, key_ok, normalize=False) for hf in range(2)]
                o = jnp.dot(_widen(e0.astype(BF16), e1.astype(BF16)), vw, preferred_element_type=F32)
                o = o * jnp.concatenate([inv0, inv1], axis=0)
                o_ref[:, lo:lo + 128] = _unstack_heads(o, first).astype(BF16)

        pl.when(i < KPAD // QB)(lambda: run(True))
        pl.when(i >= KPAD // QB)(lambda: run(False))

    res, jres = _call(
        body, name="attn_fwd", grid=(t // QB,),
        in_specs=[pl.BlockSpec((QB, DA), lambda i: (i, 0)),
                  pl.BlockSpec(memory_space=pltpu.VMEM),
                  pl.BlockSpec(memory_space=pltpu.VMEM)],
        args=[qkv, kvp, bias],
        out_specs=[pl.BlockSpec((QB, DA), lambda i: (i, 0))],
        out_shape=[jax.ShapeDtypeStruct((t, DA), BF16)],
        sem=("arbitrary",), job=job)
    return res[0], jres


def _attn_bwd(qkv, kvp, bias, datt, job=None):
    t = qkv.shape[0]
    nb = t // QB
    flush = (KW - QB) // QB

    def body(q_ref, kv_ref, b_ref, do_ref, dq_ref, dkv_ref, db_out, acc_ref, db_ref):
        i = pl.program_id(0)

        @pl.when(i == 0)
        def _():
            acc_ref[...] = jnp.zeros_like(acc_ref)
            db_ref[...] = jnp.zeros_like(db_ref)

        def run(masked):
            start = pl.multiple_of(i * QB, QB)
            col = lax.broadcasted_iota(jnp.int32, (1, KW), 1)
            key_ok = (col >= KPAD - i * QB) if masked else None
            for pair in range(4):
                lo = pair * 128
                kw = kv_ref[pl.ds(start, KW), lo:lo + 128]
                vw = kv_ref[pl.ds(start, KW), DA + lo:DA + lo + 128]
                qs, first = _stack_heads(q_ref[:, lo:lo + 128])
                qs = qs * 0.125
                dos, _ = _stack_heads(do_ref[:, lo:lo + 128])
                s = lax.dot_general(qs, kw, NT, preferred_element_type=F32)
                dp = lax.dot_general(dos, vw, NT, preferred_element_type=F32)
                ps, dss = [], []
                for hf in range(2):
                    p = _half_probs(s, b_ref, pair, hf, key_ok)
                    dph = dp[2 * hf * HQ:2 * (hf + 1) * HQ, _half_cols(hf)]
                    ds = p * (dph - jnp.sum(p * dph, axis=-1, keepdims=True))
                    for k, h in enumerate((2 * pair, 2 * pair + 1)):
                        db_ref[h * QB + hf * HQ:h * QB + (hf + 1) * HQ, _half_cols(hf)] += ds[k * HQ:(k + 1) * HQ]
                    ps.append(p.astype(BF16))
                    dss.append(ds.astype(BF16))
                pb, dsb = _widen(*ps), _widen(*dss)
                dq = jnp.dot(dsb, kw, preferred_element_type=F32)
                dq_ref[:, lo:lo + 128] = (_unstack_heads(dq, first) * 0.125).astype(BF16)
                acc_ref[:, lo:lo + 128] += lax.dot_general(dsb, qs, TN, preferred_element_type=F32)
                acc_ref[:, DA + lo:DA + lo + 128] += lax.dot_general(pb, dos, TN, preferred_element_type=F32)

        pl.when(i < KPAD // QB)(lambda: run(True))
        pl.when(jnp.logical_and(i >= KPAD // QB, i < nb))(lambda: run(False))

        dkv_ref[...] = acc_ref[0:QB, :].astype(BF16)
        rest = acc_ref[QB:KW, :]
        acc_ref[0:KW - QB, :] = rest
        acc_ref[KW - QB:KW, :] = jnp.zeros((QB, 2 * DA), F32)

        @pl.when(i == nb + flush - 1)
        def _():
            pltpu.sync_copy(db_ref, db_out)

    last = nb - 1
    res, jres = _call(
        body, name="attn_bwd", grid=(nb + flush,),
        in_specs=[pl.BlockSpec((QB, DA), lambda i: (jnp.minimum(i, last), 0)),
                  pl.BlockSpec(memory_space=pltpu.VMEM),
                  pl.BlockSpec(memory_space=pltpu.VMEM),
                  pl.BlockSpec((QB, DA), lambda i: (jnp.minimum(i, last), 0))],
        args=[qkv, kvp, bias, datt],
        out_specs=[pl.BlockSpec((QB, DA), lambda i: (jnp.minimum(i, last), 0)),
                   pl.BlockSpec((QB, 2 * DA), lambda i: (i, 0)),
                   ANY],
        out_shape=[jax.ShapeDtypeStruct((t, DA), BF16),
                   jax.ShapeDtypeStruct((t + KPAD, 2 * DA), BF16),
                   jax.ShapeDtypeStruct((NH * QB, KW), F32)],
        scratch=[pltpu.VMEM((KW, 2 * DA), F32), pltpu.VMEM((NH * QB, KW), F32)], sem=("arbitrary",), job=job)
    return res, jres


def _glu(c, gb):
    cb = c + gb
    return cb[:, :DC] * _sigmoid(cb[:, DC:])


def _ln_swish(pre, g, b):
    mu = jnp.mean(pre, axis=-1, keepdims=True)
    xc = pre - mu
    r = lax.rsqrt(jnp.mean(xc * xc, axis=-1, keepdims=True) + EPS)
    xhat = xc * r
    y = xhat * g + b
    return xhat, r, y


RT = 32


def _shifted_copies(src_ref, sh_ref, rows):
    for b in range(1, 8):
        sh_ref[b - 1, :, :] = src_ref[pl.ds(b, rows), :]


def _tap(src_ref, sh_ref, off, r0, rows=RT):
    a, b = divmod(off, 8)
    ref = src_ref if b == 0 else sh_ref.at[b - 1]
    if isinstance(r0, int):
        return ref[r0 + 8 * a:r0 + 8 * a + rows, :]
    return ref[pl.ds(pl.multiple_of(r0 + 8 * a, 8), rows), :]


def _conv_fwd(cin, glu_b, dw_w, dw_b, ln_g, ln_b, tm=TM):
    t = cin.shape[0]
    hb = tm // HALO

    def body(c_ref, h_ref, gb_ref, w_ref, wb_ref, g_ref, b_ref, cs_ref, pre_ref, ext_ref, sh_ref):
        i = pl.program_id(0)
        halo = _glu(h_ref[...], gb_ref[...])
        ext_ref[0:HALO, :] = jnp.where(i > 0, halo, jnp.zeros_like(halo))
        ext_ref[HALO:HALO + tm, :] = _glu(c_ref[...], gb_ref[...])
        ext_ref[HALO + tm:HALO + tm + 8, :] = jnp.zeros((8, DC), F32)
        _shifted_copies(ext_ref, sh_ref, HALO + tm)

        def tile(rt, carry):
            r0 = pl.multiple_of(rt * RT, RT)
            acc = jnp.zeros((RT, DC), F32) + wb_ref[...]
            for j in range(CW):
                acc = acc + w_ref[j:j + 1, :] * _tap(ext_ref, sh_ref, HALO - (CW - 1) + j, r0)
            pre_ref[pl.ds(r0, RT), :] = acc
            return carry

        lax.fori_loop(0, tm // RT, tile, 0, unroll=2)
        _, _, y = _ln_swish(pre_ref[...], g_ref[...], b_ref[...])
        cs_ref[...] = (y * _sigmoid(y)).astype(BF16)

    vec = lambda n: pl.BlockSpec((1, n), lambda i: (0, 0))
    return pl.pallas_call(
        body,
        name="conv_fwd",
        grid=(t // tm,),
        in_specs=[pl.BlockSpec((tm, 2 * DC), lambda i: (i, 0)),
                  pl.BlockSpec((HALO, 2 * DC), lambda i: (jnp.maximum(i * hb - 1, 0), 0)),
                  vec(2 * DC), pl.BlockSpec((CW, DC), lambda i: (0, 0)), vec(DC), vec(DC), vec(DC)],
        out_specs=[pl.BlockSpec((tm, DC), lambda i: (i, 0)), pl.BlockSpec((tm, DC), lambda i: (i, 0))],
        out_shape=[jax.ShapeDtypeStruct((t, DC), BF16), jax.ShapeDtypeStruct((t, DC), F32)],
        scratch_shapes=[pltpu.VMEM((HALO + tm + 8, DC), F32), pltpu.VMEM((7, HALO + tm, DC), F32)],
        compiler_params=_params(("arbitrary",)),
    )(cin, cin, glu_b, dw_w, dw_b, ln_g, ln_b)


def _conv_bwd(dcs, pre, cin, glu_b, dw_w, ln_g, ln_b, tm=TM):
    t = cin.shape[0]
    hb = tm // HALO
    steps = t // tm
    nhb = t // HALO

    def dpre_of(dcs_v, pre_v, g, b):
        xhat, r, y = _ln_swish(pre_v, g, b)
        sg = _sigmoid(y)
        dy = dcs_v * (sg * (1.0 + y * (1.0 - sg)))
        dxh = dy * g
        dpre = r * (dxh - jnp.mean(dxh, axis=-1, keepdims=True)
                    - xhat * jnp.mean(dxh * xhat, axis=-1, keepdims=True))
        return dpre, dy * xhat, dy

    def body(dcs_ref, dcsn_ref, pre_ref, pren_ref, c_ref, ch_ref, gb_ref, w_ref, g_ref, b_ref,
             dc_ref, dgb_ref, dw_ref, dwb_ref, dg_ref, db_ref,
             gext_ref, dext_ref, shg_ref, shd_ref, a_gb, a_w, a_wb, a_g, a_b):
        i = pl.program_id(0)

        @pl.when(i == 0)
        def _():
            for a in (a_gb, a_w, a_wb, a_g, a_b):
                a[...] = jnp.zeros_like(a)

        fold = lambda v: v.reshape(v.shape[0] // 8, 8, v.shape[-1]).sum(axis=0)
        g, b = g_ref[...], b_ref[...]
        dpre, dg_t, db_t = dpre_of(dcs_ref[...], pre_ref[...], g, b)
        dpre_n, _, _ = dpre_of(dcsn_ref[...], pren_ref[...], g, b)
        dext_ref[0:tm, :] = dpre
        dext_ref[tm:tm + HALO, :] = jnp.where(i < steps - 1, dpre_n, jnp.zeros_like(dpre_n))
        dext_ref[tm + HALO:tm + HALO + 8, :] = jnp.zeros((8, DC), F32)
        a_wb[...] += fold(dpre)
        a_g[...] += fold(dg_t)
        a_b[...] += fold(db_t)
        halo = _glu(ch_ref[...], gb_ref[...])
        gext_ref[0:HALO, :] = jnp.where(i > 0, halo, jnp.zeros_like(halo))
        gext_ref[HALO:HALO + tm, :] = _glu(c_ref[...], gb_ref[...])
        gext_ref[HALO + tm:HALO + tm + 8, :] = jnp.zeros((8, DC), F32)
        _shifted_copies(gext_ref, shg_ref, HALO + tm)
        _shifted_copies(dext_ref, shd_ref, HALO + tm)

        for j in range(CW):
            a_w[8 * j:8 * j + 8, :] += fold(dext_ref[0:tm, :] * _tap(gext_ref, shg_ref, HALO - (CW - 1) + j, 0, tm))

        def tile(rt, carry):
            r0 = pl.multiple_of(rt * RT, RT)
            dglu = jnp.zeros((RT, DC), F32)
            for j in range(CW):
                dglu = dglu + w_ref[j:j + 1, :] * _tap(dext_ref, shd_ref, CW - 1 - j, r0)
            gext_ref[pl.ds(r0, RT), :] = dglu
            return carry

        lax.fori_loop(0, tm // RT, tile, 0, unroll=2)
        dglu = gext_ref[0:tm, :]
        cb = c_ref[...] + gb_ref[...]
        sg = _sigmoid(cb[:, DC:])
        dc = jnp.concatenate([dglu * sg, dglu * cb[:, :DC] * sg * (1.0 - sg)], axis=1)
        dc_ref[...] = dc.astype(BF16)
        a_gb[...] += fold(dc)

        @pl.when(i == steps - 1)
        def _():
            dgb_ref[...] = a_gb[...].sum(axis=0, keepdims=True)
            dw_ref[...] = a_w[...]
            dwb_ref[...] = a_wb[...].sum(axis=0, keepdims=True)
            dg_ref[...] = a_g[...].sum(axis=0, keepdims=True)
            db_ref[...] = a_b[...].sum(axis=0, keepdims=True)

    vec = lambda n: pl.BlockSpec((1, n), lambda i: (0, 0))
    nxt = lambda i: (jnp.minimum((i + 1) * hb, nhb - 1), 0)
    prv = lambda i: (jnp.maximum(i * hb - 1, 0), 0)
    return pl.pallas_call(
        body,
        name="conv_bwd",
        grid=(steps,),
        in_specs=[pl.BlockSpec((tm, DC), lambda i: (i, 0)), pl.BlockSpec((HALO, DC), nxt),
                  pl.BlockSpec((tm, DC), lambda i: (i, 0)), pl.BlockSpec((HALO, DC), nxt),
                  pl.BlockSpec((tm, 2 * DC), lambda i: (i, 0)), pl.BlockSpec((HALO, 2 * DC), prv),
                  vec(2 * DC), pl.BlockSpec((CW, DC), lambda i: (0, 0)), vec(DC), vec(DC)],
        out_specs=[pl.BlockSpec((tm, 2 * DC), lambda i: (i, 0)), vec(2 * DC),
                   pl.BlockSpec((CW * 8, DC), lambda i: (0, 0)), vec(DC), vec(DC), vec(DC)],
        out_shape=[jax.ShapeDtypeStruct((t, 2 * DC), BF16), jax.ShapeDtypeStruct((1, 2 * DC), F32),
                   jax.ShapeDtypeStruct((CW * 8, DC), F32), jax.ShapeDtypeStruct((1, DC), F32),
                   jax.ShapeDtypeStruct((1, DC), F32), jax.ShapeDtypeStruct((1, DC), F32)],
        scratch_shapes=[pltpu.VMEM((HALO + tm + 8, DC), F32), pltpu.VMEM((tm + HALO + 8, DC), F32),
                        pltpu.VMEM((7, HALO + tm, DC), F32), pltpu.VMEM((7, HALO + tm, DC), F32),
                        pltpu.VMEM((8, 2 * DC), F32), pltpu.VMEM((CW * 8, DC), F32),
                        pltpu.VMEM((8, DC), F32), pltpu.VMEM((8, DC), F32), pltpu.VMEM((8, DC), F32)],
        compiler_params=_params(("arbitrary",)),
    )(dcs, dcs, pre, pre, cin, cin, glu_b, dw_w, ln_g, ln_b)


def _place():
    x, y, c = lax.axis_index("x"), lax.axis_index("y"), lax.axis_index("c")
    return x, y, c


def _peers(x, y, c):
    out = []
    for k in range(1, NDEV):
        fx, fy, fc = (k >> 2) & 1, (k >> 1) & 1, k & 1
        px = 1 - x if fx else x
        py = 1 - y if fy else y
        pc = 1 - c if fc else c
        out.append((px, py, pc))
    return out


def _job_out_shapes(job):
    kind, arrays = job
    if kind == "gather":
        return [jax.ShapeDtypeStruct((a.shape[0], NDEV) + a.shape[1:], a.dtype) for a in arrays]
    return [jax.ShapeDtypeStruct((NDEV, a.shape[0]) + a.shape[2:], a.dtype) for a in arrays]


def _job_scratch(job):
    n = len(job[1])
    return [pltpu.SemaphoreType.DMA((n, NDEV - 1)), pltpu.SemaphoreType.DMA((n, NDEV - 1)),
            pltpu.SemaphoreType.DMA((n,))]


def _gather_parts(ins, outs, send_sems, recv_sems, local_sems):
    x, y, c = _place()
    me, sib = (x, y, c), (x, y, 1 - c)
    chips = [(1 - x, y), (x, 1 - y), (1 - x, 1 - y)]

    def copy(a, k, block, to, src=None):
        px, py, pc = block
        dst = outs[a].at[:, 4 * px + 2 * py + pc]
        return pltpu.make_async_remote_copy(
            src_ref=dst if src is None else src, dst_ref=dst,
            send_sem=send_sems.at[a, k], recv_sem=recv_sems.at[a, k], device_id=to, device_id_type=MESH)

    n = len(ins)
    local = [pltpu.make_async_copy(ins[a], outs[a].at[:, 4 * x + 2 * y + c], local_sems.at[a]) for a in range(n)]
    first = [[copy(a, 0, me, sib, src=ins[a])] + [copy(a, 1 + j, me, (*chip, c), src=ins[a])
                                                   for j, chip in enumerate(chips)] for a in range(n)]

    def start():
        for a in range(n):
            local[a].start()
            for cp in first[a]:
                cp.start()

    def relay():
        for j, chip in enumerate(chips):
            for a in range(n):
                copy(a, 1 + j, (*chip, c), me).wait_recv()
                copy(a, 4 + j, (*chip, c), sib).start()

    def finish():
        for a in range(n):
            copy(a, 0, sib, me).wait_recv()
            for j, chip in enumerate(chips):
                copy(a, 4 + j, (*chip, 1 - c), me).wait_recv()
        for a in range(n):
            for cp in first[a]:
                cp.wait_send()
            local[a].wait()
            for j, chip in enumerate(chips):
                copy(a, 4 + j, (*chip, c), sib).wait_send()

    return start, relay, finish


def _exchange_parts(ins, outs, send_sems, recv_sems, local_sems):
    x, y, c = _place()
    me = 4 * x + 2 * y + c
    n = len(ins)
    peers = _peers(x, y, c)
    local = [pltpu.make_async_copy(ins[a].at[:, me], outs[a].at[me], local_sems.at[a]) for a in range(n)]

    def copy(a, k):
        px, py, pc = peers[k]
        return pltpu.make_async_remote_copy(
            src_ref=ins[a].at[:, 4 * px + 2 * py + pc], dst_ref=outs[a].at[me],
            send_sem=send_sems.at[a, k], recv_sem=recv_sems.at[a, k], device_id=peers[k], device_id_type=MESH)

    def arrival(a, k):
        px, py, pc = peers[k]
        return pltpu.make_async_remote_copy(
            src_ref=ins[a].at[:, me], dst_ref=outs[a].at[4 * px + 2 * py + pc],
            send_sem=send_sems.at[a, k], recv_sem=recv_sems.at[a, k], device_id=peers[k], device_id_type=MESH)

    def start():
        for a in range(n):
            local[a].start()
            for k in range(NDEV - 1):
                copy(a, k).start()

    def finish():
        for a in range(n):
            for k in range(NDEV - 1):
                arrival(a, k).wait_recv()
        for a in range(n):
            for k in range(NDEV - 1):
                copy(a, k).wait_send()
            local[a].wait()

    return start, None, finish


def _call(body, *, name, grid, in_specs, args, out_specs, out_shape, scratch=(), sem=None, aliases=None, job=None):
    aliases = dict(aliases or {})
    if job is None:
        res = pl.pallas_call(
            body, name=name, grid=grid, in_specs=list(in_specs), out_specs=list(out_specs),
            out_shape=list(out_shape), scratch_shapes=list(scratch), input_output_aliases=aliases,
            compiler_params=_params(sem))(*args)
        return list(res), []
    kind, arrays = job
    n_in, n_out, n_scr, nj = len(args), len(out_shape), len(scratch), len(arrays)

    def wrapped(*refs):
        ins = refs[:n_in]
        jin = refs[n_in:n_in + nj]
        o0 = n_in + nj
        outs = refs[o0:o0 + n_out]
        jout = refs[o0 + n_out:o0 + n_out + nj]
        s0 = o0 + n_out + nj
        scr = refs[s0:s0 + n_scr]
        sems = refs[s0 + n_scr:]
        parts = _gather_parts if kind == "gather" else _exchange_parts
        start, relay, finish = parts(jin, jout, *sems)
        if not grid:
            start()
            body(*ins, *outs, *scr)
            if relay is not None:
                relay()
            finish()
            return
        first = last = None
        step, steps = 0, 1
        for d, g in enumerate(grid):
            f, l = pl.program_id(d) == 0, pl.program_id(d) == g - 1
            first = f if first is None else jnp.logical_and(first, f)
            last = l if last is None else jnp.logical_and(last, l)
            step, steps = step * g + pl.program_id(d), steps * g
        pl.when(first)(start)
        body(*ins, *outs, *scr)
        if relay is not None:
            pl.when(step == (7 * steps) // 8)(relay)
        pl.when(last)(finish)

    res = pl.pallas_call(
        wrapped, name=name, grid=grid, in_specs=list(in_specs) + [ANY] * nj,
        out_specs=list(out_specs) + [ANY] * nj, out_shape=list(out_shape) + _job_out_shapes(job),
        scratch_shapes=list(scratch) + _job_scratch(job), input_output_aliases=aliases,
        compiler_params=pltpu.CompilerParams(
            dimension_semantics=None if not grid else ("arbitrary",) * len(grid),
            vmem_limit_bytes=VMEM_LIMIT, has_side_effects=True))(*args, *arrays)
    return list(res[:n_out]), list(res[n_out:])


def _comm_only(name, job):
    return _call(lambda: None, name=name, grid=(), in_specs=[], args=[], out_specs=[], out_shape=[], job=job)[1]


def _sum_devices(name, parts, tr):
    _, r, c = parts.shape

    def body(p_ref, o_ref):
        acc = p_ref[0].astype(F32)
        for d in range(1, NDEV):
            acc = acc + p_ref[d].astype(F32)
        o_ref[...] = acc

    return pl.pallas_call(
        body,
        name=name,
        grid=(r // tr,),
        in_specs=[pl.BlockSpec((NDEV, tr, c), lambda i: (0, i, 0))],
        out_specs=pl.BlockSpec((tr, c), lambda i: (i, 0)),
        out_shape=jax.ShapeDtypeStruct((r, c), F32),
        compiler_params=_params(("parallel",)),
    )(parts)


def _adamw_sum(name, w, parts, m, v, tr):
    r, c = w.shape

    def body(w_ref, p_ref, m_ref, v_ref, g_ref, d_ref, nm_ref, nv_ref):
        gv = p_ref[0].astype(F32)
        for d in range(1, NDEV):
            gv = gv + p_ref[d].astype(F32)
        nm = ADAM_B1 * m_ref[...] + (1.0 - ADAM_B1) * gv
        nv = ADAM_B2 * v_ref[...] + (1.0 - ADAM_B2) * (gv * gv)
        m_hat = nm / (1.0 - ADAM_B1 ** ADAM_STEP)
        v_hat = nv / (1.0 - ADAM_B2 ** ADAM_STEP)
        g_ref[...] = gv
        d_ref[...] = -ADAM_LR * (m_hat / (jnp.sqrt(v_hat) + ADAM_EPS) + ADAM_WD * w_ref[...])
        nm_ref[...] = nm
        nv_ref[...] = nv

    spec = pl.BlockSpec((tr, c), lambda i: (i, 0))
    return pl.pallas_call(
        body,
        name=name,
        grid=(r // tr,),
        in_specs=[spec, pl.BlockSpec((NDEV, tr, c), lambda i: (0, i, 0)), spec, spec],
        out_specs=[spec] * 4,
        out_shape=[jax.ShapeDtypeStruct((r, c), F32)] * 4,
        compiler_params=_params(("parallel",)),
    )(w, parts, m, v)


def _adamw(name, w, g, m, v, tr=None):
    r, c = w.shape
    tr = r if tr is None else tr

    def body(w_ref, g_ref, m_ref, v_ref, d_ref, nm_ref, nv_ref):
        gv = g_ref[...]
        nm = ADAM_B1 * m_ref[...] + (1.0 - ADAM_B1) * gv
        nv = ADAM_B2 * v_ref[...] + (1.0 - ADAM_B2) * (gv * gv)
        m_hat = nm / (1.0 - ADAM_B1 ** ADAM_STEP)
        v_hat = nv / (1.0 - ADAM_B2 ** ADAM_STEP)
        d_ref[...] = -ADAM_LR * (m_hat / (jnp.sqrt(v_hat) + ADAM_EPS) + ADAM_WD * w_ref[...])
        nm_ref[...] = nm
        nv_ref[...] = nv

    spec = pl.BlockSpec((tr, c), lambda i: (i, 0))
    return pl.pallas_call(
        body,
        name=name,
        grid=(r // tr,),
        in_specs=[spec] * 4,
        out_specs=[spec] * 3,
        out_shape=[jax.ShapeDtypeStruct((r, c), F32)] * 3,
        compiler_params=_params(("parallel",)),
    )(w, g, m, v)


SMALL = ["ffn1_norm_pre", "ffn1_norm_post", "mix_norm_pre", "gate_bias", "rel_table", "conv_glu_bias",
         "conv_dw_b", "conv_ln_g", "conv_ln_b", "mix_norm_post", "ffn2_norm_pre", "ffn2_norm_post"]
SMALL_ROWS = 24
DW_ROWS = 32


LOSS_ROW = 20
RELP = 384
SMALL_LAYOUT = {}
_r = 0
for _name, _n in zip(SMALL, [D, D, D, 2 * D, None, 2 * DC, DC, DC, DC, D, D, D]):
    if _n is None:
        SMALL_LAYOUT[_name] = (_r, NH, RELP)
        _r += NH
    else:
        SMALL_LAYOUT[_name] = (_r, -(-_n // D), min(_n, D))
        _r += -(-_n // D)
assert _r == LOSS_ROW


def _small_pieces(name):
    r0, nr, nc = SMALL_LAYOUT[name]
    if name == "rel_table":
        return [(slice(r0, r0 + NH), slice(0, nc), slice(0, NH), slice(0, nc))]
    return [(slice(r0 + k, r0 + k + 1), slice(0, nc), slice(0, 1), slice(k * nc, (k + 1) * nc)) for k in range(nr)]


def _pack_small(vals, loss_row):
    def body(*refs):
        o = refs[-1]
        o[...] = jnp.zeros_like(o)
        for ref, name in zip(refs, SMALL):
            for prow, pcol, arow, acol in _small_pieces(name):
                o[prow, pcol] = ref[arow, acol]
        o[LOSS_ROW:LOSS_ROW + 1, :] = refs[len(SMALL)][...]

    vm = pl.BlockSpec(memory_space=pltpu.VMEM)
    return pl.pallas_call(body, name="pack_small", out_shape=jax.ShapeDtypeStruct((SMALL_ROWS, D), F32),
                          in_specs=[vm] * (len(SMALL) + 1), out_specs=vm,
                          compiler_params=_params())(*[vals[n] for n in SMALL], loss_row)


def _small_update(parts, w, m, v):
    ns = len(SMALL)

    def body(p_ref, *refs):
        ins, outs = refs[:3 * ns], refs[3 * ns:]

        def total(prow, pcol):
            g = p_ref[0, prow, pcol]
            for d in range(1, NDEV):
                g = g + p_ref[d, prow, pcol]
            return g

        for q, name in enumerate(SMALL):
            w_ref, m_ref, v_ref = ins[3 * q:3 * q + 3]
            o_g, o_d, o_m, o_v = outs[4 * q:4 * q + 4]
            for prow, pcol, arow, acol in _small_pieces(name):
                g = total(prow, pcol)
                nm = ADAM_B1 * m_ref[arow, acol] + (1.0 - ADAM_B1) * g
                nv = ADAM_B2 * v_ref[arow, acol] + (1.0 - ADAM_B2) * (g * g)
                m_hat = nm / (1.0 - ADAM_B1 ** ADAM_STEP)
                v_hat = nv / (1.0 - ADAM_B2 ** ADAM_STEP)
                o_g[arow, acol] = g
                o_d[arow, acol] = -ADAM_LR * (m_hat / (jnp.sqrt(v_hat) + ADAM_EPS) + ADAM_WD * w_ref[arow, acol])
                o_m[arow, acol] = nm
                o_v[arow, acol] = nv
        outs[4 * ns][...] = total(slice(LOSS_ROW, LOSS_ROW + 1), slice(None))

    args, out_shape = [], []
    for name in SMALL:
        args += [w[name], m[name], v[name]]
        out_shape += [jax.ShapeDtypeStruct(w[name].shape, F32)] * 4
    out_shape.append(jax.ShapeDtypeStruct((1, D), F32))
    vm = pl.BlockSpec(memory_space=pltpu.VMEM)
    res = pl.pallas_call(body, name="small_update", out_shape=out_shape, in_specs=[vm] * (1 + 3 * ns),
                         out_specs=[vm] * len(out_shape), compiler_params=_params())(parts, *args)
    return {name: res[4 * q:4 * q + 4] for q, name in enumerate(SMALL)}, res[-1]


def kernel(x, ffn1_norm_pre, ffn1_w_gate, ffn1_w_up, ffn1_w_down, ffn1_norm_post, mix_norm_pre, w_in, gate_bias, rel_table, w_attn_out, conv_glu_bias, conv_dw_w, conv_dw_b, conv_ln_g, conv_ln_b, conv_w_out, w_out, mix_norm_post, ffn2_norm_pre, ffn2_w_gate, ffn2_w_up, ffn2_w_down, ffn2_norm_post, loss_target, m_ffn1_norm_pre, m_ffn1_w_gate, m_ffn1_w_up, m_ffn1_w_down, m_ffn1_norm_post, m_mix_norm_pre, m_w_in, m_gate_bias, m_rel_table, m_w_attn_out, m_conv_glu_bias, m_conv_dw_w, m_conv_dw_b, m_conv_ln_g, m_conv_ln_b, m_conv_w_out, m_w_out, m_mix_norm_post, m_ffn2_norm_pre, m_ffn2_w_gate, m_ffn2_w_up, m_ffn2_w_down, m_ffn2_norm_post, v_ffn1_norm_pre, v_ffn1_w_gate, v_ffn1_w_up, v_ffn1_w_down, v_ffn1_norm_post, v_mix_norm_pre, v_w_in, v_gate_bias, v_rel_table, v_w_attn_out, v_conv_glu_bias, v_conv_dw_w, v_conv_dw_b, v_conv_ln_g, v_conv_ln_b, v_conv_w_out, v_w_out, v_mix_norm_post, v_ffn2_norm_pre, v_ffn2_w_gate, v_ffn2_w_up, v_ffn2_w_down, v_ffn2_norm_post):
    return _step(dict(locals()))


WEIGHTS = ["ffn1_norm_pre", "ffn1_w_gate", "ffn1_w_up", "ffn1_w_down", "ffn1_norm_post", "mix_norm_pre", "w_in",
           "gate_bias", "rel_table", "w_attn_out", "conv_glu_bias", "conv_dw_w", "conv_dw_b", "conv_ln_g",
           "conv_ln_b", "conv_w_out", "w_out", "mix_norm_post", "ffn2_norm_pre", "ffn2_w_gate", "ffn2_w_up",
           "ffn2_w_down", "ffn2_norm_post"]
FS = FF // NDEV
PS = (3 * DA + 2 * DC + 2 * D) // NDEV
OS = D // NDEV


def _local_step(xs, target, w, rel_table):
    t = xs.shape[0]
    vec = lambda n: w[n].reshape(1, -1)
    g_pre1, g_post1, g_mix, g_mixp = vec("ffn1_norm_pre"), vec("ffn1_norm_post"), vec("mix_norm_pre"), vec("mix_norm_post")
    g_pre2, g_post2 = vec("ffn2_norm_pre"), vec("ffn2_norm_post")
    gate_b, glu_b = vec("gate_bias"), vec("conv_glu_bias")
    dw_b, ln_g, ln_b = vec("conv_dw_b"), vec("conv_ln_g"), vec("conv_ln_b")

    tr = lambda a: jnp.transpose(a[0]).astype(BF16)
    sh_gu1 = jnp.stack([tr(w["ffn1_w_gate"]), tr(w["ffn1_w_up"])])
    sh_mid = [w["ffn1_w_down"].astype(BF16), tr(w["w_in"])[None], w["w_out"].astype(BF16),
              jnp.stack([tr(w["w_attn_out"]), tr(w["conv_w_out"])]),
              jnp.pad(w["conv_dw_w"][0, :, 0, :], ((0, DW_ROWS - CW), (0, 0)))[None]]
    sh_2 = jnp.stack([tr(w["ffn2_w_gate"]), tr(w["ffn2_w_up"]), w["ffn2_w_down"][0].astype(BF16)])

    (n1,), (w_gu1,) = _rowwise("pre1", lambda xv, g: ((_rms(xv)[0] * g),), [xs], [g_pre1], [(D, BF16)],
                               job=("gather", [sh_gu1]))
    w_gu1 = w_gu1.reshape(2, FF, D)
    (a1, b1, s1), (w_d1, wb, wc, wd, we) = _ffn_up("ffn1_up", n1, w_gu1, 0, job=("gather", sh_mid))
    w_d1, wb, wc, wd = w_d1.reshape(1, FF, D), wb.reshape(NDEV * PS, D), wc.reshape(D, D), wd.reshape(2, D, DA)
    dw_full = jnp.transpose(we[0], (1, 0, 2)).reshape(DW_ROWS, DC)[:CW]
    row = lambda i, j, q: (i, 0)
    top = lambda i, j, q: (0, 0)
    tmr = min(512, t)

    def post1(fv, xv, gp, gm):
        h = xv + 0.5 * (_rms(fv)[0] * gp)
        return fv, h, _rms(h)[0] * gm

    f1, h1, u = _mm_rows("ffn1_down", t, tmr, [(s1, (tmr, FF), row)], [(w_d1, (None, FF, D), lambda i, j, q: (0, 0, 0))],
                         [[(0, 0)]], NN, post1, [xs], [g_post1, g_mix], [(D, F32), (D, F32), (D, BF16)])

    def proj_epi(accs, extras):
        cat = lambda parts: jnp.concatenate(parts, axis=1)
        return (accs[0], jnp.where(pl.program_id(0) > 0, cat(accs[1:3]), 0.0), cat(accs[3:5]), cat(accs[5:9]))

    prev = lambda i, j, q: (jnp.maximum(i - 1, 0), 0)
    qkv, kvp, cin, gg = _mm(
        "proj", (t // KPAD + 1, 1, 1), [(u, (KPAD, D), prev)],
        [(wb, (DA, D), (lambda i, j, q, blk=blk: (blk, 0))) for blk in range(NDEV * PS // DA)],
        [[(0, blk)] for blk in range(NDEV * PS // DA)], NT, proj_epi,
        [((t, DA), BF16, (KPAD, DA), prev), ((t + KPAD, 2 * DA), BF16, (KPAD, 2 * DA), lambda i, j, q: (i, 0)),
         ((t, 2 * DC), F32, (KPAD, 2 * DC), prev), ((t, 2 * D), BF16, (KPAD, 2 * D), prev)])

    bias = _relbias_fwd(jnp.pad(rel_table[0], ((0, 0), (0, 384 - NREL))))
    att, (w_2,) = _attn_fwd(qkv, kvp, bias, job=("gather", [sh_2]))
    w_2 = w_2.reshape(3, FF, D)
    cs, pre = _conv_fwd(cin, glu_b, dw_full, dw_b, ln_g, ln_b)
    def merge(yav, ybv, gv, gb):
        gates = _sigmoid(gv + gb)
        return yav, ybv, gates[:, :D] * yav + gates[:, D:] * ybv

    wd_block = lambda p: (wd, (None, D, DC), lambda i, j, q: (p, 0, 0))
    ya, yb, merged = _mm_rows("branch_out", t, tmr, [(att, (tmr, DA), row), (cs, (tmr, DC), row)],
                              [wd_block(0), wd_block(1)], [[(0, 0)], [(1, 1)]], NT, merge, [gg], [gate_b],
                              [(D, BF16), (D, BF16), (D, BF16)])

    def postm(mv, hv, gp, g2):
        h = hv + _rms(mv)[0] * gp
        return mv, h, _rms(h)[0] * g2

    mm_, h2, n2 = _mm_rows("mix_out", t, tmr, [(merged, (tmr, D), row)], [(wc, (D, D), top)], [[(0, 0)]], NN,
                           postm, [h1], [g_mixp, g_pre2], [(D, F32), (D, F32), (D, BF16)])
    a2, b2, s2 = _ffn_up("ffn2_up", n2, w_2, 0)

    def post2(fv, hv, tv, gp):
        fh, r = _rms(fv)
        yv = hv + 0.5 * (fh * gp)
        err = yv - tv
        dy = err * (1.0 / D)
        df, dg = _rms_bwd(fh, r, gp, 0.5 * dy)
        return dy, df, (0.5 / D) * (err * err), dg

    dy, df2, loss_row, d_post2 = _mm_rows(
        "ffn2_down", t, tmr, [(s2, (tmr, FF), row)], [(w_2, (None, FF, D), lambda i, j, q: (2, 0, 0))], [[(0, 0)]],
        NN, post2, [h2, target], [g_post2], [(D, F32), (D, BF16)], [D, D])

    tmw = 1408
    nfi = FF // tmw
    tkf = min(2048, t)
    tkp = min(1024, t)

    def wgrad_down(name, s, df, shape, part, carry_buf, job=None):
        return _wgrad(name, (s, (tkf, tmw), lambda i, j, q: (q, i)), df,
                      (None, tmw, D), lambda i, j, q: (part, i, 0), nfi, carry_buf, shape, tk=tkf, job=job)

    def wgrad_gate_up(name, dab, nrm, shape, carry_buf, job=None):
        return _wgrad(name, (dab, (None, tkf, tmw), lambda i, j, q: (i // nfi, q, i % nfi)), nrm,
                      (None, tmw, D), lambda i, j, q: (i // nfi, i % nfi, 0), 2 * nfi, carry_buf, shape,
                      tk=tkf, job=job)

    g_2 = wgrad_down("ffn2_wgrad_d", s2, df2, (3, FF, D), 2, None)
    dab2 = _ffn_bwd_act("ffn2_bwd_act", df2, w_2, 2, a2, b2)
    g_2 = wgrad_gate_up("ffn2_wgrad_gu", dab2, n2, (3, FF, D), g_2)

    def bwd_pre2(dnv, hv, dyv, mv, g2, gp):
        hh, r = _rms(hv)
        dx, dg2 = _rms_bwd(hh, r, g2, dnv)
        dh = dyv + dx
        mh, rm = _rms(mv)
        dm, dgp = _rms_bwd(mh, rm, gp, dh)
        return dh, dm, dg2, dgp

    tmb = 256
    dh2, dm, d_pre2, d_mixp = _mm_rows(
        "ffn2_bwd_in", t, tmb,
        [(dab2, (None, tmb, FF), lambda i, j, q: (0, i, 0)), (dab2, (None, tmb, FF), lambda i, j, q: (1, i, 0))],
        [(w_2, (None, FF, D), lambda i, j, q: (0, 0, 0)), (w_2, (None, FF, D), lambda i, j, q: (1, 0, 0))],
        [[(0, 0), (1, 1)]], NN, bwd_pre2, [h2, dy, mm_], [g_pre2, g_mixp], [(D, F32), (D, BF16)], [D, D])

    def merge_bwd(dmv, yav, ybv, gv, gb, wao, wco):
        gates = _sigmoid(gv + gb)
        ga, gbb = gates[:, :D], gates[:, D:]
        dgg = jnp.concatenate([dmv * yav * ga * (1.0 - ga), dmv * ybv * gbb * (1.0 - gbb)], axis=1)
        dyav, dybv = (dmv * ga).astype(BF16), (dmv * gbb).astype(BF16)
        return (dyav, dybv, dgg, jnp.dot(dyav, wao, preferred_element_type=F32),
                jnp.dot(dybv, wco, preferred_element_type=F32), dgg)

    dya, dyb, dgg, datt, dcs, d_gate_b = _mm_rows(
        "mix_out_bwd", t, tmr, [(dm, (tmr, D), row)], [(wc, (D, D), top)], [[(0, 0)]], NT, merge_bwd,
        [ya, yb, gg], [gate_b, wd[0], wd[1]], [(D, BF16), (D, BF16), (2 * D, BF16), (DA, BF16), (DC, F32)], [2 * D])
    g_c = _wgrad("mix_out_wgrad", (merged, (tkp, D), lambda i, j, q: (q, 0)), dm,
                 (D, D), lambda i, j, q: (0, 0), 1, None, (D, D), tk=tkp)
    g_d = _wgrad("attn_out_wgrad", (dya, (tkp, D), lambda i, j, q: (q, 0)), att,
                 (None, D, DA), lambda i, j, q: (0, 0, 0), 1, None, (2, D, DA), tk=tkp)
    g_d = _wgrad("conv_out_wgrad", (dyb, (tkp, D), lambda i, j, q: (q, 0)), cs,
                 (None, D, DA), lambda i, j, q: (1, 0, 0), 1, g_d, (2, D, DA), tk=tkp)
    (dq, dkvp, dbias), (x_2, x_c, x_d) = _attn_bwd(
        qkv, kvp, bias, datt,
        job=("exchange", [g_2.reshape(3, NDEV, FS, D), g_c.reshape(1, NDEV, OS, D), g_d.reshape(2, NDEV, OS, DA)]))
    d_rel = _relbias_bwd(dbias)
    dcin, d_glu_b, d_dw8, d_dw_b, d_ln_g, d_ln_b = _conv_bwd(dcs, pre, cin, glu_b, dw_full, ln_g, ln_b)
    g_dw = jnp.pad(d_dw8, ((0, 8 * (DW_ROWS - CW)), (0, 0)))
    g_dw = g_dw.reshape(DW_ROWS * 8, NDEV, DC // NDEV).transpose(1, 0, 2)

    top = lambda i, j, q: (0, 0)
    g_b = jnp.concatenate([
        _wgrad("proj_wgrad_q", (dq, (tkp, DA), lambda i, j, q: (q, 0)), u, (DA, D), top, 1, None, (DA, D), tk=tkp),
        _wgrad("proj_wgrad_kv", (dkvp, (KPAD, 2 * DA), lambda i, j, q: (q + 1, 0)), u, (2 * DA, D), top, 1, None,
               (2 * DA, D), tk=KPAD),
        _wgrad("proj_wgrad_c", (dcin, (tkp, 2 * DC), lambda i, j, q: (q, 0)), u, (2 * DC, D), top, 1, None,
               (2 * DC, D), tk=tkp),
        _wgrad("proj_wgrad_g", (dgg, (tkp, 2 * D), lambda i, j, q: (q, 0)), u, (2 * D, D), top, 1, None,
               (2 * D, D), tk=tkp)], axis=0)

    tmu = 256
    a_ops = [(dq, (tmu, DA), row),
             (dkvp, (tmu, 2 * DA), lambda i, j, q: (i + KPAD // tmu, 0)),
             (dcin, (tmu, 2 * DC), row),
             (dgg, (tmu, 2 * D), row)]
    b_ops = [(wb[:DA], (DA, D), top), (wb[DA:3 * DA], (2 * DA, D), top),
             (wb[3 * DA:3 * DA + 2 * DC], (2 * DC, D), top), (wb[3 * DA + 2 * DC:], (2 * D, D), top)]

    def bwd_mix(duv, hv, dhv, fv, gm, gp):
        hh, r = _rms(hv)
        dx, dgm = _rms_bwd(hh, r, gm, duv)
        dh = dhv + dx
        fh, rf = _rms(fv)
        df, dgp = _rms_bwd(fh, rf, gp, 0.5 * dh)
        return dh, df, dgm, dgp

    (dh1, df1, d_mix, d_post1), (x_b, x_dw) = _mm_rows(
        "proj_bwd", t, tmu, a_ops, b_ops, [[(0, 0), (1, 1), (2, 2), (3, 3)]], NN, bwd_mix, [h1, dh2, f1],
        [g_mix, g_post1], [(D, F32), (D, BF16)], [D, D],
        job=("exchange", [g_b.reshape(1, NDEV, PS, D), g_dw[None]]))
    g_d1 = wgrad_down("ffn1_wgrad_d", s1, df1, (1, FF, D), 0, None)
    dab1, (x_d1,) = _ffn_bwd_act("ffn1_bwd_act", df1, w_d1, 0, a1, b1,
                                 job=("exchange", [g_d1.reshape(1, NDEV, FS, D)]))

    def wgrad_half(name, p, job=None):
        return _wgrad(name, (dab1, (None, tkf, tmw), lambda i, j, q: (p, q, i)), n1,
                      (None, tmw, D), lambda i, j, q: (0, i, 0), nfi, None, (1, FF, D), tk=tkf, job=job)

    g_g1 = wgrad_half("ffn1_wgrad_g", 0)
    g_u1, (x_g1,) = wgrad_half("ffn1_wgrad_u", 1, job=("exchange", [g_g1.reshape(1, NDEV, FS, D)]))

    def bwd_pre1(dnv, xv, dhv, g1):
        xh, r = _rms(xv)
        dx, dg1 = _rms_bwd(xh, r, g1, dnv)
        return dhv + dx, dg1

    (dx, d_pre1), (x_u1,) = _mm_rows(
        "ffn1_bwd_in", t, tmb,
        [(dab1, (None, tmb, FF), lambda i, j, q: (0, i, 0)), (dab1, (None, tmb, FF), lambda i, j, q: (1, i, 0))],
        [(w_gu1, (None, FF, D), lambda i, j, q: (0, 0, 0)), (w_gu1, (None, FF, D), lambda i, j, q: (1, 0, 0))],
        [[(0, 0), (1, 1)]], NN, bwd_pre1, [xs, dh1], [g_pre1], [(D, F32)], [D],
        job=("exchange", [g_u1.reshape(1, NDEV, FS, D)]))

    small_g = {"ffn1_norm_pre": d_pre1, "ffn1_norm_post": d_post1, "mix_norm_pre": d_mix, "gate_bias": d_gate_b,
               "rel_table": d_rel, "conv_glu_bias": d_glu_b, "conv_dw_b": d_dw_b, "conv_ln_g": d_ln_g,
               "conv_ln_b": d_ln_b, "mix_norm_post": d_mixp, "ffn2_norm_pre": d_pre2, "ffn2_norm_post": d_post2}
    return loss_row, dx, (x_g1, x_u1, x_d1, x_2, x_b, x_c, x_d, x_dw), small_g


def _step(args):
    names = WEIGHTS
    w = {n: args[n] for n in names}
    fs, ps, os_ = FS, PS, OS
    conv_dw_w = args["conv_dw_w"]
    loss_row, dx, (x_g1, x_u1, x_d1, x_2, x_b, x_c, x_d, x_dw), small_g = _local_step(
        args["x"][0], args["loss_target"][0], w, args["rel_table"])

    g_small = _pack_small(small_g, loss_row)
    (x_s,) = _comm_only("gather_small_grads", ("gather", [g_small[None]]))

    s_g1 = _sum_devices("sum_ffn1_g", x_g1.reshape(NDEV, fs, D), fs)
    s_u1 = _sum_devices("sum_ffn1_u", x_u1.reshape(NDEV, fs, D), fs)
    s_2 = _sum_devices("sum_ffn2", x_2.reshape(NDEV, 3 * fs, D), fs).reshape(3, fs, D)
    s_b = _sum_devices("sum_proj", x_b.reshape(NDEV, ps, D), ps)
    s_d = _sum_devices("sum_out", x_d.reshape(NDEV, 2 * os_, DA), 2 * os_).reshape(2, os_, DA)
    s_dw = _sum_devices("sum_dw", x_dw.reshape(NDEV, DW_ROWS * 8, DC // NDEV), DW_ROWS * 8)

    grads = {
        "ffn1_w_gate": jnp.transpose(s_g1)[None], "ffn1_w_up": jnp.transpose(s_u1)[None],
        "ffn2_w_gate": jnp.transpose(s_2[0])[None], "ffn2_w_up": jnp.transpose(s_2[1])[None], "ffn2_w_down": s_2[2][None],
        "w_in": jnp.transpose(s_b)[None],
        "w_attn_out": jnp.transpose(s_d[0])[None], "conv_w_out": jnp.transpose(s_d[1])[None],
    }
    deltas, new_m, new_v = {}, {}, {}
    for n, parts in (("ffn1_w_down", x_d1.reshape(NDEV, fs, D)), ("w_out", x_c.reshape(NDEV, os_, D))):
        shp = w[n].shape
        two = lambda a: a.reshape(shp[1], shp[2])
        outs = _adamw_sum("adamw_" + n, two(w[n]), parts, two(args["m_" + n]), two(args["v_" + n]), shp[1] // 2)
        grads[n], deltas[n], new_m[n], new_v[n] = [a.reshape(shp) for a in outs]

    def flat2(a, n):
        return jnp.pad(a[0], ((0, 0), (0, RELP - NREL))) if n == "rel_table" else a.reshape(1, -1)

    small, loss_terms = _small_update(
        x_s.reshape(NDEV, SMALL_ROWS, D), {n: flat2(w[n], n) for n in SMALL},
        {n: flat2(args["m_" + n], n) for n in SMALL}, {n: flat2(args["v_" + n], n) for n in SMALL})
    for n in SMALL:
        vals = [a[:, :NREL] if n == "rel_table" else a for a in small[n]]
        grads[n], deltas[n], new_m[n], new_v[n] = [a.reshape(w[n].shape) for a in vals]

    big = ["ffn1_w_gate", "ffn1_w_up", "w_in", "w_attn_out", "conv_w_out", "ffn2_w_gate", "ffn2_w_up", "ffn2_w_down"]
    for n in big:
        shp = w[n].shape
        two = lambda a: a.reshape(shp[1], shp[2])
        rows = shp[1]
        tr_ = rows // 2 if rows % 16 == 0 else rows
        d_, m_, v_ = _adamw("adamw_" + n, two(w[n]), two(grads[n]), two(args["m_" + n]), two(args["v_" + n]), tr_)
        deltas[n], new_m[n], new_v[n] = d_.reshape(shp), m_.reshape(shp), v_.reshape(shp)

    g_dw_own = _fold8("fold_dw", s_dw)[:CW]
    grads["conv_dw_w"] = g_dw_own.reshape(1, CW, 1, DC // NDEV)
    flat = lambda a: a.reshape(CW, DC // NDEV)
    d_, m_, v_ = _adamw("adamw_dw", flat(conv_dw_w), g_dw_own, flat(args["m_conv_dw_w"]), flat(args["v_conv_dw_w"]))
    shp = conv_dw_w.shape
    deltas["conv_dw_w"], new_m["conv_dw_w"], new_v["conv_dw_w"] = d_.reshape(shp), m_.reshape(shp), v_.reshape(shp)

    loss = jnp.sum(loss_terms)
    return (loss, dx[None], *[grads[n] for n in names], *[deltas[n] for n in names],
            *[new_m[n] for n in names], *[new_v[n] for n in names])


def _fold8(name, a):
    r8, c = a.shape

    def body(a_ref, o_ref):
        o_ref[...] = a_ref[...].reshape(r8 // 8, 8, c).sum(axis=1)

    return pl.pallas_call(
        body,
        name=name,
        out_shape=jax.ShapeDtypeStruct((r8 // 8, c), F32),
        in_specs=[pl.BlockSpec(memory_space=pltpu.VMEM)],
        out_specs=pl.BlockSpec(memory_space=pltpu.VMEM),
        compiler_params=_params(),
    )(a)
```
